```python
import jax, jax.numpy as jnp
from jax import lax
import numpy as np

D_MODEL = 1024
BATCH = 2
SEQ = 16384
DEPTH = 2
DEC_BATCH = 32
DEC_SEQ = 32
PAST_LEN = 2048

CHUNK = 64
PLE_DIM = 256
NORM_EPS = 1e-6

A_HEADS = 8
A_HEAD_DIM = 64
A_WIDTH = A_HEADS * A_HEAD_DIM
A_PREV_CHUNKS = 8
A_WIN = A_PREV_CHUNKS * CHUNK
A_BAND = A_WIN + CHUNK
A_REL_MAX = 256
A_REL_SIZE = CHUNK + A_REL_MAX

B_HEADS = 4
B_HEAD_DIM = 128
B_WIDTH = B_HEADS * B_HEAD_DIM
ROPE_BASE = 10000.0

C_HEADS = 8
C_HEAD_DIM = 64
C_WIDTH = C_HEADS * C_HEAD_DIM
C_RANK_W = 64
C_RANK_A = 64
C_RANK_G = 128
C_SIZES = (C_WIDTH, C_WIDTH, C_WIDTH, C_RANK_W, C_RANK_A, C_RANK_G)
C_SHIFT_WIDTH = sum(C_SIZES)
C_GN_EPS = 64e-5

N_BRANCHES = 3
BRANCH_WIDTH = 512
D_FF = ((8 * D_MODEL // 3 + 255) // 256) * 256
IN_SIZES = (A_WIDTH, A_WIDTH, A_WIDTH, B_WIDTH, B_WIDTH, B_WIDTH, B_WIDTH, C_SHIFT_WIDTH, N_BRANCHES * D_MODEL)
IN_WIDTH = sum(IN_SIZES)

kernel_name = "hybrid_streaming_encoder_step"


def _split(z, sizes):
    out, off = [], 0
    for s in sizes:
        out.append(z[..., off:off + s])
        off += s
    return out


def _rms_norm(x, gain=None):
    xf = x.astype(jnp.float32)
    y = xf * lax.rsqrt(jnp.mean(xf * xf, axis=-1, keepdims=True) + NORM_EPS)
    if gain is not None:
        y = y * gain.astype(jnp.float32)
    return y.astype(x.dtype)


def _band_attend(q, k, v, q_pos, k_pos, rel_bias):
    f32 = jnp.float32
    s = jnp.einsum("bqhd,bkhd->bhqk", q.astype(f32), k.astype(f32)) * (A_HEAD_DIM ** -0.5)
    rel = q_pos[:, None] - k_pos[None, :]
    idx = jnp.clip(rel, -(CHUNK - 1), A_REL_MAX) + (CHUNK - 1)
    s = s + rel_bias.astype(f32)[:, idx][None]
    q_chunk = q_pos // CHUNK
    k_chunk = k_pos // CHUNK
    ok = (k_pos[None, :] >= 0) & (k_chunk[None, :] <= q_chunk[:, None]) & (k_chunk[None, :] >= q_chunk[:, None] - A_PREV_CHUNKS)
    s = jnp.where(ok[None, None], s, -1e30)
    p = jax.nn.softmax(s, axis=-1)
    return jnp.einsum("bhqk,bkhd->bqhd", p, v.astype(f32)).astype(q.dtype)


def _band_attention_prompt(q, k, v, rel_bias):
    B, L, H, Dh = q.shape
    pad = jnp.zeros((B, A_WIN, H, Dh), k.dtype)
    kp = jnp.concatenate([pad, k], axis=1)
    vp = jnp.concatenate([pad, v], axis=1)

    def one_chunk(c):
        start = c * CHUNK
        q_c = lax.dynamic_slice_in_dim(q, start, CHUNK, axis=1)
        k_b = lax.dynamic_slice_in_dim(kp, start, A_BAND, axis=1)
        v_b = lax.dynamic_slice_in_dim(vp, start, A_BAND, axis=1)
        q_pos = start + jnp.arange(CHUNK, dtype=jnp.int32)
        k_pos = start - A_WIN + jnp.arange(A_BAND, dtype=jnp.int32)
        return _band_attend(q_c, k_b, v_b, q_pos, k_pos, rel_bias)

    out = lax.map(one_chunk, jnp.arange(L // CHUNK, dtype=jnp.int32))
    return jnp.moveaxis(out, 0, 1).reshape(B, L, H * Dh)


def _rotary(x, pos):
    half = x.shape[-1] // 2
    inv = ROPE_BASE ** (-jnp.arange(half, dtype=jnp.float32) / half)
    ang = pos.astype(jnp.float32)[:, None] * inv[None, :]
    cos = jnp.cos(ang)[None, :, None, :]
    sin = jnp.sin(ang)[None, :, None, :]
    x1, x2 = x[..., :half], x[..., half:]
    return jnp.concatenate([x1 * cos - x2 * sin, x1 * sin + x2 * cos], axis=-1)


def _retention_log_gamma():
    return jnp.log1p(-jnp.exp2(-5.0 - jnp.arange(B_HEADS, dtype=jnp.float32)))


def _retention_chunk(S, qkv):
    q, k, v = qkv
    L = q.shape[1]
    log_g = _retention_log_gamma()
    n = jnp.arange(L)
    diff = n[:, None] - n[None, :]
    dmat = jnp.where(diff >= 0, jnp.exp(log_g[:, None, None] * jnp.maximum(diff, 0)), 0.0)
    scores = jnp.einsum("blhd,bmhd->bhlm", q, k) * dmat[None]
    o_inner = jnp.einsum("bhlm,bmhe->blhe", scores, v)
    decay_q = jnp.exp(log_g[None, :] * (n[:, None] + 1))
    o_cross = jnp.einsum("blhd,bhde->blhe", q, S) * decay_q[None, :, :, None]
    decay_k = jnp.exp(log_g[None, :] * (L - 1 - n)[:, None])
    S_new = jnp.exp(log_g * L)[None, :, None, None] * S + jnp.einsum("blhd,blhe->bhde", k * decay_k[None, :, :, None], v)
    return S_new, o_inner + o_cross


def _retention(q, k, v, S0):
    B, L, H, _ = q.shape
    c = min(CHUNK, L)
    nc = L // c

    def to_chunks(t):
        return jnp.moveaxis(t.reshape(B, nc, c, H, t.shape[-1]), 1, 0)

    S, o = lax.scan(_retention_chunk, S0, (to_chunks(q), to_chunks(k), to_chunks(v)))
    return jnp.moveaxis(o, 0, 1).reshape(B, L, H, -1), S


def _token_shift(z, prev, mu):
    z_prev = jnp.concatenate([prev, z[:, :-1]], axis=1)
    return z + (z_prev - z) * mu


def _rwkv7_scan(r, w, k, v, kk, a, S0):
    def step(S, inp):
        r_t, w_t, k_t, v_t, kk_t, a_t = inp
        sa = jnp.einsum("bhij,bhj->bhi", S, -kk_t)
        S = S * w_t[:, :, None, :] + sa[..., None] * (kk_t * a_t)[:, :, None, :] + v_t[..., None] * k_t[:, :, None, :]
        return S, jnp.einsum("bhij,bhj->bhi", S, r_t)

    xs = tuple(jnp.moveaxis(t, 1, 0) for t in (r, w, k, v, kk, a))
    S, y = lax.scan(step, S0, xs)
    return jnp.moveaxis(y, 0, 1), S


def _rwkv7(r, k, v, w_lo, a_lo, g_lo, S0, lw):
    B, L, _ = r.shape
    f32 = jnp.float32
    w_log = -jax.nn.softplus(-(lw["c_w0"].astype(f32) + jnp.tanh(w_lo) @ lw["c_w2"].astype(f32))) - 0.5
    decay = jnp.exp(-jnp.exp(w_log))
    a = jax.nn.sigmoid(lw["c_a0"].astype(f32) + a_lo @ lw["c_a2"].astype(f32))
    g = jax.nn.sigmoid(g_lo) @ lw["c_g2"].astype(f32)

    def heads(t):
        return t.reshape(B, L, C_HEADS, C_HEAD_DIM)

    kk = heads(k * lw["c_k_k"].astype(f32))
    kk = kk / jnp.maximum(jnp.sqrt(jnp.sum(kk * kk, axis=-1, keepdims=True)), 1e-12)
    k = k * (1.0 + (a - 1.0) * lw["c_k_a"].astype(f32))
    rh, kh, vh = heads(r), heads(k), heads(v)
    y, S = _rwkv7_scan(rh, heads(decay), kh, vh, kk, heads(a), S0)
    mu = jnp.mean(y, axis=-1, keepdims=True)
    var = jnp.mean(jnp.square(y - mu), axis=-1, keepdims=True)
    yn = ((y - mu) * lax.rsqrt(var + C_GN_EPS)).reshape(B, L, C_WIDTH) * lw["c_ln_w"].astype(f32) + lw["c_ln_b"].astype(f32)
    bonus = jnp.sum(rh * kh * lw["c_r_k"].astype(f32), axis=-1, keepdims=True) * vh
    return (yn + bonus.reshape(B, L, C_WIDTH)) * g, S


def _layer(x, p_l, pos0, a_ck, a_cv, ret_s0, rwkv_s0, shift_prev, lw):
    B, L, _ = x.shape
    dt = x.dtype
    f32 = jnp.float32
    pos = pos0 + jnp.arange(L, dtype=jnp.int32)
    h = _rms_norm(x, lw["norm_mix"])
    z = h @ lw["w_in"]
    aq, ak, av, bq, bk, bv, bg, cz, gl = _split(z, IN_SIZES)

    aq = _rms_norm(aq.reshape(B, L, A_HEADS, A_HEAD_DIM), lw["a_q_norm"])
    ak = _rms_norm(ak.reshape(B, L, A_HEADS, A_HEAD_DIM), lw["a_k_norm"])
    av = av.reshape(B, L, A_HEADS, A_HEAD_DIM)
    if a_ck is None:
        oa = _band_attention_prompt(aq, ak, av, lw["a_rel_bias"])
        keep = min(A_WIN, L)
        new_ak, new_av = ak[:, L - keep:], av[:, L - keep:]
    else:
        n_c = a_ck.shape[1]
        k_all = jnp.concatenate([a_ck.astype(dt), ak], axis=1)
        v_all = jnp.concatenate([a_cv.astype(dt), av], axis=1)
        k_pos = jnp.concatenate([pos0 - n_c + jnp.arange(n_c, dtype=jnp.int32), pos])
        oa = _band_attend(aq, k_all, v_all, pos, k_pos, lw["a_rel_bias"]).reshape(B, L, A_WIDTH)
        new_ak, new_av = ak, av

    bq = _rotary(bq.reshape(B, L, B_HEADS, B_HEAD_DIM).astype(f32), pos)
    bk = _rotary(bk.reshape(B, L, B_HEADS, B_HEAD_DIM).astype(f32), pos) * (B_HEAD_DIM ** -0.5)
    bv = bv.reshape(B, L, B_HEADS, B_HEAD_DIM).astype(f32)
    ob, new_ret = _retention(bq, bk, bv, ret_s0.astype(f32))
    ob = (_rms_norm(ob).reshape(B, L, B_WIDTH) * jax.nn.silu(bg.astype(f32))).astype(dt)

    cs = _token_shift(cz, shift_prev.astype(dt), lw["c_shift_mu"])
    new_shift = cz[:, -1:]
    cr, ck, cv, cw_lo, ca_lo, cg_lo = _split(cs.astype(f32), C_SIZES)
    oc, new_rwkv = _rwkv7(cr, ck, cv, cw_lo, ca_lo, cg_lo, rwkv_s0.astype(f32), lw)
    oc = oc.astype(dt)

    gl = gl.reshape(B, L, N_BRANCHES, D_MODEL)
    wb = lw["w_branch"]
    m = (jax.nn.sigmoid(gl[:, :, 0]) * (oa @ wb[0])
         + jax.nn.sigmoid(gl[:, :, 1]) * (ob @ wb[1])
         + jax.nn.sigmoid(gl[:, :, 2]) * (oc @ wb[2]))
    x = x + m @ lw["w_out"]

    hf = _rms_norm(x, lw["norm_ffn"])
    x = x + (jax.nn.silu(hf @ lw["w_ffn_gate"]) * (hf @ lw["w_ffn_up"])) @ lw["w_ffn_down"]

    e = _rms_norm(p_l.astype(dt) @ lw["w_ple_proj"], lw["ple_norm"])
    gate = jax.nn.sigmoid(_rms_norm(x) @ lw["w_ple_gate"])
    x = x + gate * e
    return x, (new_ak, new_av, new_ret, new_rwkv, new_shift)


def setup_inputs(seed: int = 0) -> dict:
    key = jax.random.key(seed)
    keys = jax.random.split(key, 40)
    cnt = [0]
    f32 = jnp.float32

    def nk():
        cnt[0] += 1
        return keys[cnt[0] - 1]

    def nrm(shape, scale):
        return scale * jax.random.normal(nk(), shape, f32)

    def unif(shape, lo, hi):
        return jax.random.uniform(nk(), shape, f32, lo, hi)

    a_cache = min(A_WIN, PAST_LEN)
    return {
        "x_prompt": nrm((BATCH, SEQ, D_MODEL), 1.0),
        "x_sample": nrm((DEC_BATCH, DEC_SEQ, D_MODEL), 1.0),
        "p_prompt": nrm((DEPTH, BATCH, SEQ, PLE_DIM), 1.0),
        "p_sample": nrm((DEPTH, DEC_BATCH, DEC_SEQ, PLE_DIM), 1.0),
        "cache_a_k": nrm((DEPTH, DEC_BATCH, a_cache, A_HEADS, A_HEAD_DIM), 1.0),
        "cache_a_v": nrm((DEPTH, DEC_BATCH, a_cache, A_HEADS, A_HEAD_DIM), 1.0),
        "state_ret": nrm((DEPTH, DEC_BATCH, B_HEADS, B_HEAD_DIM, B_HEAD_DIM), 0.5),
        "state_rwkv": nrm((DEPTH, DEC_BATCH, C_HEADS, C_HEAD_DIM, C_HEAD_DIM), 0.5),
        "state_rwkv_shift": nrm((DEPTH, DEC_BATCH, 1, C_SHIFT_WIDTH), 1.0),
        "norm_mix": 1.0 + nrm((DEPTH, D_MODEL), 0.02),
        "w_in": nrm((DEPTH, D_MODEL, IN_WIDTH), D_MODEL ** -0.5),
        "a_q_norm": 1.0 + nrm((DEPTH, A_HEAD_DIM), 0.02),
        "a_k_norm": 1.0 + nrm((DEPTH, A_HEAD_DIM), 0.02),
        "a_rel_bias": nrm((DEPTH, A_HEADS, A_REL_SIZE), 0.1),
        "c_shift_mu": unif((DEPTH, C_SHIFT_WIDTH), 0.1, 0.9),
        "c_w0": unif((DEPTH, C_WIDTH), -6.5, -1.5),
        "c_w2": nrm((DEPTH, C_RANK_W, C_WIDTH), 0.1),
        "c_a0": nrm((DEPTH, C_WIDTH), 0.1),
        "c_a2": nrm((DEPTH, C_RANK_A, C_WIDTH), 0.1),
        "c_g2": nrm((DEPTH, C_RANK_G, C_WIDTH), C_RANK_G ** -0.5),
        "c_k_k": 0.85 + nrm((DEPTH, C_WIDTH), 0.02),
        "c_k_a": 1.0 + nrm((DEPTH, C_WIDTH), 0.02),
        "c_r_k": nrm((DEPTH, C_HEADS, C_HEAD_DIM), 0.1),
        "c_ln_w": 1.0 + nrm((DEPTH, C_WIDTH), 0.02),
        "c_ln_b": nrm((DEPTH, C_WIDTH), 0.01),
        "w_branch": nrm((DEPTH, N_BRANCHES, BRANCH_WIDTH, D_MODEL), BRANCH_WIDTH ** -0.5),
        "w_out": nrm((DEPTH, D_MODEL, D_MODEL), D_MODEL ** -0.5),
        "norm_ffn": 1.0 + nrm((DEPTH, D_MODEL), 0.02),
        "w_ffn_gate": nrm((DEPTH, D_MODEL, D_FF), D_MODEL ** -0.5),
        "w_ffn_up": nrm((DEPTH, D_MODEL, D_FF), D_MODEL ** -0.5),
        "w_ffn_down": nrm((DEPTH, D_FF, D_MODEL), D_FF ** -0.5),
        "w_ple_proj": nrm((DEPTH, PLE_DIM, D_MODEL), PLE_DIM ** -0.5),
        "ple_norm": 1.0 + nrm((DEPTH, D_MODEL), 0.02),
        "w_ple_gate": nrm((DEPTH, D_MODEL, D_MODEL), D_MODEL ** -0.5),
    }


def reference(x_prompt, x_sample, p_prompt, p_sample, cache_a_k, cache_a_v, state_ret, state_rwkv, state_rwkv_shift,
              norm_mix, w_in, a_q_norm, a_k_norm, a_rel_bias, c_shift_mu, c_w0, c_w2, c_a0, c_a2, c_g2,
              c_k_k, c_k_a, c_r_k, c_ln_w, c_ln_b, w_branch, w_out, norm_ffn, w_ffn_gate, w_ffn_up, w_ffn_down,
              w_ple_proj, ple_norm, w_ple_gate):
    def layer_weights(i):
        return dict(norm_mix=norm_mix[i], w_in=w_in[i], a_q_norm=a_q_norm[i], a_k_norm=a_k_norm[i],
                    a_rel_bias=a_rel_bias[i], c_shift_mu=c_shift_mu[i], c_w0=c_w0[i], c_w2=c_w2[i],
                    c_a0=c_a0[i], c_a2=c_a2[i], c_g2=c_g2[i], c_k_k=c_k_k[i], c_k_a=c_k_a[i], c_r_k=c_r_k[i],
                    c_ln_w=c_ln_w[i], c_ln_b=c_ln_b[i], w_branch=w_branch[i], w_out=w_out[i],
                    norm_ffn=norm_ffn[i], w_ffn_gate=w_ffn_gate[i], w_ffn_up=w_ffn_up[i],
                    w_ffn_down=w_ffn_down[i], w_ple_proj=w_ple_proj[i], ple_norm=ple_norm[i],
                    w_ple_gate=w_ple_gate[i])

    bp = x_prompt.shape[0]
    ret0 = jnp.zeros((bp, B_HEADS, B_HEAD_DIM, B_HEAD_DIM), jnp.float32)
    rwkv0 = jnp.zeros((bp, C_HEADS, C_HEAD_DIM, C_HEAD_DIM), jnp.float32)
    shift0 = jnp.zeros((bp, 1, C_SHIFT_WIDTH), x_prompt.dtype)
    y_prompt = x_prompt
    st_p = []
    for i in range(DEPTH):
        y_prompt, st = _layer(y_prompt, p_prompt[i], 0, None, None, ret0, rwkv0, shift0, layer_weights(i))
        st_p.append(st)

    y_sample = x_sample
    st_s = []
    for i in range(DEPTH):
        y_sample, st = _layer(y_sample, p_sample[i], PAST_LEN, cache_a_k[i], cache_a_v[i], state_ret[i],
                              state_rwkv[i], state_rwkv_shift[i], layer_weights(i))
        st_s.append(st)

    a_k_prompt = jnp.stack([s[0] for s in st_p])
    a_v_prompt = jnp.stack([s[1] for s in st_p])
    ret_prompt = jnp.stack([s[2] for s in st_p])
    rwkv_prompt = jnp.stack([s[3] for s in st_p])
    shift_prompt = jnp.stack([s[4] for s in st_p])
    a_k_sample = jnp.stack([s[0] for s in st_s])
    a_v_sample = jnp.stack([s[1] for s in st_s])
    ret_sample = jnp.stack([s[2] for s in st_s])
    rwkv_sample = jnp.stack([s[3] for s in st_s])
    shift_sample = jnp.stack([s[4] for s in st_s])
    return (y_prompt, y_sample, a_k_prompt, a_v_prompt, ret_prompt, rwkv_prompt, shift_prompt,
            a_k_sample, a_v_sample, ret_sample, rwkv_sample, shift_sample)
```

```python
import functools
import math

import jax
import jax.numpy as jnp
from jax import lax
from jax.experimental import pallas as pl
from jax.experimental.pallas import tpu as pltpu

F32 = jnp.float32
BF16 = jnp.bfloat16

D_MODEL = 1024
PAST_LEN = 2048
CHUNK = 64
NORM_EPS = 1e-6

A_HEADS = 8
A_HEAD_DIM = 64
A_WIDTH = 512
A_WIN = 512
A_REL_MAX = 256

B_HEADS = 4
B_HEAD_DIM = 128
B_WIDTH = 512
ROPE_BASE = 10000.0

C_HEADS = 8
C_HEAD_DIM = 64
C_WIDTH = 512
C_RANK_W = 64
C_RANK_A = 64
C_RANK_G = 128
C_SHIFT_WIDTH = 3 * C_WIDTH + C_RANK_W + C_RANK_A + C_RANK_G
C_GN_EPS = 64e-5

N_BRANCHES = 3

OFF_A = 0
OFF_B = 3 * A_WIDTH
OFF_C = OFF_B + 4 * B_WIDTH
OFF_G = OFF_C + C_SHIFT_WIDTH
IN_WIDTH = OFF_G + N_BRANCHES * D_MODEL

RET_LOG_GAMMA = tuple(math.log1p(-(2.0 ** (-5.0 - h))) for h in range(B_HEADS))

VMEM_LIMIT_BYTES = 56 * 1024 * 1024

NN = (((1,), (0,)), ((), ()))
NT = (((1,), (1,)), ((), ()))
TN = (((0,), (0,)), ((), ()))


def _dot(a, b, dims=NN):
    return lax.dot_general(a.astype(BF16), b.astype(BF16), dims, preferred_element_type=F32)


def _split_bf16(a):
    hi = a.astype(BF16)
    lo = (a - hi.astype(F32)).astype(BF16)
    return hi, lo


def _dot3(a, b, dims=NN):
    a_hi, a_lo = _split_bf16(a)
    b_hi, b_lo = _split_bf16(b)
    d = functools.partial(lax.dot_general, dimension_numbers=dims, preferred_element_type=F32)
    return d(a_hi, b_hi) + (d(a_hi, b_lo) + d(a_lo, b_hi))


def _rms(x):
    return x * lax.rsqrt(jnp.mean(x * x, axis=-1, keepdims=True) + NORM_EPS)


def _sigmoid(x):
    return 1.0 / (1.0 + jnp.exp(-x))


def _const_spec(shape):
    nd = len(shape)
    return pl.BlockSpec(shape, lambda *_: (0,) * nd, pipeline_mode=pl.Buffered(1))


def _attn_kernel(*refs, tq, cq, has_cache):
    if has_cache:
        (x_ref, g_ref, w_ref, qg_ref, kg_ref, bias_ref, kc_ref, vc_ref,
         o_ref, kn_ref, v_ref, kwin, vwin, qs) = refs
    else:
        (x_ref, g_ref, w_ref, qg_ref, kg_ref, bias_ref,
         o_ref, kn_ref, v_ref, kwin, vwin, qs) = refs
    i = pl.program_id(1)
    nkeys = A_WIN + cq

    if has_cache:
        kwin[0:A_WIN, :] = kc_ref[0]
        vwin[0:A_WIN, :] = vc_ref[0]
    else:
        @pl.when(i == 0)
        def _():
            kwin[0:A_WIN, :] = jnp.zeros((A_WIN, A_WIDTH), F32)
            vwin[0:A_WIN, :] = jnp.zeros((A_WIN, A_WIDTH), F32)

    h = _rms(x_ref[0]) * g_ref[...]
    z = _dot(h, w_ref[...])
    qn, kn = [], []
    for hh in range(A_HEADS):
        lo = hh * A_HEAD_DIM
        qn.append(_rms(z[:, lo:lo + A_HEAD_DIM]) * qg_ref[...])
        kn.append(_rms(z[:, A_WIDTH + lo:A_WIDTH + lo + A_HEAD_DIM]) * kg_ref[...])
    qn = jnp.concatenate(qn, axis=-1)
    kn = jnp.concatenate(kn, axis=-1)
    v = z[:, 2 * A_WIDTH:3 * A_WIDTH]
    qs[...] = qn * (A_HEAD_DIM ** -0.5)
    kwin[A_WIN:A_WIN + tq, :] = kn
    vwin[A_WIN:A_WIN + tq, :] = v
    kn_ref[0] = kn
    v_ref[0] = v

    def chunk(j, carry):
        r0 = pl.multiple_of(j * cq, cq)
        if not has_cache:
            kidx = r0 + lax.broadcasted_iota(jnp.int32, (cq, nkeys), 1)
            ok = jnp.logical_or(kidx >= A_WIN, i > 0)
        outs = []
        for hh in range(A_HEADS):
            lo = hh * A_HEAD_DIM
            qh = qs[pl.ds(r0, cq), lo:lo + A_HEAD_DIM]
            kh = kwin[pl.ds(r0, nkeys), lo:lo + A_HEAD_DIM]
            vh = vwin[pl.ds(r0, nkeys), lo:lo + A_HEAD_DIM]
            s = _dot(qh, kh, NT) + bias_ref[hh]
            if not has_cache:
                s = jnp.where(ok, s, -1e30)
            m = jnp.max(s, axis=-1, keepdims=True)
            p = jnp.exp(s - m)
            den = jnp.sum(p, axis=-1, keepdims=True)
            outs.append(_dot(p, vh) / den)
        o_ref[0, pl.ds(r0, cq), :] = jnp.concatenate(outs, axis=-1)
        return carry

    lax.fori_loop(0, tq // cq, chunk, 0)

    if not has_cache:
        kwin[0:A_WIN, :] = kwin[tq:tq + A_WIN, :]
        vwin[0:A_WIN, :] = vwin[tq:tq + A_WIN, :]


def _attention(x, gain, w_a, q_gain, k_gain, bias, cache_k, cache_v, *, tq, cq):
    bsz, seq, _ = x.shape
    has_cache = cache_k is not None
    assert seq % tq == 0 and tq % cq == 0
    assert has_cache or tq == A_WIN
    nkeys = A_WIN + cq
    tok = lambda w: pl.BlockSpec((1, tq, w), lambda b, i: (b, i, 0))
    in_specs = [tok(D_MODEL), _const_spec((1, D_MODEL)), _const_spec((D_MODEL, 3 * A_WIDTH)),
                _const_spec((1, A_HEAD_DIM)), _const_spec((1, A_HEAD_DIM)),
                _const_spec((A_HEADS, cq, nkeys))]
    args = [x, gain, w_a, q_gain, k_gain, bias]
    if has_cache:
        cspec = pl.BlockSpec((1, A_WIN, A_WIDTH), lambda b, i: (b, 0, 0))
        in_specs += [cspec, cspec]
        args += [cache_k, cache_v]
    out = jax.ShapeDtypeStruct((bsz, seq, A_WIDTH), F32)
    return pl.pallas_call(
        functools.partial(_attn_kernel, tq=tq, cq=cq, has_cache=has_cache),
        grid=(bsz, seq // tq),
        in_specs=in_specs,
        out_specs=[tok(A_WIDTH)] * 3,
        out_shape=[out] * 3,
        scratch_shapes=[pltpu.VMEM((A_WIN + tq, A_WIDTH), F32),
                        pltpu.VMEM((A_WIN + tq, A_WIDTH), F32),
                        pltpu.VMEM((tq, A_WIDTH), F32)],
        compiler_params=pltpu.CompilerParams(
            dimension_semantics=("arbitrary", "arbitrary"),
            vmem_limit_bytes=VMEM_LIMIT_BYTES),
        name="attention",
    )(*args)


def _ret_kernel(x_ref, g_ref, w_ref, cos_ref, sin_ref, s0_ref, o_ref, sout_ref, s_scr, *, tb):
    i = pl.program_id(1)

    @pl.when(i == 0)
    def _():
        s_scr[...] = s0_ref[0]

    h = _rms(x_ref[0]) * g_ref[...]
    z = _dot(h, w_ref[...])
    cosf = cos_ref[...]
    sinf = sin_ref[...]
    row = lax.broadcasted_iota(jnp.int32, (tb, tb), 0)
    col = lax.broadcasted_iota(jnp.int32, (tb, tb), 1)
    diff = row - col
    causal = diff >= 0
    dist = jnp.maximum(diff, 0).astype(F32)
    n = lax.broadcasted_iota(jnp.int32, (tb, 1), 0).astype(F32)
    outs = []
    for hh in range(B_HEADS):
        lg = RET_LOG_GAMMA[hh]
        lo = hh * B_HEAD_DIM
        q = z[:, lo:lo + B_HEAD_DIM]
        k = z[:, B_WIDTH + lo:B_WIDTH + lo + B_HEAD_DIM]
        v = z[:, 2 * B_WIDTH + lo:2 * B_WIDTH + lo + B_HEAD_DIM]
        gate = z[:, 3 * B_WIDTH + lo:3 * B_WIDTH + lo + B_HEAD_DIM]
        q = q * cosf + pltpu.roll(q, B_HEAD_DIM // 2, 1) * sinf
        k = (k * cosf + pltpu.roll(k, B_HEAD_DIM // 2, 1) * sinf) * (B_HEAD_DIM ** -0.5)
        dmat = jnp.where(causal, jnp.exp(lg * dist), 0.0)
        scores = _dot(q, k, NT) * dmat
        s_old = s_scr[hh]
        o = _dot(scores, v) + _dot(q, s_old) * jnp.exp(lg * (n + 1.0))
        k_dec = k * jnp.exp(lg * ((tb - 1.0) - n))
        s_scr[hh] = math.exp(lg * tb) * s_old + _dot(k_dec, v, TN)
        outs.append(_rms(o) * (gate * _sigmoid(gate)))
    o_ref[0] = jnp.concatenate(outs, axis=-1)
    sout_ref[0] = s_scr[...]


def _retention(x, gain, w_b, cosf, sinf, s0, *, tb):
    bsz, seq, _ = x.shape
    assert seq % tb == 0
    sspec = pl.BlockSpec((1, B_HEADS, B_HEAD_DIM, B_HEAD_DIM), lambda b, i: (b, 0, 0, 0))
    return pl.pallas_call(
        functools.partial(_ret_kernel, tb=tb),
        grid=(bsz, seq // tb),
        in_specs=[pl.BlockSpec((1, tb, D_MODEL), lambda b, i: (b, i, 0)),
                  _const_spec((1, D_MODEL)), _const_spec((D_MODEL, 4 * B_WIDTH)),
                  pl.BlockSpec((tb, B_HEAD_DIM), lambda b, i: (i, 0)),
                  pl.BlockSpec((tb, B_HEAD_DIM), lambda b, i: (i, 0)),
                  sspec],
        out_specs=[pl.BlockSpec((1, tb, B_WIDTH), lambda b, i: (b, i, 0)), sspec],
        out_shape=[jax.ShapeDtypeStruct((bsz, seq, B_WIDTH), F32),
                   jax.ShapeDtypeStruct((bsz, B_HEADS, B_HEAD_DIM, B_HEAD_DIM), F32)],
        scratch_shapes=[pltpu.VMEM((B_HEADS, B_HEAD_DIM, B_HEAD_DIM), F32)],
        compiler_params=pltpu.CompilerParams(
            dimension_semantics=("arbitrary", "arbitrary"),
            vmem_limit_bytes=VMEM_LIMIT_BYTES),
        name="retention",
    )(x, gain, w_b, cosf, sinf, s0)


def _rwkv_kernel(x_ref, g_ref, w_ref, mu_ref, sh0_ref, s0_ref, w0_ref, w2_ref, a0_ref, a2_ref,
                 g2_ref, kk_ref, ka_ref, rk_ref, lnw_ref, lnb_ref,
                 o_ref, sout_ref, shout_ref,
                 s_scr, carry, rt_s, kkt_s, kh_s, bh_s, v_s, rk_s, g_s, cum_s, *, tt, cc):
    i = pl.program_id(1)

    @pl.when(i == 0)
    def _():
        s_scr[...] = s0_ref[0]
        carry[...] = sh0_ref[0]

    h = _rms(x_ref[0]) * g_ref[...]
    cz = _dot(h, w_ref[...])
    rows = lax.broadcasted_iota(jnp.int32, (tt, 1), 0)
    prev = jnp.where(rows == 0, carry[...], pltpu.roll(cz, 1, 0))
    cs = cz + (prev - cz) * mu_ref[...]
    last = cz[tt - 1:tt, :]
    carry[...] = last
    shout_ref[0] = last

    r = cs[:, 0:C_WIDTH]
    k = cs[:, C_WIDTH:2 * C_WIDTH]
    v = cs[:, 2 * C_WIDTH:3 * C_WIDTH]
    off = 3 * C_WIDTH
    w_lo = cs[:, off:off + C_RANK_W]
    a_lo = cs[:, off + C_RANK_W:off + C_RANK_W + C_RANK_A]
    g_lo = cs[:, off + C_RANK_W + C_RANK_A:C_SHIFT_WIDTH]

    u = -(w0_ref[...] + _dot(jnp.tanh(w_lo), w2_ref[...]))
    softplus = jnp.maximum(u, 0.0) + jnp.log1p(jnp.exp(-jnp.abs(u)))
    lw = -jnp.exp(-softplus - 0.5)
    a = _sigmoid(a0_ref[...] + _dot(a_lo, a2_ref[...]))
    g = _dot(_sigmoid(g_lo), g2_ref[...])
    kk_raw = k * kk_ref[...]
    kk = []
    for hh in range(C_HEADS):
        lo = hh * C_HEAD_DIM
        kh_ = kk_raw[:, lo:lo + C_HEAD_DIM]
        nrm = jnp.sqrt(jnp.sum(kh_ * kh_, axis=-1, keepdims=True))
        kk.append(kh_ / jnp.maximum(nrm, 1e-12))
    kk = jnp.concatenate(kk, axis=-1)
    k2 = k * (1.0 + (a - 1.0) * ka_ref[...])

    pos = jnp.bitwise_and(rows, cc - 1)
    cum = lw
    step = 1
    while step < cc:
        cum = cum + jnp.where(pos >= step, pltpu.roll(cum, step, 0), 0.0)
        step *= 2
    e_inv = jnp.exp(-cum)
    rt_s[...] = r * jnp.exp(cum)
    kkt_s[...] = kk * jnp.exp(cum - lw)
    kh_s[...] = k2 * e_inv
    bh_s[...] = kk * a * e_inv
    v_s[...] = v
    rk_s[...] = r * k2 * rk_ref[...]
    g_s[...] = g
    cum_s[...] = cum

    ri = lax.broadcasted_iota(jnp.int32, (cc, cc), 0)
    ci = lax.broadcasted_iota(jnp.int32, (cc, cc), 1)
    strict = ri > ci
    incl = ri >= ci
    eye = jnp.where(ri == ci, 1.0, 0.0).astype(F32)
    n_double = cc.bit_length() - 2

    def chunk(j, c_):
        r0 = pl.multiple_of(j * cc, cc)
        w_end = jnp.exp(cum_s[pl.ds(r0 + cc - 1, 1), :])
        outs = []
        for hh in range(C_HEADS):
            lo = hh * C_HEAD_DIM
            sl = (pl.ds(r0, cc), slice(lo, lo + C_HEAD_DIM))
            rt, kkt, kh, bh, vh = rt_s[sl], kkt_s[sl], kh_s[sl], bh_s[sl], v_s[sl]
            we = w_end[:, lo:lo + C_HEAD_DIM]
            s_old = s_scr[hh]
            a_kk = jnp.where(strict, _dot(kkt, kh, NT), 0.0)
            a_bb = jnp.where(strict, _dot(kkt, bh, NT), 0.0)
            b_kk = jnp.where(incl, _dot(rt, kh, NT), 0.0)
            b_bb = jnp.where(incl, _dot(rt, bh, NT), 0.0)
            pw = -a_bb
            tinv = eye + pw
            for _ in range(n_double):
                pw = _dot3(pw, pw)
                tinv = tinv + _dot3(pw, tinv)
            rhs = _dot(kkt, s_old, NT) + _dot(a_kk, vh)
            uu = _dot3(tinv, rhs)
            y = _dot(rt, s_old, NT) + _dot(b_kk, vh) - _dot(b_bb, uu)
            s_scr[hh] = s_old * we + _dot3(vh, kh * we, TN) - _dot3(uu, bh * we, TN)
            mean = jnp.mean(y, axis=-1, keepdims=True)
            yc = y - mean
            var = jnp.mean(yc * yc, axis=-1, keepdims=True)
            yn = (yc * lax.rsqrt(var + C_GN_EPS) * lnw_ref[:, lo:lo + C_HEAD_DIM]
                  + lnb_ref[:, lo:lo + C_HEAD_DIM])
            bonus = jnp.sum(rk_s[sl], axis=-1, keepdims=True) * vh
            outs.append(yn + bonus)
        o_ref[0, pl.ds(r0, cc), :] = jnp.concatenate(outs, axis=-1) * g_s[pl.ds(r0, cc), :]
        return c_

    lax.fori_loop(0, tt // cc, chunk, 0)
    sout_ref[0] = s_scr[...]


def _rwkv(x, gain, w_c, lw, shift0, s0, *, tt, cc):
    bsz, seq, _ = x.shape
    assert seq % tt == 0 and tt % cc == 0 and cc & (cc - 1) == 0
    row = lambda a: a.reshape(1, -1)
    sspec = pl.BlockSpec((1, C_HEADS, C_HEAD_DIM, C_HEAD_DIM), lambda b, i: (b, 0, 0, 0))
    shspec = pl.BlockSpec((1, 1, C_SHIFT_WIDTH), lambda b, i: (b, 0, 0))
    vec = _const_spec((1, C_WIDTH))
    tile = pltpu.VMEM((tt, C_WIDTH), F32)
    return pl.pallas_call(
        functools.partial(_rwkv_kernel, tt=tt, cc=cc),
        grid=(bsz, seq // tt),
        in_specs=[pl.BlockSpec((1, tt, D_MODEL), lambda b, i: (b, i, 0)),
                  _const_spec((1, D_MODEL)), _const_spec((D_MODEL, C_SHIFT_WIDTH)),
                  _const_spec((1, C_SHIFT_WIDTH)), shspec, sspec,
                  vec, _const_spec((C_RANK_W, C_WIDTH)), vec, _const_spec((C_RANK_A, C_WIDTH)),
                  _const_spec((C_RANK_G, C_WIDTH)), vec, vec, vec, vec, vec],
        out_specs=[pl.BlockSpec((1, tt, C_WIDTH), lambda b, i: (b, i, 0)), sspec, shspec],
        out_shape=[jax.ShapeDtypeStruct((bsz, seq, C_WIDTH), F32),
                   jax.ShapeDtypeStruct((bsz, C_HEADS, C_HEAD_DIM, C_HEAD_DIM), F32),
                   jax.ShapeDtypeStruct((bsz, 1, C_SHIFT_WIDTH), F32)],
        scratch_shapes=[pltpu.VMEM((C_HEADS, C_HEAD_DIM, C_HEAD_DIM), F32),
                        pltpu.VMEM((1, C_SHIFT_WIDTH), F32)] + [tile] * 8,
        compiler_params=pltpu.CompilerParams(
            dimension_semantics=("arbitrary", "arbitrary"),
            vmem_limit_bytes=VMEM_LIMIT_BYTES),
        name="rwkv",
    )(x, gain, w_c, row(lw["c_shift_mu"]), shift0, s0,
      row(lw["c_w0"]), lw["c_w2"].astype(BF16), row(lw["c_a0"]), lw["c_a2"].astype(BF16),
      lw["c_g2"].astype(BF16), row(lw["c_k_k"]), row(lw["c_k_a"]), row(lw["c_r_k"]),
      row(lw["c_ln_w"]), row(lw["c_ln_b"]))


def _merge_kernel(x_ref, oa_ref, ob_ref, oc_ref, g_ref, wg_ref, wb_ref, wo_ref, y_ref):
    x = x_ref[...]
    h = _rms(x) * g_ref[...]
    gl = _dot(h, wg_ref[...])
    m = None
    for b, o_ref in enumerate((oa_ref, ob_ref, oc_ref)):
        t = _sigmoid(gl[:, b * D_MODEL:(b + 1) * D_MODEL]) * _dot(o_ref[...], wb_ref[b])
        m = t if m is None else m + t
    y_ref[...] = x + _dot(m, wo_ref[...])


def _merge(x, oa, ob, oc, gain, w_g, w_b, w_o, *, tm):
    rows = x.shape[0]
    assert rows % tm == 0
    tok = lambda w: pl.BlockSpec((tm, w), lambda i: (i, 0))
    return pl.pallas_call(
        _merge_kernel,
        grid=(rows // tm,),
        in_specs=[tok(D_MODEL), tok(A_WIDTH), tok(B_WIDTH), tok(C_WIDTH),
                  _const_spec((1, D_MODEL)), _const_spec((D_MODEL, N_BRANCHES * D_MODEL)),
                  _const_spec((N_BRANCHES, A_WIDTH, D_MODEL)), _const_spec((D_MODEL, D_MODEL))],
        out_specs=tok(D_MODEL),
        out_shape=jax.ShapeDtypeStruct((rows, D_MODEL), F32),
        compiler_params=pltpu.CompilerParams(
            dimension_semantics=("arbitrary",), vmem_limit_bytes=VMEM_LIMIT_BYTES),
        name="merge",
    )(x, oa, ob, oc, gain, w_g, w_b, w_o)


def _ffn_kernel(x_ref, p_ref, g_ref, wgate_ref, wup_ref, wdown_ref, wpp_ref, pg_ref, wpg_ref, y_ref):
    x = x_ref[...]
    hf = (_rms(x) * g_ref[...]).astype(BF16)
    gate = _dot(hf, wgate_ref[...])
    up = _dot(hf, wup_ref[...])
    x = x + _dot(gate * _sigmoid(gate) * up, wdown_ref[...])
    e = _rms(_dot(p_ref[...], wpp_ref[...])) * pg_ref[...]
    y_ref[...] = x + _sigmoid(_dot(_rms(x), wpg_ref[...])) * e


def _ffn(x, p, gain, w_gate, w_up, w_down, w_pp, p_gain, w_pg, *, tm):
    rows = x.shape[0]
    d_ff = w_gate.shape[1]
    ple = p.shape[1]
    assert rows % tm == 0
    tok = lambda w: pl.BlockSpec((tm, w), lambda i: (i, 0))
    return pl.pallas_call(
        _ffn_kernel,
        grid=(rows // tm,),
        in_specs=[tok(D_MODEL), tok(ple), _const_spec((1, D_MODEL)),
                  _const_spec((D_MODEL, d_ff)), _const_spec((D_MODEL, d_ff)),
                  _const_spec((d_ff, D_MODEL)), _const_spec((ple, D_MODEL)),
                  _const_spec((1, D_MODEL)), _const_spec((D_MODEL, D_MODEL))],
        out_specs=tok(D_MODEL),
        out_shape=jax.ShapeDtypeStruct((rows, D_MODEL), F32),
        compiler_params=pltpu.CompilerParams(
            dimension_semantics=("arbitrary",), vmem_limit_bytes=VMEM_LIMIT_BYTES),
        name="ffn",
    )(x, p, gain, w_gate, w_up, w_down, w_pp, p_gain, w_pg)


def _rel_bias_table(rel_bias, cq):
    qi = jnp.arange(cq, dtype=jnp.int32)[:, None]
    kr = jnp.arange(A_WIN + cq, dtype=jnp.int32)[None, :]
    idx = jnp.clip(A_WIN + qi - kr, -(CHUNK - 1), A_REL_MAX) + (CHUNK - 1)
    return rel_bias.astype(F32)[:, idx]


def _rotary_tables(pos0, seq):
    half = B_HEAD_DIM // 2
    pos = pos0 + jnp.arange(seq, dtype=jnp.int32)
    inv = ROPE_BASE ** (-jnp.arange(half, dtype=F32) / half)
    ang = pos.astype(F32)[:, None] * inv[None, :]
    cos, sin = jnp.cos(ang), jnp.sin(ang)
    return jnp.concatenate([cos, cos], axis=-1), jnp.concatenate([-sin, sin], axis=-1)


def _layer(x, p_l, pos0, a_ck, a_cv, ret_s0, rwkv_s0, shift_prev, lw, cfg):
    bsz, seq, _ = x.shape
    row = lambda a: a.reshape(1, -1)
    w_in = lw["w_in"].astype(BF16)
    w_a = w_in[:, OFF_A:OFF_B]
    w_b = w_in[:, OFF_B:OFF_C]
    w_c = w_in[:, OFF_C:OFF_G]
    w_g = w_in[:, OFF_G:]
    gain = row(lw["norm_mix"])

    bias = _rel_bias_table(lw["a_rel_bias"], cfg["cq"])
    if a_ck is not None:
        a_ck = a_ck.reshape(bsz, A_WIN, A_WIDTH)
        a_cv = a_cv.reshape(bsz, A_WIN, A_WIDTH)
    oa, kn, av = _attention(x, gain, w_a, row(lw["a_q_norm"]), row(lw["a_k_norm"]), bias,
                            a_ck, a_cv, tq=cfg["tq"], cq=cfg["cq"])
    keep = min(A_WIN, seq)
    new_ak = kn[:, seq - keep:].reshape(bsz, keep, A_HEADS, A_HEAD_DIM)
    new_av = av[:, seq - keep:].reshape(bsz, keep, A_HEADS, A_HEAD_DIM)

    cosf, sinf = _rotary_tables(pos0, seq)
    ob, new_ret = _retention(x, gain, w_b, cosf, sinf, ret_s0, tb=cfg["tb"])

    oc, new_rwkv, new_shift = _rwkv(x, gain, w_c, lw, shift_prev, rwkv_s0, tt=cfg["tt"], cc=cfg["cc"])

    rows = bsz * seq
    tm = min(cfg["tm"], rows)
    flat = lambda t: t.reshape(rows, t.shape[-1])
    x1 = _merge(flat(x), flat(oa), flat(ob), flat(oc), gain, w_g,
                lw["w_branch"].astype(BF16), lw["w_out"].astype(BF16), tm=tm)
    x2 = _ffn(x1, flat(p_l), row(lw["norm_ffn"]), lw["w_ffn_gate"].astype(BF16),
              lw["w_ffn_up"].astype(BF16), lw["w_ffn_down"].astype(BF16),
              lw["w_ple_proj"].astype(BF16), row(lw["ple_norm"]), lw["w_ple_gate"].astype(BF16),
              tm=tm)
    return x2.reshape(bsz, seq, D_MODEL), (new_ak, new_av, new_ret, new_rwkv, new_shift)


def _group_config(seq):
    if seq >= A_WIN:
        return dict(tq=A_WIN, cq=CHUNK, tb=256, tt=256, cc=CHUNK, tm=256)
    return dict(tq=seq, cq=seq, tb=seq, tt=seq, cc=seq, tm=256)


def kernel(x_prompt, x_sample, p_prompt, p_sample, cache_a_k, cache_a_v, state_ret, state_rwkv, state_rwkv_shift, norm_mix, w_in, a_q_norm, a_k_norm, a_rel_bias, c_shift_mu, c_w0, c_w2, c_a0, c_a2, c_g2, c_k_k, c_k_a, c_r_k, c_ln_w, c_ln_b, w_branch, w_out, norm_ffn, w_ffn_gate, w_ffn_up, w_ffn_down, w_ple_proj, ple_norm, w_ple_gate):
    depth = w_in.shape[0]

    def layer_weights(i):
        return dict(norm_mix=norm_mix[i], w_in=w_in[i], a_q_norm=a_q_norm[i], a_k_norm=a_k_norm[i],
                    a_rel_bias=a_rel_bias[i], c_shift_mu=c_shift_mu[i], c_w0=c_w0[i], c_w2=c_w2[i],
                    c_a0=c_a0[i], c_a2=c_a2[i], c_g2=c_g2[i], c_k_k=c_k_k[i], c_k_a=c_k_a[i], c_r_k=c_r_k[i],
                    c_ln_w=c_ln_w[i], c_ln_b=c_ln_b[i], w_branch=w_branch[i], w_out=w_out[i],
                    norm_ffn=norm_ffn[i], w_ffn_gate=w_ffn_gate[i], w_ffn_up=w_ffn_up[i],
                    w_ffn_down=w_ffn_down[i], w_ple_proj=w_ple_proj[i], ple_norm=ple_norm[i],
                    w_ple_gate=w_ple_gate[i])

    bp, lp, _ = x_prompt.shape
    cfg_p = _group_config(lp)
    ret0 = jnp.zeros((bp, B_HEADS, B_HEAD_DIM, B_HEAD_DIM), F32)
    rwkv0 = jnp.zeros((bp, C_HEADS, C_HEAD_DIM, C_HEAD_DIM), F32)
    shift0 = jnp.zeros((bp, 1, C_SHIFT_WIDTH), F32)
    y_prompt = x_prompt
    st_p = []
    for i in range(depth):
        y_prompt, st = _layer(y_prompt, p_prompt[i], 0, None, None, ret0, rwkv0, shift0,
                              layer_weights(i), cfg_p)
        st_p.append(st)

    cfg_s = _group_config(x_sample.shape[1])
    y_sample = x_sample
    st_s = []
    for i in range(depth):
        y_sample, st = _layer(y_sample, p_sample[i], PAST_LEN, cache_a_k[i], cache_a_v[i], state_ret[i],
                              state_rwkv[i], state_rwkv_shift[i], layer_weights(i), cfg_s)
        st_s.append(st)

    stack = lambda sts, j: jnp.stack([s[j] for s in sts])
    return (y_prompt, y_sample,
            stack(st_p, 0), stack(st_p, 1), stack(st_p, 2), stack(st_p, 3), stack(st_p, 4),
            stack(st_s, 0), stack(st_s, 1), stack(st_s, 2), stack(st_s, 3), stack(st_s, 4))
```

```python
import functools
import math

import jax
import jax.numpy as jnp
from jax import lax
from jax.experimental import pallas as pl
from jax.experimental.pallas import tpu as pltpu

F32 = jnp.float32
BF16 = jnp.bfloat16

D_MODEL = 1024
PAST_LEN = 2048
CHUNK = 64
NORM_EPS = 1e-6

A_HEADS = 8
A_HEAD_DIM = 64
A_WIDTH = 512
A_WIN = 512
A_REL_MAX = 256

B_HEADS = 4
B_HEAD_DIM = 128
B_WIDTH = 512
ROPE_BASE = 10000.0

C_HEADS = 8
C_HEAD_DIM = 64
C_WIDTH = 512
C_RANK_W = 64
C_RANK_A = 64
C_RANK_G = 128
C_SHIFT_WIDTH = 3 * C_WIDTH + C_RANK_W + C_RANK_A + C_RANK_G
C_GN_EPS = 64e-5

N_BRANCHES = 3

OFF_A = 0
OFF_B = 3 * A_WIDTH
OFF_C = OFF_B + 4 * B_WIDTH
OFF_G = OFF_C + C_SHIFT_WIDTH
IN_WIDTH = OFF_G + N_BRANCHES * D_MODEL

RET_LOG_GAMMA = tuple(math.log1p(-(2.0 ** (-5.0 - h))) for h in range(B_HEADS))

VMEM_LIMIT_BYTES = 56 * 1024 * 1024

NN = (((1,), (0,)), ((), ()))
NT = (((1,), (1,)), ((), ()))
TN = (((0,), (0,)), ((), ()))


def _dot(a, b, dims=NN):
    return lax.dot_general(a.astype(BF16), b.astype(BF16), dims, preferred_element_type=F32)


def _split_bf16(a):
    hi = a.astype(BF16)
    lo = (a - hi.astype(F32)).astype(BF16)
    return hi, lo


def _dot3(a, b, dims=NN):
    a_hi, a_lo = _split_bf16(a)
    b_hi, b_lo = _split_bf16(b)
    d = functools.partial(lax.dot_general, dimension_numbers=dims, preferred_element_type=F32)
    return d(a_hi, b_hi) + (d(a_hi, b_lo) + d(a_lo, b_hi))


def _rms(x):
    return x * lax.rsqrt(jnp.mean(x * x, axis=-1, keepdims=True) + NORM_EPS)


def _sigmoid(x):
    return 1.0 / (1.0 + jnp.exp(-x))


def _const_spec(shape):
    nd = len(shape)
    return pl.BlockSpec(shape, lambda *_: (0,) * nd, pipeline_mode=pl.Buffered(1))


def _head_sum_matrix(width, head_dim):
    head = jnp.arange(width, dtype=jnp.int32) // head_dim
    return (head[:, None] == head[None, :]).astype(BF16)


def _attn_kernel(*refs, tq, cq, has_cache):
    if has_cache:
        (x_ref, g_ref, w_ref, qg_ref, kg_ref, bias_ref, hsum_ref, kc_ref, vc_ref,
         o_ref, kn_ref, v_ref, kwin, vwin, qs) = refs
    else:
        (x_ref, g_ref, w_ref, qg_ref, kg_ref, bias_ref, hsum_ref,
         o_ref, kn_ref, v_ref, kwin, vwin, qs) = refs
    i = pl.program_id(1)
    nkeys = A_WIN + cq
    pair_w = 2 * A_HEAD_DIM

    if has_cache:
        kwin[0:A_WIN, :] = kc_ref[0].astype(BF16)
        vwin[0:A_WIN, :] = vc_ref[0].astype(BF16)
    else:
        @pl.when(i == 0)
        def _():
            kwin[0:A_WIN, :] = jnp.zeros((A_WIN, A_WIDTH), BF16)
            vwin[0:A_WIN, :] = jnp.zeros((A_WIN, A_WIDTH), BF16)

    h = _rms(x_ref[0]) * g_ref[...]
    z = _dot(h, w_ref[...])
    q = z[:, 0:A_WIDTH]
    k = z[:, A_WIDTH:2 * A_WIDTH]
    v = z[:, 2 * A_WIDTH:3 * A_WIDTH]
    hsum = hsum_ref[...]
    inv_d = 1.0 / A_HEAD_DIM
    qn = q * lax.rsqrt(_dot(q * q, hsum) * inv_d + NORM_EPS) * qg_ref[...]
    kn = k * lax.rsqrt(_dot(k * k, hsum) * inv_d + NORM_EPS) * kg_ref[...]
    qs[...] = qn * (A_HEAD_DIM ** -0.5)
    kwin[A_WIN:A_WIN + tq, :] = kn.astype(BF16)
    vwin[A_WIN:A_WIN + tq, :] = v.astype(BF16)
    kn_ref[0] = kn
    v_ref[0] = v

    rho = lax.broadcasted_iota(jnp.int32, (2 * cq, pair_w), 0)
    lane = lax.broadcasted_iota(jnp.int32, (2 * cq, pair_w), 1)
    placed = (rho >> (cq.bit_length() - 1)) == (lane >> (A_HEAD_DIM.bit_length() - 1))

    def chunk(j, carry):
        r0 = pl.multiple_of(j * cq, cq)
        if not has_cache:
            kidx = r0 + lax.broadcasted_iota(jnp.int32, (2 * cq, nkeys), 1)
            ok = jnp.logical_or(kidx >= A_WIN, i > 0)
        for p in range(A_HEADS // 2):
            lanes = slice(p * pair_w, (p + 1) * pair_w)
            qc = qs[pl.ds(r0, cq), lanes]
            qp = jnp.where(placed, jnp.concatenate([qc, qc], axis=0), 0.0)
            s = _dot(qp, kwin[pl.ds(r0, nkeys), lanes], NT) + bias_ref[p]
            if not has_cache:
                s = jnp.where(ok, s, -1e30)
            m = jnp.max(s, axis=-1, keepdims=True)
            e = jnp.exp(s - m)
            den = jnp.sum(e, axis=-1, keepdims=True)
            o2 = jnp.where(placed, _dot(e, vwin[pl.ds(r0, nkeys), lanes]) / den, 0.0)
            o_ref[0, pl.ds(r0, cq), lanes] = o2[:cq] + o2[cq:]
        return carry

    lax.fori_loop(0, tq // cq, chunk, 0)

    if not has_cache:
        kwin[0:A_WIN, :] = kwin[tq:tq + A_WIN, :]
        vwin[0:A_WIN, :] = vwin[tq:tq + A_WIN, :]


def _attention(x, gain, w_a, q_gain, k_gain, bias, cache_k, cache_v, *, tq, cq):
    bsz, seq, _ = x.shape
    has_cache = cache_k is not None
    assert seq % tq == 0 and tq % cq == 0
    assert has_cache or tq == A_WIN
    nkeys = A_WIN + cq
    tok = lambda w: pl.BlockSpec((1, tq, w), lambda b, i: (b, i, 0))
    in_specs = [tok(D_MODEL), _const_spec((1, D_MODEL)), _const_spec((D_MODEL, 3 * A_WIDTH)),
                _const_spec((1, A_WIDTH)), _const_spec((1, A_WIDTH)),
                _const_spec((A_HEADS // 2, 2 * cq, nkeys)), _const_spec((A_WIDTH, A_WIDTH))]
    args = [x, gain, w_a, jnp.tile(q_gain, (1, A_HEADS)), jnp.tile(k_gain, (1, A_HEADS)),
            bias.reshape(A_HEADS // 2, 2 * cq, nkeys), _head_sum_matrix(A_WIDTH, A_HEAD_DIM)]
    if has_cache:
        cspec = pl.BlockSpec((1, A_WIN, A_WIDTH), lambda b, i: (b, 0, 0))
        in_specs += [cspec, cspec]
        args += [cache_k, cache_v]
    out = jax.ShapeDtypeStruct((bsz, seq, A_WIDTH), F32)
    return pl.pallas_call(
        functools.partial(_attn_kernel, tq=tq, cq=cq, has_cache=has_cache),
        grid=(bsz, seq // tq),
        in_specs=in_specs,
        out_specs=[tok(A_WIDTH)] * 3,
        out_shape=[out] * 3,
        scratch_shapes=[pltpu.VMEM((A_WIN + tq, A_WIDTH), BF16),
                        pltpu.VMEM((A_WIN + tq, A_WIDTH), BF16),
                        pltpu.VMEM((tq, A_WIDTH), F32)],
        compiler_params=pltpu.CompilerParams(
            dimension_semantics=("arbitrary", "arbitrary"),
            vmem_limit_bytes=VMEM_LIMIT_BYTES),
        name="attention",
    )(*args)


def _ret_kernel(x_ref, g_ref, w_ref, cos_ref, sin_ref, s0_ref, o_ref, sout_ref, s_scr, *, tb):
    i = pl.program_id(1)

    @pl.when(i == 0)
    def _():
        s_scr[...] = s0_ref[0]

    h = _rms(x_ref[0]) * g_ref[...]
    z = _dot(h, w_ref[...])
    cosf = cos_ref[...]
    sinf = sin_ref[...]
    row = lax.broadcasted_iota(jnp.int32, (tb, tb), 0)
    col = lax.broadcasted_iota(jnp.int32, (tb, tb), 1)
    diff = row - col
    causal = diff >= 0
    dist = jnp.maximum(diff, 0).astype(F32)
    n = lax.broadcasted_iota(jnp.int32, (tb, 1), 0).astype(F32)
    outs = []
    for hh in range(B_HEADS):
        lg = RET_LOG_GAMMA[hh]
        lo = hh * B_HEAD_DIM
        q = z[:, lo:lo + B_HEAD_DIM]
        k = z[:, B_WIDTH + lo:B_WIDTH + lo + B_HEAD_DIM]
        v = z[:, 2 * B_WIDTH + lo:2 * B_WIDTH + lo + B_HEAD_DIM]
        gate = z[:, 3 * B_WIDTH + lo:3 * B_WIDTH + lo + B_HEAD_DIM]
        q = q * cosf + pltpu.roll(q, B_HEAD_DIM // 2, 1) * sinf
        k = (k * cosf + pltpu.roll(k, B_HEAD_DIM // 2, 1) * sinf) * (B_HEAD_DIM ** -0.5)
        dmat = jnp.where(causal, jnp.exp(lg * dist), 0.0)
        scores = _dot(q, k, NT) * dmat
        s_old = s_scr[hh]
        o = _dot(scores, v) + _dot(q, s_old) * jnp.exp(lg * (n + 1.0))
        k_dec = k * jnp.exp(lg * ((tb - 1.0) - n))
        s_scr[hh] = math.exp(lg * tb) * s_old + _dot(k_dec, v, TN)
        outs.append(_rms(o) * (gate * _sigmoid(gate)))
    o_ref[0] = jnp.concatenate(outs, axis=-1)
    sout_ref[0] = s_scr[...]


def _retention(x, gain, w_b, cosf, sinf, s0, *, tb):
    bsz, seq, _ = x.shape
    assert seq % tb == 0
    sspec = pl.BlockSpec((1, B_HEADS, B_HEAD_DIM, B_HEAD_DIM), lambda b, i: (b, 0, 0, 0))
    return pl.pallas_call(
        functools.partial(_ret_kernel, tb=tb),
        grid=(bsz, seq // tb),
        in_specs=[pl.BlockSpec((1, tb, D_MODEL), lambda b, i: (b, i, 0)),
                  _const_spec((1, D_MODEL)), _const_spec((D_MODEL, 4 * B_WIDTH)),
                  pl.BlockSpec((tb, B_HEAD_DIM), lambda b, i: (i, 0)),
                  pl.BlockSpec((tb, B_HEAD_DIM), lambda b, i: (i, 0)),
                  sspec],
        out_specs=[pl.BlockSpec((1, tb, B_WIDTH), lambda b, i: (b, i, 0)), sspec],
        out_shape=[jax.ShapeDtypeStruct((bsz, seq, B_WIDTH), F32),
                   jax.ShapeDtypeStruct((bsz, B_HEADS, B_HEAD_DIM, B_HEAD_DIM), F32)],
        scratch_shapes=[pltpu.VMEM((B_HEADS, B_HEAD_DIM, B_HEAD_DIM), F32)],
        compiler_params=pltpu.CompilerParams(
            dimension_semantics=("arbitrary", "arbitrary"),
            vmem_limit_bytes=VMEM_LIMIT_BYTES),
        name="retention",
    )(x, gain, w_b, cosf, sinf, s0)


PAIR_W = 2 * C_HEAD_DIM
N_PAIRS = C_HEADS // 2


def _dot2(a, b, dims=NN):
    a_hi, a_lo = _split_bf16(a)
    b = b.astype(BF16)
    d = functools.partial(lax.dot_general, dimension_numbers=dims, preferred_element_type=F32)
    return d(a_hi, b) + d(a_lo, b)


def _rwkv_kernel(x_ref, g_ref, w_ref, mu_ref, sh0_ref, s0_ref, w0_ref, w2_ref, a0_ref, a2_ref,
                 g2_ref, kk_ref, ka_ref, rk_ref, lnw_ref, lnb_ref, hsum_ref,
                 o_ref, sout_ref, shout_ref,
                 s_scr, carry, rt_s, kkt_s, kh_s, bh_s, v_s, bon_s, g_s, cum_s, *, tt, cc):
    i = pl.program_id(1)

    @pl.when(i == 0)
    def _():
        s_scr[...] = s0_ref[0]
        carry[...] = sh0_ref[0]

    h = _rms(x_ref[0]) * g_ref[...]
    cz = _dot(h, w_ref[...])
    rows = lax.broadcasted_iota(jnp.int32, (tt, 1), 0)
    prev = jnp.where(rows == 0, carry[...], pltpu.roll(cz, 1, 0))
    cs = cz + (prev - cz) * mu_ref[...]
    last = cz[tt - 1:tt, :]
    carry[...] = last
    shout_ref[0] = last

    r = cs[:, 0:C_WIDTH]
    k = cs[:, C_WIDTH:2 * C_WIDTH]
    v = cs[:, 2 * C_WIDTH:3 * C_WIDTH]
    off = 3 * C_WIDTH
    w_lo = cs[:, off:off + C_RANK_W]
    a_lo = cs[:, off + C_RANK_W:off + C_RANK_W + C_RANK_A]
    g_lo = cs[:, off + C_RANK_W + C_RANK_A:C_SHIFT_WIDTH]

    u = -(w0_ref[...] + _dot(jnp.tanh(w_lo), w2_ref[...]))
    softplus = jnp.maximum(u, 0.0) + jnp.log1p(jnp.exp(-jnp.abs(u)))
    lw = -jnp.exp(-softplus - 0.5)
    a = _sigmoid(a0_ref[...] + _dot(a_lo, a2_ref[...]))
    g = _dot(_sigmoid(g_lo), g2_ref[...])
    hsum = hsum_ref[...]
    kk_raw = k * kk_ref[...]
    kk = kk_raw / jnp.maximum(jnp.sqrt(_dot2(kk_raw * kk_raw, hsum)), 1e-12)
    k2 = k * (1.0 + (a - 1.0) * ka_ref[...])

    pos = jnp.bitwise_and(rows, cc - 1)
    cum = lw
    step = 1
    while step < cc:
        cum = cum + jnp.where(pos >= step, pltpu.roll(cum, step, 0), 0.0)
        step *= 2
    e_inv = jnp.exp(-cum)
    rt_s[...] = r * jnp.exp(cum)
    kkt_s[...] = kk * jnp.exp(cum - lw)
    kh_s[...] = k2 * e_inv
    bh_s[...] = kk * a * e_inv
    v_s[...] = v
    bon_s[...] = _dot2(r * k2 * rk_ref[...], hsum) * v
    g_s[...] = g
    cum_s[...] = cum

    two = 2 * cc
    sh = cc.bit_length() - 1
    hd = C_HEAD_DIM.bit_length() - 1
    n_double = sh - 1
    rho = lax.broadcasted_iota(jnp.int32, (two, PAIR_W), 0)
    lane = lax.broadcasted_iota(jnp.int32, (two, PAIR_W), 1)
    placed = (rho >> sh) == (lane >> hd)
    r2 = lax.broadcasted_iota(jnp.int32, (two, two), 0)
    c2 = lax.broadcasted_iota(jnp.int32, (two, two), 1)
    same = (r2 >> sh) == (c2 >> sh)
    strict = jnp.logical_and(same, r2 > c2)
    incl = jnp.logical_and(same, r2 >= c2)
    eye = jnp.where(r2 == c2, 1.0, 0.0).astype(F32)
    gr = lax.broadcasted_iota(jnp.int32, (PAIR_W, PAIR_W), 0)
    gc = lax.broadcasted_iota(jnp.int32, (PAIR_W, PAIR_W), 1)
    hmean = jnp.where((gr >> hd) == (gc >> hd), 1.0 / C_HEAD_DIM, 0.0).astype(BF16)

    def place(xv):
        return jnp.where(placed, jnp.concatenate([xv, xv], axis=0), 0.0)

    pairs = range(N_PAIRS)

    def chunk(j, c_):
        r0 = pl.multiple_of(j * cc, cc)
        w_end = jnp.exp(cum_s[pl.ds(r0 + cc - 1, 1), :])
        sl = [(pl.ds(r0, cc), slice(p * PAIR_W, (p + 1) * PAIR_W)) for p in pairs]
        we = [w_end[:, p * PAIR_W:(p + 1) * PAIR_W] for p in pairs]
        rt = [place(rt_s[sl[p]]) for p in pairs]
        kkt = [place(kkt_s[sl[p]]) for p in pairs]
        kh = [place(kh_s[sl[p]]) for p in pairs]
        bh = [place(bh_s[sl[p]]) for p in pairs]
        vv = [place(v_s[sl[p]]) for p in pairs]
        lhs = [jnp.concatenate([kkt[p], rt[p]], axis=0) for p in pairs]
        gk = [_dot(lhs[p], kh[p], NT) for p in pairs]
        gb = [_dot(lhs[p], bh[p], NT) for p in pairs]
        a_kk = [jnp.where(strict, gk[p][:two], 0.0) for p in pairs]
        b_kk = [jnp.where(incl, gk[p][two:], 0.0) for p in pairs]
        b_bb = [jnp.where(incl, gb[p][two:], 0.0) for p in pairs]
        pw = [jnp.where(strict, -gb[p][:two], 0.0) for p in pairs]
        tinv = [eye + pw[p] for p in pairs]
        for _ in range(n_double):
            pw = [_dot(pw[p], pw[p]) for p in pairs]
            tinv = [tinv[p] + _dot(pw[p], tinv[p]) for p in pairs]
        av = [_dot(a_kk[p], vv[p]) for p in pairs]
        kkp = [_dot(tinv[p], kkt[p]) for p in pairs]
        vp = [_dot(tinv[p], av[p]) for p in pairs]
        rp = [rt[p] - _dot(b_bb[p], kkp[p]) for p in pairs]
        y0 = [_dot(b_kk[p], vv[p]) - _dot(b_bb[p], vp[p]) for p in pairs]
        bd = [bh[p] * we[p] for p in pairs]
        kd = [kh[p] * we[p] for p in pairs]
        q = [_dot(kkp[p], bd[p], TN) for p in pairs]
        z = [_dot(vv[p], kd[p], TN) - _dot(vp[p], bd[p], TN) for p in pairs]
        for p in pairs:
            s_old = s_scr[p]
            y2 = _dot(rp[p], s_old, NT) + y0[p]
            s_scr[p] = s_old * we[p] - _dot3(s_old, q[p]) + z[p]
            y = y2[:cc] + y2[cc:]
            yc = y - _dot2(y, hmean)
            var = _dot2(yc * yc, hmean)
            lanes = slice(p * PAIR_W, (p + 1) * PAIR_W)
            yn = yc * lax.rsqrt(var + C_GN_EPS) * lnw_ref[:, lanes] + lnb_ref[:, lanes]
            o_ref[0, pl.ds(r0, cc), lanes] = (yn + bon_s[sl[p]]) * g_s[sl[p]]
        return c_

    lax.fori_loop(0, tt // cc, chunk, 0)
    sout_ref[0] = s_scr[...]


def _pair_states(s):
    bsz = s.shape[0]
    s = s.reshape(bsz, N_PAIRS, 2, C_HEAD_DIM, C_HEAD_DIM)
    zero = jnp.zeros_like(s[:, :, 0])
    top = jnp.concatenate([s[:, :, 0], zero], axis=-1)
    bot = jnp.concatenate([zero, s[:, :, 1]], axis=-1)
    return jnp.concatenate([top, bot], axis=-2)


def _unpair_states(s):
    bsz = s.shape[0]
    d = C_HEAD_DIM
    return jnp.stack([s[:, :, :d, :d], s[:, :, d:, d:]], axis=2).reshape(bsz, C_HEADS, d, d)


def _rwkv(x, gain, w_c, lw, shift0, s0, *, tt, cc):
    bsz, seq, _ = x.shape
    assert seq % tt == 0 and tt % cc == 0 and cc & (cc - 1) == 0
    row = lambda a: a.reshape(1, -1)
    sspec = pl.BlockSpec((1, N_PAIRS, PAIR_W, PAIR_W), lambda b, i: (b, 0, 0, 0))
    shspec = pl.BlockSpec((1, 1, C_SHIFT_WIDTH), lambda b, i: (b, 0, 0))
    vec = _const_spec((1, C_WIDTH))
    tile = pltpu.VMEM((tt, C_WIDTH), F32)
    hsum = _head_sum_matrix(C_WIDTH, C_HEAD_DIM)
    oc, s_new, shift_new = pl.pallas_call(
        functools.partial(_rwkv_kernel, tt=tt, cc=cc),
        grid=(bsz, seq // tt),
        in_specs=[pl.BlockSpec((1, tt, D_MODEL), lambda b, i: (b, i, 0)),
                  _const_spec((1, D_MODEL)), _const_spec((D_MODEL, C_SHIFT_WIDTH)),
                  _const_spec((1, C_SHIFT_WIDTH)), shspec, sspec,
                  vec, _const_spec((C_RANK_W, C_WIDTH)), vec, _const_spec((C_RANK_A, C_WIDTH)),
                  _const_spec((C_RANK_G, C_WIDTH)), vec, vec, vec, vec, vec,
                  _const_spec((C_WIDTH, C_WIDTH))],
        out_specs=[pl.BlockSpec((1, tt, C_WIDTH), lambda b, i: (b, i, 0)), sspec, shspec],
        out_shape=[jax.ShapeDtypeStruct((bsz, seq, C_WIDTH), F32),
                   jax.ShapeDtypeStruct((bsz, N_PAIRS, PAIR_W, PAIR_W), F32),
                   jax.ShapeDtypeStruct((bsz, 1, C_SHIFT_WIDTH), F32)],
        scratch_shapes=[pltpu.VMEM((N_PAIRS, PAIR_W, PAIR_W), F32),
                        pltpu.VMEM((1, C_SHIFT_WIDTH), F32)] + [tile] * 8,
        compiler_params=pltpu.CompilerParams(
            dimension_semantics=("arbitrary", "arbitrary"),
            vmem_limit_bytes=VMEM_LIMIT_BYTES),
        name="rwkv",
    )(x, gain, w_c, row(lw["c_shift_mu"]), shift0, _pair_states(s0),
      row(lw["c_w0"]), lw["c_w2"].astype(BF16), row(lw["c_a0"]), lw["c_a2"].astype(BF16),
      lw["c_g2"].astype(BF16), row(lw["c_k_k"]), row(lw["c_k_a"]), row(lw["c_r_k"]),
      row(lw["c_ln_w"]), row(lw["c_ln_b"]), hsum)
    return oc, _unpair_states(s_new), shift_new


def _merge_kernel(x_ref, oa_ref, ob_ref, oc_ref, g_ref, wg_ref, wb_ref, wo_ref, y_ref):
    x = x_ref[...]
    h = _rms(x) * g_ref[...]
    gl = _dot(h, wg_ref[...])
    m = None
    for b, o_ref in enumerate((oa_ref, ob_ref, oc_ref)):
        t = _sigmoid(gl[:, b * D_MODEL:(b + 1) * D_MODEL]) * _dot(o_ref[...], wb_ref[b])
        m = t if m is None else m + t
    y_ref[...] = x + _dot(m, wo_ref[...])


def _merge(x, oa, ob, oc, gain, w_g, w_b, w_o, *, tm):
    rows = x.shape[0]
    assert rows % tm == 0
    tok = lambda w: pl.BlockSpec((tm, w), lambda i: (i, 0))
    return pl.pallas_call(
        _merge_kernel,
        grid=(rows // tm,),
        in_specs=[tok(D_MODEL), tok(A_WIDTH), tok(B_WIDTH), tok(C_WIDTH),
                  _const_spec((1, D_MODEL)), _const_spec((D_MODEL, N_BRANCHES * D_MODEL)),
                  _const_spec((N_BRANCHES, A_WIDTH, D_MODEL)), _const_spec((D_MODEL, D_MODEL))],
        out_specs=tok(D_MODEL),
        out_shape=jax.ShapeDtypeStruct((rows, D_MODEL), F32),
        compiler_params=pltpu.CompilerParams(
            dimension_semantics=("arbitrary",), vmem_limit_bytes=VMEM_LIMIT_BYTES),
        name="merge",
    )(x, oa, ob, oc, gain, w_g, w_b, w_o)


def _ffn_kernel(x_ref, p_ref, g_ref, wgate_ref, wup_ref, wdown_ref, wpp_ref, pg_ref, wpg_ref, y_ref):
    x = x_ref[...]
    hf = (_rms(x) * g_ref[...]).astype(BF16)
    gate = _dot(hf, wgate_ref[...])
    up = _dot(hf, wup_ref[...])
    x = x + _dot(gate * _sigmoid(gate) * up, wdown_ref[...])
    e = _rms(_dot(p_ref[...], wpp_ref[...])) * pg_ref[...]
    y_ref[...] = x + _sigmoid(_dot(_rms(x), wpg_ref[...])) * e


def _ffn(x, p, gain, w_gate, w_up, w_down, w_pp, p_gain, w_pg, *, tm):
    rows = x.shape[0]
    d_ff = w_gate.shape[1]
    ple = p.shape[1]
    assert rows % tm == 0
    tok = lambda w: pl.BlockSpec((tm, w), lambda i: (i, 0))
    return pl.pallas_call(
        _ffn_kernel,
        grid=(rows // tm,),
        in_specs=[tok(D_MODEL), tok(ple), _const_spec((1, D_MODEL)),
                  _const_spec((D_MODEL, d_ff)), _const_spec((D_MODEL, d_ff)),
                  _const_spec((d_ff, D_MODEL)), _const_spec((ple, D_MODEL)),
                  _const_spec((1, D_MODEL)), _const_spec((D_MODEL, D_MODEL))],
        out_specs=tok(D_MODEL),
        out_shape=jax.ShapeDtypeStruct((rows, D_MODEL), F32),
        compiler_params=pltpu.CompilerParams(
            dimension_semantics=("arbitrary",), vmem_limit_bytes=VMEM_LIMIT_BYTES),
        name="ffn",
    )(x, p, gain, w_gate, w_up, w_down, w_pp, p_gain, w_pg)


def _rel_bias_table(rel_bias, cq):
    nkeys = A_WIN + cq
    t_max = A_WIN + CHUNK - 1 + cq - 1
    n_rel = rel_bias.shape[1]
    tail = jnp.broadcast_to(rel_bias[:, n_rel - 1:], (rel_bias.shape[0], t_max + 1 - n_rel))
    rev = jnp.concatenate([rel_bias, tail], axis=1).astype(F32)[:, ::-1]
    return jnp.stack([rev[:, cq - 1 - i:cq - 1 - i + nkeys] for i in range(cq)], axis=1)


def _rotary_tables(pos0, seq):
    half = B_HEAD_DIM // 2
    pos = pos0 + jnp.arange(seq, dtype=jnp.int32)
    inv = ROPE_BASE ** (-jnp.arange(half, dtype=F32) / half)
    ang = pos.astype(F32)[:, None] * inv[None, :]
    cos, sin = jnp.cos(ang), jnp.sin(ang)
    return jnp.concatenate([cos, cos], axis=-1), jnp.concatenate([-sin, sin], axis=-1)


def _layer(x, p_l, pos0, a_ck, a_cv, ret_s0, rwkv_s0, shift_prev, lw, cfg):
    bsz, seq, _ = x.shape
    row = lambda a: a.reshape(1, -1)
    w_in = lw["w_in"].astype(BF16)
    w_a = w_in[:, OFF_A:OFF_B]
    w_b = w_in[:, OFF_B:OFF_C]
    w_c = w_in[:, OFF_C:OFF_G]
    w_g = w_in[:, OFF_G:]
    gain = row(lw["norm_mix"])

    bias = _rel_bias_table(lw["a_rel_bias"], cfg["cq"])
    if a_ck is not None:
        a_ck = a_ck.reshape(bsz, A_WIN, A_WIDTH)
        a_cv = a_cv.reshape(bsz, A_WIN, A_WIDTH)
    oa, kn, av = _attention(x, gain, w_a, row(lw["a_q_norm"]), row(lw["a_k_norm"]), bias,
                            a_ck, a_cv, tq=cfg["tq"], cq=cfg["cq"])
    keep = min(A_WIN, seq)
    new_ak = kn[:, seq - keep:].reshape(bsz, keep, A_HEADS, A_HEAD_DIM)
    new_av = av[:, seq - keep:].reshape(bsz, keep, A_HEADS, A_HEAD_DIM)

    cosf, sinf = _rotary_tables(pos0, seq)
    ob, new_ret = _retention(x, gain, w_b, cosf, sinf, ret_s0, tb=cfg["tb"])

    oc, new_rwkv, new_shift = _rwkv(x, gain, w_c, lw, shift_prev, rwkv_s0, tt=cfg["tt"], cc=cfg["cc"])

    rows = bsz * seq
    tm = min(cfg["tm"], rows)
    flat = lambda t: t.reshape(rows, t.shape[-1])
    x1 = _merge(flat(x), flat(oa), flat(ob), flat(oc), gain, w_g,
                lw["w_branch"].astype(BF16), lw["w_out"].astype(BF16), tm=tm)
    x2 = _ffn(x1, flat(p_l), row(lw["norm_ffn"]), lw["w_ffn_gate"].astype(BF16),
              lw["w_ffn_up"].astype(BF16), lw["w_ffn_down"].astype(BF16),
              lw["w_ple_proj"].astype(BF16), row(lw["ple_norm"]), lw["w_ple_gate"].astype(BF16),
              tm=tm)
    return x2.reshape(bsz, seq, D_MODEL), (new_ak, new_av, new_ret, new_rwkv, new_shift)


def _group_config(seq):
    if seq >= A_WIN:
        return dict(tq=A_WIN, cq=CHUNK, tb=256, tt=256, cc=CHUNK, tm=256)
    return dict(tq=seq, cq=seq, tb=seq, tt=seq, cc=seq, tm=256)


def kernel(x_prompt, x_sample, p_prompt, p_sample, cache_a_k, cache_a_v, state_ret, state_rwkv, state_rwkv_shift, norm_mix, w_in, a_q_norm, a_k_norm, a_rel_bias, c_shift_mu, c_w0, c_w2, c_a0, c_a2, c_g2, c_k_k, c_k_a, c_r_k, c_ln_w, c_ln_b, w_branch, w_out, norm_ffn, w_ffn_gate, w_ffn_up, w_ffn_down, w_ple_proj, ple_norm, w_ple_gate):
    depth = w_in.shape[0]

    def layer_weights(i):
        return dict(norm_mix=norm_mix[i], w_in=w_in[i], a_q_norm=a_q_norm[i], a_k_norm=a_k_norm[i],
                    a_rel_bias=a_rel_bias[i], c_shift_mu=c_shift_mu[i], c_w0=c_w0[i], c_w2=c_w2[i],
                    c_a0=c_a0[i], c_a2=c_a2[i], c_g2=c_g2[i], c_k_k=c_k_k[i], c_k_a=c_k_a[i], c_r_k=c_r_k[i],
                    c_ln_w=c_ln_w[i], c_ln_b=c_ln_b[i], w_branch=w_branch[i], w_out=w_out[i],
                    norm_ffn=norm_ffn[i], w_ffn_gate=w_ffn_gate[i], w_ffn_up=w_ffn_up[i],
                    w_ffn_down=w_ffn_down[i], w_ple_proj=w_ple_proj[i], ple_norm=ple_norm[i],
                    w_ple_gate=w_ple_gate[i])

    bp, lp, _ = x_prompt.shape
    cfg_p = _group_config(lp)
    ret0 = jnp.zeros((bp, B_HEADS, B_HEAD_DIM, B_HEAD_DIM), F32)
    rwkv0 = jnp.zeros((bp, C_HEADS, C_HEAD_DIM, C_HEAD_DIM), F32)
    shift0 = jnp.zeros((bp, 1, C_SHIFT_WIDTH), F32)
    y_prompt = x_prompt
    st_p = []
    for i in range(depth):
        y_prompt, st = _layer(y_prompt, p_prompt[i], 0, None, None, ret0, rwkv0, shift0,
                              layer_weights(i), cfg_p)
        st_p.append(st)

    cfg_s = _group_config(x_sample.shape[1])
    y_sample = x_sample
    st_s = []
    for i in range(depth):
        y_sample, st = _layer(y_sample, p_sample[i], PAST_LEN, cache_a_k[i], cache_a_v[i], state_ret[i],
                              state_rwkv[i], state_rwkv_shift[i], layer_weights(i), cfg_s)
        st_s.append(st)

    stack = lambda sts, j: jnp.stack([s[j] for s in sts])
    return (y_prompt, y_sample,
            stack(st_p, 0), stack(st_p, 1), stack(st_p, 2), stack(st_p, 3), stack(st_p, 4),
            stack(st_s, 0), stack(st_s, 1), stack(st_s, 2), stack(st_s, 3), stack(st_s, 4))
```

```python
import functools
import math

import jax
import jax.numpy as jnp
from jax import lax
from jax.experimental import pallas as pl
from jax.experimental.pallas import tpu as pltpu

F32 = jnp.float32
BF16 = jnp.bfloat16

D_MODEL = 1024
PAST_LEN = 2048
CHUNK = 64
NORM_EPS = 1e-6

A_HEADS = 8
A_HEAD_DIM = 64
A_WIDTH = 512
A_WIN = 512
A_REL_MAX = 256

B_HEADS = 4
B_HEAD_DIM = 128
B_WIDTH = 512
ROPE_BASE = 10000.0

C_HEADS = 8
C_HEAD_DIM = 64
C_WIDTH = 512
C_RANK_W = 64
C_RANK_A = 64
C_RANK_G = 128
C_SHIFT_WIDTH = 3 * C_WIDTH + C_RANK_W + C_RANK_A + C_RANK_G
C_GN_EPS = 64e-5

N_BRANCHES = 3

OFF_A = 0
OFF_B = 3 * A_WIDTH
OFF_C = OFF_B + 4 * B_WIDTH
OFF_G = OFF_C + C_SHIFT_WIDTH
IN_WIDTH = OFF_G + N_BRANCHES * D_MODEL

RET_LOG_GAMMA = tuple(math.log1p(-(2.0 ** (-5.0 - h))) for h in range(B_HEADS))

VMEM_LIMIT_BYTES = 56 * 1024 * 1024

NN = (((1,), (0,)), ((), ()))
NT = (((1,), (1,)), ((), ()))
TN = (((0,), (0,)), ((), ()))


def _dot(a, b, dims=NN):
    return lax.dot_general(a.astype(BF16), b.astype(BF16), dims, preferred_element_type=F32)


def _split_bf16(a):
    hi = a.astype(BF16)
    lo = (a - hi.astype(F32)).astype(BF16)
    return hi, lo


def _dot3(a, b, dims=NN):
    a_hi, a_lo = _split_bf16(a)
    b_hi, b_lo = _split_bf16(b)
    d = functools.partial(lax.dot_general, dimension_numbers=dims, preferred_element_type=F32)
    return d(a_hi, b_hi) + (d(a_hi, b_lo) + d(a_lo, b_hi))


def _rms(x):
    return x * lax.rsqrt(jnp.mean(x * x, axis=-1, keepdims=True) + NORM_EPS)


def _sigmoid(x):
    return 1.0 / (1.0 + jnp.exp(-x))


def _const_spec(shape):
    nd = len(shape)
    return pl.BlockSpec(shape, lambda *_: (0,) * nd, pipeline_mode=pl.Buffered(1))


def _head_sum_matrix(width, head_dim):
    head = jnp.arange(width, dtype=jnp.int32) // head_dim
    return (head[:, None] == head[None, :]).astype(BF16)


def _attn_kernel(*refs, tq, cq, has_cache):
    if has_cache:
        (x_ref, g_ref, w_ref, qg_ref, kg_ref, bias_ref, hsum_ref, kc_ref, vc_ref,
         o_ref, kn_ref, v_ref, kwin, vwin, qs) = refs
    else:
        (x_ref, g_ref, w_ref, qg_ref, kg_ref, bias_ref, hsum_ref,
         o_ref, kn_ref, v_ref, kwin, vwin, qs) = refs
    i = pl.program_id(1)
    nkeys = A_WIN + cq
    pair_w = 2 * A_HEAD_DIM

    if has_cache:
        kwin[0:A_WIN, :] = kc_ref[0].astype(BF16)
        vwin[0:A_WIN, :] = vc_ref[0].astype(BF16)
    else:
        @pl.when(i == 0)
        def _():
            kwin[0:A_WIN, :] = jnp.zeros((A_WIN, A_WIDTH), BF16)
            vwin[0:A_WIN, :] = jnp.zeros((A_WIN, A_WIDTH), BF16)

    h = _rms(x_ref[0]) * g_ref[...]
    z = _dot(h, w_ref[...])
    q = z[:, 0:A_WIDTH]
    k = z[:, A_WIDTH:2 * A_WIDTH]
    v = z[:, 2 * A_WIDTH:3 * A_WIDTH]
    hsum = hsum_ref[...]
    inv_d = 1.0 / A_HEAD_DIM
    qn = q * lax.rsqrt(_dot(q * q, hsum) * inv_d + NORM_EPS) * qg_ref[...]
    kn = k * lax.rsqrt(_dot(k * k, hsum) * inv_d + NORM_EPS) * kg_ref[...]
    qs[...] = qn * (A_HEAD_DIM ** -0.5)
    kwin[A_WIN:A_WIN + tq, :] = kn.astype(BF16)
    vwin[A_WIN:A_WIN + tq, :] = v.astype(BF16)
    kn_ref[0] = kn
    v_ref[0] = v

    rho = lax.broadcasted_iota(jnp.int32, (2 * cq, pair_w), 0)
    lane = lax.broadcasted_iota(jnp.int32, (2 * cq, pair_w), 1)
    placed = (rho >> (cq.bit_length() - 1)) == (lane >> (A_HEAD_DIM.bit_length() - 1))

    n_chunks = tq // cq
    per_trip = 2 if n_chunks % 2 == 0 else 1
    units = [(c, p) for c in range(per_trip) for p in range(A_HEADS // 2)]
    ids = range(len(units))

    def chunk(j, carry):
        r0s = [pl.multiple_of((j * per_trip + c) * cq, cq) for c in range(per_trip)]
        lanes = [slice(p * pair_w, (p + 1) * pair_w) for _, p in units]
        qc = [qs[pl.ds(r0s[c], cq), lanes[u]] for u, (c, _) in enumerate(units)]
        qp = [jnp.where(placed, jnp.concatenate([qc[u], qc[u]], axis=0), 0.0) for u in ids]
        s = [_dot(qp[u], kwin[pl.ds(r0s[c], nkeys), lanes[u]], NT) + bias_ref[p]
             for u, (c, p) in enumerate(units)]
        if not has_cache:
            col = lax.broadcasted_iota(jnp.int32, (2 * cq, nkeys), 1)
            ok = [jnp.logical_or(r0 + col >= A_WIN, i > 0) for r0 in r0s]
            s = [jnp.where(ok[c], s[u], -1e30) for u, (c, _) in enumerate(units)]
        m = [jnp.max(s[u], axis=-1, keepdims=True) for u in ids]
        e = [jnp.exp(s[u] - m[u]) for u in ids]
        den = [jnp.sum(e[u], axis=-1, keepdims=True) for u in ids]
        o2 = [_dot(e[u], vwin[pl.ds(r0s[c], nkeys), lanes[u]]) for u, (c, _) in enumerate(units)]
        for u, (c, _) in enumerate(units):
            o = jnp.where(placed, o2[u] / den[u], 0.0)
            o_ref[0, pl.ds(r0s[c], cq), lanes[u]] = o[:cq] + o[cq:]
        return carry

    lax.fori_loop(0, n_chunks // per_trip, chunk, 0)

    if not has_cache:
        kwin[0:A_WIN, :] = kwin[tq:tq + A_WIN, :]
        vwin[0:A_WIN, :] = vwin[tq:tq + A_WIN, :]


def _attention(x, gain, w_a, q_gain, k_gain, bias, cache_k, cache_v, *, tq, cq):
    bsz, seq, _ = x.shape
    has_cache = cache_k is not None
    assert seq % tq == 0 and tq % cq == 0
    assert has_cache or tq == A_WIN
    nkeys = A_WIN + cq
    tok = lambda w: pl.BlockSpec((1, tq, w), lambda b, i: (b, i, 0))
    in_specs = [tok(D_MODEL), _const_spec((1, D_MODEL)), _const_spec((D_MODEL, 3 * A_WIDTH)),
                _const_spec((1, A_WIDTH)), _const_spec((1, A_WIDTH)),
                _const_spec((A_HEADS // 2, 2 * cq, nkeys)), _const_spec((A_WIDTH, A_WIDTH))]
    args = [x, gain, w_a, jnp.tile(q_gain, (1, A_HEADS)), jnp.tile(k_gain, (1, A_HEADS)),
            bias.reshape(A_HEADS // 2, 2 * cq, nkeys), _head_sum_matrix(A_WIDTH, A_HEAD_DIM)]
    if has_cache:
        cspec = pl.BlockSpec((1, A_WIN, A_WIDTH), lambda b, i: (b, 0, 0))
        in_specs += [cspec, cspec]
        args += [cache_k, cache_v]
    out = jax.ShapeDtypeStruct((bsz, seq, A_WIDTH), F32)
    return pl.pallas_call(
        functools.partial(_attn_kernel, tq=tq, cq=cq, has_cache=has_cache),
        grid=(bsz, seq // tq),
        in_specs=in_specs,
        out_specs=[tok(A_WIDTH)] * 3,
        out_shape=[out] * 3,
        scratch_shapes=[pltpu.VMEM((A_WIN + tq, A_WIDTH), BF16),
                        pltpu.VMEM((A_WIN + tq, A_WIDTH), BF16),
                        pltpu.VMEM((tq, A_WIDTH), F32)],
        compiler_params=pltpu.CompilerParams(
            dimension_semantics=("arbitrary", "arbitrary"),
            vmem_limit_bytes=VMEM_LIMIT_BYTES),
        name="attention",
    )(*args)


def _ret_kernel(x_ref, g_ref, w_ref, cos_ref, sin_ref, s0_ref, o_ref, sout_ref, s_scr, *, tb):
    i = pl.program_id(1)

    @pl.when(i == 0)
    def _():
        s_scr[...] = s0_ref[0]

    h = _rms(x_ref[0]) * g_ref[...]
    z = _dot(h, w_ref[...])
    cosf = cos_ref[...]
    sinf = sin_ref[...]
    row = lax.broadcasted_iota(jnp.int32, (tb, tb), 0)
    col = lax.broadcasted_iota(jnp.int32, (tb, tb), 1)
    diff = row - col
    causal = diff >= 0
    dist = jnp.maximum(diff, 0).astype(F32)
    n = lax.broadcasted_iota(jnp.int32, (tb, 1), 0).astype(F32)
    outs = []
    for hh in range(B_HEADS):
        lg = RET_LOG_GAMMA[hh]
        lo = hh * B_HEAD_DIM
        q = z[:, lo:lo + B_HEAD_DIM]
        k = z[:, B_WIDTH + lo:B_WIDTH + lo + B_HEAD_DIM]
        v = z[:, 2 * B_WIDTH + lo:2 * B_WIDTH + lo + B_HEAD_DIM]
        gate = z[:, 3 * B_WIDTH + lo:3 * B_WIDTH + lo + B_HEAD_DIM]
        q = q * cosf + pltpu.roll(q, B_HEAD_DIM // 2, 1) * sinf
        k = (k * cosf + pltpu.roll(k, B_HEAD_DIM // 2, 1) * sinf) * (B_HEAD_DIM ** -0.5)
        dmat = jnp.where(causal, jnp.exp(lg * dist), 0.0)
        scores = _dot(q, k, NT) * dmat
        s_old = s_scr[hh]
        o = _dot(scores, v) + _dot(q, s_old) * jnp.exp(lg * (n + 1.0))
        k_dec = k * jnp.exp(lg * ((tb - 1.0) - n))
        s_scr[hh] = math.exp(lg * tb) * s_old + _dot(k_dec, v, TN)
        outs.append(_rms(o) * (gate * _sigmoid(gate)))
    o_ref[0] = jnp.concatenate(outs, axis=-1)
    sout_ref[0] = s_scr[...]


def _retention(x, gain, w_b, cosf, sinf, s0, *, tb):
    bsz, seq, _ = x.shape
    assert seq % tb == 0
    sspec = pl.BlockSpec((1, B_HEADS, B_HEAD_DIM, B_HEAD_DIM), lambda b, i: (b, 0, 0, 0))
    return pl.pallas_call(
        functools.partial(_ret_kernel, tb=tb),
        grid=(bsz, seq // tb),
        in_specs=[pl.BlockSpec((1, tb, D_MODEL), lambda b, i: (b, i, 0)),
                  _const_spec((1, D_MODEL)), _const_spec((D_MODEL, 4 * B_WIDTH)),
                  pl.BlockSpec((tb, B_HEAD_DIM), lambda b, i: (i, 0)),
                  pl.BlockSpec((tb, B_HEAD_DIM), lambda b, i: (i, 0)),
                  sspec],
        out_specs=[pl.BlockSpec((1, tb, B_WIDTH), lambda b, i: (b, i, 0)), sspec],
        out_shape=[jax.ShapeDtypeStruct((bsz, seq, B_WIDTH), F32),
                   jax.ShapeDtypeStruct((bsz, B_HEADS, B_HEAD_DIM, B_HEAD_DIM), F32)],
        scratch_shapes=[pltpu.VMEM((B_HEADS, B_HEAD_DIM, B_HEAD_DIM), F32)],
        compiler_params=pltpu.CompilerParams(
            dimension_semantics=("arbitrary", "arbitrary"),
            vmem_limit_bytes=VMEM_LIMIT_BYTES),
        name="retention",
    )(x, gain, w_b, cosf, sinf, s0)


PAIR_W = 2 * C_HEAD_DIM
N_PAIRS = C_HEADS // 2


def _dot2(a, b, dims=NN):
    a_hi, a_lo = _split_bf16(a)
    b = b.astype(BF16)
    d = functools.partial(lax.dot_general, dimension_numbers=dims, preferred_element_type=F32)
    return d(a_hi, b) + d(a_lo, b)


def _rwkv_kernel(x_ref, g_ref, w_ref, mu_ref, sh0_ref, s0_ref, w0_ref, w2_ref, a0_ref, a2_ref,
                 g2_ref, kk_ref, ka_ref, rk_ref, lnw_ref, lnb_ref, hsum_ref,
                 o_ref, sout_ref, shout_ref,
                 s_scr, carry, rt_s, kkt_s, kh_s, bh_s, v_s, bon_s, g_s, cum_s, *, tt, cc):
    i = pl.program_id(1)

    @pl.when(i == 0)
    def _():
        s_scr[...] = s0_ref[0]
        carry[...] = sh0_ref[0]

    h = _rms(x_ref[0]) * g_ref[...]
    cz = _dot(h, w_ref[...])
    rows = lax.broadcasted_iota(jnp.int32, (tt, 1), 0)
    prev = jnp.where(rows == 0, carry[...], pltpu.roll(cz, 1, 0))
    cs = cz + (prev - cz) * mu_ref[...]
    last = cz[tt - 1:tt, :]
    carry[...] = last
    shout_ref[0] = last

    r = cs[:, 0:C_WIDTH]
    k = cs[:, C_WIDTH:2 * C_WIDTH]
    v = cs[:, 2 * C_WIDTH:3 * C_WIDTH]
    off = 3 * C_WIDTH
    w_lo = cs[:, off:off + C_RANK_W]
    a_lo = cs[:, off + C_RANK_W:off + C_RANK_W + C_RANK_A]
    g_lo = cs[:, off + C_RANK_W + C_RANK_A:C_SHIFT_WIDTH]

    lw = -math.exp(-0.5) * _sigmoid(w0_ref[...] + _dot(jnp.tanh(w_lo), w2_ref[...]))
    a = _sigmoid(a0_ref[...] + _dot(a_lo, a2_ref[...]))
    g = _dot(_sigmoid(g_lo), g2_ref[...])
    hsum = hsum_ref[...]
    kk_raw = k * kk_ref[...]
    kk = kk_raw / jnp.maximum(jnp.sqrt(_dot(kk_raw * kk_raw, hsum)), 1e-12)
    k2 = k * (1.0 + (a - 1.0) * ka_ref[...])

    pos = jnp.bitwise_and(rows, cc - 1)
    cum = lw
    step = 1
    while step < cc:
        cum = cum + jnp.where(pos >= step, pltpu.roll(cum, step, 0), 0.0)
        step *= 2
    e_inv = jnp.exp(-cum)
    rt_s[...] = r * jnp.exp(cum)
    kkt_s[...] = kk * jnp.exp(cum - lw)
    kh_s[...] = k2 * e_inv
    bh_s[...] = kk * a * e_inv
    v_s[...] = v
    bon_s[...] = _dot(r * k2 * rk_ref[...], hsum) * v
    g_s[...] = g
    cum_s[...] = cum

    two = 2 * cc
    sh = cc.bit_length() - 1
    hd = C_HEAD_DIM.bit_length() - 1
    n_double = sh - 1
    rho = lax.broadcasted_iota(jnp.int32, (two, PAIR_W), 0)
    lane = lax.broadcasted_iota(jnp.int32, (two, PAIR_W), 1)
    placed = (rho >> sh) == (lane >> hd)
    r2 = lax.broadcasted_iota(jnp.int32, (two, two), 0)
    c2 = lax.broadcasted_iota(jnp.int32, (two, two), 1)
    same = (r2 >> sh) == (c2 >> sh)
    strict = jnp.logical_and(same, r2 > c2)
    incl = jnp.logical_and(same, r2 >= c2)
    eye = jnp.where(r2 == c2, 1.0, 0.0).astype(F32)
    gr = lax.broadcasted_iota(jnp.int32, (PAIR_W, PAIR_W), 0)
    gc = lax.broadcasted_iota(jnp.int32, (PAIR_W, PAIR_W), 1)
    hmean = jnp.where((gr >> hd) == (gc >> hd), 1.0 / C_HEAD_DIM, 0.0).astype(BF16)

    def place(xv):
        return jnp.where(placed, jnp.concatenate([xv, xv], axis=0), 0.0)

    n_chunks = tt // cc
    per_trip = 2 if n_chunks % 2 == 0 else 1
    units = [(c, p) for c in range(per_trip) for p in range(N_PAIRS)]
    pairs = range(len(units))

    def chunk(j, c_):
        r0s = [pl.multiple_of((j * per_trip + c) * cc, cc) for c in range(per_trip)]
        w_end = [jnp.exp(cum_s[pl.ds(r0 + cc - 1, 1), :]) for r0 in r0s]
        lanes = [slice(p * PAIR_W, (p + 1) * PAIR_W) for _, p in units]
        sl = [(pl.ds(r0s[c], cc), lanes[u]) for u, (c, _) in enumerate(units)]
        we = [w_end[c][:, lanes[u]] for u, (c, _) in enumerate(units)]
        rt = [place(rt_s[sl[p]]) for p in pairs]
        kkt = [place(kkt_s[sl[p]]) for p in pairs]
        kh = [place(kh_s[sl[p]]) for p in pairs]
        bh = [place(bh_s[sl[p]]) for p in pairs]
        vv = [place(v_s[sl[p]]) for p in pairs]
        lhs = [jnp.concatenate([kkt[p], rt[p]], axis=0) for p in pairs]
        gk = [_dot(lhs[p], kh[p], NT) for p in pairs]
        gb = [_dot(lhs[p], bh[p], NT) for p in pairs]
        a_kk = [jnp.where(strict, gk[p][:two], 0.0) for p in pairs]
        b_kk = [jnp.where(incl, gk[p][two:], 0.0) for p in pairs]
        b_bb = [jnp.where(incl, gb[p][two:], 0.0) for p in pairs]
        pw = [jnp.where(strict, -gb[p][:two], 0.0) for p in pairs]
        tinv = [eye + pw[p] for p in pairs]
        for _ in range(n_double):
            pw = [_dot(pw[p], pw[p]) for p in pairs]
            tinv = [tinv[p] + _dot(pw[p], tinv[p]) for p in pairs]
        av = [_dot(a_kk[p], vv[p]) for p in pairs]
        kkp = [_dot(tinv[p], kkt[p]) for p in pairs]
        vp = [_dot(tinv[p], av[p]) for p in pairs]
        rp = [rt[p] - _dot(b_bb[p], kkp[p]) for p in pairs]
        y0 = [_dot(b_kk[p], vv[p]) - _dot(b_bb[p], vp[p]) for p in pairs]
        bd = [bh[p] * we[p] for p in pairs]
        kd = [kh[p] * we[p] for p in pairs]
        q = [_dot(kkp[p], bd[p], TN) for p in pairs]
        z = [_dot(vv[p], kd[p], TN) - _dot(vp[p], bd[p], TN) for p in pairs]
        state = [s_scr[p] for p in range(N_PAIRS)]
        y2 = []
        for u, (_, p) in enumerate(units):
            s_old = state[p]
            y2.append(_dot(rp[u], s_old, NT) + y0[u])
            state[p] = s_old * we[u] - _dot3(s_old, q[u]) + z[u]
        for p in range(N_PAIRS):
            s_scr[p] = state[p]
        ys = [y2[u][:cc] + y2[u][cc:] for u in pairs]
        yc = [ys[u] - _dot2(ys[u], hmean) for u in pairs]
        var = [_dot2(yc[u] * yc[u], hmean) for u in pairs]
        for u in pairs:
            yn = yc[u] * lax.rsqrt(var[u] + C_GN_EPS) * lnw_ref[:, lanes[u]] + lnb_ref[:, lanes[u]]
            o_ref[(0,) + sl[u]] = (yn + bon_s[sl[u]]) * g_s[sl[u]]
        return c_

    lax.fori_loop(0, n_chunks // per_trip, chunk, 0)
    sout_ref[0] = s_scr[...]


def _pair_states(s):
    bsz = s.shape[0]
    s = s.reshape(bsz, N_PAIRS, 2, C_HEAD_DIM, C_HEAD_DIM)
    zero = jnp.zeros_like(s[:, :, 0])
    top = jnp.concatenate([s[:, :, 0], zero], axis=-1)
    bot = jnp.concatenate([zero, s[:, :, 1]], axis=-1)
    return jnp.concatenate([top, bot], axis=-2)


def _unpair_states(s):
    bsz = s.shape[0]
    d = C_HEAD_DIM
    return jnp.stack([s[:, :, :d, :d], s[:, :, d:, d:]], axis=2).reshape(bsz, C_HEADS, d, d)


def _rwkv(x, gain, w_c, lw, shift0, s0, *, tt, cc):
    bsz, seq, _ = x.shape
    assert seq % tt == 0 and tt % cc == 0 and cc & (cc - 1) == 0
    row = lambda a: a.reshape(1, -1)
    sspec = pl.BlockSpec((1, N_PAIRS, PAIR_W, PAIR_W), lambda b, i: (b, 0, 0, 0))
    shspec = pl.BlockSpec((1, 1, C_SHIFT_WIDTH), lambda b, i: (b, 0, 0))
    vec = _const_spec((1, C_WIDTH))
    tile = pltpu.VMEM((tt, C_WIDTH), F32)
    hsum = _head_sum_matrix(C_WIDTH, C_HEAD_DIM)
    oc, s_new, shift_new = pl.pallas_call(
        functools.partial(_rwkv_kernel, tt=tt, cc=cc),
        grid=(bsz, seq // tt),
        in_specs=[pl.BlockSpec((1, tt, D_MODEL), lambda b, i: (b, i, 0)),
                  _const_spec((1, D_MODEL)), _const_spec((D_MODEL, C_SHIFT_WIDTH)),
                  _const_spec((1, C_SHIFT_WIDTH)), shspec, sspec,
                  vec, _const_spec((C_RANK_W, C_WIDTH)), vec, _const_spec((C_RANK_A, C_WIDTH)),
                  _const_spec((C_RANK_G, C_WIDTH)), vec, vec, vec, vec, vec,
                  _const_spec((C_WIDTH, C_WIDTH))],
        out_specs=[pl.BlockSpec((1, tt, C_WIDTH), lambda b, i: (b, i, 0)), sspec, shspec],
        out_shape=[jax.ShapeDtypeStruct((bsz, seq, C_WIDTH), F32),
                   jax.ShapeDtypeStruct((bsz, N_PAIRS, PAIR_W, PAIR_W), F32),
                   jax.ShapeDtypeStruct((bsz, 1, C_SHIFT_WIDTH), F32)],
        scratch_shapes=[pltpu.VMEM((N_PAIRS, PAIR_W, PAIR_W), F32),
                        pltpu.VMEM((1, C_SHIFT_WIDTH), F32)] + [tile] * 8,
        compiler_params=pltpu.CompilerParams(
            dimension_semantics=("arbitrary", "arbitrary"),
            vmem_limit_bytes=VMEM_LIMIT_BYTES),
        name="rwkv",
    )(x, gain, w_c, row(lw["c_shift_mu"]), shift0, _pair_states(s0),
      row(lw["c_w0"]), lw["c_w2"].astype(BF16), row(lw["c_a0"]), lw["c_a2"].astype(BF16),
      lw["c_g2"].astype(BF16), row(lw["c_k_k"]), row(lw["c_k_a"]), row(lw["c_r_k"]),
      row(lw["c_ln_w"]), row(lw["c_ln_b"]), hsum)
    return oc, _unpair_states(s_new), shift_new


def _merge_kernel(x_ref, oa_ref, ob_ref, oc_ref, g_ref, wg_ref, wb_ref, wo_ref, y_ref):
    x = x_ref[...]
    h = _rms(x) * g_ref[...]
    gl = _dot(h, wg_ref[...])
    m = None
    for b, o_ref in enumerate((oa_ref, ob_ref, oc_ref)):
        t = _sigmoid(gl[:, b * D_MODEL:(b + 1) * D_MODEL]) * _dot(o_ref[...], wb_ref[b])
        m = t if m is None else m + t
    y_ref[...] = x + _dot(m, wo_ref[...])


def _merge(x, oa, ob, oc, gain, w_g, w_b, w_o, *, tm):
    rows = x.shape[0]
    assert rows % tm == 0
    tok = lambda w: pl.BlockSpec((tm, w), lambda i: (i, 0))
    return pl.pallas_call(
        _merge_kernel,
        grid=(rows // tm,),
        in_specs=[tok(D_MODEL), tok(A_WIDTH), tok(B_WIDTH), tok(C_WIDTH),
                  _const_spec((1, D_MODEL)), _const_spec((D_MODEL, N_BRANCHES * D_MODEL)),
                  _const_spec((N_BRANCHES, A_WIDTH, D_MODEL)), _const_spec((D_MODEL, D_MODEL))],
        out_specs=tok(D_MODEL),
        out_shape=jax.ShapeDtypeStruct((rows, D_MODEL), F32),
        compiler_params=pltpu.CompilerParams(
            dimension_semantics=("arbitrary",), vmem_limit_bytes=VMEM_LIMIT_BYTES),
        name="merge",
    )(x, oa, ob, oc, gain, w_g, w_b, w_o)


def _ffn_kernel(x_ref, p_ref, g_ref, wgate_ref, wup_ref, wdown_ref, wpp_ref, pg_ref, wpg_ref, y_ref):
    x = x_ref[...]
    hf = (_rms(x) * g_ref[...]).astype(BF16)
    gate = _dot(hf, wgate_ref[...])
    up = _dot(hf, wup_ref[...])
    x = x + _dot(gate * _sigmoid(gate) * up, wdown_ref[...])
    e = _rms(_dot(p_ref[...], wpp_ref[...])) * pg_ref[...]
    y_ref[...] = x + _sigmoid(_dot(_rms(x), wpg_ref[...])) * e


def _ffn(x, p, gain, w_gate, w_up, w_down, w_pp, p_gain, w_pg, *, tm):
    rows = x.shape[0]
    d_ff = w_gate.shape[1]
    ple = p.shape[1]
    assert rows % tm == 0
    tok = lambda w: pl.BlockSpec((tm, w), lambda i: (i, 0))
    return pl.pallas_call(
        _ffn_kernel,
        grid=(rows // tm,),
        in_specs=[tok(D_MODEL), tok(ple), _const_spec((1, D_MODEL)),
                  _const_spec((D_MODEL, d_ff)), _const_spec((D_MODEL, d_ff)),
                  _const_spec((d_ff, D_MODEL)), _const_spec((ple, D_MODEL)),
                  _const_spec((1, D_MODEL)), _const_spec((D_MODEL, D_MODEL))],
        out_specs=tok(D_MODEL),
        out_shape=jax.ShapeDtypeStruct((rows, D_MODEL), F32),
        compiler_params=pltpu.CompilerParams(
            dimension_semantics=("arbitrary",), vmem_limit_bytes=VMEM_LIMIT_BYTES),
        name="ffn",
    )(x, p, gain, w_gate, w_up, w_down, w_pp, p_gain, w_pg)


def _rel_bias_table(rel_bias, cq):
    nkeys = A_WIN + cq
    t_max = A_WIN + CHUNK - 1 + cq - 1
    n_rel = rel_bias.shape[1]
    tail = jnp.broadcast_to(rel_bias[:, n_rel - 1:], (rel_bias.shape[0], t_max + 1 - n_rel))
    rev = jnp.concatenate([rel_bias, tail], axis=1).astype(F32)[:, ::-1]
    return jnp.stack([rev[:, cq - 1 - i:cq - 1 - i + nkeys] for i in range(cq)], axis=1)


def _rotary_tables(pos0, seq):
    half = B_HEAD_DIM // 2
    pos = pos0 + jnp.arange(seq, dtype=jnp.int32)
    inv = ROPE_BASE ** (-jnp.arange(half, dtype=F32) / half)
    ang = pos.astype(F32)[:, None] * inv[None, :]
    cos, sin = jnp.cos(ang), jnp.sin(ang)
    return jnp.concatenate([cos, cos], axis=-1), jnp.concatenate([-sin, sin], axis=-1)


def _layer(x, p_l, pos0, a_ck, a_cv, ret_s0, rwkv_s0, shift_prev, lw, cfg):
    bsz, seq, _ = x.shape
    row = lambda a: a.reshape(1, -1)
    w_in = lw["w_in"].astype(BF16)
    w_a = w_in[:, OFF_A:OFF_B]
    w_b = w_in[:, OFF_B:OFF_C]
    w_c = w_in[:, OFF_C:OFF_G]
    w_g = w_in[:, OFF_G:]
    gain = row(lw["norm_mix"])

    bias = _rel_bias_table(lw["a_rel_bias"], cfg["cq"])
    if a_ck is not None:
        a_ck = a_ck.reshape(bsz, A_WIN, A_WIDTH)
        a_cv = a_cv.reshape(bsz, A_WIN, A_WIDTH)
    oa, kn, av = _attention(x, gain, w_a, row(lw["a_q_norm"]), row(lw["a_k_norm"]), bias,
                            a_ck, a_cv, tq=cfg["tq"], cq=cfg["cq"])
    keep = min(A_WIN, seq)
    new_ak = kn[:, seq - keep:].reshape(bsz, keep, A_HEADS, A_HEAD_DIM)
    new_av = av[:, seq - keep:].reshape(bsz, keep, A_HEADS, A_HEAD_DIM)

    cosf, sinf = _rotary_tables(pos0, seq)
    ob, new_ret = _retention(x, gain, w_b, cosf, sinf, ret_s0, tb=cfg["tb"])

    oc, new_rwkv, new_shift = _rwkv(x, gain, w_c, lw, shift_prev, rwkv_s0, tt=cfg["tt"], cc=cfg["cc"])

    rows = bsz * seq
    tm = min(cfg["tm"], rows)
    flat = lambda t: t.reshape(rows, t.shape[-1])
    x1 = _merge(flat(x), flat(oa), flat(ob), flat(oc), gain, w_g,
                lw["w_branch"].astype(BF16), lw["w_out"].astype(BF16), tm=tm)
    x2 = _ffn(x1, flat(p_l), row(lw["norm_ffn"]), lw["w_ffn_gate"].astype(BF16),
              lw["w_ffn_up"].astype(BF16), lw["w_ffn_down"].astype(BF16),
              lw["w_ple_proj"].astype(BF16), row(lw["ple_norm"]), lw["w_ple_gate"].astype(BF16),
              tm=tm)
    return x2.reshape(bsz, seq, D_MODEL), (new_ak, new_av, new_ret, new_rwkv, new_shift)


def _group_config(seq):
    if seq >= A_WIN:
        return dict(tq=A_WIN, cq=CHUNK, tb=256, tt=256, cc=CHUNK, tm=256)
    return dict(tq=seq, cq=seq, tb=seq, tt=seq, cc=seq, tm=256)


def kernel(x_prompt, x_sample, p_prompt, p_sample, cache_a_k, cache_a_v, state_ret, state_rwkv, state_rwkv_shift, norm_mix, w_in, a_q_norm, a_k_norm, a_rel_bias, c_shift_mu, c_w0, c_w2, c_a0, c_a2, c_g2, c_k_k, c_k_a, c_r_k, c_ln_w, c_ln_b, w_branch, w_out, norm_ffn, w_ffn_gate, w_ffn_up, w_ffn_down, w_ple_proj, ple_norm, w_ple_gate):
    depth = w_in.shape[0]

    def layer_weights(i):
        return dict(norm_mix=norm_mix[i], w_in=w_in[i], a_q_norm=a_q_norm[i], a_k_norm=a_k_norm[i],
                    a_rel_bias=a_rel_bias[i], c_shift_mu=c_shift_mu[i], c_w0=c_w0[i], c_w2=c_w2[i],
                    c_a0=c_a0[i], c_a2=c_a2[i], c_g2=c_g2[i], c_k_k=c_k_k[i], c_k_a=c_k_a[i], c_r_k=c_r_k[i],
                    c_ln_w=c_ln_w[i], c_ln_b=c_ln_b[i], w_branch=w_branch[i], w_out=w_out[i],
                    norm_ffn=norm_ffn[i], w_ffn_gate=w_ffn_gate[i], w_ffn_up=w_ffn_up[i],
                    w_ffn_down=w_ffn_down[i], w_ple_proj=w_ple_proj[i], ple_norm=ple_norm[i],
                    w_ple_gate=w_ple_gate[i])

    bp, lp, _ = x_prompt.shape
    cfg_p = _group_config(lp)
    ret0 = jnp.zeros((bp, B_HEADS, B_HEAD_DIM, B_HEAD_DIM), F32)
    rwkv0 = jnp.zeros((bp, C_HEADS, C_HEAD_DIM, C_HEAD_DIM), F32)
    shift0 = jnp.zeros((bp, 1, C_SHIFT_WIDTH), F32)
    y_prompt = x_prompt
    st_p = []
    for i in range(depth):
        y_prompt, st = _layer(y_prompt, p_prompt[i], 0, None, None, ret0, rwkv0, shift0,
                              layer_weights(i), cfg_p)
        st_p.append(st)

    cfg_s = _group_config(x_sample.shape[1])
    y_sample = x_sample
    st_s = []
    for i in range(depth):
        y_sample, st = _layer(y_sample, p_sample[i], PAST_LEN, cache_a_k[i], cache_a_v[i], state_ret[i],
                              state_rwkv[i], state_rwkv_shift[i], layer_weights(i), cfg_s)
        st_s.append(st)

    stack = lambda sts, j: jnp.stack([s[j] for s in sts])
    return (y_prompt, y_sample,
            stack(st_p, 0), stack(st_p, 1), stack(st_p, 2), stack(st_p, 3), stack(st_p, 4),
            stack(st_s, 0), stack(st_s, 1), stack(st_s, 2), stack(st_s, 3), stack(st_s, 4))
```

```python
import functools
import math

import jax
import jax.numpy as jnp
from jax import lax
from jax.experimental import pallas as pl
from jax.experimental.pallas import tpu as pltpu

F32 = jnp.float32
BF16 = jnp.bfloat16

D_MODEL = 1024
PAST_LEN = 2048
CHUNK = 64
NORM_EPS = 1e-6

A_HEADS = 8
A_HEAD_DIM = 64
A_WIDTH = 512
A_WIN = 512
A_REL_MAX = 256

B_HEADS = 4
B_HEAD_DIM = 128
B_WIDTH = 512
ROPE_BASE = 10000.0

C_HEADS = 8
C_HEAD_DIM = 64
C_WIDTH = 512
C_RANK_W = 64
C_RANK_A = 64
C_RANK_G = 128
C_SHIFT_WIDTH = 3 * C_WIDTH + C_RANK_W + C_RANK_A + C_RANK_G
C_GN_EPS = 64e-5

N_BRANCHES = 3

OFF_A = 0
OFF_B = 3 * A_WIDTH
OFF_C = OFF_B + 4 * B_WIDTH
OFF_G = OFF_C + C_SHIFT_WIDTH
IN_WIDTH = OFF_G + N_BRANCHES * D_MODEL

RET_LOG_GAMMA = tuple(math.log1p(-(2.0 ** (-5.0 - h))) for h in range(B_HEADS))

VMEM_LIMIT_BYTES = 56 * 1024 * 1024
MXU_TILE = 256
LOG2_E = math.log2(math.e)

NN = (((1,), (0,)), ((), ()))
NT = (((1,), (1,)), ((), ()))
TN = (((0,), (0,)), ((), ()))


def _dot(a, b, dims=NN):
    return lax.dot_general(a.astype(BF16), b.astype(BF16), dims, preferred_element_type=F32)


def _split_bf16(a):
    hi = a.astype(BF16)
    lo = (a - hi.astype(F32)).astype(BF16)
    return hi, lo


def _dot3(a, b, dims=NN):
    a_hi, a_lo = _split_bf16(a)
    b_hi, b_lo = _split_bf16(b)
    d = functools.partial(lax.dot_general, dimension_numbers=dims, preferred_element_type=F32)
    return d(a_hi, b_hi) + (d(a_hi, b_lo) + d(a_lo, b_hi))


def _rms(x):
    return x * lax.rsqrt(jnp.mean(x * x, axis=-1, keepdims=True) + NORM_EPS)


def _sigmoid(x):
    return 1.0 / (1.0 + jnp.exp(-x))


def _const_spec(shape):
    nd = len(shape)
    return pl.BlockSpec(shape, lambda *_: (0,) * nd, pipeline_mode=pl.Buffered(1))


def _head_sum_matrix(head_dim):
    head = jnp.arange(MXU_TILE, dtype=jnp.int32) // head_dim
    return (head[:, None] == head[None, :]).astype(BF16)


def _head_sum(x, hsum):
    w = hsum.shape[0]
    return jnp.concatenate([_dot(x[:, g * w:(g + 1) * w], hsum) for g in range(x.shape[1] // w)], axis=1)


def _attn_kernel(*refs, tq, cq, has_cache):
    if has_cache:
        (x_ref, g_ref, w_ref, qg_ref, kg_ref, bias_ref, hsum_ref, kc_ref, vc_ref,
         o_ref, kn_ref, v_ref, kwin, vwin, qs) = refs
    else:
        (x_ref, g_ref, w_ref, qg_ref, kg_ref, bias_ref, hsum_ref,
         o_ref, kn_ref, v_ref, kwin, vwin, qs) = refs
    i = pl.program_id(1)
    nkeys = A_WIN + cq
    pair_w = 2 * A_HEAD_DIM

    if has_cache:
        kwin[0:A_WIN, :] = kc_ref[0].astype(BF16)
        vwin[0:A_WIN, :] = vc_ref[0].astype(BF16)
    else:
        @pl.when(i == 0)
        def _():
            kwin[0:A_WIN, :] = jnp.zeros((A_WIN, A_WIDTH), BF16)
            vwin[0:A_WIN, :] = jnp.zeros((A_WIN, A_WIDTH), BF16)

    h = _rms(x_ref[0]) * g_ref[...]
    z = _dot(h, w_ref[...])
    q = z[:, 0:A_WIDTH]
    k = z[:, A_WIDTH:2 * A_WIDTH]
    v = z[:, 2 * A_WIDTH:3 * A_WIDTH]
    hsum = hsum_ref[...]
    inv_d = 1.0 / A_HEAD_DIM
    qn = q * lax.rsqrt(_head_sum(q * q, hsum) * inv_d + NORM_EPS) * qg_ref[...]
    kn = k * lax.rsqrt(_head_sum(k * k, hsum) * inv_d + NORM_EPS) * kg_ref[...]
    qs[...] = qn * (A_HEAD_DIM ** -0.5 * LOG2_E)
    kwin[A_WIN:A_WIN + tq, :] = kn.astype(BF16)
    vwin[A_WIN:A_WIN + tq, :] = v.astype(BF16)
    kn_ref[0] = kn
    v_ref[0] = v

    rho = lax.broadcasted_iota(jnp.int32, (2 * cq, pair_w), 0)
    lane = lax.broadcasted_iota(jnp.int32, (2 * cq, pair_w), 1)
    placed = (rho >> (cq.bit_length() - 1)) == (lane >> (A_HEAD_DIM.bit_length() - 1))

    n_chunks = tq // cq
    per_trip = 2 if n_chunks % 2 == 0 else 1
    units = [(c, p) for c in range(per_trip) for p in range(A_HEADS // 2)]
    ids = range(len(units))

    def chunk(j, carry):
        r0s = [pl.multiple_of((j * per_trip + c) * cq, cq) for c in range(per_trip)]
        lanes = [slice(p * pair_w, (p + 1) * pair_w) for _, p in units]
        qc = [qs[pl.ds(r0s[c], cq), lanes[u]] for u, (c, _) in enumerate(units)]
        qp = [jnp.where(placed, jnp.concatenate([qc[u], qc[u]], axis=0), 0.0) for u in ids]
        s = [_dot(qp[u], kwin[pl.ds(r0s[c], nkeys), lanes[u]], NT) + bias_ref[p]
             for u, (c, p) in enumerate(units)]
        if not has_cache:
            col = lax.broadcasted_iota(jnp.int32, (2 * cq, nkeys), 1)
            ok = [jnp.logical_or(r0 + col >= A_WIN, i > 0) for r0 in r0s]
            s = [jnp.where(ok[c], s[u], -1e30) for u, (c, _) in enumerate(units)]
        m = [jnp.max(s[u], axis=-1, keepdims=True) for u in ids]
        e = [jnp.exp2(s[u] - m[u]) for u in ids]
        den = [jnp.sum(e[u], axis=-1, keepdims=True) for u in ids]
        o2 = [_dot(e[u], vwin[pl.ds(r0s[c], nkeys), lanes[u]]) for u, (c, _) in enumerate(units)]
        for u, (c, _) in enumerate(units):
            o = jnp.where(placed, o2[u] / den[u], 0.0)
            o_ref[0, pl.ds(r0s[c], cq), lanes[u]] = o[:cq] + o[cq:]
        return carry

    lax.fori_loop(0, n_chunks // per_trip, chunk, 0)

    if not has_cache:
        kwin[0:A_WIN, :] = kwin[tq:tq + A_WIN, :]
        vwin[0:A_WIN, :] = vwin[tq:tq + A_WIN, :]


def _attention(x, gain, w_a, q_gain, k_gain, bias, cache_k, cache_v, layer, *, tq, cq):
    bsz, seq, _ = x.shape
    has_cache = cache_k is not None
    assert seq % tq == 0 and tq % cq == 0
    assert has_cache or tq == A_WIN
    nkeys = A_WIN + cq
    tok = lambda w: pl.BlockSpec((1, tq, w), lambda b, i: (b, i, 0))
    in_specs = [tok(D_MODEL), _const_spec((1, D_MODEL)), _const_spec((D_MODEL, 3 * A_WIDTH)),
                _const_spec((1, A_WIDTH)), _const_spec((1, A_WIDTH)),
                _const_spec((A_HEADS // 2, 2 * cq, nkeys)), _const_spec((MXU_TILE, MXU_TILE))]
    args = [x, gain, w_a, jnp.tile(q_gain, (1, A_HEADS)), jnp.tile(k_gain, (1, A_HEADS)),
            bias.reshape(A_HEADS // 2, 2 * cq, nkeys), _head_sum_matrix(A_HEAD_DIM)]
    if has_cache:
        cspec = pl.BlockSpec((None, 1, A_WIN, A_WIDTH), lambda b, i: (layer, b, 0, 0))
        in_specs += [cspec, cspec]
        args += [cache_k, cache_v]
    out = jax.ShapeDtypeStruct((bsz, seq, A_WIDTH), F32)
    return pl.pallas_call(
        functools.partial(_attn_kernel, tq=tq, cq=cq, has_cache=has_cache),
        grid=(bsz, seq // tq),
        in_specs=in_specs,
        out_specs=[tok(A_WIDTH)] * 3,
        out_shape=[out] * 3,
        scratch_shapes=[pltpu.VMEM((A_WIN + tq, A_WIDTH), BF16),
                        pltpu.VMEM((A_WIN + tq, A_WIDTH), BF16),
                        pltpu.VMEM((tq, A_WIDTH), F32)],
        compiler_params=pltpu.CompilerParams(
            dimension_semantics=("arbitrary", "arbitrary"),
            vmem_limit_bytes=VMEM_LIMIT_BYTES),
        name="attention",
    )(*args)


def _ret_kernel(x_ref, g_ref, w_ref, cos_ref, sin_ref, s0_ref, o_ref, sout_ref, s_scr, *, tb):
    i = pl.program_id(1)

    @pl.when(i == 0)
    def _():
        s_scr[...] = s0_ref[0]

    h = _rms(x_ref[0]) * g_ref[...]
    z = _dot(h, w_ref[...])
    cosf = cos_ref[...]
    sinf = sin_ref[...]
    row = lax.broadcasted_iota(jnp.int32, (tb, tb), 0)
    col = lax.broadcasted_iota(jnp.int32, (tb, tb), 1)
    diff = row - col
    causal = diff >= 0
    dist = jnp.maximum(diff, 0).astype(F32)
    n = lax.broadcasted_iota(jnp.int32, (tb, 1), 0).astype(F32)
    outs = []
    for hh in range(B_HEADS):
        lg = RET_LOG_GAMMA[hh]
        lo = hh * B_HEAD_DIM
        q = z[:, lo:lo + B_HEAD_DIM]
        k = z[:, B_WIDTH + lo:B_WIDTH + lo + B_HEAD_DIM]
        v = z[:, 2 * B_WIDTH + lo:2 * B_WIDTH + lo + B_HEAD_DIM]
        gate = z[:, 3 * B_WIDTH + lo:3 * B_WIDTH + lo + B_HEAD_DIM]
        q = q * cosf + pltpu.roll(q, B_HEAD_DIM // 2, 1) * sinf
        k = (k * cosf + pltpu.roll(k, B_HEAD_DIM // 2, 1) * sinf) * (B_HEAD_DIM ** -0.5)
        dmat = jnp.where(causal, jnp.exp(lg * dist), 0.0)
        scores = _dot(q, k, NT) * dmat
        s_old = s_scr[hh]
        o = _dot(scores, v) + _dot(q, s_old) * jnp.exp(lg * (n + 1.0))
        k_dec = k * jnp.exp(lg * ((tb - 1.0) - n))
        s_scr[hh] = math.exp(lg * tb) * s_old + _dot(k_dec, v, TN)
        outs.append(_rms(o) * (gate * _sigmoid(gate)))
    o_ref[0] = jnp.concatenate(outs, axis=-1)
    sout_ref[0] = s_scr[...]


def _retention(x, gain, w_b, cosf, sinf, s0, *, tb):
    bsz, seq, _ = x.shape
    assert seq % tb == 0
    sspec = pl.BlockSpec((1, B_HEADS, B_HEAD_DIM, B_HEAD_DIM), lambda b, i: (b, 0, 0, 0))
    return pl.pallas_call(
        functools.partial(_ret_kernel, tb=tb),
        grid=(bsz, seq // tb),
        in_specs=[pl.BlockSpec((1, tb, D_MODEL), lambda b, i: (b, i, 0)),
                  _const_spec((1, D_MODEL)), _const_spec((D_MODEL, 4 * B_WIDTH)),
                  pl.BlockSpec((tb, B_HEAD_DIM), lambda b, i: (i, 0)),
                  pl.BlockSpec((tb, B_HEAD_DIM), lambda b, i: (i, 0)),
                  sspec],
        out_specs=[pl.BlockSpec((1, tb, B_WIDTH), lambda b, i: (b, i, 0)), sspec],
        out_shape=[jax.ShapeDtypeStruct((bsz, seq, B_WIDTH), F32),
                   jax.ShapeDtypeStruct((bsz, B_HEADS, B_HEAD_DIM, B_HEAD_DIM), F32)],
        scratch_shapes=[pltpu.VMEM((B_HEADS, B_HEAD_DIM, B_HEAD_DIM), F32)],
        compiler_params=pltpu.CompilerParams(
            dimension_semantics=("arbitrary", "arbitrary"),
            vmem_limit_bytes=VMEM_LIMIT_BYTES),
        name="retention",
    )(x, gain, w_b, cosf, sinf, s0)


PAIR_W = 2 * C_HEAD_DIM
N_PAIRS = C_HEADS // 2


def _dot2(a, b, dims=NN):
    a_hi, a_lo = _split_bf16(a)
    b = b.astype(BF16)
    d = functools.partial(lax.dot_general, dimension_numbers=dims, preferred_element_type=F32)
    return d(a_hi, b) + d(a_lo, b)


def _rwkv_kernel(x_ref, g_ref, w_ref, mu_ref, sh0_ref, s0_ref, w0_ref, w2_ref, a0_ref, a2_ref,
                 g2_ref, kk_ref, ka_ref, rk_ref, lnw_ref, lnb_ref, hsum_ref,
                 o_ref, sout_ref, shout_ref,
                 s_scr, carry, rt_s, kkt_s, kh_s, bh_s, v_s, bon_s, g_s, cum_s, *, tt, cc):
    i = pl.program_id(1)

    @pl.when(i == 0)
    def _():
        s_scr[...] = s0_ref[0]
        carry[...] = sh0_ref[0]

    h = _rms(x_ref[0]) * g_ref[...]
    cz = _dot(h, w_ref[...])
    rows = lax.broadcasted_iota(jnp.int32, (tt, 1), 0)
    prev = jnp.where(rows == 0, carry[...], pltpu.roll(cz, 1, 0))
    cs = cz + (prev - cz) * mu_ref[...]
    last = cz[tt - 1:tt, :]
    carry[...] = last
    shout_ref[0] = last

    r = cs[:, 0:C_WIDTH]
    k = cs[:, C_WIDTH:2 * C_WIDTH]
    v = cs[:, 2 * C_WIDTH:3 * C_WIDTH]
    off = 3 * C_WIDTH
    w_lo = cs[:, off:off + C_RANK_W]
    a_lo = cs[:, off + C_RANK_W:off + C_RANK_W + C_RANK_A]
    g_lo = cs[:, off + C_RANK_W + C_RANK_A:C_SHIFT_WIDTH]

    lw = -math.exp(-0.5) * _sigmoid(w0_ref[...] + _dot(jnp.tanh(w_lo), w2_ref[...]))
    a = _sigmoid(a0_ref[...] + _dot(a_lo, a2_ref[...]))
    g = _dot(_sigmoid(g_lo), g2_ref[...])
    hsum = hsum_ref[...]
    kk_raw = k * kk_ref[...]
    kk = kk_raw / jnp.maximum(jnp.sqrt(_head_sum(kk_raw * kk_raw, hsum)), 1e-12)
    k2 = k * (1.0 + (a - 1.0) * ka_ref[...])

    pos = jnp.bitwise_and(rows, cc - 1)
    cum = lw
    step = 1
    while step < cc:
        cum = cum + jnp.where(pos >= step, pltpu.roll(cum, step, 0), 0.0)
        step *= 2
    e_inv = jnp.exp(-cum)
    rt_s[...] = r * jnp.exp(cum)
    kkt_s[...] = kk * jnp.exp(cum - lw)
    kh_s[...] = k2 * e_inv
    bh_s[...] = kk * a * e_inv
    v_s[...] = v
    bon_s[...] = _head_sum(r * k2 * rk_ref[...], hsum) * v
    g_s[...] = g
    cum_s[...] = cum

    two = 2 * cc
    sh = cc.bit_length() - 1
    hd = C_HEAD_DIM.bit_length() - 1
    n_double = sh - 1
    rho = lax.broadcasted_iota(jnp.int32, (two, PAIR_W), 0)
    lane = lax.broadcasted_iota(jnp.int32, (two, PAIR_W), 1)
    placed = (rho >> sh) == (lane >> hd)
    r2 = lax.broadcasted_iota(jnp.int32, (two, two), 0)
    c2 = lax.broadcasted_iota(jnp.int32, (two, two), 1)
    same = (r2 >> sh) == (c2 >> sh)
    strict = jnp.logical_and(same, r2 > c2)
    incl = jnp.logical_and(same, r2 >= c2)
    eye = jnp.where(r2 == c2, 1.0, 0.0).astype(F32)
    gr = lax.broadcasted_iota(jnp.int32, (PAIR_W, PAIR_W), 0)
    gc = lax.broadcasted_iota(jnp.int32, (PAIR_W, PAIR_W), 1)
    hmean = jnp.where((gr >> hd) == (gc >> hd), 1.0 / C_HEAD_DIM, 0.0).astype(BF16)

    def place(xv):
        return jnp.where(placed, jnp.concatenate([xv, xv], axis=0), 0.0)

    n_chunks = tt // cc
    per_trip = 2 if n_chunks % 2 == 0 else 1
    units = [(c, p) for c in range(per_trip) for p in range(N_PAIRS)]
    pairs = range(len(units))

    def chunk(j, c_):
        r0s = [pl.multiple_of((j * per_trip + c) * cc, cc) for c in range(per_trip)]
        w_end = [jnp.exp(cum_s[pl.ds(r0 + cc - 1, 1), :]) for r0 in r0s]
        lanes = [slice(p * PAIR_W, (p + 1) * PAIR_W) for _, p in units]
        sl = [(pl.ds(r0s[c], cc), lanes[u]) for u, (c, _) in enumerate(units)]
        we = [w_end[c][:, lanes[u]] for u, (c, _) in enumerate(units)]
        rt = [place(rt_s[sl[p]]) for p in pairs]
        kkt = [place(kkt_s[sl[p]]) for p in pairs]
        kh = [place(kh_s[sl[p]]) for p in pairs]
        bh = [place(bh_s[sl[p]]) for p in pairs]
        vv = [place(v_s[sl[p]]) for p in pairs]
        cat0 = lambda *xs: jnp.concatenate(xs, axis=0)
        cat1 = lambda *xs: jnp.concatenate(xs, axis=1)
        gg = [_dot(cat0(kkt[p], rt[p]), cat0(kh[p], bh[p]), NT) for p in pairs]
        a_kk = [jnp.where(strict, gg[p][:two, :two], 0.0) for p in pairs]
        b_kk = [jnp.where(incl, gg[p][two:, :two], 0.0) for p in pairs]
        b_bb = [jnp.where(incl, gg[p][two:, two:], 0.0) for p in pairs]
        pw = [jnp.where(strict, -gg[p][:two, two:], 0.0) for p in pairs]
        tinv = [eye + pw[p] for p in pairs]
        pw = [_dot(pw[p], pw[p]) for p in pairs]
        for _ in range(n_double - 1):
            pt = [_dot(pw[p], cat1(pw[p], tinv[p])) for p in pairs]
            pw = [pt[p][:, :two] for p in pairs]
            tinv = [tinv[p] + pt[p][:, two:] for p in pairs]
        tinv = [tinv[p] + _dot(pw[p], tinv[p]) for p in pairs]
        av = [_dot(a_kk[p], vv[p]) for p in pairs]
        kv = [_dot(tinv[p], cat1(kkt[p], av[p])) for p in pairs]
        zero = jnp.zeros((two, PAIR_W), F32)
        ry = [_dot(cat1(b_kk[p], -b_bb[p]), cat0(cat1(zero, vv[p]), kv[p])) for p in pairs]
        rp = [rt[p] + ry[p][:, :PAIR_W] for p in pairs]
        y0 = [ry[p][:, PAIR_W:] for p in pairs]
        kkp = [kv[p][:, :PAIR_W] for p in pairs]
        vp = [kv[p][:, PAIR_W:] for p in pairs]
        bd = [bh[p] * we[p] for p in pairs]
        kd = [kh[p] * we[p] for p in pairs]
        q = [_dot(kkp[p], bd[p], TN) for p in pairs]
        z = [_dot(cat0(vv[p], vp[p]), cat0(kd[p], -bd[p]), TN) for p in pairs]
        qs = [_split_bf16(q[p]) for p in pairs]
        qq = [cat1(qs[p][0], qs[p][1]) for p in pairs]
        state = [s_scr[p] for p in range(N_PAIRS)]
        y2 = []
        mm = functools.partial(lax.dot_general, dimension_numbers=NN, preferred_element_type=F32)
        for u, (_, p) in enumerate(units):
            s_old = state[p]
            y2.append(_dot(rp[u], s_old, NT) + y0[u])
            s_hi, s_lo = _split_bf16(s_old)
            sq = mm(s_hi, qq[u])
            state[p] = s_old * we[u] - (sq[:, :PAIR_W] + (sq[:, PAIR_W:] + mm(s_lo, qs[u][0]))) + z[u]
        for p in range(N_PAIRS):
            s_scr[p] = state[p]
        ys = [y2[u][:cc] + y2[u][cc:] for u in pairs]
        yc = [ys[u] - _dot(ys[u], hmean) for u in pairs]
        var = [_dot(yc[u] * yc[u], hmean) for u in pairs]
        for u in pairs:
            yn = yc[u] * lax.rsqrt(var[u] + C_GN_EPS) * lnw_ref[:, lanes[u]] + lnb_ref[:, lanes[u]]
            o_ref[(0,) + sl[u]] = (yn + bon_s[sl[u]]) * g_s[sl[u]]
        return c_

    lax.fori_loop(0, n_chunks // per_trip, chunk, 0)
    sout_ref[0] = s_scr[...]


def _pair_states(s):
    bsz = s.shape[0]
    s = s.reshape(bsz, N_PAIRS, 2, C_HEAD_DIM, C_HEAD_DIM)
    zero = jnp.zeros_like(s[:, :, 0])
    top = jnp.concatenate([s[:, :, 0], zero], axis=-1)
    bot = jnp.concatenate([zero, s[:, :, 1]], axis=-1)
    return jnp.concatenate([top, bot], axis=-2)


def _unpair_states(s):
    bsz = s.shape[0]
    d = C_HEAD_DIM
    return jnp.stack([s[:, :, :d, :d], s[:, :, d:, d:]], axis=2).reshape(bsz, C_HEADS, d, d)


def _rwkv(x, gain, w_c, lw, shift0, s0, *, tt, cc):
    bsz, seq, _ = x.shape
    assert seq % tt == 0 and tt % cc == 0 and cc & (cc - 1) == 0
    row = lambda a: a.reshape(1, -1)
    sspec = pl.BlockSpec((1, N_PAIRS, PAIR_W, PAIR_W), lambda b, i: (b, 0, 0, 0))
    shspec = pl.BlockSpec((1, 1, C_SHIFT_WIDTH), lambda b, i: (b, 0, 0))
    vec = _const_spec((1, C_WIDTH))
    tile = pltpu.VMEM((tt, C_WIDTH), F32)
    hsum = _head_sum_matrix(C_HEAD_DIM)
    oc, s_new, shift_new = pl.pallas_call(
        functools.partial(_rwkv_kernel, tt=tt, cc=cc),
        grid=(bsz, seq // tt),
        in_specs=[pl.BlockSpec((1, tt, D_MODEL), lambda b, i: (b, i, 0)),
                  _const_spec((1, D_MODEL)), _const_spec((D_MODEL, C_SHIFT_WIDTH)),
                  _const_spec((1, C_SHIFT_WIDTH)), shspec, sspec,
                  vec, _const_spec((C_RANK_W, C_WIDTH)), vec, _const_spec((C_RANK_A, C_WIDTH)),
                  _const_spec((C_RANK_G, C_WIDTH)), vec, vec, vec, vec, vec,
                  _const_spec((MXU_TILE, MXU_TILE))],
        out_specs=[pl.BlockSpec((1, tt, C_WIDTH), lambda b, i: (b, i, 0)), sspec, shspec],
        out_shape=[jax.ShapeDtypeStruct((bsz, seq, C_WIDTH), F32),
                   jax.ShapeDtypeStruct((bsz, N_PAIRS, PAIR_W, PAIR_W), F32),
                   jax.ShapeDtypeStruct((bsz, 1, C_SHIFT_WIDTH), F32)],
        scratch_shapes=[pltpu.VMEM((N_PAIRS, PAIR_W, PAIR_W), F32),
                        pltpu.VMEM((1, C_SHIFT_WIDTH), F32)] + [tile] * 8,
        compiler_params=pltpu.CompilerParams(
            dimension_semantics=("arbitrary", "arbitrary"),
            vmem_limit_bytes=VMEM_LIMIT_BYTES),
        name="rwkv",
    )(x, gain, w_c, row(lw["c_shift_mu"]), shift0, _pair_states(s0),
      row(lw["c_w0"]), lw["c_w2"].astype(BF16), row(lw["c_a0"]), lw["c_a2"].astype(BF16),
      lw["c_g2"].astype(BF16), row(lw["c_k_k"]), row(lw["c_k_a"]), row(lw["c_r_k"]),
      row(lw["c_ln_w"]), row(lw["c_ln_b"]), hsum)
    return oc, _unpair_states(s_new), shift_new


def _merge_kernel(x_ref, oa_ref, ob_ref, oc_ref, g_ref, wg_ref, wb_ref, wo_ref, y_ref):
    x = x_ref[...]
    h = _rms(x) * g_ref[...]
    gl = _dot(h, wg_ref[...])
    m = None
    for b, o_ref in enumerate((oa_ref, ob_ref, oc_ref)):
        t = _sigmoid(gl[:, b * D_MODEL:(b + 1) * D_MODEL]) * _dot(o_ref[...], wb_ref[b])
        m = t if m is None else m + t
    y_ref[...] = x + _dot(m, wo_ref[...])


def _merge(x, oa, ob, oc, gain, w_g, w_b, w_o, *, tm):
    rows = x.shape[0]
    assert rows % tm == 0
    tok = lambda w: pl.BlockSpec((tm, w), lambda i: (i, 0))
    return pl.pallas_call(
        _merge_kernel,
        grid=(rows // tm,),
        in_specs=[tok(D_MODEL), tok(A_WIDTH), tok(B_WIDTH), tok(C_WIDTH),
                  _const_spec((1, D_MODEL)), _const_spec((D_MODEL, N_BRANCHES * D_MODEL)),
                  _const_spec((N_BRANCHES, A_WIDTH, D_MODEL)), _const_spec((D_MODEL, D_MODEL))],
        out_specs=tok(D_MODEL),
        out_shape=jax.ShapeDtypeStruct((rows, D_MODEL), F32),
        compiler_params=pltpu.CompilerParams(
            dimension_semantics=("arbitrary",), vmem_limit_bytes=VMEM_LIMIT_BYTES),
        name="merge",
    )(x, oa, ob, oc, gain, w_g, w_b, w_o)


def _ffn_kernel(x_ref, p_ref, g_ref, wgate_ref, wup_ref, wdown_ref, wpp_ref, pg_ref, wpg_ref, y_ref):
    x = x_ref[...]
    hf = (_rms(x) * g_ref[...]).astype(BF16)
    gate = _dot(hf, wgate_ref[...])
    up = _dot(hf, wup_ref[...])
    x = x + _dot(gate * _sigmoid(gate) * up, wdown_ref[...])
    e = _rms(_dot(p_ref[...], wpp_ref[...])) * pg_ref[...]
    y_ref[...] = x + _sigmoid(_dot(_rms(x), wpg_ref[...])) * e


def _ffn(x, p_all, layer, gain, w_gate, w_up, w_down, w_pp, p_gain, w_pg, *, tm):
    rows = x.shape[0]
    d_ff = w_gate.shape[1]
    ple = p_all.shape[-1]
    p = p_all.reshape(p_all.shape[0], rows, ple)
    assert rows % tm == 0
    tok = lambda w: pl.BlockSpec((tm, w), lambda i: (i, 0))
    return pl.pallas_call(
        _ffn_kernel,
        grid=(rows // tm,),
        in_specs=[tok(D_MODEL), pl.BlockSpec((None, tm, ple), lambda i: (layer, i, 0)),
                  _const_spec((1, D_MODEL)),
                  _const_spec((D_MODEL, d_ff)), _const_spec((D_MODEL, d_ff)),
                  _const_spec((d_ff, D_MODEL)), _const_spec((ple, D_MODEL)),
                  _const_spec((1, D_MODEL)), _const_spec((D_MODEL, D_MODEL))],
        out_specs=tok(D_MODEL),
        out_shape=jax.ShapeDtypeStruct((rows, D_MODEL), F32),
        compiler_params=pltpu.CompilerParams(
            dimension_semantics=("arbitrary",), vmem_limit_bytes=VMEM_LIMIT_BYTES),
        name="ffn",
    )(x, p, gain, w_gate, w_up, w_down, w_pp, p_gain, w_pg)


def _rel_bias_table(rel_bias, cq):
    nkeys = A_WIN + cq
    t_max = A_WIN + CHUNK - 1 + cq - 1
    heads, n_rel = rel_bias.shape
    w = t_max + 1
    tail = jnp.broadcast_to(rel_bias[:, n_rel - 1:], (heads, w - n_rel))
    rev = jnp.concatenate([tail, rel_bias[:, ::-1].astype(F32), jnp.zeros((heads, 1), F32)], axis=1)
    skew = jnp.tile(rev, (1, cq))[:, :cq * w].reshape(heads, cq, w)
    return skew[:, :, cq - 1:cq - 1 + nkeys] * LOG2_E


def _rotary_tables(pos0, seq):
    half = B_HEAD_DIM // 2
    pos = pos0 + jnp.arange(seq, dtype=jnp.int32)
    inv = ROPE_BASE ** (-jnp.arange(half, dtype=F32) / half)
    ang = pos.astype(F32)[:, None] * inv[None, :]
    cos, sin = jnp.cos(ang), jnp.sin(ang)
    return jnp.concatenate([cos, cos], axis=-1), jnp.concatenate([-sin, sin], axis=-1)


def _layer(x, p_all, layer, pos0, a_ck, a_cv, ret_s0, rwkv_s0, shift_prev, lw, cfg):
    bsz, seq, _ = x.shape
    row = lambda a: a.reshape(1, -1)
    w_in = lw["w_in"].astype(BF16)
    w_a = w_in[:, OFF_A:OFF_B]
    w_b = w_in[:, OFF_B:OFF_C]
    w_c = w_in[:, OFF_C:OFF_G]
    w_g = w_in[:, OFF_G:]
    gain = row(lw["norm_mix"])

    bias = _rel_bias_table(lw["a_rel_bias"], cfg["cq"])
    if a_ck is not None:
        a_ck = a_ck.reshape(a_ck.shape[0], bsz, A_WIN, A_WIDTH)
        a_cv = a_cv.reshape(a_cv.shape[0], bsz, A_WIN, A_WIDTH)
    oa, kn, av = _attention(x, gain, w_a, row(lw["a_q_norm"]), row(lw["a_k_norm"]), bias,
                            a_ck, a_cv, layer, tq=cfg["tq"], cq=cfg["cq"])
    keep = min(A_WIN, seq)
    new_ak = kn[:, seq - keep:].reshape(bsz, keep, A_HEADS, A_HEAD_DIM)
    new_av = av[:, seq - keep:].reshape(bsz, keep, A_HEADS, A_HEAD_DIM)

    cosf, sinf = _rotary_tables(pos0, seq)
    ob, new_ret = _retention(x, gain, w_b, cosf, sinf, ret_s0, tb=cfg["tb"])

    oc, new_rwkv, new_shift = _rwkv(x, gain, w_c, lw, shift_prev, rwkv_s0, tt=cfg["tt"], cc=cfg["cc"])

    rows = bsz * seq
    tm = min(cfg["tm"], rows)
    flat = lambda t: t.reshape(rows, t.shape[-1])
    x1 = _merge(flat(x), flat(oa), flat(ob), flat(oc), gain, w_g,
                lw["w_branch"].astype(BF16), lw["w_out"].astype(BF16), tm=tm)
    x2 = _ffn(x1, p_all, layer, row(lw["norm_ffn"]), lw["w_ffn_gate"].astype(BF16),
              lw["w_ffn_up"].astype(BF16), lw["w_ffn_down"].astype(BF16),
              lw["w_ple_proj"].astype(BF16), row(lw["ple_norm"]), lw["w_ple_gate"].astype(BF16),
              tm=tm)
    return x2.reshape(bsz, seq, D_MODEL), (new_ak, new_av, new_ret, new_rwkv, new_shift)


def _group_config(seq):
    if seq >= A_WIN:
        return dict(tq=A_WIN, cq=CHUNK, tb=256, tt=256, cc=CHUNK, tm=256)
    return dict(tq=seq, cq=seq, tb=seq, tt=seq, cc=seq, tm=256)


def kernel(x_prompt, x_sample, p_prompt, p_sample, cache_a_k, cache_a_v, state_ret, state_rwkv, state_rwkv_shift, norm_mix, w_in, a_q_norm, a_k_norm, a_rel_bias, c_shift_mu, c_w0, c_w2, c_a0, c_a2, c_g2, c_k_k, c_k_a, c_r_k, c_ln_w, c_ln_b, w_branch, w_out, norm_ffn, w_ffn_gate, w_ffn_up, w_ffn_down, w_ple_proj, ple_norm, w_ple_gate):
    depth = w_in.shape[0]

    def layer_weights(i):
        return dict(norm_mix=norm_mix[i], w_in=w_in[i], a_q_norm=a_q_norm[i], a_k_norm=a_k_norm[i],
                    a_rel_bias=a_rel_bias[i], c_shift_mu=c_shift_mu[i], c_w0=c_w0[i], c_w2=c_w2[i],
                    c_a0=c_a0[i], c_a2=c_a2[i], c_g2=c_g2[i], c_k_k=c_k_k[i], c_k_a=c_k_a[i], c_r_k=c_r_k[i],
                    c_ln_w=c_ln_w[i], c_ln_b=c_ln_b[i], w_branch=w_branch[i], w_out=w_out[i],
                    norm_ffn=norm_ffn[i], w_ffn_gate=w_ffn_gate[i], w_ffn_up=w_ffn_up[i],
                    w_ffn_down=w_ffn_down[i], w_ple_proj=w_ple_proj[i], ple_norm=ple_norm[i],
                    w_ple_gate=w_ple_gate[i])

    bp, lp, _ = x_prompt.shape
    cfg_p = _group_config(lp)
    ret0 = jnp.zeros((bp, B_HEADS, B_HEAD_DIM, B_HEAD_DIM), F32)
    rwkv0 = jnp.zeros((bp, C_HEADS, C_HEAD_DIM, C_HEAD_DIM), F32)
    shift0 = jnp.zeros((bp, 1, C_SHIFT_WIDTH), F32)
    y_prompt = x_prompt
    st_p = []
    for i in range(depth):
        y_prompt, st = _layer(y_prompt, p_prompt, i, 0, None, None, ret0, rwkv0, shift0,
                              layer_weights(i), cfg_p)
        st_p.append(st)

    cfg_s = _group_config(x_sample.shape[1])
    y_sample = x_sample
    st_s = []
    for i in range(depth):
        y_sample, st = _layer(y_sample, p_sample, i, PAST_LEN, cache_a_k, cache_a_v, state_ret[i],
                              state_rwkv[i], state_rwkv_shift[i], layer_weights(i), cfg_s)
        st_s.append(st)

    stack = lambda sts, j: jnp.stack([s[j] for s in sts])
    return (y_prompt, y_sample,
            stack(st_p, 0), stack(st_p, 1), stack(st_p, 2), stack(st_p, 3), stack(st_p, 4),
            stack(st_s, 0), stack(st_s, 1), stack(st_s, 2), stack(st_s, 3), stack(st_s, 4))
```

```python
import functools
import math

import jax
import jax.numpy as jnp
from jax import lax
from jax.experimental import pallas as pl
from jax.experimental.pallas import tpu as pltpu

F32 = jnp.float32
BF16 = jnp.bfloat16

D_MODEL = 1024
PAST_LEN = 2048
CHUNK = 64
NORM_EPS = 1e-6

A_HEADS = 8
A_HEAD_DIM = 64
A_WIDTH = 512
A_WIN = 512
A_REL_MAX = 256

B_HEADS = 4
B_HEAD_DIM = 128
B_WIDTH = 512
ROPE_BASE = 10000.0

C_HEADS = 8
C_HEAD_DIM = 64
C_WIDTH = 512
C_RANK_W = 64
C_RANK_A = 64
C_RANK_G = 128
C_SHIFT_WIDTH = 3 * C_WIDTH + C_RANK_W + C_RANK_A + C_RANK_G
C_GN_EPS = 64e-5

N_BRANCHES = 3

OFF_A = 0
OFF_B = 3 * A_WIDTH
OFF_C = OFF_B + 4 * B_WIDTH
OFF_G = OFF_C + C_SHIFT_WIDTH
IN_WIDTH = OFF_G + N_BRANCHES * D_MODEL

RET_LOG_GAMMA = tuple(math.log1p(-(2.0 ** (-5.0 - h))) for h in range(B_HEADS))

VMEM_LIMIT_BYTES = 56 * 1024 * 1024
MXU_TILE = 256
LOG2_E = math.log2(math.e)

NN = (((1,), (0,)), ((), ()))
NT = (((1,), (1,)), ((), ()))
TN = (((0,), (0,)), ((), ()))


def _dot(a, b, dims=NN):
    return lax.dot_general(a.astype(BF16), b.astype(BF16), dims, preferred_element_type=F32)


def _split_bf16(a):
    hi = a.astype(BF16)
    lo = (a - hi.astype(F32)).astype(BF16)
    return hi, lo


def _dot3(a, b, dims=NN):
    a_hi, a_lo = _split_bf16(a)
    b_hi, b_lo = _split_bf16(b)
    d = functools.partial(lax.dot_general, dimension_numbers=dims, preferred_element_type=F32)
    return d(a_hi, b_hi) + (d(a_hi, b_lo) + d(a_lo, b_hi))


def _rms(x):
    return x * lax.rsqrt(jnp.mean(x * x, axis=-1, keepdims=True) + NORM_EPS)


def _sigmoid(x):
    return 1.0 / (1.0 + jnp.exp(-x))


def _const_spec(shape):
    nd = len(shape)
    return pl.BlockSpec(shape, lambda *_: (0,) * nd, pipeline_mode=pl.Buffered(1))


def _head_sum_matrix(head_dim):
    head = jnp.arange(MXU_TILE, dtype=jnp.int32) // head_dim
    return (head[:, None] == head[None, :]).astype(BF16)


def _head_sum(x, hsum):
    w = hsum.shape[0]
    return jnp.concatenate([_dot(x[:, g * w:(g + 1) * w], hsum) for g in range(x.shape[1] // w)], axis=1)


def _attn_kernel(*refs, tq, cq, has_cache):
    if has_cache:
        (x_ref, g_ref, w_ref, qg_ref, kg_ref, bias_ref, hsum_ref, kc_ref, vc_ref,
         o_ref, kn_ref, v_ref, kwin, vwin, qs) = refs
    else:
        (x_ref, g_ref, w_ref, qg_ref, kg_ref, bias_ref, hsum_ref,
         o_ref, kn_ref, v_ref, kwin, vwin, qs) = refs
    i = pl.program_id(1)
    nkeys = A_WIN + cq
    pair_w = 2 * A_HEAD_DIM

    if has_cache:
        kwin[0:A_WIN, :] = kc_ref[0].astype(BF16)
        vwin[0:A_WIN, :] = vc_ref[0].astype(BF16)
    else:
        @pl.when(i == 0)
        def _():
            kwin[0:A_WIN, :] = jnp.zeros((A_WIN, A_WIDTH), BF16)
            vwin[0:A_WIN, :] = jnp.zeros((A_WIN, A_WIDTH), BF16)

    h = _rms(x_ref[0]) * g_ref[...]
    z = _dot(h, w_ref[...])
    q = z[:, 0:A_WIDTH]
    k = z[:, A_WIDTH:2 * A_WIDTH]
    v = z[:, 2 * A_WIDTH:3 * A_WIDTH]
    hsum = hsum_ref[...]
    inv_d = 1.0 / A_HEAD_DIM
    qn = q * lax.rsqrt(_head_sum(q * q, hsum) * inv_d + NORM_EPS) * qg_ref[...]
    kn = k * lax.rsqrt(_head_sum(k * k, hsum) * inv_d + NORM_EPS) * kg_ref[...]
    qs[...] = qn * (A_HEAD_DIM ** -0.5 * LOG2_E)
    kwin[A_WIN:A_WIN + tq, :] = kn.astype(BF16)
    vwin[A_WIN:A_WIN + tq, :] = v.astype(BF16)
    kn_ref[0] = kn
    v_ref[0] = v

    rho = lax.broadcasted_iota(jnp.int32, (2 * cq, pair_w), 0)
    lane = lax.broadcasted_iota(jnp.int32, (2 * cq, pair_w), 1)
    placed = (rho >> (cq.bit_length() - 1)) == (lane >> (A_HEAD_DIM.bit_length() - 1))

    n_chunks = tq // cq
    per_trip = 2 if n_chunks % 2 == 0 else 1
    units = [(c, p) for c in range(per_trip) for p in range(A_HEADS // 2)]
    ids = range(len(units))

    def chunk(j, carry):
        r0s = [pl.multiple_of((j * per_trip + c) * cq, cq) for c in range(per_trip)]
        lanes = [slice(p * pair_w, (p + 1) * pair_w) for _, p in units]
        qc = [qs[pl.ds(r0s[c], cq), lanes[u]] for u, (c, _) in enumerate(units)]
        qp = [jnp.where(placed, jnp.concatenate([qc[u], qc[u]], axis=0), 0.0) for u in ids]
        s = [_dot(qp[u], kwin[pl.ds(r0s[c], nkeys), lanes[u]], NT) + bias_ref[p]
             for u, (c, p) in enumerate(units)]
        if not has_cache:
            col = lax.broadcasted_iota(jnp.int32, (2 * cq, nkeys), 1)
            ok = [jnp.logical_or(r0 + col >= A_WIN, i > 0) for r0 in r0s]
            s = [jnp.where(ok[c], s[u], -1e30) for u, (c, _) in enumerate(units)]
        m = [jnp.max(s[u], axis=-1, keepdims=True) for u in ids]
        e = [jnp.exp2(s[u] - m[u]) for u in ids]
        den = [jnp.sum(e[u], axis=-1, keepdims=True) for u in ids]
        o2 = [_dot(e[u], vwin[pl.ds(r0s[c], nkeys), lanes[u]]) for u, (c, _) in enumerate(units)]
        for u, (c, _) in enumerate(units):
            o = jnp.where(placed, o2[u] / den[u], 0.0)
            o_ref[0, pl.ds(r0s[c], cq), lanes[u]] = o[:cq] + o[cq:]
        return carry

    lax.fori_loop(0, n_chunks // per_trip, chunk, 0)

    if not has_cache:
        kwin[0:A_WIN, :] = kwin[tq:tq + A_WIN, :]
        vwin[0:A_WIN, :] = vwin[tq:tq + A_WIN, :]


def _attention(x, gain, w_a, q_gain, k_gain, bias, cache_k, cache_v, layer, *, tq, cq):
    bsz, seq, _ = x.shape
    has_cache = cache_k is not None
    assert seq % tq == 0 and tq % cq == 0
    assert has_cache or tq == A_WIN
    nkeys = A_WIN + cq
    tok = lambda w: pl.BlockSpec((1, tq, w), lambda b, i: (b, i, 0))
    in_specs = [tok(D_MODEL), _const_spec((1, D_MODEL)), _const_spec((D_MODEL, 3 * A_WIDTH)),
                _const_spec((1, A_WIDTH)), _const_spec((1, A_WIDTH)),
                _const_spec((A_HEADS // 2, 2 * cq, nkeys)), _const_spec((MXU_TILE, MXU_TILE))]
    args = [x, gain, w_a, jnp.tile(q_gain, (1, A_HEADS)), jnp.tile(k_gain, (1, A_HEADS)),
            bias.reshape(A_HEADS // 2, 2 * cq, nkeys), _head_sum_matrix(A_HEAD_DIM)]
    if has_cache:
        cspec = pl.BlockSpec((None, 1, A_WIN, A_WIDTH), lambda b, i: (layer, b, 0, 0))
        in_specs += [cspec, cspec]
        args += [cache_k, cache_v]
    out = jax.ShapeDtypeStruct((bsz, seq, A_WIDTH), F32)
    return pl.pallas_call(
        functools.partial(_attn_kernel, tq=tq, cq=cq, has_cache=has_cache),
        grid=(bsz, seq // tq),
        in_specs=in_specs,
        out_specs=[tok(A_WIDTH)] * 3,
        out_shape=[out] * 3,
        scratch_shapes=[pltpu.VMEM((A_WIN + tq, A_WIDTH), BF16),
                        pltpu.VMEM((A_WIN + tq, A_WIDTH), BF16),
                        pltpu.VMEM((tq, A_WIDTH), F32)],
        compiler_params=pltpu.CompilerParams(
            dimension_semantics=("arbitrary", "arbitrary"),
            vmem_limit_bytes=VMEM_LIMIT_BYTES),
        name="attention",
    )(*args)


def _ret_kernel(x_ref, g_ref, w_ref, cos_ref, sin_ref, s0_ref, o_ref, sout_ref, s_scr, *, tb):
    i = pl.program_id(1)

    @pl.when(i == 0)
    def _():
        s_scr[...] = s0_ref[0]

    h = _rms(x_ref[0]) * g_ref[...]
    z = _dot(h, w_ref[...])
    cosf = cos_ref[...]
    sinf = sin_ref[...]
    row = lax.broadcasted_iota(jnp.int32, (tb, tb), 0)
    col = lax.broadcasted_iota(jnp.int32, (tb, tb), 1)
    diff = row - col
    causal = diff >= 0
    dist = jnp.maximum(diff, 0).astype(F32)
    n = lax.broadcasted_iota(jnp.int32, (tb, 1), 0).astype(F32)
    outs = []
    for hh in range(B_HEADS):
        lg = RET_LOG_GAMMA[hh]
        lo = hh * B_HEAD_DIM
        q = z[:, lo:lo + B_HEAD_DIM]
        k = z[:, B_WIDTH + lo:B_WIDTH + lo + B_HEAD_DIM]
        v = z[:, 2 * B_WIDTH + lo:2 * B_WIDTH + lo + B_HEAD_DIM]
        gate = z[:, 3 * B_WIDTH + lo:3 * B_WIDTH + lo + B_HEAD_DIM]
        q = q * cosf + pltpu.roll(q, B_HEAD_DIM // 2, 1) * sinf
        k = (k * cosf + pltpu.roll(k, B_HEAD_DIM // 2, 1) * sinf) * (B_HEAD_DIM ** -0.5)
        dmat = jnp.where(causal, jnp.exp(lg * dist), 0.0)
        scores = _dot(q, k, NT) * dmat
        s_old = s_scr[hh]
        o = _dot(scores, v) + _dot(q, s_old) * jnp.exp(lg * (n + 1.0))
        k_dec = k * jnp.exp(lg * ((tb - 1.0) - n))
        s_scr[hh] = math.exp(lg * tb) * s_old + _dot(k_dec, v, TN)
        outs.append(_rms(o) * (gate * _sigmoid(gate)))
    o_ref[0] = jnp.concatenate(outs, axis=-1)
    sout_ref[0] = s_scr[...]


def _retention(x, gain, w_b, cosf, sinf, s0, *, tb):
    bsz, seq, _ = x.shape
    assert seq % tb == 0
    sspec = pl.BlockSpec((1, B_HEADS, B_HEAD_DIM, B_HEAD_DIM), lambda b, i: (b, 0, 0, 0))
    return pl.pallas_call(
        functools.partial(_ret_kernel, tb=tb),
        grid=(bsz, seq // tb),
        in_specs=[pl.BlockSpec((1, tb, D_MODEL), lambda b, i: (b, i, 0)),
                  _const_spec((1, D_MODEL)), _const_spec((D_MODEL, 4 * B_WIDTH)),
                  pl.BlockSpec((tb, B_HEAD_DIM), lambda b, i: (i, 0)),
                  pl.BlockSpec((tb, B_HEAD_DIM), lambda b, i: (i, 0)),
                  sspec],
        out_specs=[pl.BlockSpec((1, tb, B_WIDTH), lambda b, i: (b, i, 0)), sspec],
        out_shape=[jax.ShapeDtypeStruct((bsz, seq, B_WIDTH), F32),
                   jax.ShapeDtypeStruct((bsz, B_HEADS, B_HEAD_DIM, B_HEAD_DIM), F32)],
        scratch_shapes=[pltpu.VMEM((B_HEADS, B_HEAD_DIM, B_HEAD_DIM), F32)],
        compiler_params=pltpu.CompilerParams(
            dimension_semantics=("arbitrary", "arbitrary"),
            vmem_limit_bytes=VMEM_LIMIT_BYTES),
        name="retention",
    )(x, gain, w_b, cosf, sinf, s0)


PAIR_W = 2 * C_HEAD_DIM
N_PAIRS = C_HEADS // 2


def _dot2(a, b, dims=NN):
    a_hi, a_lo = _split_bf16(a)
    b = b.astype(BF16)
    d = functools.partial(lax.dot_general, dimension_numbers=dims, preferred_element_type=F32)
    return d(a_hi, b) + d(a_lo, b)


def _rwkv_kernel(*refs, tt, cc, lookahead):
    if lookahead:
        x_ref, xn_ref = refs[:2]
        refs = refs[2:]
    else:
        x_ref, xn_ref = refs[0], None
        refs = refs[1:]
    (g_ref, w_ref, mu_ref, sh0_ref, s0_ref, w0_ref, w2_ref, a0_ref, a2_ref,
     g2_ref, kk_ref, ka_ref, rk_ref, lnw_ref, lnb_ref, hsum_ref,
     o_ref, sout_ref, shout_ref,
     s_scr, carry, rk_raw, uw_raw, ua_raw, rt_s, kkt_s, kh_s, bh_s, v_s, bon_s, g_s, cum_s) = refs
    i = pl.program_id(1)

    n_chunks = tt // cc

    def project(x, slot):
        h = _rms(x) * g_ref[...]
        cz = _dot(h, w_ref[...])
        rows = lax.broadcasted_iota(jnp.int32, (tt, 1), 0)
        prev = jnp.where(rows == 0, carry[...], pltpu.roll(cz, 1, 0))
        cs = cz + (prev - cz) * mu_ref[...]
        last = cz[tt - 1:tt, :]
        carry[...] = last
        shout_ref[0] = last
        off = 3 * C_WIDTH
        w_lo = cs[:, off:off + C_RANK_W]
        a_lo = cs[:, off + C_RANK_W:off + C_RANK_W + C_RANK_A]
        g_lo = cs[:, off + C_RANK_W + C_RANK_A:C_SHIFT_WIDTH]
        rk_raw[...] = cs[:, 0:2 * C_WIDTH]
        v_s[slot] = cs[:, 2 * C_WIDTH:3 * C_WIDTH]
        uw_raw[...] = w0_ref[...] + _dot(jnp.tanh(w_lo), w2_ref[...])
        ua_raw[...] = a0_ref[...] + _dot(a_lo, a2_ref[...])
        g_s[slot] = _dot(_sigmoid(g_lo), g2_ref[...])

    def token_block(b, slot):
        rs = slice(b * cc, (b + 1) * cc)
        r = rk_raw[rs, 0:C_WIDTH]
        k = rk_raw[rs, C_WIDTH:2 * C_WIDTH]
        v = v_s[slot, rs, :]
        lw = -math.exp(-0.5) * _sigmoid(uw_raw[rs, :])
        a = _sigmoid(ua_raw[rs, :])
        hsum = hsum_ref[...]
        kk_raw = k * kk_ref[...]
        kk = kk_raw / jnp.maximum(jnp.sqrt(_head_sum(kk_raw * kk_raw, hsum)), 1e-12)
        k2 = k * (1.0 + (a - 1.0) * ka_ref[...])
        pos = lax.broadcasted_iota(jnp.int32, (cc, 1), 0)
        cum = lw
        step = 1
        while step < cc:
            cum = cum + jnp.where(pos >= step, pltpu.roll(cum, step, 0), 0.0)
            step *= 2
        e_inv = jnp.exp(-cum)
        rt_s[slot, rs, :] = r * jnp.exp(cum)
        kkt_s[slot, rs, :] = kk * jnp.exp(cum - lw)
        kh_s[slot, rs, :] = k2 * e_inv
        bh_s[slot, rs, :] = kk * a * e_inv
        bon_s[slot, rs, :] = _head_sum(r * k2 * rk_ref[...], hsum) * v
        cum_s[slot, rs, :] = cum

    @pl.when(i == 0)
    def _():
        s_scr[...] = s0_ref[0]
        carry[...] = sh0_ref[0]
        if lookahead:
            project(x_ref[0], 0)
            for b in range(n_chunks):
                token_block(b, 0)

    if lookahead:
        cur = lax.rem(i, 2)
        nxt = 1 - cur
        project(xn_ref[0], nxt)
        pending = [functools.partial(token_block, b, nxt) for b in range(n_chunks)]
    else:
        cur = 0
        project(x_ref[0], 0)
        for b in range(n_chunks):
            token_block(b, 0)
        pending = []

    def interleave():
        if pending:
            pending.pop(0)()

    two = 2 * cc
    sh = cc.bit_length() - 1
    hd = C_HEAD_DIM.bit_length() - 1
    n_double = sh - 1
    rho = lax.broadcasted_iota(jnp.int32, (two, PAIR_W), 0)
    lane = lax.broadcasted_iota(jnp.int32, (two, PAIR_W), 1)
    placed = (rho >> sh) == (lane >> hd)
    r2 = lax.broadcasted_iota(jnp.int32, (two, two), 0)
    c2 = lax.broadcasted_iota(jnp.int32, (two, two), 1)
    same = (r2 >> sh) == (c2 >> sh)
    strict = jnp.logical_and(same, r2 > c2)
    incl = jnp.logical_and(same, r2 >= c2)
    eye = jnp.where(r2 == c2, 1.0, 0.0).astype(F32)
    gr = lax.broadcasted_iota(jnp.int32, (PAIR_W, PAIR_W), 0)
    gc = lax.broadcasted_iota(jnp.int32, (PAIR_W, PAIR_W), 1)
    hmean = jnp.where((gr >> hd) == (gc >> hd), 1.0 / C_HEAD_DIM, 0.0).astype(BF16)

    def place(xv):
        return jnp.where(placed, jnp.concatenate([xv, xv], axis=0), 0.0)

    per_trip = 2 if n_chunks % 2 == 0 else 1
    units = [(c, p) for c in range(per_trip) for p in range(N_PAIRS)]
    pairs = range(len(units))

    def chunk(j):
        r0s = [(j * per_trip + c) * cc for c in range(per_trip)]
        w_end = [jnp.exp(cum_s[cur, r0 + cc - 1:r0 + cc, :]) for r0 in r0s]
        lanes = [slice(p * PAIR_W, (p + 1) * PAIR_W) for _, p in units]
        sl = [(cur, slice(r0s[c], r0s[c] + cc), lanes[u]) for u, (c, _) in enumerate(units)]
        we = [w_end[c][:, lanes[u]] for u, (c, _) in enumerate(units)]
        rt = [place(rt_s[sl[p]]) for p in pairs]
        kkt = [place(kkt_s[sl[p]]) for p in pairs]
        kh = [place(kh_s[sl[p]]) for p in pairs]
        bh = [place(bh_s[sl[p]]) for p in pairs]
        vv = [place(v_s[sl[p]]) for p in pairs]
        cat0 = lambda *xs: jnp.concatenate(xs, axis=0)
        cat1 = lambda *xs: jnp.concatenate(xs, axis=1)
        gg = [_dot(cat0(kkt[p], rt[p]), cat0(kh[p], bh[p]), NT) for p in pairs]
        interleave()
        a_kk = [jnp.where(strict, gg[p][:two, :two], 0.0) for p in pairs]
        b_kk = [jnp.where(incl, gg[p][two:, :two], 0.0) for p in pairs]
        b_bb = [jnp.where(incl, gg[p][two:, two:], 0.0) for p in pairs]
        pw = [jnp.where(strict, -gg[p][:two, two:], 0.0) for p in pairs]
        tinv = [eye + pw[p] for p in pairs]
        pw = [_dot(pw[p], pw[p]) for p in pairs]
        for _ in range(n_double - 1):
            pt = [_dot(pw[p], cat1(pw[p], tinv[p])) for p in pairs]
            pw = [pt[p][:, :two] for p in pairs]
            tinv = [tinv[p] + pt[p][:, two:] for p in pairs]
        tinv = [tinv[p] + _dot(pw[p], tinv[p]) for p in pairs]
        interleave()
        av = [_dot(a_kk[p], vv[p]) for p in pairs]
        kv = [_dot(tinv[p], cat1(kkt[p], av[p])) for p in pairs]
        zero = jnp.zeros((two, PAIR_W), F32)
        ry = [_dot(cat1(b_kk[p], -b_bb[p]), cat0(cat1(zero, vv[p]), kv[p])) for p in pairs]
        rp = [rt[p] + ry[p][:, :PAIR_W] for p in pairs]
        y0 = [ry[p][:, PAIR_W:] for p in pairs]
        kkp = [kv[p][:, :PAIR_W] for p in pairs]
        vp = [kv[p][:, PAIR_W:] for p in pairs]
        bd = [bh[p] * we[p] for p in pairs]
        kd = [kh[p] * we[p] for p in pairs]
        q = [_dot(kkp[p], bd[p], TN) for p in pairs]
        z = [_dot(cat0(vv[p], vp[p]), cat0(kd[p], -bd[p]), TN) for p in pairs]
        state = [s_scr[p] for p in range(N_PAIRS)]
        y2 = []
        for u, (_, p) in enumerate(units):
            s_old = state[p]
            y2.append(_dot(rp[u], s_old, NT) + y0[u])
            state[p] = s_old * we[u] - _dot(s_old, q[u]) + z[u]
        for p in range(N_PAIRS):
            s_scr[p] = state[p]
        ys = [y2[u][:cc] + y2[u][cc:] for u in pairs]
        yc = [ys[u] - _dot(ys[u], hmean) for u in pairs]
        var = [_dot(yc[u] * yc[u], hmean) for u in pairs]
        for u in pairs:
            yn = yc[u] * lax.rsqrt(var[u] + C_GN_EPS) * lnw_ref[:, lanes[u]] + lnb_ref[:, lanes[u]]
            o_ref[(0,) + sl[u][1:]] = (yn + bon_s[sl[u]]) * g_s[sl[u]]

    for j in range(n_chunks // per_trip):
        chunk(j)
    while pending:
        interleave()
    sout_ref[0] = s_scr[...]


def _pair_states(s):
    bsz = s.shape[0]
    s = s.reshape(bsz, N_PAIRS, 2, C_HEAD_DIM, C_HEAD_DIM)
    zero = jnp.zeros_like(s[:, :, 0])
    top = jnp.concatenate([s[:, :, 0], zero], axis=-1)
    bot = jnp.concatenate([zero, s[:, :, 1]], axis=-1)
    return jnp.concatenate([top, bot], axis=-2)


def _unpair_states(s):
    bsz = s.shape[0]
    d = C_HEAD_DIM
    return jnp.stack([s[:, :, :d, :d], s[:, :, d:, d:]], axis=2).reshape(bsz, C_HEADS, d, d)


def _rwkv(x, gain, w_c, lw, shift0, s0, *, tt, cc):
    bsz, seq, _ = x.shape
    assert seq % tt == 0 and tt % cc == 0 and cc & (cc - 1) == 0
    row = lambda a: a.reshape(1, -1)
    sspec = pl.BlockSpec((1, N_PAIRS, PAIR_W, PAIR_W), lambda b, i: (b, 0, 0, 0))
    shspec = pl.BlockSpec((1, 1, C_SHIFT_WIDTH), lambda b, i: (b, 0, 0))
    vec = _const_spec((1, C_WIDTH))
    n_tiles = seq // tt
    lookahead = n_tiles > 1
    tile = pltpu.VMEM((2 if lookahead else 1, tt, C_WIDTH), F32)
    hsum = _head_sum_matrix(C_HEAD_DIM)
    x_specs = [pl.BlockSpec((1, tt, D_MODEL), lambda b, i: (b, i, 0))]
    x_args = [x]
    if lookahead:
        x_specs.append(pl.BlockSpec((1, tt, D_MODEL), lambda b, i: (b, jnp.minimum(i + 1, n_tiles - 1), 0)))
        x_args.append(x)
    oc, s_new, shift_new = pl.pallas_call(
        functools.partial(_rwkv_kernel, tt=tt, cc=cc, lookahead=lookahead),
        grid=(bsz, n_tiles),
        in_specs=x_specs + [
                  _const_spec((1, D_MODEL)), _const_spec((D_MODEL, C_SHIFT_WIDTH)),
                  _const_spec((1, C_SHIFT_WIDTH)), shspec, sspec,
                  vec, _const_spec((C_RANK_W, C_WIDTH)), vec, _const_spec((C_RANK_A, C_WIDTH)),
                  _const_spec((C_RANK_G, C_WIDTH)), vec, vec, vec, vec, vec,
                  _const_spec((MXU_TILE, MXU_TILE))],
        out_specs=[pl.BlockSpec((1, tt, C_WIDTH), lambda b, i: (b, i, 0)), sspec, shspec],
        out_shape=[jax.ShapeDtypeStruct((bsz, seq, C_WIDTH), F32),
                   jax.ShapeDtypeStruct((bsz, N_PAIRS, PAIR_W, PAIR_W), F32),
                   jax.ShapeDtypeStruct((bsz, 1, C_SHIFT_WIDTH), F32)],
        scratch_shapes=[pltpu.VMEM((N_PAIRS, PAIR_W, PAIR_W), F32),
                        pltpu.VMEM((1, C_SHIFT_WIDTH), F32),
                        pltpu.VMEM((tt, 2 * C_WIDTH), F32),
                        pltpu.VMEM((tt, C_WIDTH), F32),
                        pltpu.VMEM((tt, C_WIDTH), F32)] + [tile] * 8,
        compiler_params=pltpu.CompilerParams(
            dimension_semantics=("arbitrary", "arbitrary"),
            vmem_limit_bytes=VMEM_LIMIT_BYTES),
        name="rwkv",
    )(*x_args, gain, w_c, row(lw["c_shift_mu"]), shift0, _pair_states(s0),
      row(lw["c_w0"]), lw["c_w2"].astype(BF16), row(lw["c_a0"]), lw["c_a2"].astype(BF16),
      lw["c_g2"].astype(BF16), row(lw["c_k_k"]), row(lw["c_k_a"]), row(lw["c_r_k"]),
      row(lw["c_ln_w"]), row(lw["c_ln_b"]), hsum)
    return oc, _unpair_states(s_new), shift_new


def _merge_kernel(x_ref, oa_ref, ob_ref, oc_ref, g_ref, wg_ref, wb_ref, wo_ref, y_ref):
    x = x_ref[...]
    h = _rms(x) * g_ref[...]
    gl = _dot(h, wg_ref[...])
    m = None
    for b, o_ref in enumerate((oa_ref, ob_ref, oc_ref)):
        t = _sigmoid(gl[:, b * D_MODEL:(b + 1) * D_MODEL]) * _dot(o_ref[...], wb_ref[b])
        m = t if m is None else m + t
    y_ref[...] = x + _dot(m, wo_ref[...])


def _merge(x, oa, ob, oc, gain, w_g, w_b, w_o, *, tm):
    rows = x.shape[0]
    assert rows % tm == 0
    tok = lambda w: pl.BlockSpec((tm, w), lambda i: (i, 0))
    return pl.pallas_call(
        _merge_kernel,
        grid=(rows // tm,),
        in_specs=[tok(D_MODEL), tok(A_WIDTH), tok(B_WIDTH), tok(C_WIDTH),
                  _const_spec((1, D_MODEL)), _const_spec((D_MODEL, N_BRANCHES * D_MODEL)),
                  _const_spec((N_BRANCHES, A_WIDTH, D_MODEL)), _const_spec((D_MODEL, D_MODEL))],
        out_specs=tok(D_MODEL),
        out_shape=jax.ShapeDtypeStruct((rows, D_MODEL), F32),
        compiler_params=pltpu.CompilerParams(
            dimension_semantics=("arbitrary",), vmem_limit_bytes=VMEM_LIMIT_BYTES),
        name="merge",
    )(x, oa, ob, oc, gain, w_g, w_b, w_o)


def _ffn_kernel(x_ref, p_ref, g_ref, wgate_ref, wup_ref, wdown_ref, wpp_ref, pg_ref, wpg_ref, y_ref):
    x = x_ref[...]
    hf = (_rms(x) * g_ref[...]).astype(BF16)
    gate = _dot(hf, wgate_ref[...])
    up = _dot(hf, wup_ref[...])
    x = x + _dot(gate * _sigmoid(gate) * up, wdown_ref[...])
    e = _rms(_dot(p_ref[...], wpp_ref[...])) * pg_ref[...]
    y_ref[...] = x + _sigmoid(_dot(_rms(x), wpg_ref[...])) * e


def _ffn(x, p_all, layer, gain, w_gate, w_up, w_down, w_pp, p_gain, w_pg, *, tm):
    rows = x.shape[0]
    d_ff = w_gate.shape[1]
    ple = p_all.shape[-1]
    p = p_all.reshape(p_all.shape[0], rows, ple)
    assert rows % tm == 0
    tok = lambda w: pl.BlockSpec((tm, w), lambda i: (i, 0))
    return pl.pallas_call(
        _ffn_kernel,
        grid=(rows // tm,),
        in_specs=[tok(D_MODEL), pl.BlockSpec((None, tm, ple), lambda i: (layer, i, 0)),
                  _const_spec((1, D_MODEL)),
                  _const_spec((D_MODEL, d_ff)), _const_spec((D_MODEL, d_ff)),
                  _const_spec((d_ff, D_MODEL)), _const_spec((ple, D_MODEL)),
                  _const_spec((1, D_MODEL)), _const_spec((D_MODEL, D_MODEL))],
        out_specs=tok(D_MODEL),
        out_shape=jax.ShapeDtypeStruct((rows, D_MODEL), F32),
        compiler_params=pltpu.CompilerParams(
            dimension_semantics=("arbitrary",), vmem_limit_bytes=VMEM_LIMIT_BYTES),
        name="ffn",
    )(x, p, gain, w_gate, w_up, w_down, w_pp, p_gain, w_pg)


def _rel_bias_table(rel_bias, cq):
    nkeys = A_WIN + cq
    t_max = A_WIN + CHUNK - 1 + cq - 1
    heads, n_rel = rel_bias.shape
    w = t_max + 1
    tail = jnp.broadcast_to(rel_bias[:, n_rel - 1:], (heads, w - n_rel))
    rev = jnp.concatenate([tail, rel_bias[:, ::-1].astype(F32), jnp.zeros((heads, 1), F32)], axis=1)
    skew = jnp.tile(rev, (1, cq))[:, :cq * w].reshape(heads, cq, w)
    return skew[:, :, cq - 1:cq - 1 + nkeys] * LOG2_E


def _rotary_tables(pos0, seq):
    half = B_HEAD_DIM // 2
    pos = pos0 + jnp.arange(seq, dtype=jnp.int32)
    inv = ROPE_BASE ** (-jnp.arange(half, dtype=F32) / half)
    ang = pos.astype(F32)[:, None] * inv[None, :]
    cos, sin = jnp.cos(ang), jnp.sin(ang)
    return jnp.concatenate([cos, cos], axis=-1), jnp.concatenate([-sin, sin], axis=-1)


def _layer(x, p_all, layer, pos0, a_ck, a_cv, ret_s0, rwkv_s0, shift_prev, lw, cfg):
    bsz, seq, _ = x.shape
    row = lambda a: a.reshape(1, -1)
    w_in = lw["w_in"].astype(BF16)
    w_a = w_in[:, OFF_A:OFF_B]
    w_b = w_in[:, OFF_B:OFF_C]
    w_c = w_in[:, OFF_C:OFF_G]
    w_g = w_in[:, OFF_G:]
    gain = row(lw["norm_mix"])

    bias = _rel_bias_table(lw["a_rel_bias"], cfg["cq"])
    if a_ck is not None:
        a_ck = a_ck.reshape(a_ck.shape[0], bsz, A_WIN, A_WIDTH)
        a_cv = a_cv.reshape(a_cv.shape[0], bsz, A_WIN, A_WIDTH)
    oa, kn, av = _attention(x, gain, w_a, row(lw["a_q_norm"]), row(lw["a_k_norm"]), bias,
                            a_ck, a_cv, layer, tq=cfg["tq"], cq=cfg["cq"])
    keep = min(A_WIN, seq)
    new_ak = kn[:, seq - keep:].reshape(bsz, keep, A_HEADS, A_HEAD_DIM)
    new_av = av[:, seq - keep:].reshape(bsz, keep, A_HEADS, A_HEAD_DIM)

    cosf, sinf = _rotary_tables(pos0, seq)
    ob, new_ret = _retention(x, gain, w_b, cosf, sinf, ret_s0, tb=cfg["tb"])

    oc, new_rwkv, new_shift = _rwkv(x, gain, w_c, lw, shift_prev, rwkv_s0, tt=cfg["tt"], cc=cfg["cc"])

    rows = bsz * seq
    tm = min(cfg["tm"], rows)
    flat = lambda t: t.reshape(rows, t.shape[-1])
    x1 = _merge(flat(x), flat(oa), flat(ob), flat(oc), gain, w_g,
                lw["w_branch"].astype(BF16), lw["w_out"].astype(BF16), tm=tm)
    x2 = _ffn(x1, p_all, layer, row(lw["norm_ffn"]), lw["w_ffn_gate"].astype(BF16),
              lw["w_ffn_up"].astype(BF16), lw["w_ffn_down"].astype(BF16),
              lw["w_ple_proj"].astype(BF16), row(lw["ple_norm"]), lw["w_ple_gate"].astype(BF16),
              tm=tm)
    return x2.reshape(bsz, seq, D_MODEL), (new_ak, new_av, new_ret, new_rwkv, new_shift)


def _group_config(seq):
    if seq >= A_WIN:
        return dict(tq=A_WIN, cq=CHUNK, tb=256, tt=256, cc=CHUNK, tm=256)
    return dict(tq=seq, cq=seq, tb=seq, tt=seq, cc=seq, tm=256)


def kernel(x_prompt, x_sample, p_prompt, p_sample, cache_a_k, cache_a_v, state_ret, state_rwkv, state_rwkv_shift, norm_mix, w_in, a_q_norm, a_k_norm, a_rel_bias, c_shift_mu, c_w0, c_w2, c_a0, c_a2, c_g2, c_k_k, c_k_a, c_r_k, c_ln_w, c_ln_b, w_branch, w_out, norm_ffn, w_ffn_gate, w_ffn_up, w_ffn_down, w_ple_proj, ple_norm, w_ple_gate):
    depth = w_in.shape[0]

    def layer_weights(i):
        return dict(norm_mix=norm_mix[i], w_in=w_in[i], a_q_norm=a_q_norm[i], a_k_norm=a_k_norm[i],
                    a_rel_bias=a_rel_bias[i], c_shift_mu=c_shift_mu[i], c_w0=c_w0[i], c_w2=c_w2[i],
                    c_a0=c_a0[i], c_a2=c_a2[i], c_g2=c_g2[i], c_k_k=c_k_k[i], c_k_a=c_k_a[i], c_r_k=c_r_k[i],
                    c_ln_w=c_ln_w[i], c_ln_b=c_ln_b[i], w_branch=w_branch[i], w_out=w_out[i],
                    norm_ffn=norm_ffn[i], w_ffn_gate=w_ffn_gate[i], w_ffn_up=w_ffn_up[i],
                    w_ffn_down=w_ffn_down[i], w_ple_proj=w_ple_proj[i], ple_norm=ple_norm[i],
                    w_ple_gate=w_ple_gate[i])

    bp, lp, _ = x_prompt.shape
    cfg_p = _group_config(lp)
    ret0 = jnp.zeros((bp, B_HEADS, B_HEAD_DIM, B_HEAD_DIM), F32)
    rwkv0 = jnp.zeros((bp, C_HEADS, C_HEAD_DIM, C_HEAD_DIM), F32)
    shift0 = jnp.zeros((bp, 1, C_SHIFT_WIDTH), F32)
    y_prompt = x_prompt
    st_p = []
    for i in range(depth):
        y_prompt, st = _layer(y_prompt, p_prompt, i, 0, None, None, ret0, rwkv0, shift0,
                              layer_weights(i), cfg_p)
        st_p.append(st)

    cfg_s = _group_config(x_sample.shape[1])
    y_sample = x_sample
    st_s = []
    for i in range(depth):
        y_sample, st = _layer(y_sample, p_sample, i, PAST_LEN, cache_a_k, cache_a_v, state_ret[i],
                              state_rwkv[i], state_rwkv_shift[i], layer_weights(i), cfg_s)
        st_s.append(st)

    stack = lambda sts, j: jnp.stack([s[j] for s in sts])
    return (y_prompt, y_sample,
            stack(st_p, 0), stack(st_p, 1), stack(st_p, 2), stack(st_p, 3), stack(st_p, 4),
            stack(st_s, 0), stack(st_s, 1), stack(st_s, 2), stack(st_s, 3), stack(st_s, 4))
```

```python
import functools
import math

import jax
import jax.numpy as jnp
from jax import lax
from jax.experimental import pallas as pl
from jax.experimental.pallas import tpu as pltpu

F32 = jnp.float32
BF16 = jnp.bfloat16

D_MODEL = 1024
PAST_LEN = 2048
CHUNK = 64
NORM_EPS = 1e-6

A_HEADS = 8
A_HEAD_DIM = 64
A_WIDTH = 512
A_WIN = 512
A_REL_MAX = 256

B_HEADS = 4
B_HEAD_DIM = 128
B_WIDTH = 512
ROPE_BASE = 10000.0

C_HEADS = 8
C_HEAD_DIM = 64
C_WIDTH = 512
C_RANK_W = 64
C_RANK_A = 64
C_RANK_G = 128
C_SHIFT_WIDTH = 3 * C_WIDTH + C_RANK_W + C_RANK_A + C_RANK_G
C_GN_EPS = 64e-5

N_BRANCHES = 3

OFF_A = 0
OFF_B = 3 * A_WIDTH
OFF_C = OFF_B + 4 * B_WIDTH
OFF_G = OFF_C + C_SHIFT_WIDTH
IN_WIDTH = OFF_G + N_BRANCHES * D_MODEL

RET_LOG_GAMMA = tuple(math.log1p(-(2.0 ** (-5.0 - h))) for h in range(B_HEADS))

VMEM_LIMIT_BYTES = 56 * 1024 * 1024
MXU_TILE = 256
LOG2_E = math.log2(math.e)

NN = (((1,), (0,)), ((), ()))
NT = (((1,), (1,)), ((), ()))
TN = (((0,), (0,)), ((), ()))


def _dot(a, b, dims=NN):
    return lax.dot_general(a.astype(BF16), b.astype(BF16), dims, preferred_element_type=F32)


def _split_bf16(a):
    hi = a.astype(BF16)
    lo = (a - hi.astype(F32)).astype(BF16)
    return hi, lo


def _dot3(a, b, dims=NN):
    a_hi, a_lo = _split_bf16(a)
    b_hi, b_lo = _split_bf16(b)
    d = functools.partial(lax.dot_general, dimension_numbers=dims, preferred_element_type=F32)
    return d(a_hi, b_hi) + (d(a_hi, b_lo) + d(a_lo, b_hi))


def _rms(x):
    return x * lax.rsqrt(jnp.mean(x * x, axis=-1, keepdims=True) + NORM_EPS)


def _sigmoid(x):
    return 1.0 / (1.0 + jnp.exp(-x))


def _const_spec(shape):
    nd = len(shape)
    return pl.BlockSpec(shape, lambda *_: (0,) * nd, pipeline_mode=pl.Buffered(1))


def _head_sum_matrix(head_dim):
    head = jnp.arange(MXU_TILE, dtype=jnp.int32) // head_dim
    return (head[:, None] == head[None, :]).astype(BF16)


def _head_sum(x, hsum):
    w = hsum.shape[0]
    return jnp.concatenate([_dot(x[:, g * w:(g + 1) * w], hsum) for g in range(x.shape[1] // w)], axis=1)


def _attn_kernel(*refs, tq, cq, has_cache):
    if has_cache:
        (x_ref, g_ref, w_ref, qg_ref, kg_ref, bias_ref, hsum_ref, kc_ref, vc_ref,
         o_ref, kn_ref, v_ref, kwin, vwin, qs) = refs
    else:
        (x_ref, g_ref, w_ref, qg_ref, kg_ref, bias_ref, hsum_ref,
         o_ref, kn_ref, v_ref, kwin, vwin, qs) = refs
    i = pl.program_id(1)
    nkeys = A_WIN + cq
    pair_w = 2 * A_HEAD_DIM

    if has_cache:
        kwin[0:A_WIN, :] = kc_ref[0].astype(BF16)
        vwin[0:A_WIN, :] = vc_ref[0].astype(BF16)
    else:
        @pl.when(i == 0)
        def _():
            kwin[0:A_WIN, :] = jnp.zeros((A_WIN, A_WIDTH), BF16)
            vwin[0:A_WIN, :] = jnp.zeros((A_WIN, A_WIDTH), BF16)

    h = _rms(x_ref[0]) * g_ref[...]
    z = _dot(h, w_ref[...])
    q = z[:, 0:A_WIDTH]
    k = z[:, A_WIDTH:2 * A_WIDTH]
    v = z[:, 2 * A_WIDTH:3 * A_WIDTH]
    hsum = hsum_ref[...]
    inv_d = 1.0 / A_HEAD_DIM
    qn = q * lax.rsqrt(_head_sum(q * q, hsum) * inv_d + NORM_EPS) * qg_ref[...]
    kn = k * lax.rsqrt(_head_sum(k * k, hsum) * inv_d + NORM_EPS) * kg_ref[...]
    qs[...] = qn * (A_HEAD_DIM ** -0.5 * LOG2_E)
    kwin[A_WIN:A_WIN + tq, :] = kn.astype(BF16)
    vwin[A_WIN:A_WIN + tq, :] = v.astype(BF16)
    kn_ref[0] = kn
    v_ref[0] = v

    rho = lax.broadcasted_iota(jnp.int32, (2 * cq, pair_w), 0)
    lane = lax.broadcasted_iota(jnp.int32, (2 * cq, pair_w), 1)
    placed = (rho >> (cq.bit_length() - 1)) == (lane >> (A_HEAD_DIM.bit_length() - 1))

    n_chunks = tq // cq
    per_trip = 4 if n_chunks % 4 == 0 else (2 if n_chunks % 2 == 0 else 1)
    units = [(c, p) for c in range(per_trip) for p in range(A_HEADS // 2)]
    ids = range(len(units))

    def chunk(j, carry):
        r0s = [pl.multiple_of((j * per_trip + c) * cq, cq) for c in range(per_trip)]
        lanes = [slice(p * pair_w, (p + 1) * pair_w) for _, p in units]
        qc = [qs[pl.ds(r0s[c], cq), lanes[u]] for u, (c, _) in enumerate(units)]
        qp = [jnp.where(placed, jnp.concatenate([qc[u], qc[u]], axis=0), 0.0) for u in ids]
        s = [_dot(qp[u], kwin[pl.ds(r0s[c], nkeys), lanes[u]], NT) + bias_ref[p]
             for u, (c, p) in enumerate(units)]
        if not has_cache:
            col = lax.broadcasted_iota(jnp.int32, (2 * cq, nkeys), 1)
            ok = [jnp.logical_or(r0 + col >= A_WIN, i > 0) for r0 in r0s]
            s = [jnp.where(ok[c], s[u], -1e30) for u, (c, _) in enumerate(units)]
        m = [jnp.max(s[u], axis=-1, keepdims=True) for u in ids]
        e = [jnp.exp2(s[u] - m[u]) for u in ids]
        den = [jnp.sum(e[u], axis=-1, keepdims=True) for u in ids]
        o2 = [_dot(e[u], vwin[pl.ds(r0s[c], nkeys), lanes[u]]) for u, (c, _) in enumerate(units)]
        for u, (c, _) in enumerate(units):
            o = jnp.where(placed, o2[u] / den[u], 0.0)
            o_ref[0, pl.ds(r0s[c], cq), lanes[u]] = o[:cq] + o[cq:]
        return carry

    lax.fori_loop(0, n_chunks // per_trip, chunk, 0)

    if not has_cache:
        kwin[0:A_WIN, :] = kwin[tq:tq + A_WIN, :]
        vwin[0:A_WIN, :] = vwin[tq:tq + A_WIN, :]


def _attention(x, gain, w_a, q_gain, k_gain, bias, cache_k, cache_v, layer, *, tq, cq):
    bsz, seq, _ = x.shape
    has_cache = cache_k is not None
    assert seq % tq == 0 and tq % cq == 0
    assert has_cache or tq == A_WIN
    nkeys = A_WIN + cq
    tok = lambda w: pl.BlockSpec((1, tq, w), lambda b, i: (b, i, 0))
    in_specs = [tok(D_MODEL), _const_spec((1, D_MODEL)), _const_spec((D_MODEL, 3 * A_WIDTH)),
                _const_spec((1, A_WIDTH)), _const_spec((1, A_WIDTH)),
                _const_spec((A_HEADS // 2, 2 * cq, nkeys)), _const_spec((MXU_TILE, MXU_TILE))]
    args = [x, gain, w_a, jnp.tile(q_gain, (1, A_HEADS)), jnp.tile(k_gain, (1, A_HEADS)),
            bias.reshape(A_HEADS // 2, 2 * cq, nkeys), _head_sum_matrix(A_HEAD_DIM)]
    if has_cache:
        cspec = pl.BlockSpec((None, 1, A_WIN, A_WIDTH), lambda b, i: (layer, b, 0, 0))
        in_specs += [cspec, cspec]
        args += [cache_k, cache_v]
    out = jax.ShapeDtypeStruct((bsz, seq, A_WIDTH), F32)
    return pl.pallas_call(
        functools.partial(_attn_kernel, tq=tq, cq=cq, has_cache=has_cache),
        grid=(bsz, seq // tq),
        in_specs=in_specs,
        out_specs=[tok(A_WIDTH)] * 3,
        out_shape=[out] * 3,
        scratch_shapes=[pltpu.VMEM((A_WIN + tq, A_WIDTH), BF16),
                        pltpu.VMEM((A_WIN + tq, A_WIDTH), BF16),
                        pltpu.VMEM((tq, A_WIDTH), F32)],
        compiler_params=pltpu.CompilerParams(
            dimension_semantics=("arbitrary", "arbitrary"),
            vmem_limit_bytes=VMEM_LIMIT_BYTES),
        name="attention",
    )(*args)


def _ret_kernel(x_ref, g_ref, w_ref, cos_ref, sin_ref, s0_ref, o_ref, sout_ref, s_scr, *, tb):
    i = pl.program_id(1)

    @pl.when(i == 0)
    def _():
        s_scr[...] = s0_ref[0]

    h = _rms(x_ref[0]) * g_ref[...]
    z = _dot(h, w_ref[...])
    cosf = cos_ref[...]
    sinf = sin_ref[...]
    row = lax.broadcasted_iota(jnp.int32, (tb, tb), 0)
    col = lax.broadcasted_iota(jnp.int32, (tb, tb), 1)
    diff = row - col
    causal = diff >= 0
    dist = jnp.maximum(diff, 0).astype(F32)
    n = lax.broadcasted_iota(jnp.int32, (tb, 1), 0).astype(F32)
    outs = []
    for hh in range(B_HEADS):
        lg = RET_LOG_GAMMA[hh]
        lo = hh * B_HEAD_DIM
        q = z[:, lo:lo + B_HEAD_DIM]
        k = z[:, B_WIDTH + lo:B_WIDTH + lo + B_HEAD_DIM]
        v = z[:, 2 * B_WIDTH + lo:2 * B_WIDTH + lo + B_HEAD_DIM]
        gate = z[:, 3 * B_WIDTH + lo:3 * B_WIDTH + lo + B_HEAD_DIM]
        q = q * cosf + pltpu.roll(q, B_HEAD_DIM // 2, 1) * sinf
        k = (k * cosf + pltpu.roll(k, B_HEAD_DIM // 2, 1) * sinf) * (B_HEAD_DIM ** -0.5)
        dmat = jnp.where(causal, jnp.exp(lg * dist), 0.0)
        scores = _dot(q, k, NT) * dmat
        s_old = s_scr[hh]
        o = _dot(scores, v) + _dot(q, s_old) * jnp.exp(lg * (n + 1.0))
        k_dec = k * jnp.exp(lg * ((tb - 1.0) - n))
        s_scr[hh] = math.exp(lg * tb) * s_old + _dot(k_dec, v, TN)
        outs.append(_rms(o) * (gate * _sigmoid(gate)))
    o_ref[0] = jnp.concatenate(outs, axis=-1)
    sout_ref[0] = s_scr[...]


def _retention(x, gain, w_b, cosf, sinf, s0, *, tb):
    bsz, seq, _ = x.shape
    assert seq % tb == 0
    sspec = pl.BlockSpec((1, B_HEADS, B_HEAD_DIM, B_HEAD_DIM), lambda b, i: (b, 0, 0, 0))
    return pl.pallas_call(
        functools.partial(_ret_kernel, tb=tb),
        grid=(bsz, seq // tb),
        in_specs=[pl.BlockSpec((1, tb, D_MODEL), lambda b, i: (b, i, 0)),
                  _const_spec((1, D_MODEL)), _const_spec((D_MODEL, 4 * B_WIDTH)),
                  pl.BlockSpec((tb, B_HEAD_DIM), lambda b, i: (i, 0)),
                  pl.BlockSpec((tb, B_HEAD_DIM), lambda b, i: (i, 0)),
                  sspec],
        out_specs=[pl.BlockSpec((1, tb, B_WIDTH), lambda b, i: (b, i, 0)), sspec],
        out_shape=[jax.ShapeDtypeStruct((bsz, seq, B_WIDTH), F32),
                   jax.ShapeDtypeStruct((bsz, B_HEADS, B_HEAD_DIM, B_HEAD_DIM), F32)],
        scratch_shapes=[pltpu.VMEM((B_HEADS, B_HEAD_DIM, B_HEAD_DIM), F32)],
        compiler_params=pltpu.CompilerParams(
            dimension_semantics=("arbitrary", "arbitrary"),
            vmem_limit_bytes=VMEM_LIMIT_BYTES),
        name="retention",
    )(x, gain, w_b, cosf, sinf, s0)


PAIR_W = 2 * C_HEAD_DIM
N_PAIRS = C_HEADS // 2


def _dot2(a, b, dims=NN):
    a_hi, a_lo = _split_bf16(a)
    b = b.astype(BF16)
    d = functools.partial(lax.dot_general, dimension_numbers=dims, preferred_element_type=F32)
    return d(a_hi, b) + d(a_lo, b)


def _rwkv_kernel(*refs, tt, cc, lookahead):
    if lookahead:
        x_ref, xn_ref = refs[:2]
        refs = refs[2:]
    else:
        x_ref, xn_ref = refs[0], None
        refs = refs[1:]
    (g_ref, w_ref, mu_ref, sh0_ref, s0_ref, w0_ref, w2_ref, a0_ref, a2_ref,
     g2_ref, kk_ref, ka_ref, rk_ref, lnw_ref, lnb_ref, hsum_ref,
     o_ref, sout_ref, shout_ref,
     s_scr, carry, rk_raw, uw_raw, ua_raw, rt_s, kkt_s, kh_s, bh_s, v_s, bon_s, g_s, cum_s) = refs
    i = pl.program_id(1)

    n_chunks = tt // cc

    def project(x, slot):
        h = _rms(x) * g_ref[...]
        cz = _dot(h, w_ref[...])
        rows = lax.broadcasted_iota(jnp.int32, (tt, 1), 0)
        prev = jnp.where(rows == 0, carry[...], pltpu.roll(cz, 1, 0))
        cs = cz + (prev - cz) * mu_ref[...]
        last = cz[tt - 1:tt, :]
        carry[...] = last
        shout_ref[0] = last
        off = 3 * C_WIDTH
        w_lo = cs[:, off:off + C_RANK_W]
        a_lo = cs[:, off + C_RANK_W:off + C_RANK_W + C_RANK_A]
        g_lo = cs[:, off + C_RANK_W + C_RANK_A:C_SHIFT_WIDTH]
        rk_raw[...] = cs[:, 0:2 * C_WIDTH]
        v_s[slot] = cs[:, 2 * C_WIDTH:3 * C_WIDTH]
        uw_raw[...] = w0_ref[...] + _dot(jnp.tanh(w_lo), w2_ref[...])
        ua_raw[...] = a0_ref[...] + _dot(a_lo, a2_ref[...])
        g_s[slot] = _dot(_sigmoid(g_lo), g2_ref[...])

    def token_block(b, slot):
        rs = slice(b * cc, (b + 1) * cc)
        r = rk_raw[rs, 0:C_WIDTH]
        k = rk_raw[rs, C_WIDTH:2 * C_WIDTH]
        v = v_s[slot, rs, :]
        lw = -math.exp(-0.5) * _sigmoid(uw_raw[rs, :])
        a = _sigmoid(ua_raw[rs, :])
        hsum = hsum_ref[...]
        kk_raw = k * kk_ref[...]
        kk = kk_raw / jnp.maximum(jnp.sqrt(_head_sum(kk_raw * kk_raw, hsum)), 1e-12)
        k2 = k * (1.0 + (a - 1.0) * ka_ref[...])
        pos = lax.broadcasted_iota(jnp.int32, (cc, 1), 0)
        cum = lw
        step = 1
        while step < cc:
            cum = cum + jnp.where(pos >= step, pltpu.roll(cum, step, 0), 0.0)
            step *= 2
        e_inv = jnp.exp(-cum)
        rt_s[slot, rs, :] = r * jnp.exp(cum)
        kkt_s[slot, rs, :] = kk * jnp.exp(cum - lw)
        kh_s[slot, rs, :] = k2 * e_inv
        bh_s[slot, rs, :] = kk * a * e_inv
        bon_s[slot, rs, :] = _head_sum(r * k2 * rk_ref[...], hsum) * v
        cum_s[slot, rs, :] = cum

    @pl.when(i == 0)
    def _():
        s_scr[...] = s0_ref[0]
        carry[...] = sh0_ref[0]
        if lookahead:
            project(x_ref[0], 0)
            for b in range(n_chunks):
                token_block(b, 0)

    if lookahead:
        cur = lax.rem(i, 2)
        nxt = 1 - cur
        project(xn_ref[0], nxt)
        pending = [functools.partial(token_block, b, nxt) for b in range(n_chunks)]
    else:
        cur = 0
        project(x_ref[0], 0)
        for b in range(n_chunks):
            token_block(b, 0)
        pending = []

    def interleave():
        if pending:
            pending.pop(0)()

    two = 2 * cc
    sh = cc.bit_length() - 1
    hd = C_HEAD_DIM.bit_length() - 1
    n_double = sh - 1
    rho = lax.broadcasted_iota(jnp.int32, (two, PAIR_W), 0)
    lane = lax.broadcasted_iota(jnp.int32, (two, PAIR_W), 1)
    placed = (rho >> sh) == (lane >> hd)
    r2 = lax.broadcasted_iota(jnp.int32, (two, two), 0)
    c2 = lax.broadcasted_iota(jnp.int32, (two, two), 1)
    same = (r2 >> sh) == (c2 >> sh)
    strict = jnp.logical_and(same, r2 > c2)
    incl = jnp.logical_and(same, r2 >= c2)
    eye = jnp.where(r2 == c2, 1.0, 0.0).astype(F32)
    gr = lax.broadcasted_iota(jnp.int32, (PAIR_W, PAIR_W), 0)
    gc = lax.broadcasted_iota(jnp.int32, (PAIR_W, PAIR_W), 1)
    hmean = jnp.where((gr >> hd) == (gc >> hd), 1.0 / C_HEAD_DIM, 0.0).astype(BF16)

    def place(xv):
        return jnp.where(placed, jnp.concatenate([xv, xv], axis=0), 0.0)

    per_trip = 4 if n_chunks % 4 == 0 else (2 if n_chunks % 2 == 0 else 1)
    units = [(c, p) for c in range(per_trip) for p in range(N_PAIRS)]
    pairs = range(len(units))

    def chunk(j):
        r0s = [(j * per_trip + c) * cc for c in range(per_trip)]
        w_end = [jnp.exp(cum_s[cur, r0 + cc - 1:r0 + cc, :]) for r0 in r0s]
        lanes = [slice(p * PAIR_W, (p + 1) * PAIR_W) for _, p in units]
        sl = [(cur, slice(r0s[c], r0s[c] + cc), lanes[u]) for u, (c, _) in enumerate(units)]
        we = [w_end[c][:, lanes[u]] for u, (c, _) in enumerate(units)]
        rt = [place(rt_s[sl[p]]) for p in pairs]
        kkt = [place(kkt_s[sl[p]]) for p in pairs]
        kh = [place(kh_s[sl[p]]) for p in pairs]
        bh = [place(bh_s[sl[p]]) for p in pairs]
        vv = [place(v_s[sl[p]]) for p in pairs]
        cat0 = lambda *xs: jnp.concatenate(xs, axis=0)
        cat1 = lambda *xs: jnp.concatenate(xs, axis=1)
        gg = [_dot(cat0(kkt[p], rt[p]), cat0(kh[p], bh[p]), NT) for p in pairs]
        interleave()
        a_kk = [jnp.where(strict, gg[p][:two, :two], 0.0) for p in pairs]
        b_kk = [jnp.where(incl, gg[p][two:, :two], 0.0) for p in pairs]
        b_bb = [jnp.where(incl, gg[p][two:, two:], 0.0) for p in pairs]
        pw = [jnp.where(strict, -gg[p][:two, two:], 0.0) for p in pairs]
        tinv = [eye + pw[p] for p in pairs]
        pw = [_dot(pw[p], pw[p]) for p in pairs]
        for _ in range(n_double - 1):
            pt = [_dot(pw[p], cat1(pw[p], tinv[p])) for p in pairs]
            pw = [pt[p][:, :two] for p in pairs]
            tinv = [tinv[p] + pt[p][:, two:] for p in pairs]
            interleave()
        tinv = [tinv[p] + _dot(pw[p], tinv[p]) for p in pairs]
        interleave()
        av = [_dot(a_kk[p], vv[p]) for p in pairs]
        kv = [_dot(tinv[p], cat1(kkt[p], av[p])) for p in pairs]
        zero = jnp.zeros((two, PAIR_W), F32)
        ry = [_dot(cat1(b_kk[p], -b_bb[p]), cat0(cat1(zero, vv[p]), kv[p])) for p in pairs]
        rp = [rt[p] + ry[p][:, :PAIR_W] for p in pairs]
        y0 = [ry[p][:, PAIR_W:] for p in pairs]
        kkp = [kv[p][:, :PAIR_W] for p in pairs]
        vp = [kv[p][:, PAIR_W:] for p in pairs]
        bd = [bh[p] * we[p] for p in pairs]
        kd = [kh[p] * we[p] for p in pairs]
        q = [_dot(kkp[p], bd[p], TN) for p in pairs]
        z = [_dot(cat0(vv[p], vp[p]), cat0(kd[p], -bd[p]), TN) for p in pairs]
        state = [s_scr[p] for p in range(N_PAIRS)]
        y2 = []
        for u, (_, p) in enumerate(units):
            s_old = state[p]
            y2.append(_dot(rp[u], s_old, NT) + y0[u])
            state[p] = s_old * we[u] - _dot(s_old, q[u]) + z[u]
        for p in range(N_PAIRS):
            s_scr[p] = state[p]
        ys = [y2[u][:cc] + y2[u][cc:] for u in pairs]
        yc = [ys[u] - _dot(ys[u], hmean) for u in pairs]
        var = [_dot(yc[u] * yc[u], hmean) for u in pairs]
        for u in pairs:
            yn = yc[u] * lax.rsqrt(var[u] + C_GN_EPS) * lnw_ref[:, lanes[u]] + lnb_ref[:, lanes[u]]
            o_ref[(0,) + sl[u][1:]] = (yn + bon_s[sl[u]]) * g_s[sl[u]]

    for j in range(n_chunks // per_trip):
        chunk(j)
    while pending:
        interleave()
    sout_ref[0] = s_scr[...]


def _pair_states(s):
    bsz = s.shape[0]
    s = s.reshape(bsz, N_PAIRS, 2, C_HEAD_DIM, C_HEAD_DIM)
    zero = jnp.zeros_like(s[:, :, 0])
    top = jnp.concatenate([s[:, :, 0], zero], axis=-1)
    bot = jnp.concatenate([zero, s[:, :, 1]], axis=-1)
    return jnp.concatenate([top, bot], axis=-2)


def _unpair_states(s):
    bsz = s.shape[0]
    d = C_HEAD_DIM
    return jnp.stack([s[:, :, :d, :d], s[:, :, d:, d:]], axis=2).reshape(bsz, C_HEADS, d, d)


def _rwkv(x, gain, w_c, lw, shift0, s0, *, tt, cc):
    bsz, seq, _ = x.shape
    assert seq % tt == 0 and tt % cc == 0 and cc & (cc - 1) == 0
    row = lambda a: a.reshape(1, -1)
    sspec = pl.BlockSpec((1, N_PAIRS, PAIR_W, PAIR_W), lambda b, i: (b, 0, 0, 0))
    shspec = pl.BlockSpec((1, 1, C_SHIFT_WIDTH), lambda b, i: (b, 0, 0))
    vec = _const_spec((1, C_WIDTH))
    n_tiles = seq // tt
    lookahead = n_tiles > 1
    tile = pltpu.VMEM((2 if lookahead else 1, tt, C_WIDTH), F32)
    hsum = _head_sum_matrix(C_HEAD_DIM)
    x_specs = [pl.BlockSpec((1, tt, D_MODEL), lambda b, i: (b, i, 0))]
    x_args = [x]
    if lookahead:
        x_specs.append(pl.BlockSpec((1, tt, D_MODEL), lambda b, i: (b, jnp.minimum(i + 1, n_tiles - 1), 0)))
        x_args.append(x)
    oc, s_new, shift_new = pl.pallas_call(
        functools.partial(_rwkv_kernel, tt=tt, cc=cc, lookahead=lookahead),
        grid=(bsz, n_tiles),
        in_specs=x_specs + [
                  _const_spec((1, D_MODEL)), _const_spec((D_MODEL, C_SHIFT_WIDTH)),
                  _const_spec((1, C_SHIFT_WIDTH)), shspec, sspec,
                  vec, _const_spec((C_RANK_W, C_WIDTH)), vec, _const_spec((C_RANK_A, C_WIDTH)),
                  _const_spec((C_RANK_G, C_WIDTH)), vec, vec, vec, vec, vec,
                  _const_spec((MXU_TILE, MXU_TILE))],
        out_specs=[pl.BlockSpec((1, tt, C_WIDTH), lambda b, i: (b, i, 0)), sspec, shspec],
        out_shape=[jax.ShapeDtypeStruct((bsz, seq, C_WIDTH), F32),
                   jax.ShapeDtypeStruct((bsz, N_PAIRS, PAIR_W, PAIR_W), F32),
                   jax.ShapeDtypeStruct((bsz, 1, C_SHIFT_WIDTH), F32)],
        scratch_shapes=[pltpu.VMEM((N_PAIRS, PAIR_W, PAIR_W), F32),
                        pltpu.VMEM((1, C_SHIFT_WIDTH), F32),
                        pltpu.VMEM((tt, 2 * C_WIDTH), F32),
                        pltpu.VMEM((tt, C_WIDTH), F32),
                        pltpu.VMEM((tt, C_WIDTH), F32)] + [tile] * 8,
        compiler_params=pltpu.CompilerParams(
            dimension_semantics=("arbitrary", "arbitrary"),
            vmem_limit_bytes=VMEM_LIMIT_BYTES),
        name="rwkv",
    )(*x_args, gain, w_c, row(lw["c_shift_mu"]), shift0, _pair_states(s0),
      row(lw["c_w0"]), lw["c_w2"].astype(BF16), row(lw["c_a0"]), lw["c_a2"].astype(BF16),
      lw["c_g2"].astype(BF16), row(lw["c_k_k"]), row(lw["c_k_a"]), row(lw["c_r_k"]),
      row(lw["c_ln_w"]), row(lw["c_ln_b"]), hsum)
    return oc, _unpair_states(s_new), shift_new


def _merge_kernel(x_ref, oa_ref, ob_ref, oc_ref, g_ref, wg_ref, wb_ref, wo_ref, y_ref):
    x = x_ref[...]
    h = _rms(x) * g_ref[...]
    gl = _dot(h, wg_ref[...])
    m = None
    for b, o_ref in enumerate((oa_ref, ob_ref, oc_ref)):
        t = _sigmoid(gl[:, b * D_MODEL:(b + 1) * D_MODEL]) * _dot(o_ref[...], wb_ref[b])
        m = t if m is None else m + t
    y_ref[...] = x + _dot(m, wo_ref[...])


def _merge(x, oa, ob, oc, gain, w_g, w_b, w_o, *, tm):
    rows = x.shape[0]
    assert rows % tm == 0
    tok = lambda w: pl.BlockSpec((tm, w), lambda i: (i, 0))
    return pl.pallas_call(
        _merge_kernel,
        grid=(rows // tm,),
        in_specs=[tok(D_MODEL), tok(A_WIDTH), tok(B_WIDTH), tok(C_WIDTH),
                  _const_spec((1, D_MODEL)), _const_spec((D_MODEL, N_BRANCHES * D_MODEL)),
                  _const_spec((N_BRANCHES, A_WIDTH, D_MODEL)), _const_spec((D_MODEL, D_MODEL))],
        out_specs=tok(D_MODEL),
        out_shape=jax.ShapeDtypeStruct((rows, D_MODEL), F32),
        compiler_params=pltpu.CompilerParams(
            dimension_semantics=("arbitrary",), vmem_limit_bytes=VMEM_LIMIT_BYTES),
        name="merge",
    )(x, oa, ob, oc, gain, w_g, w_b, w_o)


def _ffn_kernel(x_ref, p_ref, g_ref, wgate_ref, wup_ref, wdown_ref, wpp_ref, pg_ref, wpg_ref, y_ref):
    x = x_ref[...]
    hf = (_rms(x) * g_ref[...]).astype(BF16)
    gate = _dot(hf, wgate_ref[...])
    up = _dot(hf, wup_ref[...])
    x = x + _dot(gate * _sigmoid(gate) * up, wdown_ref[...])
    e = _rms(_dot(p_ref[...], wpp_ref[...])) * pg_ref[...]
    y_ref[...] = x + _sigmoid(_dot(_rms(x), wpg_ref[...])) * e


def _ffn(x, p_all, layer, gain, w_gate, w_up, w_down, w_pp, p_gain, w_pg, *, tm):
    rows = x.shape[0]
    d_ff = w_gate.shape[1]
    ple = p_all.shape[-1]
    p = p_all.reshape(p_all.shape[0], rows, ple)
    assert rows % tm == 0
    tok = lambda w: pl.BlockSpec((tm, w), lambda i: (i, 0))
    return pl.pallas_call(
        _ffn_kernel,
        grid=(rows // tm,),
        in_specs=[tok(D_MODEL), pl.BlockSpec((None, tm, ple), lambda i: (layer, i, 0)),
                  _const_spec((1, D_MODEL)),
                  _const_spec((D_MODEL, d_ff)), _const_spec((D_MODEL, d_ff)),
                  _const_spec((d_ff, D_MODEL)), _const_spec((ple, D_MODEL)),
                  _const_spec((1, D_MODEL)), _const_spec((D_MODEL, D_MODEL))],
        out_specs=tok(D_MODEL),
        out_shape=jax.ShapeDtypeStruct((rows, D_MODEL), F32),
        compiler_params=pltpu.CompilerParams(
            dimension_semantics=("arbitrary",), vmem_limit_bytes=VMEM_LIMIT_BYTES),
        name="ffn",
    )(x, p, gain, w_gate, w_up, w_down, w_pp, p_gain, w_pg)


def _rel_bias_table(rel_bias, cq):
    nkeys = A_WIN + cq
    t_max = A_WIN + CHUNK - 1 + cq - 1
    heads, n_rel = rel_bias.shape
    w = t_max + 1
    tail = jnp.broadcast_to(rel_bias[:, n_rel - 1:], (heads, w - n_rel))
    rev = jnp.concatenate([tail, rel_bias[:, ::-1].astype(F32), jnp.zeros((heads, 1), F32)], axis=1)
    skew = jnp.tile(rev, (1, cq))[:, :cq * w].reshape(heads, cq, w)
    return skew[:, :, cq - 1:cq - 1 + nkeys] * LOG2_E


def _rotary_tables(pos0, seq):
    half = B_HEAD_DIM // 2
    pos = pos0 + jnp.arange(seq, dtype=jnp.int32)
    inv = ROPE_BASE ** (-jnp.arange(half, dtype=F32) / half)
    ang = pos.astype(F32)[:, None] * inv[None, :]
    cos, sin = jnp.cos(ang), jnp.sin(ang)
    return jnp.concatenate([cos, cos], axis=-1), jnp.concatenate([-sin, sin], axis=-1)


def _layer(x, p_all, layer, pos0, a_ck, a_cv, ret_s0, rwkv_s0, shift_prev, lw, cfg):
    bsz, seq, _ = x.shape
    row = lambda a: a.reshape(1, -1)
    w_in = lw["w_in"].astype(BF16)
    w_a = w_in[:, OFF_A:OFF_B]
    w_b = w_in[:, OFF_B:OFF_C]
    w_c = w_in[:, OFF_C:OFF_G]
    w_g = w_in[:, OFF_G:]
    gain = row(lw["norm_mix"])

    bias = _rel_bias_table(lw["a_rel_bias"], cfg["cq"])
    if a_ck is not None:
        a_ck = a_ck.reshape(a_ck.shape[0], bsz, A_WIN, A_WIDTH)
        a_cv = a_cv.reshape(a_cv.shape[0], bsz, A_WIN, A_WIDTH)
    oa, kn, av = _attention(x, gain, w_a, row(lw["a_q_norm"]), row(lw["a_k_norm"]), bias,
                            a_ck, a_cv, layer, tq=cfg["tq"], cq=cfg["cq"])
    keep = min(A_WIN, seq)
    new_ak = kn[:, seq - keep:].reshape(bsz, keep, A_HEADS, A_HEAD_DIM)
    new_av = av[:, seq - keep:].reshape(bsz, keep, A_HEADS, A_HEAD_DIM)

    cosf, sinf = _rotary_tables(pos0, seq)
    ob, new_ret = _retention(x, gain, w_b, cosf, sinf, ret_s0, tb=cfg["tb"])

    oc, new_rwkv, new_shift = _rwkv(x, gain, w_c, lw, shift_prev, rwkv_s0, tt=cfg["tt"], cc=cfg["cc"])

    rows = bsz * seq
    tm = min(cfg["tm"], rows)
    flat = lambda t: t.reshape(rows, t.shape[-1])
    x1 = _merge(flat(x), flat(oa), flat(ob), flat(oc), gain, w_g,
                lw["w_branch"].astype(BF16), lw["w_out"].astype(BF16), tm=tm)
    x2 = _ffn(x1, p_all, layer, row(lw["norm_ffn"]), lw["w_ffn_gate"].astype(BF16),
              lw["w_ffn_up"].astype(BF16), lw["w_ffn_down"].astype(BF16),
              lw["w_ple_proj"].astype(BF16), row(lw["ple_norm"]), lw["w_ple_gate"].astype(BF16),
              tm=tm)
    return x2.reshape(bsz, seq, D_MODEL), (new_ak, new_av, new_ret, new_rwkv, new_shift)


def _group_config(seq):
    if seq >= A_WIN:
        return dict(tq=A_WIN, cq=CHUNK, tb=256, tt=256, cc=CHUNK, tm=512)
    return dict(tq=seq, cq=seq, tb=seq, tt=seq, cc=seq, tm=256)


def kernel(x_prompt, x_sample, p_prompt, p_sample, cache_a_k, cache_a_v, state_ret, state_rwkv, state_rwkv_shift, norm_mix, w_in, a_q_norm, a_k_norm, a_rel_bias, c_shift_mu, c_w0, c_w2, c_a0, c_a2, c_g2, c_k_k, c_k_a, c_r_k, c_ln_w, c_ln_b, w_branch, w_out, norm_ffn, w_ffn_gate, w_ffn_up, w_ffn_down, w_ple_proj, ple_norm, w_ple_gate):
    depth = w_in.shape[0]

    def layer_weights(i):
        return dict(norm_mix=norm_mix[i], w_in=w_in[i], a_q_norm=a_q_norm[i], a_k_norm=a_k_norm[i],
                    a_rel_bias=a_rel_bias[i], c_shift_mu=c_shift_mu[i], c_w0=c_w0[i], c_w2=c_w2[i],
                    c_a0=c_a0[i], c_a2=c_a2[i], c_g2=c_g2[i], c_k_k=c_k_k[i], c_k_a=c_k_a[i], c_r_k=c_r_k[i],
                    c_ln_w=c_ln_w[i], c_ln_b=c_ln_b[i], w_branch=w_branch[i], w_out=w_out[i],
                    norm_ffn=norm_ffn[i], w_ffn_gate=w_ffn_gate[i], w_ffn_up=w_ffn_up[i],
                    w_ffn_down=w_ffn_down[i], w_ple_proj=w_ple_proj[i], ple_norm=ple_norm[i],
                    w_ple_gate=w_ple_gate[i])

    bp, lp, _ = x_prompt.shape
    cfg_p = _group_config(lp)
    ret0 = jnp.zeros((bp, B_HEADS, B_HEAD_DIM, B_HEAD_DIM), F32)
    rwkv0 = jnp.zeros((bp, C_HEADS, C_HEAD_DIM, C_HEAD_DIM), F32)
    shift0 = jnp.zeros((bp, 1, C_SHIFT_WIDTH), F32)
    y_prompt = x_prompt
    st_p = []
    for i in range(depth):
        y_prompt, st = _layer(y_prompt, p_prompt, i, 0, None, None, ret0, rwkv0, shift0,
                              layer_weights(i), cfg_p)
        st_p.append(st)

    cfg_s = _group_config(x_sample.shape[1])
    y_sample = x_sample
    st_s = []
    for i in range(depth):
        y_sample, st = _layer(y_sample, p_sample, i, PAST_LEN, cache_a_k, cache_a_v, state_ret[i],
                              state_rwkv[i], state_rwkv_shift[i], layer_weights(i), cfg_s)
        st_s.append(st)

    stack = lambda sts, j: jnp.stack([s[j] for s in sts])
    return (y_prompt, y_sample,
            stack(st_p, 0), stack(st_p, 1), stack(st_p, 2), stack(st_p, 3), stack(st_p, 4),
            stack(st_s, 0), stack(st_s, 1), stack(st_s, 2), stack(st_s, 3), stack(st_s, 4))
```

```python
import functools
import math

import jax
import jax.numpy as jnp
from jax import lax
from jax.experimental import pallas as pl
from jax.experimental.pallas import tpu as pltpu

F32 = jnp.float32
BF16 = jnp.bfloat16

D_MODEL = 1024
PAST_LEN = 2048
CHUNK = 64
NORM_EPS = 1e-6

A_HEADS = 8
A_HEAD_DIM = 64
A_WIDTH = 512
A_WIN = 512
A_REL_MAX = 256

B_HEADS = 4
B_HEAD_DIM = 128
B_WIDTH = 512
ROPE_BASE = 10000.0

C_HEADS = 8
C_HEAD_DIM = 64
C_WIDTH = 512
C_RANK_W = 64
C_RANK_A = 64
C_RANK_G = 128
C_SHIFT_WIDTH = 3 * C_WIDTH + C_RANK_W + C_RANK_A + C_RANK_G
C_GN_EPS = 64e-5

N_BRANCHES = 3

OFF_A = 0
OFF_B = 3 * A_WIDTH
OFF_C = OFF_B + 4 * B_WIDTH
OFF_G = OFF_C + C_SHIFT_WIDTH
IN_WIDTH = OFF_G + N_BRANCHES * D_MODEL

RET_LOG_GAMMA = tuple(math.log1p(-(2.0 ** (-5.0 - h))) for h in range(B_HEADS))

VMEM_LIMIT_BYTES = 56 * 1024 * 1024
MXU_TILE = 256
LOG2_E = math.log2(math.e)

NN = (((1,), (0,)), ((), ()))
NT = (((1,), (1,)), ((), ()))
TN = (((0,), (0,)), ((), ()))


def _dot(a, b, dims=NN):
    return lax.dot_general(a.astype(BF16), b.astype(BF16), dims, preferred_element_type=F32)


def _split_bf16(a):
    hi = a.astype(BF16)
    lo = (a - hi.astype(F32)).astype(BF16)
    return hi, lo


def _dot3(a, b, dims=NN):
    a_hi, a_lo = _split_bf16(a)
    b_hi, b_lo = _split_bf16(b)
    d = functools.partial(lax.dot_general, dimension_numbers=dims, preferred_element_type=F32)
    return d(a_hi, b_hi) + (d(a_hi, b_lo) + d(a_lo, b_hi))


def _rms(x):
    return x * lax.rsqrt(jnp.mean(x * x, axis=-1, keepdims=True) + NORM_EPS)


def _sigmoid(x):
    return 1.0 / (1.0 + jnp.exp(-x))


def _const_spec(shape):
    nd = len(shape)
    return pl.BlockSpec(shape, lambda *_: (0,) * nd, pipeline_mode=pl.Buffered(1))


def _head_sum_matrix(head_dim):
    head = jnp.arange(MXU_TILE, dtype=jnp.int32) // head_dim
    return (head[:, None] == head[None, :]).astype(BF16)


def _head_sum(x, hsum):
    w = hsum.shape[0]
    return jnp.concatenate([_dot(x[:, g * w:(g + 1) * w], hsum) for g in range(x.shape[1] // w)], axis=1)


def _attn_kernel(*refs, tq, cq, has_cache, nb):
    if has_cache:
        (x_ref, g_ref, w_ref, qg_ref, kg_ref, bias_ref, hsum_ref, kc_ref, vc_ref,
         o_ref, kn_ref, v_ref, kwin, vwin, qs) = refs
    else:
        (x_ref, g_ref, w_ref, qg_ref, kg_ref, bias_ref, hsum_ref,
         o_ref, kn_ref, v_ref, kwin, vwin, qs) = refs
    i = pl.program_id(1)
    nkeys = A_WIN + cq
    pair_w = 2 * A_HEAD_DIM

    if has_cache:
        kwin[:, 0:A_WIN, :] = kc_ref[...].astype(BF16)
        vwin[:, 0:A_WIN, :] = vc_ref[...].astype(BF16)
    else:
        @pl.when(i == 0)
        def _():
            kwin[0:A_WIN, :] = jnp.zeros((A_WIN, A_WIDTH), BF16)
            vwin[0:A_WIN, :] = jnp.zeros((A_WIN, A_WIDTH), BF16)

    h = _rms(x_ref[...].reshape(nb * tq, D_MODEL)) * g_ref[...]
    z = _dot(h, w_ref[...])
    q = z[:, 0:A_WIDTH]
    k = z[:, A_WIDTH:2 * A_WIDTH]
    v = z[:, 2 * A_WIDTH:3 * A_WIDTH]
    hsum = hsum_ref[...]
    inv_d = 1.0 / A_HEAD_DIM
    qn = q * lax.rsqrt(_head_sum(q * q, hsum) * inv_d + NORM_EPS) * qg_ref[...]
    kn = k * lax.rsqrt(_head_sum(k * k, hsum) * inv_d + NORM_EPS) * kg_ref[...]
    qs[...] = qn * (A_HEAD_DIM ** -0.5 * LOG2_E)
    if has_cache:
        kwin[:, A_WIN:A_WIN + tq, :] = kn.astype(BF16).reshape(nb, tq, A_WIDTH)
        vwin[:, A_WIN:A_WIN + tq, :] = v.astype(BF16).reshape(nb, tq, A_WIDTH)
    else:
        kwin[A_WIN:A_WIN + tq, :] = kn.astype(BF16)
        vwin[A_WIN:A_WIN + tq, :] = v.astype(BF16)
    kn_ref[...] = kn.reshape(nb, tq, A_WIDTH)
    v_ref[...] = v.reshape(nb, tq, A_WIDTH)

    rho = lax.broadcasted_iota(jnp.int32, (2 * cq, pair_w), 0)
    lane = lax.broadcasted_iota(jnp.int32, (2 * cq, pair_w), 1)
    placed = (rho >> (cq.bit_length() - 1)) == (lane >> (A_HEAD_DIM.bit_length() - 1))

    n_chunks = nb * tq // cq
    per_trip = 4 if n_chunks % 4 == 0 else (2 if n_chunks % 2 == 0 else 1)
    units = [(c, p) for c in range(per_trip) for p in range(A_HEADS // 2)]
    ids = range(len(units))

    def chunk(j, carry):
        r0s = [pl.multiple_of((j * per_trip + c) * cq, cq) for c in range(per_trip)]
        lanes = [slice(p * pair_w, (p + 1) * pair_w) for _, p in units]
        qc = [qs[pl.ds(r0s[c], cq), lanes[u]] for u, (c, _) in enumerate(units)]
        qp = [jnp.where(placed, jnp.concatenate([qc[u], qc[u]], axis=0), 0.0) for u in ids]
        if has_cache:
            win = [(j * per_trip + c, slice(None), lanes[u]) for u, (c, _) in enumerate(units)]
        else:
            win = [(pl.ds(r0s[c], nkeys), lanes[u]) for u, (c, _) in enumerate(units)]
        s = [_dot(qp[u], kwin[win[u]], NT) + bias_ref[p] for u, (_, p) in enumerate(units)]
        if not has_cache:
            col = lax.broadcasted_iota(jnp.int32, (2 * cq, nkeys), 1)
            ok = [jnp.logical_or(r0 + col >= A_WIN, i > 0) for r0 in r0s]
            s = [jnp.where(ok[c], s[u], -1e30) for u, (c, _) in enumerate(units)]
        m = [jnp.max(s[u], axis=-1, keepdims=True) for u in ids]
        e = [jnp.exp2(s[u] - m[u]) for u in ids]
        den = [jnp.sum(e[u], axis=-1, keepdims=True) for u in ids]
        o2 = [_dot(e[u], vwin[win[u]]) for u in ids]
        for u, (c, _) in enumerate(units):
            o = jnp.where(placed, o2[u] / den[u], 0.0)
            if has_cache:
                o_ref[j * per_trip + c, :, lanes[u]] = o[:cq] + o[cq:]
            else:
                o_ref[0, pl.ds(r0s[c], cq), lanes[u]] = o[:cq] + o[cq:]
        return carry

    lax.fori_loop(0, n_chunks // per_trip, chunk, 0)

    if not has_cache:
        kwin[0:A_WIN, :] = kwin[tq:tq + A_WIN, :]
        vwin[0:A_WIN, :] = vwin[tq:tq + A_WIN, :]


def _attention(x, gain, w_a, q_gain, k_gain, bias, cache_k, cache_v, layer, *, tq, cq, nb):
    bsz, seq, _ = x.shape
    has_cache = cache_k is not None
    assert seq % tq == 0 and tq % cq == 0
    assert (seq == tq == cq and bsz % nb == 0) if has_cache else (tq == A_WIN and nb == 1)
    nkeys = A_WIN + cq
    tok = lambda w: pl.BlockSpec((nb, tq, w), lambda b, i: (b, i, 0))
    win_shape = (nb, nkeys, A_WIDTH) if has_cache else (A_WIN + tq, A_WIDTH)
    in_specs = [tok(D_MODEL), _const_spec((1, D_MODEL)), _const_spec((D_MODEL, 3 * A_WIDTH)),
                _const_spec((1, A_WIDTH)), _const_spec((1, A_WIDTH)),
                _const_spec((A_HEADS // 2, 2 * cq, nkeys)), _const_spec((MXU_TILE, MXU_TILE))]
    args = [x, gain, w_a, jnp.tile(q_gain, (1, A_HEADS)), jnp.tile(k_gain, (1, A_HEADS)),
            bias.reshape(A_HEADS // 2, 2 * cq, nkeys), _head_sum_matrix(A_HEAD_DIM)]
    if has_cache:
        cspec = pl.BlockSpec((None, nb, A_WIN, A_WIDTH), lambda b, i: (layer, b, 0, 0))
        in_specs += [cspec, cspec]
        args += [cache_k, cache_v]
    out = jax.ShapeDtypeStruct((bsz, seq, A_WIDTH), F32)
    return pl.pallas_call(
        functools.partial(_attn_kernel, tq=tq, cq=cq, has_cache=has_cache, nb=nb),
        grid=(bsz // nb, seq // tq),
        in_specs=in_specs,
        out_specs=[tok(A_WIDTH)] * 3,
        out_shape=[out] * 3,
        scratch_shapes=[pltpu.VMEM(win_shape, BF16),
                        pltpu.VMEM(win_shape, BF16),
                        pltpu.VMEM((nb * tq, A_WIDTH), F32)],
        compiler_params=pltpu.CompilerParams(
            dimension_semantics=("arbitrary", "arbitrary"),
            vmem_limit_bytes=VMEM_LIMIT_BYTES),
        name="attention",
    )(*args)


def _ret_kernel(x_ref, g_ref, w_ref, cos_ref, sin_ref, s0_ref, o_ref, sout_ref, s_scr, *, tb, nb):
    batched = nb > 1
    i = pl.program_id(1)

    if not batched:
        @pl.when(i == 0)
        def _():
            s_scr[...] = s0_ref[0]

    h = _rms(x_ref[...].reshape(nb * tb, D_MODEL)) * g_ref[...]
    z = _dot(h, w_ref[...])
    cosf = cos_ref[...]
    sinf = sin_ref[...]
    row = lax.broadcasted_iota(jnp.int32, (tb, tb), 0)
    col = lax.broadcasted_iota(jnp.int32, (tb, tb), 1)
    diff = row - col
    causal = diff >= 0
    dist = jnp.maximum(diff, 0).astype(F32)
    n = lax.broadcasted_iota(jnp.int32, (tb, 1), 0).astype(F32)
    for hh in range(B_HEADS):
        lg = RET_LOG_GAMMA[hh]
        lo = hh * B_HEAD_DIM
        dmat = jnp.where(causal, jnp.exp(lg * dist), 0.0)
        dec_q = jnp.exp(lg * (n + 1.0))
        dec_k = jnp.exp(lg * ((tb - 1.0) - n))
        for bb in range(nb):
            rs = slice(bb * tb, (bb + 1) * tb)
            q = z[rs, lo:lo + B_HEAD_DIM]
            k = z[rs, B_WIDTH + lo:B_WIDTH + lo + B_HEAD_DIM]
            v = z[rs, 2 * B_WIDTH + lo:2 * B_WIDTH + lo + B_HEAD_DIM]
            gate = z[rs, 3 * B_WIDTH + lo:3 * B_WIDTH + lo + B_HEAD_DIM]
            q = q * cosf + pltpu.roll(q, B_HEAD_DIM // 2, 1) * sinf
            k = (k * cosf + pltpu.roll(k, B_HEAD_DIM // 2, 1) * sinf) * (B_HEAD_DIM ** -0.5)
            scores = _dot(q, k, NT) * dmat
            s_old = s0_ref[bb, hh] if batched else s_scr[hh]
            o = _dot(scores, v) + _dot(q, s_old) * dec_q
            s_new = math.exp(lg * tb) * s_old + _dot(k * dec_k, v, TN)
            if batched:
                sout_ref[bb, hh] = s_new
            else:
                s_scr[hh] = s_new
            o_ref[bb, :, lo:lo + B_HEAD_DIM] = _rms(o) * (gate * _sigmoid(gate))
    if not batched:
        sout_ref[0] = s_scr[...]


def _retention(x, gain, w_b, cosf, sinf, s0, *, tb, nb):
    bsz, seq, _ = x.shape
    assert (seq == tb and bsz % nb == 0) if nb > 1 else seq % tb == 0
    sspec = pl.BlockSpec((nb, B_HEADS, B_HEAD_DIM, B_HEAD_DIM), lambda b, i: (b, 0, 0, 0))
    tok = lambda w: pl.BlockSpec((nb, tb, w), lambda b, i: (b, i, 0))
    return pl.pallas_call(
        functools.partial(_ret_kernel, tb=tb, nb=nb),
        grid=(bsz // nb, seq // tb),
        in_specs=[tok(D_MODEL),
                  _const_spec((1, D_MODEL)), _const_spec((D_MODEL, 4 * B_WIDTH)),
                  pl.BlockSpec((tb, B_HEAD_DIM), lambda b, i: (i, 0)),
                  pl.BlockSpec((tb, B_HEAD_DIM), lambda b, i: (i, 0)),
                  sspec],
        out_specs=[tok(B_WIDTH), sspec],
        out_shape=[jax.ShapeDtypeStruct((bsz, seq, B_WIDTH), F32),
                   jax.ShapeDtypeStruct((bsz, B_HEADS, B_HEAD_DIM, B_HEAD_DIM), F32)],
        scratch_shapes=[pltpu.VMEM((B_HEADS, B_HEAD_DIM, B_HEAD_DIM), F32)],
        compiler_params=pltpu.CompilerParams(
            dimension_semantics=("arbitrary", "arbitrary"),
            vmem_limit_bytes=VMEM_LIMIT_BYTES),
        name="retention",
    )(x, gain, w_b, cosf, sinf, s0)


PAIR_W = 2 * C_HEAD_DIM
N_PAIRS = C_HEADS // 2


def _dot2(a, b, dims=NN):
    a_hi, a_lo = _split_bf16(a)
    b = b.astype(BF16)
    d = functools.partial(lax.dot_general, dimension_numbers=dims, preferred_element_type=F32)
    return d(a_hi, b) + d(a_lo, b)


def _rwkv_kernel(*refs, tt, cc, lookahead, batched):
    if lookahead:
        x_ref, xn_ref = refs[:2]
        refs = refs[2:]
    else:
        x_ref, xn_ref = refs[0], None
        refs = refs[1:]
    (g_ref, w_ref, mu_ref, sh0_ref, s0_ref, w0_ref, w2_ref, a0_ref, a2_ref,
     g2_ref, kk_ref, ka_ref, rk_ref, lnw_ref, lnb_ref, hsum_ref,
     o_ref, sout_ref, shout_ref,
     s_scr, carry, rk_raw, uw_raw, ua_raw, rt_s, kkt_s, kh_s, bh_s, v_s, bon_s, g_s, cum_s) = refs
    i = pl.program_id(1)

    n_chunks = tt // cc

    def project(x, slot):
        h = _rms(x.reshape(tt, D_MODEL)) * g_ref[...]
        cz = _dot(h, w_ref[...])
        rows = lax.broadcasted_iota(jnp.int32, (tt, 1), 0)
        if batched:
            first = jnp.bitwise_and(rows, cc - 1) == 0
            before = jnp.broadcast_to(sh0_ref[...], (n_chunks, cc, C_SHIFT_WIDTH)).reshape(tt, C_SHIFT_WIDTH)
            shout_ref[...] = cz.reshape(n_chunks, cc, C_SHIFT_WIDTH)[:, cc - 1:cc, :]
        else:
            first = rows == 0
            before = carry[...]
            last = cz[tt - 1:tt, :]
            carry[...] = last
            shout_ref[0] = last
        prev = jnp.where(first, before, pltpu.roll(cz, 1, 0))
        cs = cz + (prev - cz) * mu_ref[...]
        off = 3 * C_WIDTH
        w_lo = cs[:, off:off + C_RANK_W]
        a_lo = cs[:, off + C_RANK_W:off + C_RANK_W + C_RANK_A]
        g_lo = cs[:, off + C_RANK_W + C_RANK_A:C_SHIFT_WIDTH]
        rk_raw[...] = cs[:, 0:2 * C_WIDTH]
        v_s[slot] = cs[:, 2 * C_WIDTH:3 * C_WIDTH]
        uw_raw[...] = w0_ref[...] + _dot(jnp.tanh(w_lo), w2_ref[...])
        ua_raw[...] = a0_ref[...] + _dot(a_lo, a2_ref[...])
        g_s[slot] = _dot(_sigmoid(g_lo), g2_ref[...])

    def token_block(b, slot):
        rs = slice(b * cc, (b + 1) * cc)
        r = rk_raw[rs, 0:C_WIDTH]
        k = rk_raw[rs, C_WIDTH:2 * C_WIDTH]
        v = v_s[slot, rs, :]
        lw = -math.exp(-0.5) * _sigmoid(uw_raw[rs, :])
        a = _sigmoid(ua_raw[rs, :])
        hsum = hsum_ref[...]
        kk_raw = k * kk_ref[...]
        kk = kk_raw / jnp.maximum(jnp.sqrt(_head_sum(kk_raw * kk_raw, hsum)), 1e-12)
        k2 = k * (1.0 + (a - 1.0) * ka_ref[...])
        pos = lax.broadcasted_iota(jnp.int32, (cc, 1), 0)
        cum = lw
        step = 1
        while step < cc:
            cum = cum + jnp.where(pos >= step, pltpu.roll(cum, step, 0), 0.0)
            step *= 2
        e_inv = jnp.exp(-cum)
        rt_s[slot, rs, :] = r * jnp.exp(cum)
        kkt_s[slot, rs, :] = kk * jnp.exp(cum - lw)
        kh_s[slot, rs, :] = k2 * e_inv
        bh_s[slot, rs, :] = kk * a * e_inv
        bon_s[slot, rs, :] = _head_sum(r * k2 * rk_ref[...], hsum) * v
        cum_s[slot, rs, :] = cum

    if not batched:
        @pl.when(i == 0)
        def _():
            s_scr[...] = s0_ref[0]
            carry[...] = sh0_ref[0]
            if lookahead:
                project(x_ref[...], 0)
                for b in range(n_chunks):
                    token_block(b, 0)

    if lookahead:
        cur = lax.rem(i, 2)
        nxt = 1 - cur
        project(xn_ref[...], nxt)
        pending = [functools.partial(token_block, b, nxt) for b in range(n_chunks)]
    else:
        cur = 0
        project(x_ref[...], 0)
        for b in range(n_chunks):
            token_block(b, 0)
        pending = []

    def interleave():
        if pending:
            pending.pop(0)()

    two = 2 * cc
    sh = cc.bit_length() - 1
    hd = C_HEAD_DIM.bit_length() - 1
    n_double = sh - 1
    rho = lax.broadcasted_iota(jnp.int32, (two, PAIR_W), 0)
    lane = lax.broadcasted_iota(jnp.int32, (two, PAIR_W), 1)
    placed = (rho >> sh) == (lane >> hd)
    r2 = lax.broadcasted_iota(jnp.int32, (two, two), 0)
    c2 = lax.broadcasted_iota(jnp.int32, (two, two), 1)
    same = (r2 >> sh) == (c2 >> sh)
    strict = jnp.logical_and(same, r2 > c2)
    incl = jnp.logical_and(same, r2 >= c2)
    eye = jnp.where(r2 == c2, 1.0, 0.0).astype(F32)
    gr = lax.broadcasted_iota(jnp.int32, (PAIR_W, PAIR_W), 0)
    gc = lax.broadcasted_iota(jnp.int32, (PAIR_W, PAIR_W), 1)
    hmean = jnp.where((gr >> hd) == (gc >> hd), 1.0 / C_HEAD_DIM, 0.0).astype(BF16)

    def place(xv):
        return jnp.where(placed, jnp.concatenate([xv, xv], axis=0), 0.0)

    per_trip = 4 if n_chunks % 4 == 0 else (2 if n_chunks % 2 == 0 else 1)
    units = [(c, p) for c in range(per_trip) for p in range(N_PAIRS)]
    pairs = range(len(units))

    def chunk(j):
        r0s = [(j * per_trip + c) * cc for c in range(per_trip)]
        w_end = [jnp.exp(cum_s[cur, r0 + cc - 1:r0 + cc, :]) for r0 in r0s]
        lanes = [slice(p * PAIR_W, (p + 1) * PAIR_W) for _, p in units]
        sl = [(cur, slice(r0s[c], r0s[c] + cc), lanes[u]) for u, (c, _) in enumerate(units)]
        we = [w_end[c][:, lanes[u]] for u, (c, _) in enumerate(units)]
        rt = [place(rt_s[sl[p]]) for p in pairs]
        kkt = [place(kkt_s[sl[p]]) for p in pairs]
        kh = [place(kh_s[sl[p]]) for p in pairs]
        bh = [place(bh_s[sl[p]]) for p in pairs]
        vv = [place(v_s[sl[p]]) for p in pairs]
        cat0 = lambda *xs: jnp.concatenate(xs, axis=0)
        cat1 = lambda *xs: jnp.concatenate(xs, axis=1)
        gg = [_dot(cat0(kkt[p], rt[p]), cat0(kh[p], bh[p]), NT) for p in pairs]
        interleave()
        a_kk = [jnp.where(strict, gg[p][:two, :two], 0.0) for p in pairs]
        b_kk = [jnp.where(incl, gg[p][two:, :two], 0.0) for p in pairs]
        b_bb = [jnp.where(incl, gg[p][two:, two:], 0.0) for p in pairs]
        pw = [jnp.where(strict, -gg[p][:two, two:], 0.0) for p in pairs]
        tinv = [eye + pw[p] for p in pairs]
        pw = [_dot(pw[p], pw[p]) for p in pairs]
        for _ in range(n_double - 1):
            pt = [_dot(pw[p], cat1(pw[p], tinv[p])) for p in pairs]
            pw = [pt[p][:, :two] for p in pairs]
            tinv = [tinv[p] + pt[p][:, two:] for p in pairs]
            interleave()
        tinv = [tinv[p] + _dot(pw[p], tinv[p]) for p in pairs]
        interleave()
        av = [_dot(a_kk[p], vv[p]) for p in pairs]
        kv = [_dot(tinv[p], cat1(kkt[p], av[p])) for p in pairs]
        zero = jnp.zeros((two, PAIR_W), F32)
        ry = [_dot(cat1(b_kk[p], -b_bb[p]), cat0(cat1(zero, vv[p]), kv[p])) for p in pairs]
        rp = [rt[p] + ry[p][:, :PAIR_W] for p in pairs]
        y0 = [ry[p][:, PAIR_W:] for p in pairs]
        kkp = [kv[p][:, :PAIR_W] for p in pairs]
        vp = [kv[p][:, PAIR_W:] for p in pairs]
        bd = [bh[p] * we[p] for p in pairs]
        kd = [kh[p] * we[p] for p in pairs]
        q = [_dot(kkp[p], bd[p], TN) for p in pairs]
        z = [_dot(cat0(vv[p], vp[p]), cat0(kd[p], -bd[p]), TN) for p in pairs]
        y2 = []
        if batched:
            for u, (c, p) in enumerate(units):
                s_old = s0_ref[j * per_trip + c, p]
                y2.append(_dot(rp[u], s_old, NT) + y0[u])
                sout_ref[j * per_trip + c, p] = s_old * we[u] - _dot(s_old, q[u]) + z[u]
        else:
            state = [s_scr[p] for p in range(N_PAIRS)]
            for u, (_, p) in enumerate(units):
                s_old = state[p]
                y2.append(_dot(rp[u], s_old, NT) + y0[u])
                state[p] = s_old * we[u] - _dot(s_old, q[u]) + z[u]
            for p in range(N_PAIRS):
                s_scr[p] = state[p]
        ys = [y2[u][:cc] + y2[u][cc:] for u in pairs]
        yc = [ys[u] - _dot(ys[u], hmean) for u in pairs]
        var = [_dot(yc[u] * yc[u], hmean) for u in pairs]
        for u, (c, _) in enumerate(units):
            yn = yc[u] * lax.rsqrt(var[u] + C_GN_EPS) * lnw_ref[:, lanes[u]] + lnb_ref[:, lanes[u]]
            out = (yn + bon_s[sl[u]]) * g_s[sl[u]]
            if batched:
                o_ref[j * per_trip + c, :, lanes[u]] = out
            else:
                o_ref[(0,) + sl[u][1:]] = out

    for j in range(n_chunks // per_trip):
        chunk(j)
    while pending:
        interleave()
    if not batched:
        sout_ref[0] = s_scr[...]


def _pair_states(s):
    bsz = s.shape[0]
    s = s.reshape(bsz, N_PAIRS, 2, C_HEAD_DIM, C_HEAD_DIM)
    zero = jnp.zeros_like(s[:, :, 0])
    top = jnp.concatenate([s[:, :, 0], zero], axis=-1)
    bot = jnp.concatenate([zero, s[:, :, 1]], axis=-1)
    return jnp.concatenate([top, bot], axis=-2)


def _unpair_states(s):
    bsz = s.shape[0]
    d = C_HEAD_DIM
    return jnp.stack([s[:, :, :d, :d], s[:, :, d:, d:]], axis=2).reshape(bsz, C_HEADS, d, d)


def _rwkv(x, gain, w_c, lw, shift0, s0, *, tt, cc, nb):
    bsz, seq, _ = x.shape
    batched = nb > 1
    assert cc & (cc - 1) == 0
    assert (seq == cc and tt == nb * cc and bsz % nb == 0) if batched else (seq % tt == 0 and tt % cc == 0)
    row = lambda a: a.reshape(1, -1)
    sspec = pl.BlockSpec((nb, N_PAIRS, PAIR_W, PAIR_W), lambda b, i: (b, 0, 0, 0))
    shspec = pl.BlockSpec((nb, 1, C_SHIFT_WIDTH), lambda b, i: (b, 0, 0))
    vec = _const_spec((1, C_WIDTH))
    n_tiles = 1 if batched else seq // tt
    lookahead = n_tiles > 1
    tile = pltpu.VMEM((2 if lookahead else 1, tt, C_WIDTH), F32)
    hsum = _head_sum_matrix(C_HEAD_DIM)
    tok = lambda w: pl.BlockSpec((nb, tt // nb, w), lambda b, i: (b, i, 0))
    x_specs = [tok(D_MODEL)]
    x_args = [x]
    if lookahead:
        x_specs.append(pl.BlockSpec((1, tt, D_MODEL), lambda b, i: (b, jnp.minimum(i + 1, n_tiles - 1), 0)))
        x_args.append(x)
    oc, s_new, shift_new = pl.pallas_call(
        functools.partial(_rwkv_kernel, tt=tt, cc=cc, lookahead=lookahead, batched=batched),
        grid=(bsz // nb, n_tiles),
        in_specs=x_specs + [
                  _const_spec((1, D_MODEL)), _const_spec((D_MODEL, C_SHIFT_WIDTH)),
                  _const_spec((1, C_SHIFT_WIDTH)), shspec, sspec,
                  vec, _const_spec((C_RANK_W, C_WIDTH)), vec, _const_spec((C_RANK_A, C_WIDTH)),
                  _const_spec((C_RANK_G, C_WIDTH)), vec, vec, vec, vec, vec,
                  _const_spec((MXU_TILE, MXU_TILE))],
        out_specs=[tok(C_WIDTH), sspec, shspec],
        out_shape=[jax.ShapeDtypeStruct((bsz, seq, C_WIDTH), F32),
                   jax.ShapeDtypeStruct((bsz, N_PAIRS, PAIR_W, PAIR_W), F32),
                   jax.ShapeDtypeStruct((bsz, 1, C_SHIFT_WIDTH), F32)],
        scratch_shapes=[pltpu.VMEM((N_PAIRS, PAIR_W, PAIR_W), F32),
                        pltpu.VMEM((1, C_SHIFT_WIDTH), F32),
                        pltpu.VMEM((tt, 2 * C_WIDTH), F32),
                        pltpu.VMEM((tt, C_WIDTH), F32),
                        pltpu.VMEM((tt, C_WIDTH), F32)] + [tile] * 8,
        compiler_params=pltpu.CompilerParams(
            dimension_semantics=("arbitrary", "arbitrary"),
            vmem_limit_bytes=VMEM_LIMIT_BYTES),
        name="rwkv",
    )(*x_args, gain, w_c, row(lw["c_shift_mu"]), shift0, _pair_states(s0),
      row(lw["c_w0"]), lw["c_w2"].astype(BF16), row(lw["c_a0"]), lw["c_a2"].astype(BF16),
      lw["c_g2"].astype(BF16), row(lw["c_k_k"]), row(lw["c_k_a"]), row(lw["c_r_k"]),
      row(lw["c_ln_w"]), row(lw["c_ln_b"]), hsum)
    return oc, _unpair_states(s_new), shift_new


def _merge_kernel(x_ref, oa_ref, ob_ref, oc_ref, g_ref, wg_ref, wb_ref, wo_ref, y_ref):
    x = x_ref[...]
    h = _rms(x) * g_ref[...]
    gl = _dot(h, wg_ref[...])
    m = None
    for b, o_ref in enumerate((oa_ref, ob_ref, oc_ref)):
        t = _sigmoid(gl[:, b * D_MODEL:(b + 1) * D_MODEL]) * _dot(o_ref[...], wb_ref[b])
        m = t if m is None else m + t
    y_ref[...] = x + _dot(m, wo_ref[...])


def _merge(x, oa, ob, oc, gain, w_g, w_b, w_o, *, tm):
    rows = x.shape[0]
    assert rows % tm == 0
    tok = lambda w: pl.BlockSpec((tm, w), lambda i: (i, 0))
    return pl.pallas_call(
        _merge_kernel,
        grid=(rows // tm,),
        in_specs=[tok(D_MODEL), tok(A_WIDTH), tok(B_WIDTH), tok(C_WIDTH),
                  _const_spec((1, D_MODEL)), _const_spec((D_MODEL, N_BRANCHES * D_MODEL)),
                  _const_spec((N_BRANCHES, A_WIDTH, D_MODEL)), _const_spec((D_MODEL, D_MODEL))],
        out_specs=tok(D_MODEL),
        out_shape=jax.ShapeDtypeStruct((rows, D_MODEL), F32),
        compiler_params=pltpu.CompilerParams(
            dimension_semantics=("arbitrary",), vmem_limit_bytes=VMEM_LIMIT_BYTES),
        name="merge",
    )(x, oa, ob, oc, gain, w_g, w_b, w_o)


def _ffn_kernel(x_ref, p_ref, g_ref, wgate_ref, wup_ref, wdown_ref, wpp_ref, pg_ref, wpg_ref, y_ref):
    x = x_ref[...]
    hf = (_rms(x) * g_ref[...]).astype(BF16)
    gate = _dot(hf, wgate_ref[...])
    up = _dot(hf, wup_ref[...])
    x = x + _dot(gate * _sigmoid(gate) * up, wdown_ref[...])
    e = _rms(_dot(p_ref[...], wpp_ref[...])) * pg_ref[...]
    y_ref[...] = x + _sigmoid(_dot(_rms(x), wpg_ref[...])) * e


def _ffn(x, p_all, layer, gain, w_gate, w_up, w_down, w_pp, p_gain, w_pg, *, tm):
    rows = x.shape[0]
    d_ff = w_gate.shape[1]
    ple = p_all.shape[-1]
    p = p_all.reshape(p_all.shape[0], rows, ple)
    assert rows % tm == 0
    tok = lambda w: pl.BlockSpec((tm, w), lambda i: (i, 0))
    return pl.pallas_call(
        _ffn_kernel,
        grid=(rows // tm,),
        in_specs=[tok(D_MODEL), pl.BlockSpec((None, tm, ple), lambda i: (layer, i, 0)),
                  _const_spec((1, D_MODEL)),
                  _const_spec((D_MODEL, d_ff)), _const_spec((D_MODEL, d_ff)),
                  _const_spec((d_ff, D_MODEL)), _const_spec((ple, D_MODEL)),
                  _const_spec((1, D_MODEL)), _const_spec((D_MODEL, D_MODEL))],
        out_specs=tok(D_MODEL),
        out_shape=jax.ShapeDtypeStruct((rows, D_MODEL), F32),
        compiler_params=pltpu.CompilerParams(
            dimension_semantics=("arbitrary",), vmem_limit_bytes=VMEM_LIMIT_BYTES),
        name="ffn",
    )(x, p, gain, w_gate, w_up, w_down, w_pp, p_gain, w_pg)


def _rel_bias_table(rel_bias, cq):
    nkeys = A_WIN + cq
    t_max = A_WIN + CHUNK - 1 + cq - 1
    heads, n_rel = rel_bias.shape
    w = t_max + 1
    tail = jnp.broadcast_to(rel_bias[:, n_rel - 1:], (heads, w - n_rel))
    rev = jnp.concatenate([tail, rel_bias[:, ::-1].astype(F32), jnp.zeros((heads, 1), F32)], axis=1)
    skew = jnp.tile(rev, (1, cq))[:, :cq * w].reshape(heads, cq, w)
    return skew[:, :, cq - 1:cq - 1 + nkeys] * LOG2_E


def _rotary_tables(pos0, seq):
    half = B_HEAD_DIM // 2
    pos = pos0 + jnp.arange(seq, dtype=jnp.int32)
    inv = ROPE_BASE ** (-jnp.arange(half, dtype=F32) / half)
    ang = pos.astype(F32)[:, None] * inv[None, :]
    cos, sin = jnp.cos(ang), jnp.sin(ang)
    return jnp.concatenate([cos, cos], axis=-1), jnp.concatenate([-sin, sin], axis=-1)


def _layer(x, p_all, layer, pos0, a_ck, a_cv, ret_s0, rwkv_s0, shift_prev, lw, cfg):
    bsz, seq, _ = x.shape
    row = lambda a: a.reshape(1, -1)
    w_in = lw["w_in"].astype(BF16)
    w_a = w_in[:, OFF_A:OFF_B]
    w_b = w_in[:, OFF_B:OFF_C]
    w_c = w_in[:, OFF_C:OFF_G]
    w_g = w_in[:, OFF_G:]
    gain = row(lw["norm_mix"])

    nb = math.gcd(bsz, cfg["nb"])
    bias = _rel_bias_table(lw["a_rel_bias"], cfg["cq"])
    if a_ck is not None:
        a_ck = a_ck.reshape(a_ck.shape[0], bsz, A_WIN, A_WIDTH)
        a_cv = a_cv.reshape(a_cv.shape[0], bsz, A_WIN, A_WIDTH)
    oa, kn, av = _attention(x, gain, w_a, row(lw["a_q_norm"]), row(lw["a_k_norm"]), bias,
                            a_ck, a_cv, layer, tq=cfg["tq"], cq=cfg["cq"], nb=nb)
    keep = min(A_WIN, seq)
    new_ak = kn[:, seq - keep:].reshape(bsz, keep, A_HEADS, A_HEAD_DIM)
    new_av = av[:, seq - keep:].reshape(bsz, keep, A_HEADS, A_HEAD_DIM)

    cosf, sinf = _rotary_tables(pos0, seq)
    ob, new_ret = _retention(x, gain, w_b, cosf, sinf, ret_s0, tb=cfg["tb"], nb=nb)

    oc, new_rwkv, new_shift = _rwkv(x, gain, w_c, lw, shift_prev, rwkv_s0,
                                    tt=nb * cfg["tt"], cc=cfg["cc"], nb=nb)

    rows = bsz * seq
    tm = min(cfg["tm"], rows)
    flat = lambda t: t.reshape(rows, t.shape[-1])
    x1 = _merge(flat(x), flat(oa), flat(ob), flat(oc), gain, w_g,
                lw["w_branch"].astype(BF16), lw["w_out"].astype(BF16), tm=tm)
    x2 = _ffn(x1, p_all, layer, row(lw["norm_ffn"]), lw["w_ffn_gate"].astype(BF16),
              lw["w_ffn_up"].astype(BF16), lw["w_ffn_down"].astype(BF16),
              lw["w_ple_proj"].astype(BF16), row(lw["ple_norm"]), lw["w_ple_gate"].astype(BF16),
              tm=tm)
    return x2.reshape(bsz, seq, D_MODEL), (new_ak, new_av, new_ret, new_rwkv, new_shift)


def _group_config(seq):
    if seq >= A_WIN:
        return dict(tq=A_WIN, cq=CHUNK, tb=256, tt=256, cc=CHUNK, tm=512, nb=1)
    return dict(tq=seq, cq=seq, tb=seq, tt=seq, cc=seq, tm=256, nb=8)


def kernel(x_prompt, x_sample, p_prompt, p_sample, cache_a_k, cache_a_v, state_ret, state_rwkv, state_rwkv_shift, norm_mix, w_in, a_q_norm, a_k_norm, a_rel_bias, c_shift_mu, c_w0, c_w2, c_a0, c_a2, c_g2, c_k_k, c_k_a, c_r_k, c_ln_w, c_ln_b, w_branch, w_out, norm_ffn, w_ffn_gate, w_ffn_up, w_ffn_down, w_ple_proj, ple_norm, w_ple_gate):
    depth = w_in.shape[0]

    def layer_weights(i):
        return dict(norm_mix=norm_mix[i], w_in=w_in[i], a_q_norm=a_q_norm[i], a_k_norm=a_k_norm[i],
                    a_rel_bias=a_rel_bias[i], c_shift_mu=c_shift_mu[i], c_w0=c_w0[i], c_w2=c_w2[i],
                    c_a0=c_a0[i], c_a2=c_a2[i], c_g2=c_g2[i], c_k_k=c_k_k[i], c_k_a=c_k_a[i], c_r_k=c_r_k[i],
                    c_ln_w=c_ln_w[i], c_ln_b=c_ln_b[i], w_branch=w_branch[i], w_out=w_out[i],
                    norm_ffn=norm_ffn[i], w_ffn_gate=w_ffn_gate[i], w_ffn_up=w_ffn_up[i],
                    w_ffn_down=w_ffn_down[i], w_ple_proj=w_ple_proj[i], ple_norm=ple_norm[i],
                    w_ple_gate=w_ple_gate[i])

    bp, lp, _ = x_prompt.shape
    cfg_p = _group_config(lp)
    ret0 = jnp.zeros((bp, B_HEADS, B_HEAD_DIM, B_HEAD_DIM), F32)
    rwkv0 = jnp.zeros((bp, C_HEADS, C_HEAD_DIM, C_HEAD_DIM), F32)
    shift0 = jnp.zeros((bp, 1, C_SHIFT_WIDTH), F32)
    y_prompt = x_prompt
    st_p = []
    for i in range(depth):
        y_prompt, st = _layer(y_prompt, p_prompt, i, 0, None, None, ret0, rwkv0, shift0,
                              layer_weights(i), cfg_p)
        st_p.append(st)

    cfg_s = _group_config(x_sample.shape[1])
    y_sample = x_sample
    st_s = []
    for i in range(depth):
        y_sample, st = _layer(y_sample, p_sample, i, PAST_LEN, cache_a_k, cache_a_v, state_ret[i],
                              state_rwkv[i], state_rwkv_shift[i], layer_weights(i), cfg_s)
        st_s.append(st)

    stack = lambda sts, j: jnp.stack([s[j] for s in sts])
    return (y_prompt, y_sample,
            stack(st_p, 0), stack(st_p, 1), stack(st_p, 2), stack(st_p, 3), stack(st_p, 4),
            stack(st_s, 0), stack(st_s, 1), stack(st_s, 2), stack(st_s, 3), stack(st_s, 4))
```

```python
import functools
import math

import jax
import jax.numpy as jnp
from jax import lax
from jax.experimental import pallas as pl
from jax.experimental.pallas import tpu as pltpu

F32 = jnp.float32
BF16 = jnp.bfloat16

D_MODEL = 1024
PAST_LEN = 2048
CHUNK = 64
NORM_EPS = 1e-6

A_HEADS = 8
A_HEAD_DIM = 64
A_WIDTH = 512
A_WIN = 512
A_REL_MAX = 256

B_HEADS = 4
B_HEAD_DIM = 128
B_WIDTH = 512
ROPE_BASE = 10000.0

C_HEADS = 8
C_HEAD_DIM = 64
C_WIDTH = 512
C_RANK_W = 64
C_RANK_A = 64
C_RANK_G = 128
C_SHIFT_WIDTH = 3 * C_WIDTH + C_RANK_W + C_RANK_A + C_RANK_G
C_GN_EPS = 64e-5

N_BRANCHES = 3

OFF_A = 0
OFF_B = 3 * A_WIDTH
OFF_C = OFF_B + 4 * B_WIDTH
OFF_G = OFF_C + C_SHIFT_WIDTH
IN_WIDTH = OFF_G + N_BRANCHES * D_MODEL

RET_LOG_GAMMA = tuple(math.log1p(-(2.0 ** (-5.0 - h))) for h in range(B_HEADS))

VMEM_LIMIT_BYTES = 56 * 1024 * 1024
MXU_TILE = 256
LOG2_E = math.log2(math.e)

NN = (((1,), (0,)), ((), ()))
NT = (((1,), (1,)), ((), ()))
TN = (((0,), (0,)), ((), ()))


def _dot(a, b, dims=NN):
    return lax.dot_general(a.astype(BF16), b.astype(BF16), dims, preferred_element_type=F32)


def _split_bf16(a):
    hi = a.astype(BF16)
    lo = (a - hi.astype(F32)).astype(BF16)
    return hi, lo


def _dot3(a, b, dims=NN):
    a_hi, a_lo = _split_bf16(a)
    b_hi, b_lo = _split_bf16(b)
    d = functools.partial(lax.dot_general, dimension_numbers=dims, preferred_element_type=F32)
    return d(a_hi, b_hi) + (d(a_hi, b_lo) + d(a_lo, b_hi))


def _rms(x):
    return x * lax.rsqrt(jnp.mean(x * x, axis=-1, keepdims=True) + NORM_EPS)


def _sigmoid(x):
    return 1.0 / (1.0 + jnp.exp(-x))


def _const_spec(shape):
    nd = len(shape)
    return pl.BlockSpec(shape, lambda *_: (0,) * nd, pipeline_mode=pl.Buffered(1))


def _layer_weight(stacked, layer, cols=None):
    if cols is not None:
        off, width = cols
        if off % width == 0:
            spec = pl.BlockSpec((None, stacked.shape[1], width), lambda *_: (layer, 0, off // width),
                                pipeline_mode=pl.Buffered(1))
            return stacked, spec
        stacked = stacked[:, :, off:off + width]
    shape = stacked.shape[1:]
    spec = pl.BlockSpec((None,) + shape, lambda *_: (layer,) + (0,) * len(shape),
                        pipeline_mode=pl.Buffered(1))
    return stacked, spec


def _head_sum_matrix(head_dim):
    head = jnp.arange(MXU_TILE, dtype=jnp.int32) // head_dim
    return (head[:, None] == head[None, :]).astype(BF16)


def _head_sum(x, hsum):
    w = hsum.shape[0]
    return jnp.concatenate([_dot(x[:, g * w:(g + 1) * w], hsum) for g in range(x.shape[1] // w)], axis=1)


def _attn_kernel(*refs, tq, cq, has_cache, nb):
    if has_cache:
        (x_ref, g_ref, w_ref, qg_ref, kg_ref, bias_ref, hsum_ref, kc_ref, vc_ref,
         o_ref, kn_ref, v_ref, kwin, vwin, qs) = refs
    else:
        (x_ref, g_ref, w_ref, qg_ref, kg_ref, bias_ref, hsum_ref,
         o_ref, kn_ref, v_ref, kwin, vwin, qs) = refs
    i = pl.program_id(1)
    nkeys = A_WIN + cq
    pair_w = 2 * A_HEAD_DIM

    if has_cache:
        kwin[:, 0:A_WIN, :] = kc_ref[...].astype(BF16)
        vwin[:, 0:A_WIN, :] = vc_ref[...].astype(BF16)
    else:
        @pl.when(i == 0)
        def _():
            kwin[0:A_WIN, :] = jnp.zeros((A_WIN, A_WIDTH), BF16)
            vwin[0:A_WIN, :] = jnp.zeros((A_WIN, A_WIDTH), BF16)

    h = _rms(x_ref[...].reshape(nb * tq, D_MODEL)) * g_ref[...]
    z = _dot(h, w_ref[...])
    q = z[:, 0:A_WIDTH]
    k = z[:, A_WIDTH:2 * A_WIDTH]
    v = z[:, 2 * A_WIDTH:3 * A_WIDTH]
    hsum = hsum_ref[...]
    inv_d = 1.0 / A_HEAD_DIM
    qn = q * lax.rsqrt(_head_sum(q * q, hsum) * inv_d + NORM_EPS) * qg_ref[...]
    kn = k * lax.rsqrt(_head_sum(k * k, hsum) * inv_d + NORM_EPS) * kg_ref[...]
    qs[...] = qn * (A_HEAD_DIM ** -0.5 * LOG2_E)
    if has_cache:
        kwin[:, A_WIN:A_WIN + tq, :] = kn.astype(BF16).reshape(nb, tq, A_WIDTH)
        vwin[:, A_WIN:A_WIN + tq, :] = v.astype(BF16).reshape(nb, tq, A_WIDTH)
    else:
        kwin[A_WIN:A_WIN + tq, :] = kn.astype(BF16)
        vwin[A_WIN:A_WIN + tq, :] = v.astype(BF16)
    kn_ref[...] = kn.reshape(nb, tq, A_WIDTH)
    v_ref[...] = v.reshape(nb, tq, A_WIDTH)

    rho = lax.broadcasted_iota(jnp.int32, (2 * cq, pair_w), 0)
    lane = lax.broadcasted_iota(jnp.int32, (2 * cq, pair_w), 1)
    placed = (rho >> (cq.bit_length() - 1)) == (lane >> (A_HEAD_DIM.bit_length() - 1))

    n_chunks = nb * tq // cq
    per_trip = 4 if n_chunks % 4 == 0 else (2 if n_chunks % 2 == 0 else 1)
    units = [(c, p) for c in range(per_trip) for p in range(A_HEADS // 2)]
    ids = range(len(units))

    def chunk(j, carry):
        r0s = [pl.multiple_of((j * per_trip + c) * cq, cq) for c in range(per_trip)]
        lanes = [slice(p * pair_w, (p + 1) * pair_w) for _, p in units]
        qc = [qs[pl.ds(r0s[c], cq), lanes[u]] for u, (c, _) in enumerate(units)]
        qp = [jnp.where(placed, jnp.concatenate([qc[u], qc[u]], axis=0), 0.0) for u in ids]
        if has_cache:
            win = [(j * per_trip + c, slice(None), lanes[u]) for u, (c, _) in enumerate(units)]
        else:
            win = [(pl.ds(r0s[c], nkeys), lanes[u]) for u, (c, _) in enumerate(units)]
        s = [_dot(qp[u], kwin[win[u]], NT) + bias_ref[p] for u, (_, p) in enumerate(units)]
        if not has_cache:
            col = lax.broadcasted_iota(jnp.int32, (2 * cq, nkeys), 1)
            ok = [jnp.logical_or(r0 + col >= A_WIN, i > 0) for r0 in r0s]
            s = [jnp.where(ok[c], s[u], -1e30) for u, (c, _) in enumerate(units)]
        m = [jnp.max(s[u], axis=-1, keepdims=True) for u in ids]
        e = [jnp.exp2(s[u] - m[u]) for u in ids]
        den = [jnp.sum(e[u], axis=-1, keepdims=True) for u in ids]
        o2 = [_dot(e[u], vwin[win[u]]) for u in ids]
        for u, (c, _) in enumerate(units):
            o = jnp.where(placed, o2[u] / den[u], 0.0)
            if has_cache:
                o_ref[j * per_trip + c, :, lanes[u]] = o[:cq] + o[cq:]
            else:
                o_ref[0, pl.ds(r0s[c], cq), lanes[u]] = o[:cq] + o[cq:]
        return carry

    lax.fori_loop(0, n_chunks // per_trip, chunk, 0)

    if not has_cache:
        kwin[0:A_WIN, :] = kwin[tq:tq + A_WIN, :]
        vwin[0:A_WIN, :] = vwin[tq:tq + A_WIN, :]


def _attention(x, gain, w_a, q_gain, k_gain, bias, cache_k, cache_v, layer, *, tq, cq, nb):
    bsz, seq, _ = x.shape
    has_cache = cache_k is not None
    assert seq % tq == 0 and tq % cq == 0
    assert (seq == tq == cq and bsz % nb == 0) if has_cache else (tq == A_WIN and nb == 1)
    nkeys = A_WIN + cq
    tok = lambda w: pl.BlockSpec((nb, tq, w), lambda b, i: (b, i, 0))
    win_shape = (nb, nkeys, A_WIDTH) if has_cache else (A_WIN + tq, A_WIDTH)
    in_specs = [tok(D_MODEL), _const_spec((1, D_MODEL)), w_a[1],
                _const_spec((1, A_WIDTH)), _const_spec((1, A_WIDTH)),
                _const_spec((A_HEADS // 2, 2 * cq, nkeys)), _const_spec((MXU_TILE, MXU_TILE))]
    args = [x, gain, w_a[0], jnp.tile(q_gain, (1, A_HEADS)), jnp.tile(k_gain, (1, A_HEADS)),
            bias.reshape(A_HEADS // 2, 2 * cq, nkeys), _head_sum_matrix(A_HEAD_DIM)]
    if has_cache:
        cspec = pl.BlockSpec((None, nb, A_WIN, A_WIDTH), lambda b, i: (layer, b, 0, 0))
        in_specs += [cspec, cspec]
        args += [cache_k, cache_v]
    out = jax.ShapeDtypeStruct((bsz, seq, A_WIDTH), F32)
    return pl.pallas_call(
        functools.partial(_attn_kernel, tq=tq, cq=cq, has_cache=has_cache, nb=nb),
        grid=(bsz // nb, seq // tq),
        in_specs=in_specs,
        out_specs=[tok(A_WIDTH)] * 3,
        out_shape=[out] * 3,
        scratch_shapes=[pltpu.VMEM(win_shape, BF16),
                        pltpu.VMEM(win_shape, BF16),
                        pltpu.VMEM((nb * tq, A_WIDTH), F32)],
        compiler_params=pltpu.CompilerParams(
            dimension_semantics=("arbitrary", "arbitrary"),
            vmem_limit_bytes=VMEM_LIMIT_BYTES),
        name="attention",
    )(*args)


def _ret_kernel(x_ref, g_ref, w_ref, cos_ref, sin_ref, s0_ref, o_ref, sout_ref, s_scr, dmat_s, *, tb, nb):
    batched = nb > 1
    i = pl.program_id(1)

    if not batched:
        @pl.when(i == 0)
        def _():
            s_scr[...] = s0_ref[0]

    @pl.when(jnp.logical_and(pl.program_id(0) == 0, i == 0))
    def _():
        row = lax.broadcasted_iota(jnp.int32, (tb, tb), 0)
        col = lax.broadcasted_iota(jnp.int32, (tb, tb), 1)
        diff = row - col
        dist = jnp.maximum(diff, 0).astype(F32)
        for hh in range(B_HEADS):
            dmat_s[hh] = jnp.where(diff >= 0, jnp.exp(RET_LOG_GAMMA[hh] * dist), 0.0)

    h = _rms(x_ref[...].reshape(nb * tb, D_MODEL)) * g_ref[...]
    z = _dot(h, w_ref[...])
    cosf = cos_ref[...]
    sinf = sin_ref[...]
    n = lax.broadcasted_iota(jnp.int32, (tb, 1), 0).astype(F32)
    for hh in range(B_HEADS):
        lg = RET_LOG_GAMMA[hh]
        lo = hh * B_HEAD_DIM
        dmat = dmat_s[hh]
        dec_q = jnp.exp(lg * (n + 1.0))
        dec_k = jnp.exp(lg * ((tb - 1.0) - n))
        for bb in range(nb):
            rs = slice(bb * tb, (bb + 1) * tb)
            q = z[rs, lo:lo + B_HEAD_DIM]
            k = z[rs, B_WIDTH + lo:B_WIDTH + lo + B_HEAD_DIM]
            v = z[rs, 2 * B_WIDTH + lo:2 * B_WIDTH + lo + B_HEAD_DIM]
            gate = z[rs, 3 * B_WIDTH + lo:3 * B_WIDTH + lo + B_HEAD_DIM]
            q = q * cosf + pltpu.roll(q, B_HEAD_DIM // 2, 1) * sinf
            k = (k * cosf + pltpu.roll(k, B_HEAD_DIM // 2, 1) * sinf) * (B_HEAD_DIM ** -0.5)
            scores = _dot(q, k, NT) * dmat
            s_old = s0_ref[bb, hh] if batched else s_scr[hh]
            o = _dot(scores, v) + _dot(q, s_old) * dec_q
            s_new = math.exp(lg * tb) * s_old + _dot(k * dec_k, v, TN)
            if batched:
                sout_ref[bb, hh] = s_new
            else:
                s_scr[hh] = s_new
            o_ref[bb, :, lo:lo + B_HEAD_DIM] = _rms(o) * (gate * _sigmoid(gate))
    if not batched:
        sout_ref[0] = s_scr[...]


def _retention(x, gain, w_b, cosf, sinf, s0, *, tb, nb):
    bsz, seq, _ = x.shape
    assert (seq == tb and bsz % nb == 0) if nb > 1 else seq % tb == 0
    sspec = pl.BlockSpec((nb, B_HEADS, B_HEAD_DIM, B_HEAD_DIM), lambda b, i: (b, 0, 0, 0))
    tok = lambda w: pl.BlockSpec((nb, tb, w), lambda b, i: (b, i, 0))
    return pl.pallas_call(
        functools.partial(_ret_kernel, tb=tb, nb=nb),
        grid=(bsz // nb, seq // tb),
        in_specs=[tok(D_MODEL),
                  _const_spec((1, D_MODEL)), w_b[1],
                  pl.BlockSpec((tb, B_HEAD_DIM), lambda b, i: (i, 0)),
                  pl.BlockSpec((tb, B_HEAD_DIM), lambda b, i: (i, 0)),
                  sspec],
        out_specs=[tok(B_WIDTH), sspec],
        out_shape=[jax.ShapeDtypeStruct((bsz, seq, B_WIDTH), F32),
                   jax.ShapeDtypeStruct((bsz, B_HEADS, B_HEAD_DIM, B_HEAD_DIM), F32)],
        scratch_shapes=[pltpu.VMEM((B_HEADS, B_HEAD_DIM, B_HEAD_DIM), F32),
                        pltpu.VMEM((B_HEADS, tb, tb), F32)],
        compiler_params=pltpu.CompilerParams(
            dimension_semantics=("arbitrary", "arbitrary"),
            vmem_limit_bytes=VMEM_LIMIT_BYTES),
        name="retention",
    )(x, gain, w_b[0], cosf, sinf, s0)


PAIR_W = 2 * C_HEAD_DIM
N_PAIRS = C_HEADS // 2


def _dot2(a, b, dims=NN):
    a_hi, a_lo = _split_bf16(a)
    b = b.astype(BF16)
    d = functools.partial(lax.dot_general, dimension_numbers=dims, preferred_element_type=F32)
    return d(a_hi, b) + d(a_lo, b)


def _rwkv_kernel(*refs, tt, cc, lookahead, batched):
    if lookahead:
        x_ref, xn_ref = refs[:2]
        refs = refs[2:]
    else:
        x_ref, xn_ref = refs[0], None
        refs = refs[1:]
    (g_ref, w_ref, mu_ref, sh0_ref, s0_ref, w0_ref, w2_ref, a0_ref, a2_ref,
     g2_ref, kk_ref, ka_ref, rk_ref, lnw_ref, lnb_ref, hsum_ref,
     o_ref, sout_ref, shout_ref,
     s_scr, carry, rk_raw, uw_raw, ua_raw, rt_s, kkt_s, kh_s, bh_s, v_s, bon_s, g_s, cum_s) = refs
    i = pl.program_id(1)

    n_chunks = tt // cc

    def project(x, slot):
        h = _rms(x.reshape(tt, D_MODEL)) * g_ref[...]
        cz = _dot(h, w_ref[...])
        rows = lax.broadcasted_iota(jnp.int32, (tt, 1), 0)
        if batched:
            first = jnp.bitwise_and(rows, cc - 1) == 0
            before = jnp.broadcast_to(sh0_ref[...], (n_chunks, cc, C_SHIFT_WIDTH)).reshape(tt, C_SHIFT_WIDTH)
            shout_ref[...] = cz.reshape(n_chunks, cc, C_SHIFT_WIDTH)[:, cc - 1:cc, :]
        else:
            first = rows == 0
            before = carry[...]
            last = cz[tt - 1:tt, :]
            carry[...] = last
            shout_ref[0] = last
        prev = jnp.where(first, before, pltpu.roll(cz, 1, 0))
        cs = cz + (prev - cz) * mu_ref[...]
        off = 3 * C_WIDTH
        w_lo = cs[:, off:off + C_RANK_W]
        a_lo = cs[:, off + C_RANK_W:off + C_RANK_W + C_RANK_A]
        g_lo = cs[:, off + C_RANK_W + C_RANK_A:C_SHIFT_WIDTH]
        rk_raw[...] = cs[:, 0:2 * C_WIDTH]
        v_s[slot] = cs[:, 2 * C_WIDTH:3 * C_WIDTH]
        uw_raw[...] = w0_ref[...] + _dot(jnp.tanh(w_lo), w2_ref[...])
        ua_raw[...] = a0_ref[...] + _dot(a_lo, a2_ref[...])
        g_s[slot] = _dot(_sigmoid(g_lo), g2_ref[...])

    def token_block(b, slot):
        rs = slice(b * cc, (b + 1) * cc)
        r = rk_raw[rs, 0:C_WIDTH]
        k = rk_raw[rs, C_WIDTH:2 * C_WIDTH]
        v = v_s[slot, rs, :]
        lw = -math.exp(-0.5) * _sigmoid(uw_raw[rs, :])
        a = _sigmoid(ua_raw[rs, :])
        hsum = hsum_ref[...]
        kk_raw = k * kk_ref[...]
        kk = kk_raw / jnp.maximum(jnp.sqrt(_head_sum(kk_raw * kk_raw, hsum)), 1e-12)
        k2 = k * (1.0 + (a - 1.0) * ka_ref[...])
        pos = lax.broadcasted_iota(jnp.int32, (cc, 1), 0)
        cum = lw
        step = 1
        while step < cc:
            cum = cum + jnp.where(pos >= step, pltpu.roll(cum, step, 0), 0.0)
            step *= 2
        e_inv = jnp.exp(-cum)
        rt_s[slot, rs, :] = r * jnp.exp(cum)
        kkt_s[slot, rs, :] = kk * jnp.exp(cum - lw)
        kh_s[slot, rs, :] = k2 * e_inv
        bh_s[slot, rs, :] = kk * a * e_inv
        bon_s[slot, rs, :] = _head_sum(r * k2 * rk_ref[...], hsum) * v
        cum_s[slot, rs, :] = cum

    if not batched:
        @pl.when(i == 0)
        def _():
            s_scr[...] = s0_ref[0]
            carry[...] = sh0_ref[0]
            if lookahead:
                project(x_ref[...], 0)
                for b in range(n_chunks):
                    token_block(b, 0)

    if lookahead:
        cur = lax.rem(i, 2)
        nxt = 1 - cur
        project(xn_ref[...], nxt)
        pending = [functools.partial(token_block, b, nxt) for b in range(n_chunks)]
    else:
        cur = 0
        project(x_ref[...], 0)
        for b in range(n_chunks):
            token_block(b, 0)
        pending = []

    def interleave():
        if pending:
            pending.pop(0)()

    two = 2 * cc
    sh = cc.bit_length() - 1
    hd = C_HEAD_DIM.bit_length() - 1
    n_double = sh - 1
    rho = lax.broadcasted_iota(jnp.int32, (two, PAIR_W), 0)
    lane = lax.broadcasted_iota(jnp.int32, (two, PAIR_W), 1)
    placed = (rho >> sh) == (lane >> hd)
    r2 = lax.broadcasted_iota(jnp.int32, (two, two), 0)
    c2 = lax.broadcasted_iota(jnp.int32, (two, two), 1)
    same = (r2 >> sh) == (c2 >> sh)
    strict = jnp.logical_and(same, r2 > c2)
    incl = jnp.logical_and(same, r2 >= c2)
    eye = jnp.where(r2 == c2, 1.0, 0.0).astype(F32)
    gr = lax.broadcasted_iota(jnp.int32, (PAIR_W, PAIR_W), 0)
    gc = lax.broadcasted_iota(jnp.int32, (PAIR_W, PAIR_W), 1)
    hmean = jnp.where((gr >> hd) == (gc >> hd), 1.0 / C_HEAD_DIM, 0.0).astype(BF16)

    def place(xv):
        return jnp.where(placed, jnp.concatenate([xv, xv], axis=0), 0.0)

    per_trip = 4 if n_chunks % 4 == 0 else (2 if n_chunks % 2 == 0 else 1)
    units = [(c, p) for c in range(per_trip) for p in range(N_PAIRS)]
    pairs = range(len(units))

    def chunk(j):
        r0s = [(j * per_trip + c) * cc for c in range(per_trip)]
        w_end = [jnp.exp(cum_s[cur, r0 + cc - 1:r0 + cc, :]) for r0 in r0s]
        lanes = [slice(p * PAIR_W, (p + 1) * PAIR_W) for _, p in units]
        sl = [(cur, slice(r0s[c], r0s[c] + cc), lanes[u]) for u, (c, _) in enumerate(units)]
        we = [w_end[c][:, lanes[u]] for u, (c, _) in enumerate(units)]
        rt = [place(rt_s[sl[p]]) for p in pairs]
        kkt = [place(kkt_s[sl[p]]) for p in pairs]
        kh = [place(kh_s[sl[p]]) for p in pairs]
        bh = [place(bh_s[sl[p]]) for p in pairs]
        vv = [place(v_s[sl[p]]) for p in pairs]
        cat0 = lambda *xs: jnp.concatenate(xs, axis=0)
        cat1 = lambda *xs: jnp.concatenate(xs, axis=1)
        gg = [_dot(cat0(kkt[p], rt[p]), cat0(kh[p], bh[p]), NT) for p in pairs]
        interleave()
        a_kk = [jnp.where(strict, gg[p][:two, :two], 0.0) for p in pairs]
        b_kk = [jnp.where(incl, gg[p][two:, :two], 0.0) for p in pairs]
        b_bb = [jnp.where(incl, gg[p][two:, two:], 0.0) for p in pairs]
        pw = [jnp.where(strict, -gg[p][:two, two:], 0.0) for p in pairs]
        tinv = [eye + pw[p] for p in pairs]
        pw = [_dot(pw[p], pw[p]) for p in pairs]
        for _ in range(n_double - 1):
            pt = [_dot(pw[p], cat1(pw[p], tinv[p])) for p in pairs]
            pw = [pt[p][:, :two] for p in pairs]
            tinv = [tinv[p] + pt[p][:, two:] for p in pairs]
            interleave()
        tinv = [tinv[p] + _dot(pw[p], tinv[p]) for p in pairs]
        interleave()
        av = [_dot(a_kk[p], vv[p]) for p in pairs]
        kv = [_dot(tinv[p], cat1(kkt[p], av[p])) for p in pairs]
        zero = jnp.zeros((two, PAIR_W), F32)
        ry = [_dot(cat1(b_kk[p], -b_bb[p]), cat0(cat1(zero, vv[p]), kv[p])) for p in pairs]
        rp = [rt[p] + ry[p][:, :PAIR_W] for p in pairs]
        y0 = [ry[p][:, PAIR_W:] for p in pairs]
        kkp = [kv[p][:, :PAIR_W] for p in pairs]
        vp = [kv[p][:, PAIR_W:] for p in pairs]
        bd = [bh[p] * we[p] for p in pairs]
        kd = [kh[p] * we[p] for p in pairs]
        q = [_dot(kkp[p], bd[p], TN) for p in pairs]
        z = [_dot(cat0(vv[p], vp[p]), cat0(kd[p], -bd[p]), TN) for p in pairs]
        y2 = []
        if batched:
            for u, (c, p) in enumerate(units):
                s_old = s0_ref[j * per_trip + c, p]
                y2.append(_dot(rp[u], s_old, NT) + y0[u])
                sout_ref[j * per_trip + c, p] = s_old * we[u] - _dot(s_old, q[u]) + z[u]
        else:
            state = [s_scr[p] for p in range(N_PAIRS)]
            for u, (_, p) in enumerate(units):
                s_old = state[p]
                y2.append(_dot(rp[u], s_old, NT) + y0[u])
                state[p] = s_old * we[u] - _dot(s_old, q[u]) + z[u]
            for p in range(N_PAIRS):
                s_scr[p] = state[p]
        ys = [y2[u][:cc] + y2[u][cc:] for u in pairs]
        yc = [ys[u] - _dot(ys[u], hmean) for u in pairs]
        var = [_dot(yc[u] * yc[u], hmean) for u in pairs]
        for u, (c, _) in enumerate(units):
            yn = yc[u] * lax.rsqrt(var[u] + C_GN_EPS) * lnw_ref[:, lanes[u]] + lnb_ref[:, lanes[u]]
            out = (yn + bon_s[sl[u]]) * g_s[sl[u]]
            if batched:
                o_ref[j * per_trip + c, :, lanes[u]] = out
            else:
                o_ref[(0,) + sl[u][1:]] = out

    for j in range(n_chunks // per_trip):
        chunk(j)
    while pending:
        interleave()
    if not batched:
        sout_ref[0] = s_scr[...]


def _pair_states(s):
    bsz = s.shape[0]
    s = s.reshape(bsz, N_PAIRS, 2, C_HEAD_DIM, C_HEAD_DIM)
    zero = jnp.zeros_like(s[:, :, 0])
    top = jnp.concatenate([s[:, :, 0], zero], axis=-1)
    bot = jnp.concatenate([zero, s[:, :, 1]], axis=-1)
    return jnp.concatenate([top, bot], axis=-2)


def _unpair_states(s):
    bsz = s.shape[0]
    d = C_HEAD_DIM
    return jnp.stack([s[:, :, :d, :d], s[:, :, d:, d:]], axis=2).reshape(bsz, C_HEADS, d, d)


def _rwkv(x, gain, w_c, lw, shift0, s0, *, tt, cc, nb):
    bsz, seq, _ = x.shape
    batched = nb > 1
    assert cc & (cc - 1) == 0
    assert (seq == cc and tt == nb * cc and bsz % nb == 0) if batched else (seq % tt == 0 and tt % cc == 0)
    row = lambda a: a.reshape(1, -1)
    sspec = pl.BlockSpec((nb, N_PAIRS, PAIR_W, PAIR_W), lambda b, i: (b, 0, 0, 0))
    shspec = pl.BlockSpec((nb, 1, C_SHIFT_WIDTH), lambda b, i: (b, 0, 0))
    vec = _const_spec((1, C_WIDTH))
    n_tiles = 1 if batched else seq // tt
    lookahead = n_tiles > 1
    tile = pltpu.VMEM((2 if lookahead else 1, tt, C_WIDTH), F32)
    hsum = _head_sum_matrix(C_HEAD_DIM)
    tok = lambda w: pl.BlockSpec((nb, tt // nb, w), lambda b, i: (b, i, 0))
    x_specs = [tok(D_MODEL)]
    x_args = [x]
    if lookahead:
        x_specs.append(pl.BlockSpec((1, tt, D_MODEL), lambda b, i: (b, jnp.minimum(i + 1, n_tiles - 1), 0)))
        x_args.append(x)
    oc, s_new, shift_new = pl.pallas_call(
        functools.partial(_rwkv_kernel, tt=tt, cc=cc, lookahead=lookahead, batched=batched),
        grid=(bsz // nb, n_tiles),
        in_specs=x_specs + [
                  _const_spec((1, D_MODEL)), w_c[1],
                  _const_spec((1, C_SHIFT_WIDTH)), shspec, sspec,
                  vec, _const_spec((C_RANK_W, C_WIDTH)), vec, _const_spec((C_RANK_A, C_WIDTH)),
                  _const_spec((C_RANK_G, C_WIDTH)), vec, vec, vec, vec, vec,
                  _const_spec((MXU_TILE, MXU_TILE))],
        out_specs=[tok(C_WIDTH), sspec, shspec],
        out_shape=[jax.ShapeDtypeStruct((bsz, seq, C_WIDTH), F32),
                   jax.ShapeDtypeStruct((bsz, N_PAIRS, PAIR_W, PAIR_W), F32),
                   jax.ShapeDtypeStruct((bsz, 1, C_SHIFT_WIDTH), F32)],
        scratch_shapes=[pltpu.VMEM((N_PAIRS, PAIR_W, PAIR_W), F32),
                        pltpu.VMEM((1, C_SHIFT_WIDTH), F32),
                        pltpu.VMEM((tt, 2 * C_WIDTH), F32),
                        pltpu.VMEM((tt, C_WIDTH), F32),
                        pltpu.VMEM((tt, C_WIDTH), F32)] + [tile] * 8,
        compiler_params=pltpu.CompilerParams(
            dimension_semantics=("arbitrary", "arbitrary"),
            vmem_limit_bytes=VMEM_LIMIT_BYTES),
        name="rwkv",
    )(*x_args, gain, w_c[0], row(lw["c_shift_mu"]), shift0, _pair_states(s0),
      row(lw["c_w0"]), lw["c_w2"].astype(BF16), row(lw["c_a0"]), lw["c_a2"].astype(BF16),
      lw["c_g2"].astype(BF16), row(lw["c_k_k"]), row(lw["c_k_a"]), row(lw["c_r_k"]),
      row(lw["c_ln_w"]), row(lw["c_ln_b"]), hsum)
    return oc, _unpair_states(s_new), shift_new


def _merge_kernel(x_ref, oa_ref, ob_ref, oc_ref, g_ref, wg_ref, wb_ref, wo_ref, y_ref):
    x = x_ref[...]
    h = _rms(x) * g_ref[...]
    gl = _dot(h, wg_ref[...])
    m = None
    for b, o_ref in enumerate((oa_ref, ob_ref, oc_ref)):
        t = _sigmoid(gl[:, b * D_MODEL:(b + 1) * D_MODEL]) * _dot(o_ref[...], wb_ref[b])
        m = t if m is None else m + t
    y_ref[...] = x + _dot(m, wo_ref[...])


def _merge(x, oa, ob, oc, gain, w_g, w_b, w_o, *, tm):
    rows = x.shape[0]
    assert rows % tm == 0
    tok = lambda w: pl.BlockSpec((tm, w), lambda i: (i, 0))
    return pl.pallas_call(
        _merge_kernel,
        grid=(rows // tm,),
        in_specs=[tok(D_MODEL), tok(A_WIDTH), tok(B_WIDTH), tok(C_WIDTH),
                  _const_spec((1, D_MODEL)), w_g[1], w_b[1], w_o[1]],
        out_specs=tok(D_MODEL),
        out_shape=jax.ShapeDtypeStruct((rows, D_MODEL), F32),
        compiler_params=pltpu.CompilerParams(
            dimension_semantics=("arbitrary",), vmem_limit_bytes=VMEM_LIMIT_BYTES),
        name="merge",
    )(x, oa, ob, oc, gain, w_g[0], w_b[0], w_o[0])


def _ffn_kernel(x_ref, p_ref, g_ref, wgate_ref, wup_ref, wdown_ref, wpp_ref, pg_ref, wpg_ref, y_ref):
    x = x_ref[...]
    hf = (_rms(x) * g_ref[...]).astype(BF16)
    gate = _dot(hf, wgate_ref[...])
    up = _dot(hf, wup_ref[...])
    x = x + _dot(gate * _sigmoid(gate) * up, wdown_ref[...])
    e = _rms(_dot(p_ref[...], wpp_ref[...])) * pg_ref[...]
    y_ref[...] = x + _sigmoid(_dot(_rms(x), wpg_ref[...])) * e


def _ffn(x, p_all, layer, gain, w_gate, w_up, w_down, w_pp, p_gain, w_pg, *, tm):
    rows = x.shape[0]
    ple = p_all.shape[-1]
    p = p_all.reshape(p_all.shape[0], rows, ple)
    assert rows % tm == 0
    tok = lambda w: pl.BlockSpec((tm, w), lambda i: (i, 0))
    return pl.pallas_call(
        _ffn_kernel,
        grid=(rows // tm,),
        in_specs=[tok(D_MODEL), pl.BlockSpec((None, tm, ple), lambda i: (layer, i, 0)),
                  _const_spec((1, D_MODEL)),
                  w_gate[1], w_up[1], w_down[1], w_pp[1],
                  _const_spec((1, D_MODEL)), w_pg[1]],
        out_specs=tok(D_MODEL),
        out_shape=jax.ShapeDtypeStruct((rows, D_MODEL), F32),
        compiler_params=pltpu.CompilerParams(
            dimension_semantics=("arbitrary",), vmem_limit_bytes=VMEM_LIMIT_BYTES),
        name="ffn",
    )(x, p, gain, w_gate[0], w_up[0], w_down[0], w_pp[0], p_gain, w_pg[0])


def _rel_bias_table(rel_bias, cq):
    nkeys = A_WIN + cq
    t_max = A_WIN + CHUNK - 1 + cq - 1
    heads, n_rel = rel_bias.shape
    w = t_max + 1
    tail = jnp.broadcast_to(rel_bias[:, n_rel - 1:], (heads, w - n_rel))
    rev = jnp.concatenate([tail, rel_bias[:, ::-1].astype(F32), jnp.zeros((heads, 1), F32)], axis=1)
    skew = jnp.tile(rev, (1, cq))[:, :cq * w].reshape(heads, cq, w)
    return skew[:, :, cq - 1:cq - 1 + nkeys] * LOG2_E


def _rotary_tables(pos0, seq):
    half = B_HEAD_DIM // 2
    pos = pos0 + jnp.arange(seq, dtype=jnp.int32)
    inv = ROPE_BASE ** (-jnp.arange(half, dtype=F32) / half)
    ang = pos.astype(F32)[:, None] * inv[None, :]
    cos, sin = jnp.cos(ang), jnp.sin(ang)
    return jnp.concatenate([cos, cos], axis=-1), jnp.concatenate([-sin, sin], axis=-1)


def _layer(x, p_all, layer, pos0, a_ck, a_cv, ret_s0, rwkv_s0, shift_prev, lw, big, cfg):
    bsz, seq, _ = x.shape
    row = lambda a: a.reshape(1, -1)
    w_a = _layer_weight(big["w_in"], layer, (OFF_A, OFF_B - OFF_A))
    w_b = _layer_weight(big["w_in"], layer, (OFF_B, OFF_C - OFF_B))
    w_c = _layer_weight(big["w_in"], layer, (OFF_C, OFF_G - OFF_C))
    w_g = _layer_weight(big["w_in"], layer, (OFF_G, IN_WIDTH - OFF_G))
    gain = row(lw["norm_mix"])

    nb = math.gcd(bsz, cfg["nb"])
    bias = _rel_bias_table(lw["a_rel_bias"], cfg["cq"])
    if a_ck is not None:
        a_ck = a_ck.reshape(a_ck.shape[0], bsz, A_WIN, A_WIDTH)
        a_cv = a_cv.reshape(a_cv.shape[0], bsz, A_WIN, A_WIDTH)
    oa, kn, av = _attention(x, gain, w_a, row(lw["a_q_norm"]), row(lw["a_k_norm"]), bias,
                            a_ck, a_cv, layer, tq=cfg["tq"], cq=cfg["cq"], nb=nb)
    keep = min(A_WIN, seq)
    new_ak = kn[:, seq - keep:].reshape(bsz, keep, A_HEADS, A_HEAD_DIM)
    new_av = av[:, seq - keep:].reshape(bsz, keep, A_HEADS, A_HEAD_DIM)

    cosf, sinf = _rotary_tables(pos0, seq)
    ob, new_ret = _retention(x, gain, w_b, cosf, sinf, ret_s0, tb=cfg["tb"], nb=nb)

    oc, new_rwkv, new_shift = _rwkv(x, gain, w_c, lw, shift_prev, rwkv_s0,
                                    tt=nb * cfg["tt"], cc=cfg["cc"], nb=nb)

    rows = bsz * seq
    tm = min(cfg["tm"], rows)
    flat = lambda t: t.reshape(rows, t.shape[-1])
    pick = lambda name: _layer_weight(big[name], layer)
    x1 = _merge(flat(x), flat(oa), flat(ob), flat(oc), gain, w_g,
                pick("w_branch"), pick("w_out"), tm=tm)
    x2 = _ffn(x1, p_all, layer, row(lw["norm_ffn"]), pick("w_ffn_gate"), pick("w_ffn_up"),
              pick("w_ffn_down"), pick("w_ple_proj"), row(lw["ple_norm"]), pick("w_ple_gate"), tm=tm)
    return x2.reshape(bsz, seq, D_MODEL), (new_ak, new_av, new_ret, new_rwkv, new_shift)


def _group_config(seq):
    if seq >= A_WIN:
        return dict(tq=A_WIN, cq=CHUNK, tb=256, tt=512, cc=CHUNK, tm=512, nb=1)
    return dict(tq=seq, cq=seq, tb=seq, tt=seq, cc=seq, tm=256, nb=8)


def kernel(x_prompt, x_sample, p_prompt, p_sample, cache_a_k, cache_a_v, state_ret, state_rwkv, state_rwkv_shift, norm_mix, w_in, a_q_norm, a_k_norm, a_rel_bias, c_shift_mu, c_w0, c_w2, c_a0, c_a2, c_g2, c_k_k, c_k_a, c_r_k, c_ln_w, c_ln_b, w_branch, w_out, norm_ffn, w_ffn_gate, w_ffn_up, w_ffn_down, w_ple_proj, ple_norm, w_ple_gate):
    depth = w_in.shape[0]

    def layer_weights(i):
        return dict(norm_mix=norm_mix[i], a_q_norm=a_q_norm[i], a_k_norm=a_k_norm[i],
                    a_rel_bias=a_rel_bias[i], c_shift_mu=c_shift_mu[i], c_w0=c_w0[i], c_w2=c_w2[i],
                    c_a0=c_a0[i], c_a2=c_a2[i], c_g2=c_g2[i], c_k_k=c_k_k[i], c_k_a=c_k_a[i], c_r_k=c_r_k[i],
                    c_ln_w=c_ln_w[i], c_ln_b=c_ln_b[i], norm_ffn=norm_ffn[i], ple_norm=ple_norm[i])

    big = dict(w_in=w_in, w_branch=w_branch, w_out=w_out, w_ffn_gate=w_ffn_gate, w_ffn_up=w_ffn_up,
               w_ffn_down=w_ffn_down, w_ple_proj=w_ple_proj, w_ple_gate=w_ple_gate)
    big = {name: w.astype(BF16) for name, w in big.items()}

    bp, lp, _ = x_prompt.shape
    cfg_p = _group_config(lp)
    ret0 = jnp.zeros((bp, B_HEADS, B_HEAD_DIM, B_HEAD_DIM), F32)
    rwkv0 = jnp.zeros((bp, C_HEADS, C_HEAD_DIM, C_HEAD_DIM), F32)
    shift0 = jnp.zeros((bp, 1, C_SHIFT_WIDTH), F32)
    y_prompt = x_prompt
    st_p = []
    for i in range(depth):
        y_prompt, st = _layer(y_prompt, p_prompt, i, 0, None, None, ret0, rwkv0, shift0,
                              layer_weights(i), big, cfg_p)
        st_p.append(st)

    cfg_s = _group_config(x_sample.shape[1])
    y_sample = x_sample
    st_s = []
    for i in range(depth):
        y_sample, st = _layer(y_sample, p_sample, i, PAST_LEN, cache_a_k, cache_a_v, state_ret[i],
                              state_rwkv[i], state_rwkv_shift[i], layer_weights(i), big, cfg_s)
        st_s.append(st)

    stack = lambda sts, j: jnp.stack([s[j] for s in sts])
    return (y_prompt, y_sample,
            stack(st_p, 0), stack(st_p, 1), stack(st_p, 2), stack(st_p, 3), stack(st_p, 4),
            stack(st_s, 0), stack(st_s, 1), stack(st_s, 2), stack(st_s, 3), stack(st_s, 4))
```

```python
import functools
import math

import jax
import jax.numpy as jnp
from jax import lax
from jax.experimental import pallas as pl
from jax.experimental.pallas import tpu as pltpu

F32 = jnp.float32
BF16 = jnp.bfloat16

D_MODEL = 1024
PAST_LEN = 2048
CHUNK = 64
NORM_EPS = 1e-6

A_HEADS = 8
A_HEAD_DIM = 64
A_WIDTH = 512
A_WIN = 512
A_REL_MAX = 256

B_HEADS = 4
B_HEAD_DIM = 128
B_WIDTH = 512
ROPE_BASE = 10000.0

C_HEADS = 8
C_HEAD_DIM = 64
C_WIDTH = 512
C_RANK_W = 64
C_RANK_A = 64
C_RANK_G = 128
C_SHIFT_WIDTH = 3 * C_WIDTH + C_RANK_W + C_RANK_A + C_RANK_G
C_GN_EPS = 64e-5

N_BRANCHES = 3

OFF_A = 0
OFF_B = 3 * A_WIDTH
OFF_C = OFF_B + 4 * B_WIDTH
OFF_G = OFF_C + C_SHIFT_WIDTH
IN_WIDTH = OFF_G + N_BRANCHES * D_MODEL

RET_LOG_GAMMA = tuple(math.log1p(-(2.0 ** (-5.0 - h))) for h in range(B_HEADS))

VMEM_LIMIT_BYTES = 56 * 1024 * 1024
MXU_TILE = 256
LOG2_E = math.log2(math.e)

NN = (((1,), (0,)), ((), ()))
NT = (((1,), (1,)), ((), ()))
TN = (((0,), (0,)), ((), ()))


def _dot(a, b, dims=NN):
    return lax.dot_general(a.astype(BF16), b.astype(BF16), dims, preferred_element_type=F32)


def _split_bf16(a):
    hi = a.astype(BF16)
    lo = (a - hi.astype(F32)).astype(BF16)
    return hi, lo


def _dot3(a, b, dims=NN):
    a_hi, a_lo = _split_bf16(a)
    b_hi, b_lo = _split_bf16(b)
    d = functools.partial(lax.dot_general, dimension_numbers=dims, preferred_element_type=F32)
    return d(a_hi, b_hi) + (d(a_hi, b_lo) + d(a_lo, b_hi))


def _rms(x):
    return x * lax.rsqrt(jnp.mean(x * x, axis=-1, keepdims=True) + NORM_EPS)


def _sigmoid(x):
    return 1.0 / (1.0 + jnp.exp(-x))


def _const_spec(shape):
    nd = len(shape)
    return pl.BlockSpec(shape, lambda *_: (0,) * nd, pipeline_mode=pl.Buffered(1))


def _layer_weight(stacked, layer, cols=None):
    if cols is not None:
        off, width = cols
        if off % width == 0:
            spec = pl.BlockSpec((None, stacked.shape[1], width), lambda *_: (layer, 0, off // width),
                                pipeline_mode=pl.Buffered(1))
            return stacked, spec
        stacked = stacked[:, :, off:off + width]
    shape = stacked.shape[1:]
    spec = pl.BlockSpec((None,) + shape, lambda *_: (layer,) + (0,) * len(shape),
                        pipeline_mode=pl.Buffered(1))
    return stacked, spec


def _head_sum_matrix(head_dim):
    head = jnp.arange(MXU_TILE, dtype=jnp.int32) // head_dim
    return (head[:, None] == head[None, :]).astype(BF16)


def _head_sum(x, hsum):
    w = hsum.shape[0]
    return jnp.concatenate([_dot(x[:, g * w:(g + 1) * w], hsum) for g in range(x.shape[1] // w)], axis=1)


def _attn_kernel(*refs, tq, cq, has_cache, nb):
    if has_cache:
        (x_ref, g_ref, w_ref, qg_ref, kg_ref, bias_ref, hsum_ref, kc_ref, vc_ref,
         o_ref, kn_ref, v_ref, kwin, vwin, qs) = refs
    else:
        (x_ref, g_ref, w_ref, qg_ref, kg_ref, bias_ref, hsum_ref,
         o_ref, kn_ref, v_ref, kwin, vwin, qs) = refs
    i = pl.program_id(1)
    nkeys = A_WIN + cq
    pair_w = 2 * A_HEAD_DIM

    if has_cache:
        kwin[:, 0:A_WIN, :] = kc_ref[...].astype(BF16)
        vwin[:, 0:A_WIN, :] = vc_ref[...].astype(BF16)
    else:
        @pl.when(i == 0)
        def _():
            kwin[0:A_WIN, :] = jnp.zeros((A_WIN, A_WIDTH), BF16)
            vwin[0:A_WIN, :] = jnp.zeros((A_WIN, A_WIDTH), BF16)

    h = _rms(x_ref[...].reshape(nb * tq, D_MODEL)) * g_ref[...]
    z = _dot(h, w_ref[...])
    q = z[:, 0:A_WIDTH]
    k = z[:, A_WIDTH:2 * A_WIDTH]
    v = z[:, 2 * A_WIDTH:3 * A_WIDTH]
    hsum = hsum_ref[...]
    inv_d = 1.0 / A_HEAD_DIM
    qn = q * lax.rsqrt(_head_sum(q * q, hsum) * inv_d + NORM_EPS) * qg_ref[...]
    kn = k * lax.rsqrt(_head_sum(k * k, hsum) * inv_d + NORM_EPS) * kg_ref[...]
    qs[...] = qn * (A_HEAD_DIM ** -0.5 * LOG2_E)
    if has_cache:
        kwin[:, A_WIN:A_WIN + tq, :] = kn.astype(BF16).reshape(nb, tq, A_WIDTH)
        vwin[:, A_WIN:A_WIN + tq, :] = v.astype(BF16).reshape(nb, tq, A_WIDTH)
    else:
        kwin[A_WIN:A_WIN + tq, :] = kn.astype(BF16)
        vwin[A_WIN:A_WIN + tq, :] = v.astype(BF16)
    kn_ref[...] = kn.reshape(nb, tq, A_WIDTH)
    v_ref[...] = v.reshape(nb, tq, A_WIDTH)

    rho = lax.broadcasted_iota(jnp.int32, (2 * cq, pair_w), 0)
    lane = lax.broadcasted_iota(jnp.int32, (2 * cq, pair_w), 1)
    placed = (rho >> (cq.bit_length() - 1)) == (lane >> (A_HEAD_DIM.bit_length() - 1))

    n_chunks = nb * tq // cq
    per_trip = 4 if n_chunks % 4 == 0 else (2 if n_chunks % 2 == 0 else 1)
    units = [(c, p) for c in range(per_trip) for p in range(A_HEADS // 2)]
    ids = range(len(units))

    def chunk(j, carry, mask_past):
        r0s = [pl.multiple_of((j * per_trip + c) * cq, cq) for c in range(per_trip)]
        lanes = [slice(p * pair_w, (p + 1) * pair_w) for _, p in units]
        qc = [qs[pl.ds(r0s[c], cq), lanes[u]] for u, (c, _) in enumerate(units)]
        qp = [jnp.where(placed, jnp.concatenate([qc[u], qc[u]], axis=0), 0.0) for u in ids]
        if has_cache:
            win = [(j * per_trip + c, slice(None), lanes[u]) for u, (c, _) in enumerate(units)]
        else:
            win = [(pl.ds(r0s[c], nkeys), lanes[u]) for u, (c, _) in enumerate(units)]
        s = [_dot(qp[u], kwin[win[u]], NT) + bias_ref[p] for u, (_, p) in enumerate(units)]
        if mask_past:
            col = lax.broadcasted_iota(jnp.int32, (2 * cq, nkeys), 1)
            s = [jnp.where(r0s[c] + col >= A_WIN, s[u], -1e30) for u, (c, _) in enumerate(units)]
        m = [jnp.max(s[u], axis=-1, keepdims=True) for u in ids]
        e = [jnp.exp2(s[u] - m[u]) for u in ids]
        den = [jnp.sum(e[u], axis=-1, keepdims=True) for u in ids]
        o2 = [_dot(e[u], vwin[win[u]]) for u in ids]
        for u, (c, _) in enumerate(units):
            o = jnp.where(placed, o2[u] / den[u], 0.0)
            if has_cache:
                o_ref[j * per_trip + c, :, lanes[u]] = o[:cq] + o[cq:]
            else:
                o_ref[0, pl.ds(r0s[c], cq), lanes[u]] = o[:cq] + o[cq:]
        return carry

    n_trips = n_chunks // per_trip
    if has_cache:
        lax.fori_loop(0, n_trips, functools.partial(chunk, mask_past=False), 0)
    else:
        @pl.when(i == 0)
        def _():
            lax.fori_loop(0, n_trips, functools.partial(chunk, mask_past=True), 0)

        @pl.when(i > 0)
        def _():
            lax.fori_loop(0, n_trips, functools.partial(chunk, mask_past=False), 0)

    if not has_cache:
        kwin[0:A_WIN, :] = kwin[tq:tq + A_WIN, :]
        vwin[0:A_WIN, :] = vwin[tq:tq + A_WIN, :]


def _attention(x, gain, w_a, q_gain, k_gain, bias, cache_k, cache_v, layer, *, tq, cq, nb):
    bsz, seq, _ = x.shape
    has_cache = cache_k is not None
    assert seq % tq == 0 and tq % cq == 0
    assert (seq == tq == cq and bsz % nb == 0) if has_cache else (tq == A_WIN and nb == 1)
    nkeys = A_WIN + cq
    tok = lambda w: pl.BlockSpec((nb, tq, w), lambda b, i: (b, i, 0))
    win_shape = (nb, nkeys, A_WIDTH) if has_cache else (A_WIN + tq, A_WIDTH)
    in_specs = [tok(D_MODEL), _const_spec((1, D_MODEL)), w_a[1],
                _const_spec((1, A_WIDTH)), _const_spec((1, A_WIDTH)),
                _const_spec((A_HEADS // 2, 2 * cq, nkeys)), _const_spec((MXU_TILE, MXU_TILE))]
    args = [x, gain, w_a[0], jnp.tile(q_gain, (1, A_HEADS)), jnp.tile(k_gain, (1, A_HEADS)),
            bias.reshape(A_HEADS // 2, 2 * cq, nkeys), _head_sum_matrix(A_HEAD_DIM)]
    if has_cache:
        cspec = pl.BlockSpec((None, nb, A_WIN, A_WIDTH), lambda b, i: (layer, b, 0, 0))
        in_specs += [cspec, cspec]
        args += [cache_k, cache_v]
    out = jax.ShapeDtypeStruct((bsz, seq, A_WIDTH), F32)
    return pl.pallas_call(
        functools.partial(_attn_kernel, tq=tq, cq=cq, has_cache=has_cache, nb=nb),
        grid=(bsz // nb, seq // tq),
        in_specs=in_specs,
        out_specs=[tok(A_WIDTH)] * 3,
        out_shape=[out] * 3,
        scratch_shapes=[pltpu.VMEM(win_shape, BF16),
                        pltpu.VMEM(win_shape, BF16),
                        pltpu.VMEM((nb * tq, A_WIDTH), F32)],
        compiler_params=pltpu.CompilerParams(
            dimension_semantics=("arbitrary", "arbitrary"),
            vmem_limit_bytes=VMEM_LIMIT_BYTES),
        name="attention",
    )(*args)


def _ret_kernel(x_ref, g_ref, w_ref, cos_ref, sin_ref, s0_ref, o_ref, sout_ref, s_scr, dmat_s, *, tb, nb, nsub):
    batched = nb > 1
    i = pl.program_id(1)

    if not batched:
        @pl.when(i == 0)
        def _():
            s_scr[...] = s0_ref[0]

    @pl.when(jnp.logical_and(pl.program_id(0) == 0, i == 0))
    def _():
        row = lax.broadcasted_iota(jnp.int32, (tb, tb), 0)
        col = lax.broadcasted_iota(jnp.int32, (tb, tb), 1)
        diff = row - col
        dist = jnp.maximum(diff, 0).astype(F32)
        for hh in range(B_HEADS):
            dmat_s[hh] = jnp.where(diff >= 0, jnp.exp(RET_LOG_GAMMA[hh] * dist), 0.0)

    rows = nb * nsub * tb
    h = _rms(x_ref[...].reshape(rows, D_MODEL)) * g_ref[...]
    z = _dot(h, w_ref[...])
    n = lax.broadcasted_iota(jnp.int32, (tb, 1), 0).astype(F32)
    for hh in range(B_HEADS):
        lg = RET_LOG_GAMMA[hh]
        lo = hh * B_HEAD_DIM
        dmat = dmat_s[hh]
        dec_q = jnp.exp(lg * (n + 1.0))
        dec_k = jnp.exp(lg * ((tb - 1.0) - n))
        for bb in range(nb):
            state = s0_ref[bb, hh] if batched else s_scr[hh]
            for sub in range(nsub):
                rs = slice((bb * nsub + sub) * tb, (bb * nsub + sub + 1) * tb)
                pos = slice(sub * tb, (sub + 1) * tb)
                cosf = cos_ref[pos, :]
                sinf = sin_ref[pos, :]
                q = z[rs, lo:lo + B_HEAD_DIM]
                k = z[rs, B_WIDTH + lo:B_WIDTH + lo + B_HEAD_DIM]
                v = z[rs, 2 * B_WIDTH + lo:2 * B_WIDTH + lo + B_HEAD_DIM]
                gate = z[rs, 3 * B_WIDTH + lo:3 * B_WIDTH + lo + B_HEAD_DIM]
                q = q * cosf + pltpu.roll(q, B_HEAD_DIM // 2, 1) * sinf
                k = (k * cosf + pltpu.roll(k, B_HEAD_DIM // 2, 1) * sinf) * (B_HEAD_DIM ** -0.5)
                scores = _dot(q, k, NT) * dmat
                o = _dot(scores, v) + _dot(q, state) * dec_q
                state = math.exp(lg * tb) * state + _dot(k * dec_k, v, TN)
                o_ref[bb, pos, lo:lo + B_HEAD_DIM] = _rms(o) * (gate * _sigmoid(gate))
            if batched:
                sout_ref[bb, hh] = state
            else:
                s_scr[hh] = state
    if not batched:
        sout_ref[0] = s_scr[...]


def _retention(x, gain, w_b, cosf, sinf, s0, *, tb, nb, nsub):
    bsz, seq, _ = x.shape
    assert (seq == tb and bsz % nb == 0 and nsub == 1) if nb > 1 else seq % (nsub * tb) == 0
    sspec = pl.BlockSpec((nb, B_HEADS, B_HEAD_DIM, B_HEAD_DIM), lambda b, i: (b, 0, 0, 0))
    tok = lambda w: pl.BlockSpec((nb, nsub * tb, w), lambda b, i: (b, i, 0))
    return pl.pallas_call(
        functools.partial(_ret_kernel, tb=tb, nb=nb, nsub=nsub),
        grid=(bsz // nb, seq // (nsub * tb)),
        in_specs=[tok(D_MODEL),
                  _const_spec((1, D_MODEL)), w_b[1],
                  pl.BlockSpec((nsub * tb, B_HEAD_DIM), lambda b, i: (i, 0)),
                  pl.BlockSpec((nsub * tb, B_HEAD_DIM), lambda b, i: (i, 0)),
                  sspec],
        out_specs=[tok(B_WIDTH), sspec],
        out_shape=[jax.ShapeDtypeStruct((bsz, seq, B_WIDTH), F32),
                   jax.ShapeDtypeStruct((bsz, B_HEADS, B_HEAD_DIM, B_HEAD_DIM), F32)],
        scratch_shapes=[pltpu.VMEM((B_HEADS, B_HEAD_DIM, B_HEAD_DIM), F32),
                        pltpu.VMEM((B_HEADS, tb, tb), F32)],
        compiler_params=pltpu.CompilerParams(
            dimension_semantics=("arbitrary", "arbitrary"),
            vmem_limit_bytes=VMEM_LIMIT_BYTES),
        name="retention",
    )(x, gain, w_b[0], cosf, sinf, s0)


PAIR_W = 2 * C_HEAD_DIM
N_PAIRS = C_HEADS // 2


def _dot2(a, b, dims=NN):
    a_hi, a_lo = _split_bf16(a)
    b = b.astype(BF16)
    d = functools.partial(lax.dot_general, dimension_numbers=dims, preferred_element_type=F32)
    return d(a_hi, b) + d(a_lo, b)


def _rwkv_kernel(*refs, tt, cc, lookahead, batched):
    if lookahead:
        x_ref, xn_ref = refs[:2]
        refs = refs[2:]
    else:
        x_ref, xn_ref = refs[0], None
        refs = refs[1:]
    (g_ref, w_ref, mu_ref, sh0_ref, s0_ref, w0_ref, w2_ref, a0_ref, a2_ref,
     g2_ref, kk_ref, ka_ref, rk_ref, lnw_ref, lnb_ref, hsum_ref,
     o_ref, sout_ref, shout_ref,
     s_scr, carry, rk_raw, uw_raw, ua_raw, rt_s, kkt_s, kh_s, bh_s, v_s, bon_s, g_s, cum_s) = refs
    i = pl.program_id(1)

    n_chunks = tt // cc

    def project(x, slot):
        h = _rms(x.reshape(tt, D_MODEL)) * g_ref[...]
        cz = _dot(h, w_ref[...])
        rows = lax.broadcasted_iota(jnp.int32, (tt, 1), 0)
        if batched:
            first = jnp.bitwise_and(rows, cc - 1) == 0
            before = jnp.broadcast_to(sh0_ref[...], (n_chunks, cc, C_SHIFT_WIDTH)).reshape(tt, C_SHIFT_WIDTH)
            shout_ref[...] = cz.reshape(n_chunks, cc, C_SHIFT_WIDTH)[:, cc - 1:cc, :]
        else:
            first = rows == 0
            before = carry[...]
            last = cz[tt - 1:tt, :]
            carry[...] = last
            shout_ref[0] = last
        prev = jnp.where(first, before, pltpu.roll(cz, 1, 0))
        cs = cz + (prev - cz) * mu_ref[...]
        off = 3 * C_WIDTH
        w_lo = cs[:, off:off + C_RANK_W]
        a_lo = cs[:, off + C_RANK_W:off + C_RANK_W + C_RANK_A]
        g_lo = cs[:, off + C_RANK_W + C_RANK_A:C_SHIFT_WIDTH]
        rk_raw[...] = cs[:, 0:2 * C_WIDTH]
        v_s[slot] = cs[:, 2 * C_WIDTH:3 * C_WIDTH]
        uw_raw[...] = w0_ref[...] + _dot(jnp.tanh(w_lo), w2_ref[...])
        ua_raw[...] = a0_ref[...] + _dot(a_lo, a2_ref[...])
        g_s[slot] = _dot(_sigmoid(g_lo), g2_ref[...])

    def token_block(b, slot):
        rs = slice(b * cc, (b + 1) * cc)
        r = rk_raw[rs, 0:C_WIDTH]
        k = rk_raw[rs, C_WIDTH:2 * C_WIDTH]
        v = v_s[slot, rs, :]
        lw = -math.exp(-0.5) * _sigmoid(uw_raw[rs, :])
        a = _sigmoid(ua_raw[rs, :])
        hsum = hsum_ref[...]
        kk_raw = k * kk_ref[...]
        kk = kk_raw / jnp.maximum(jnp.sqrt(_head_sum(kk_raw * kk_raw, hsum)), 1e-12)
        k2 = k * (1.0 + (a - 1.0) * ka_ref[...])
        pos = lax.broadcasted_iota(jnp.int32, (cc, 1), 0)
        cum = lw
        step = 1
        while step < cc:
            cum = cum + jnp.where(pos >= step, pltpu.roll(cum, step, 0), 0.0)
            step *= 2
        e_inv = jnp.exp(-cum)
        rt_s[slot, rs, :] = r * jnp.exp(cum)
        kkt_s[slot, rs, :] = kk * jnp.exp(cum - lw)
        kh_s[slot, rs, :] = k2 * e_inv
        bh_s[slot, rs, :] = kk * a * e_inv
        bon_s[slot, rs, :] = _head_sum(r * k2 * rk_ref[...], hsum) * v
        cum_s[slot, rs, :] = cum

    if not batched:
        @pl.when(i == 0)
        def _():
            s_scr[...] = s0_ref[0]
            carry[...] = sh0_ref[0]
            if lookahead:
                project(x_ref[...], 0)
                for b in range(n_chunks):
                    token_block(b, 0)

    if lookahead:
        cur = lax.rem(i, 2)
        nxt = 1 - cur
        project(xn_ref[...], nxt)
        pending = [functools.partial(token_block, b, nxt) for b in range(n_chunks)]
    else:
        cur = 0
        project(x_ref[...], 0)
        for b in range(n_chunks):
            token_block(b, 0)
        pending = []

    def interleave():
        if pending:
            pending.pop(0)()

    two = 2 * cc
    sh = cc.bit_length() - 1
    hd = C_HEAD_DIM.bit_length() - 1
    n_double = sh - 1
    rho = lax.broadcasted_iota(jnp.int32, (two, PAIR_W), 0)
    lane = lax.broadcasted_iota(jnp.int32, (two, PAIR_W), 1)
    placed = (rho >> sh) == (lane >> hd)
    r2 = lax.broadcasted_iota(jnp.int32, (two, two), 0)
    c2 = lax.broadcasted_iota(jnp.int32, (two, two), 1)
    same = (r2 >> sh) == (c2 >> sh)
    strict = jnp.logical_and(same, r2 > c2)
    incl = jnp.logical_and(same, r2 >= c2)
    eye = jnp.where(r2 == c2, 1.0, 0.0).astype(F32)
    gr = lax.broadcasted_iota(jnp.int32, (PAIR_W, PAIR_W), 0)
    gc = lax.broadcasted_iota(jnp.int32, (PAIR_W, PAIR_W), 1)
    hmean = jnp.where((gr >> hd) == (gc >> hd), 1.0 / C_HEAD_DIM, 0.0).astype(BF16)

    def place(xv):
        return jnp.where(placed, jnp.concatenate([xv, xv], axis=0), 0.0)

    per_trip = 4 if n_chunks % 4 == 0 else (2 if n_chunks % 2 == 0 else 1)
    units = [(c, p) for c in range(per_trip) for p in range(N_PAIRS)]
    pairs = range(len(units))

    def chunk(j):
        r0s = [(j * per_trip + c) * cc for c in range(per_trip)]
        w_end = [jnp.exp(cum_s[cur, r0 + cc - 1:r0 + cc, :]) for r0 in r0s]
        lanes = [slice(p * PAIR_W, (p + 1) * PAIR_W) for _, p in units]
        sl = [(cur, slice(r0s[c], r0s[c] + cc), lanes[u]) for u, (c, _) in enumerate(units)]
        we = [w_end[c][:, lanes[u]] for u, (c, _) in enumerate(units)]
        rt = [place(rt_s[sl[p]]) for p in pairs]
        kkt = [place(kkt_s[sl[p]]) for p in pairs]
        kh = [place(kh_s[sl[p]]) for p in pairs]
        bh = [place(bh_s[sl[p]]) for p in pairs]
        vv = [place(v_s[sl[p]]) for p in pairs]
        cat0 = lambda *xs: jnp.concatenate(xs, axis=0)
        cat1 = lambda *xs: jnp.concatenate(xs, axis=1)
        gg = [_dot(cat0(kkt[p], rt[p]), cat0(kh[p], bh[p]), NT) for p in pairs]
        interleave()
        a_kk = [jnp.where(strict, gg[p][:two, :two], 0.0) for p in pairs]
        b_kk = [jnp.where(incl, gg[p][two:, :two], 0.0) for p in pairs]
        b_bb = [jnp.where(incl, gg[p][two:, two:], 0.0) for p in pairs]
        pw = [jnp.where(strict, -gg[p][:two, two:], 0.0) for p in pairs]
        tinv = [eye + pw[p] for p in pairs]
        pw = [_dot(pw[p], pw[p]) for p in pairs]
        for _ in range(n_double - 1):
            pt = [_dot(pw[p], cat1(pw[p], tinv[p])) for p in pairs]
            pw = [pt[p][:, :two] for p in pairs]
            tinv = [tinv[p] + pt[p][:, two:] for p in pairs]
            interleave()
        tinv = [tinv[p] + _dot(pw[p], tinv[p]) for p in pairs]
        interleave()
        av = [_dot(a_kk[p], vv[p]) for p in pairs]
        kv = [_dot(tinv[p], cat1(kkt[p], av[p])) for p in pairs]
        zero = jnp.zeros((two, PAIR_W), F32)
        ry = [_dot(cat1(b_kk[p], -b_bb[p]), cat0(cat1(zero, vv[p]), kv[p])) for p in pairs]
        rp = [rt[p] + ry[p][:, :PAIR_W] for p in pairs]
        y0 = [ry[p][:, PAIR_W:] for p in pairs]
        kkp = [kv[p][:, :PAIR_W] for p in pairs]
        vp = [kv[p][:, PAIR_W:] for p in pairs]
        bd = [bh[p] * we[p] for p in pairs]
        kd = [kh[p] * we[p] for p in pairs]
        q = [_dot(kkp[p], bd[p], TN) for p in pairs]
        z = [_dot(cat0(vv[p], vp[p]), cat0(kd[p], -bd[p]), TN) for p in pairs]
        y2 = []
        if batched:
            for u, (c, p) in enumerate(units):
                s_old = s0_ref[j * per_trip + c, p]
                y2.append(_dot(rp[u], s_old, NT) + y0[u])
                sout_ref[j * per_trip + c, p] = s_old * we[u] - _dot(s_old, q[u]) + z[u]
        else:
            state = [s_scr[p] for p in range(N_PAIRS)]
            for u, (_, p) in enumerate(units):
                s_old = state[p]
                y2.append(_dot(rp[u], s_old, NT) + y0[u])
                state[p] = s_old * we[u] - _dot(s_old, q[u]) + z[u]
            for p in range(N_PAIRS):
                s_scr[p] = state[p]
        ys = [y2[u][:cc] + y2[u][cc:] for u in pairs]
        yc = [ys[u] - _dot(ys[u], hmean) for u in pairs]
        var = [_dot(yc[u] * yc[u], hmean) for u in pairs]
        for u, (c, _) in enumerate(units):
            yn = yc[u] * lax.rsqrt(var[u] + C_GN_EPS) * lnw_ref[:, lanes[u]] + lnb_ref[:, lanes[u]]
            out = (yn + bon_s[sl[u]]) * g_s[sl[u]]
            if batched:
                o_ref[j * per_trip + c, :, lanes[u]] = out
            else:
                o_ref[(0,) + sl[u][1:]] = out

    for j in range(n_chunks // per_trip):
        chunk(j)
    while pending:
        interleave()
    if not batched:
        sout_ref[0] = s_scr[...]


def _pair_states(s):
    bsz = s.shape[0]
    s = s.reshape(bsz, N_PAIRS, 2, C_HEAD_DIM, C_HEAD_DIM)
    zero = jnp.zeros_like(s[:, :, 0])
    top = jnp.concatenate([s[:, :, 0], zero], axis=-1)
    bot = jnp.concatenate([zero, s[:, :, 1]], axis=-1)
    return jnp.concatenate([top, bot], axis=-2)


def _unpair_states(s):
    bsz = s.shape[0]
    d = C_HEAD_DIM
    return jnp.stack([s[:, :, :d, :d], s[:, :, d:, d:]], axis=2).reshape(bsz, C_HEADS, d, d)


def _rwkv(x, gain, w_c, lw, shift0, s0, *, tt, cc, nb):
    bsz, seq, _ = x.shape
    batched = nb > 1
    assert cc & (cc - 1) == 0
    assert (seq == cc and tt == nb * cc and bsz % nb == 0) if batched else (seq % tt == 0 and tt % cc == 0)
    row = lambda a: a.reshape(1, -1)
    sspec = pl.BlockSpec((nb, N_PAIRS, PAIR_W, PAIR_W), lambda b, i: (b, 0, 0, 0))
    shspec = pl.BlockSpec((nb, 1, C_SHIFT_WIDTH), lambda b, i: (b, 0, 0))
    vec = _const_spec((1, C_WIDTH))
    n_tiles = 1 if batched else seq // tt
    lookahead = n_tiles > 1
    tile = pltpu.VMEM((2 if lookahead else 1, tt, C_WIDTH), F32)
    hsum = _head_sum_matrix(C_HEAD_DIM)
    tok = lambda w: pl.BlockSpec((nb, tt // nb, w), lambda b, i: (b, i, 0))
    x_specs = [tok(D_MODEL)]
    x_args = [x]
    if lookahead:
        x_specs.append(pl.BlockSpec((1, tt, D_MODEL), lambda b, i: (b, jnp.minimum(i + 1, n_tiles - 1), 0)))
        x_args.append(x)
    oc, s_new, shift_new = pl.pallas_call(
        functools.partial(_rwkv_kernel, tt=tt, cc=cc, lookahead=lookahead, batched=batched),
        grid=(bsz // nb, n_tiles),
        in_specs=x_specs + [
                  _const_spec((1, D_MODEL)), w_c[1],
                  _const_spec((1, C_SHIFT_WIDTH)), shspec, sspec,
                  vec, _const_spec((C_RANK_W, C_WIDTH)), vec, _const_spec((C_RANK_A, C_WIDTH)),
                  _const_spec((C_RANK_G, C_WIDTH)), vec, vec, vec, vec, vec,
                  _const_spec((MXU_TILE, MXU_TILE))],
        out_specs=[tok(C_WIDTH), sspec, shspec],
        out_shape=[jax.ShapeDtypeStruct((bsz, seq, C_WIDTH), F32),
                   jax.ShapeDtypeStruct((bsz, N_PAIRS, PAIR_W, PAIR_W), F32),
                   jax.ShapeDtypeStruct((bsz, 1, C_SHIFT_WIDTH), F32)],
        scratch_shapes=[pltpu.VMEM((N_PAIRS, PAIR_W, PAIR_W), F32),
                        pltpu.VMEM((1, C_SHIFT_WIDTH), F32),
                        pltpu.VMEM((tt, 2 * C_WIDTH), F32),
                        pltpu.VMEM((tt, C_WIDTH), F32),
                        pltpu.VMEM((tt, C_WIDTH), F32)] + [tile] * 8,
        compiler_params=pltpu.CompilerParams(
            dimension_semantics=("arbitrary", "arbitrary"),
            vmem_limit_bytes=VMEM_LIMIT_BYTES),
        name="rwkv",
    )(*x_args, gain, w_c[0], row(lw["c_shift_mu"]), shift0, _pair_states(s0),
      row(lw["c_w0"]), lw["c_w2"].astype(BF16), row(lw["c_a0"]), lw["c_a2"].astype(BF16),
      lw["c_g2"].astype(BF16), row(lw["c_k_k"]), row(lw["c_k_a"]), row(lw["c_r_k"]),
      row(lw["c_ln_w"]), row(lw["c_ln_b"]), hsum)
    return oc, _unpair_states(s_new), shift_new


def _merge_kernel(x_ref, oa_ref, ob_ref, oc_ref, g_ref, wg_ref, wb_ref, wo_ref, y_ref):
    x = x_ref[...]
    h = _rms(x) * g_ref[...]
    gl = _dot(h, wg_ref[...])
    m = None
    for b, o_ref in enumerate((oa_ref, ob_ref, oc_ref)):
        t = _sigmoid(gl[:, b * D_MODEL:(b + 1) * D_MODEL]) * _dot(o_ref[...], wb_ref[b])
        m = t if m is None else m + t
    y_ref[...] = x + _dot(m, wo_ref[...])


def _merge(x, oa, ob, oc, gain, w_g, w_b, w_o, *, tm):
    rows = x.shape[0]
    assert rows % tm == 0
    tok = lambda w: pl.BlockSpec((tm, w), lambda i: (i, 0))
    return pl.pallas_call(
        _merge_kernel,
        grid=(rows // tm,),
        in_specs=[tok(D_MODEL), tok(A_WIDTH), tok(B_WIDTH), tok(C_WIDTH),
                  _const_spec((1, D_MODEL)), w_g[1], w_b[1], w_o[1]],
        out_specs=tok(D_MODEL),
        out_shape=jax.ShapeDtypeStruct((rows, D_MODEL), F32),
        compiler_params=pltpu.CompilerParams(
            dimension_semantics=("arbitrary",), vmem_limit_bytes=VMEM_LIMIT_BYTES),
        name="merge",
    )(x, oa, ob, oc, gain, w_g[0], w_b[0], w_o[0])


def _ffn_kernel(x_ref, p_ref, g_ref, wgate_ref, wup_ref, wdown_ref, wpp_ref, pg_ref, wpg_ref, y_ref):
    x = x_ref[...]
    hf = (_rms(x) * g_ref[...]).astype(BF16)
    gate = _dot(hf, wgate_ref[...])
    up = _dot(hf, wup_ref[...])
    x = x + _dot(gate * _sigmoid(gate) * up, wdown_ref[...])
    e = _rms(_dot(p_ref[...], wpp_ref[...])) * pg_ref[...]
    y_ref[...] = x + _sigmoid(_dot(_rms(x), wpg_ref[...])) * e


def _ffn(x, p_all, layer, gain, w_gate, w_up, w_down, w_pp, p_gain, w_pg, *, tm):
    rows = x.shape[0]
    ple = p_all.shape[-1]
    p = p_all.reshape(p_all.shape[0], rows, ple)
    assert rows % tm == 0
    tok = lambda w: pl.BlockSpec((tm, w), lambda i: (i, 0))
    return pl.pallas_call(
        _ffn_kernel,
        grid=(rows // tm,),
        in_specs=[tok(D_MODEL), pl.BlockSpec((None, tm, ple), lambda i: (layer, i, 0)),
                  _const_spec((1, D_MODEL)),
                  w_gate[1], w_up[1], w_down[1], w_pp[1],
                  _const_spec((1, D_MODEL)), w_pg[1]],
        out_specs=tok(D_MODEL),
        out_shape=jax.ShapeDtypeStruct((rows, D_MODEL), F32),
        compiler_params=pltpu.CompilerParams(
            dimension_semantics=("arbitrary",), vmem_limit_bytes=VMEM_LIMIT_BYTES),
        name="ffn",
    )(x, p, gain, w_gate[0], w_up[0], w_down[0], w_pp[0], p_gain, w_pg[0])


def _rel_bias_table(rel_bias, cq):
    nkeys = A_WIN + cq
    t_max = A_WIN + CHUNK - 1 + cq - 1
    heads, n_rel = rel_bias.shape
    w = t_max + 1
    tail = jnp.broadcast_to(rel_bias[:, n_rel - 1:], (heads, w - n_rel))
    rev = jnp.concatenate([tail, rel_bias[:, ::-1].astype(F32), jnp.zeros((heads, 1), F32)], axis=1)
    skew = jnp.tile(rev, (1, cq))[:, :cq * w].reshape(heads, cq, w)
    return skew[:, :, cq - 1:cq - 1 + nkeys] * LOG2_E


def _rotary_tables(pos0, seq):
    half = B_HEAD_DIM // 2
    pos = pos0 + jnp.arange(seq, dtype=jnp.int32)
    inv = ROPE_BASE ** (-jnp.arange(half, dtype=F32) / half)
    ang = pos.astype(F32)[:, None] * inv[None, :]
    cos, sin = jnp.cos(ang), jnp.sin(ang)
    return jnp.concatenate([cos, cos], axis=-1), jnp.concatenate([-sin, sin], axis=-1)


def _layer(x, p_all, layer, pos0, a_ck, a_cv, ret_s0, rwkv_s0, shift_prev, lw, big, cfg):
    bsz, seq, _ = x.shape
    row = lambda a: a.reshape(1, -1)
    w_a = _layer_weight(big["w_in"], layer, (OFF_A, OFF_B - OFF_A))
    w_b = _layer_weight(big["w_in"], layer, (OFF_B, OFF_C - OFF_B))
    w_c = _layer_weight(big["w_in"], layer, (OFF_C, OFF_G - OFF_C))
    w_g = _layer_weight(big["w_in"], layer, (OFF_G, IN_WIDTH - OFF_G))
    gain = row(lw["norm_mix"])

    nb = math.gcd(bsz, cfg["nb"])
    bias = _rel_bias_table(lw["a_rel_bias"], cfg["cq"])
    if a_ck is not None:
        a_ck = a_ck.astype(BF16).reshape(a_ck.shape[0], bsz, A_WIN, A_WIDTH)
        a_cv = a_cv.astype(BF16).reshape(a_cv.shape[0], bsz, A_WIN, A_WIDTH)
    oa, kn, av = _attention(x, gain, w_a, row(lw["a_q_norm"]), row(lw["a_k_norm"]), bias,
                            a_ck, a_cv, layer, tq=cfg["tq"], cq=cfg["cq"], nb=nb)
    keep = min(A_WIN, seq)
    new_ak = kn[:, seq - keep:].reshape(bsz, keep, A_HEADS, A_HEAD_DIM)
    new_av = av[:, seq - keep:].reshape(bsz, keep, A_HEADS, A_HEAD_DIM)

    cosf, sinf = _rotary_tables(pos0, seq)
    ob, new_ret = _retention(x, gain, w_b, cosf, sinf, ret_s0, tb=cfg["tb"], nb=nb, nsub=cfg["nsub"])

    oc, new_rwkv, new_shift = _rwkv(x, gain, w_c, lw, shift_prev, rwkv_s0,
                                    tt=nb * cfg["tt"], cc=cfg["cc"], nb=nb)

    rows = bsz * seq
    tm = min(cfg["tm"], rows)
    flat = lambda t: t.reshape(rows, t.shape[-1])
    pick = lambda name: _layer_weight(big[name], layer)
    x1 = _merge(flat(x), flat(oa), flat(ob), flat(oc), gain, w_g,
                pick("w_branch"), pick("w_out"), tm=tm)
    x2 = _ffn(x1, p_all, layer, row(lw["norm_ffn"]), pick("w_ffn_gate"), pick("w_ffn_up"),
              pick("w_ffn_down"), pick("w_ple_proj"), row(lw["ple_norm"]), pick("w_ple_gate"), tm=tm)
    return x2.reshape(bsz, seq, D_MODEL), (new_ak, new_av, new_ret, new_rwkv, new_shift)


def _group_config(seq):
    if seq >= A_WIN:
        return dict(tq=A_WIN, cq=CHUNK, tb=256, nsub=2, tt=512, cc=CHUNK, tm=512, nb=1)
    return dict(tq=seq, cq=seq, tb=seq, nsub=1, tt=seq, cc=seq, tm=256, nb=8)


def kernel(x_prompt, x_sample, p_prompt, p_sample, cache_a_k, cache_a_v, state_ret, state_rwkv, state_rwkv_shift, norm_mix, w_in, a_q_norm, a_k_norm, a_rel_bias, c_shift_mu, c_w0, c_w2, c_a0, c_a2, c_g2, c_k_k, c_k_a, c_r_k, c_ln_w, c_ln_b, w_branch, w_out, norm_ffn, w_ffn_gate, w_ffn_up, w_ffn_down, w_ple_proj, ple_norm, w_ple_gate):
    depth = w_in.shape[0]

    def layer_weights(i):
        return dict(norm_mix=norm_mix[i], a_q_norm=a_q_norm[i], a_k_norm=a_k_norm[i],
                    a_rel_bias=a_rel_bias[i], c_shift_mu=c_shift_mu[i], c_w0=c_w0[i], c_w2=c_w2[i],
                    c_a0=c_a0[i], c_a2=c_a2[i], c_g2=c_g2[i], c_k_k=c_k_k[i], c_k_a=c_k_a[i], c_r_k=c_r_k[i],
                    c_ln_w=c_ln_w[i], c_ln_b=c_ln_b[i], norm_ffn=norm_ffn[i], ple_norm=ple_norm[i])

    big = dict(w_in=w_in, w_branch=w_branch, w_out=w_out, w_ffn_gate=w_ffn_gate, w_ffn_up=w_ffn_up,
               w_ffn_down=w_ffn_down, w_ple_proj=w_ple_proj, w_ple_gate=w_ple_gate)
    big = {name: w.astype(BF16) for name, w in big.items()}

    bp, lp, _ = x_prompt.shape
    cfg_p = _group_config(lp)
    ret0 = jnp.zeros((bp, B_HEADS, B_HEAD_DIM, B_HEAD_DIM), F32)
    rwkv0 = jnp.zeros((bp, C_HEADS, C_HEAD_DIM, C_HEAD_DIM), F32)
    shift0 = jnp.zeros((bp, 1, C_SHIFT_WIDTH), F32)
    y_prompt = x_prompt
    st_p = []
    for i in range(depth):
        y_prompt, st = _layer(y_prompt, p_prompt, i, 0, None, None, ret0, rwkv0, shift0,
                              layer_weights(i), big, cfg_p)
        st_p.append(st)

    cfg_s = _group_config(x_sample.shape[1])
    y_sample = x_sample
    st_s = []
    for i in range(depth):
        y_sample, st = _layer(y_sample, p_sample, i, PAST_LEN, cache_a_k, cache_a_v, state_ret[i],
                              state_rwkv[i], state_rwkv_shift[i], layer_weights(i), big, cfg_s)
        st_s.append(st)

    stack = lambda sts, j: jnp.stack([s[j] for s in sts])
    return (y_prompt, y_sample,
            stack(st_p, 0), stack(st_p, 1), stack(st_p, 2), stack(st_p, 3), stack(st_p, 4),
            stack(st_s, 0), stack(st_s, 1), stack(st_s, 2), stack(st_s, 3), stack(st_s, 4))
```

```python
import functools
import math

import jax
import jax.numpy as jnp
from jax import lax
from jax.experimental import pallas as pl
from jax.experimental.pallas import tpu as pltpu

F32 = jnp.float32
BF16 = jnp.bfloat16

D_MODEL = 1024
PAST_LEN = 2048
CHUNK = 64
NORM_EPS = 1e-6

A_HEADS = 8
A_HEAD_DIM = 64
A_WIDTH = 512
A_WIN = 512
A_REL_MAX = 256

B_HEADS = 4
B_HEAD_DIM = 128
B_WIDTH = 512
ROPE_BASE = 10000.0

C_HEADS = 8
C_HEAD_DIM = 64
C_WIDTH = 512
C_RANK_W = 64
C_RANK_A = 64
C_RANK_G = 128
C_SHIFT_WIDTH = 3 * C_WIDTH + C_RANK_W + C_RANK_A + C_RANK_G
C_GN_EPS = 64e-5

N_BRANCHES = 3

OFF_A = 0
OFF_B = 3 * A_WIDTH
OFF_C = OFF_B + 4 * B_WIDTH
OFF_G = OFF_C + C_SHIFT_WIDTH
IN_WIDTH = OFF_G + N_BRANCHES * D_MODEL

RET_LOG_GAMMA = tuple(math.log1p(-(2.0 ** (-5.0 - h))) for h in range(B_HEADS))

VMEM_LIMIT_BYTES = 56 * 1024 * 1024
MXU_TILE = 256
LOG2_E = math.log2(math.e)

NN = (((1,), (0,)), ((), ()))
NT = (((1,), (1,)), ((), ()))
TN = (((0,), (0,)), ((), ()))


def _dot(a, b, dims=NN):
    return lax.dot_general(a.astype(BF16), b.astype(BF16), dims, preferred_element_type=F32)


def _split_bf16(a):
    hi = a.astype(BF16)
    lo = (a - hi.astype(F32)).astype(BF16)
    return hi, lo


def _dot3(a, b, dims=NN):
    a_hi, a_lo = _split_bf16(a)
    b_hi, b_lo = _split_bf16(b)
    d = functools.partial(lax.dot_general, dimension_numbers=dims, preferred_element_type=F32)
    return d(a_hi, b_hi) + (d(a_hi, b_lo) + d(a_lo, b_hi))


def _rms(x):
    return x * lax.rsqrt(jnp.mean(x * x, axis=-1, keepdims=True) + NORM_EPS)


def _sigmoid(x):
    return 1.0 / (1.0 + jnp.exp(-x))


def _const_spec(shape):
    nd = len(shape)
    return pl.BlockSpec(shape, lambda *_: (0,) * nd, pipeline_mode=pl.Buffered(1))


def _layer_weight(stacked, layer, cols=None):
    if cols is not None:
        off, width = cols
        if off % width == 0:
            spec = pl.BlockSpec((None, stacked.shape[1], width), lambda *_: (layer, 0, off // width),
                                pipeline_mode=pl.Buffered(1))
            return stacked, spec
        stacked = stacked[:, :, off:off + width]
    shape = stacked.shape[1:]
    spec = pl.BlockSpec((None,) + shape, lambda *_: (layer,) + (0,) * len(shape),
                        pipeline_mode=pl.Buffered(1))
    return stacked, spec


def _head_sum_matrix(head_dim):
    head = jnp.arange(MXU_TILE, dtype=jnp.int32) // head_dim
    return (head[:, None] == head[None, :]).astype(BF16)


def _head_sum(x, hsum):
    w = hsum.shape[0]
    return jnp.concatenate([_dot(x[:, g * w:(g + 1) * w], hsum) for g in range(x.shape[1] // w)], axis=1)


def _attn_kernel(*refs, tq, cq, has_cache, nb):
    if has_cache:
        (x_ref, g_ref, w_ref, qg_ref, kg_ref, bias_ref, hsum_ref, kc_ref, vc_ref,
         o_ref, kn_ref, v_ref, kwin, vwin, qs) = refs
    else:
        (x_ref, g_ref, w_ref, qg_ref, kg_ref, bias_ref, hsum_ref,
         o_ref, kn_ref, v_ref, kwin, vwin, qs) = refs
    i = pl.program_id(1)
    nkeys = A_WIN + cq
    pair_w = 2 * A_HEAD_DIM

    if has_cache:
        kwin[:, 0:A_WIN, :] = kc_ref[...].astype(BF16)
        vwin[:, 0:A_WIN, :] = vc_ref[...].astype(BF16)
    else:
        @pl.when(i == 0)
        def _():
            kwin[0:A_WIN, :] = jnp.zeros((A_WIN, A_WIDTH), BF16)
            vwin[0:A_WIN, :] = jnp.zeros((A_WIN, A_WIDTH), BF16)

    h = _rms(x_ref[...].reshape(nb * tq, D_MODEL)) * g_ref[...]
    z = _dot(h, w_ref[...])
    q = z[:, 0:A_WIDTH]
    k = z[:, A_WIDTH:2 * A_WIDTH]
    v = z[:, 2 * A_WIDTH:3 * A_WIDTH]
    hsum = hsum_ref[...]
    inv_d = 1.0 / A_HEAD_DIM
    qn = q * lax.rsqrt(_head_sum(q * q, hsum) * inv_d + NORM_EPS) * qg_ref[...]
    kn = k * lax.rsqrt(_head_sum(k * k, hsum) * inv_d + NORM_EPS) * kg_ref[...]
    qs[...] = qn * (A_HEAD_DIM ** -0.5 * LOG2_E)
    if has_cache:
        kwin[:, A_WIN:A_WIN + tq, :] = kn.astype(BF16).reshape(nb, tq, A_WIDTH)
        vwin[:, A_WIN:A_WIN + tq, :] = v.astype(BF16).reshape(nb, tq, A_WIDTH)
    else:
        kwin[A_WIN:A_WIN + tq, :] = kn.astype(BF16)
        vwin[A_WIN:A_WIN + tq, :] = v.astype(BF16)
    kn_ref[...] = kn.reshape(nb, tq, A_WIDTH)
    v_ref[...] = v.reshape(nb, tq, A_WIDTH)

    rho = lax.broadcasted_iota(jnp.int32, (2 * cq, pair_w), 0)
    lane = lax.broadcasted_iota(jnp.int32, (2 * cq, pair_w), 1)
    placed = (rho >> (cq.bit_length() - 1)) == (lane >> (A_HEAD_DIM.bit_length() - 1))

    n_chunks = nb * tq // cq
    per_trip = 4 if n_chunks % 4 == 0 else (2 if n_chunks % 2 == 0 else 1)
    units = [(c, p) for c in range(per_trip) for p in range(A_HEADS // 2)]
    ids = range(len(units))

    def chunk(j, carry, mask_past):
        r0s = [pl.multiple_of((j * per_trip + c) * cq, cq) for c in range(per_trip)]
        lanes = [slice(p * pair_w, (p + 1) * pair_w) for _, p in units]
        qc = [qs[pl.ds(r0s[c], cq), lanes[u]] for u, (c, _) in enumerate(units)]
        qp = [jnp.where(placed, jnp.concatenate([qc[u], qc[u]], axis=0), 0.0) for u in ids]
        if has_cache:
            win = [(j * per_trip + c, slice(None), lanes[u]) for u, (c, _) in enumerate(units)]
        else:
            win = [(pl.ds(r0s[c], nkeys), lanes[u]) for u, (c, _) in enumerate(units)]
        s = [_dot(qp[u], kwin[win[u]], NT) + bias_ref[p] for u, (_, p) in enumerate(units)]
        if mask_past:
            col = lax.broadcasted_iota(jnp.int32, (2 * cq, nkeys), 1)
            s = [jnp.where(r0s[c] + col >= A_WIN, s[u], -1e30) for u, (c, _) in enumerate(units)]
        m = [jnp.max(s[u], axis=-1, keepdims=True) for u in ids]
        e = [jnp.exp2(s[u] - m[u]) for u in ids]
        den = [jnp.sum(e[u], axis=-1, keepdims=True) for u in ids]
        o2 = [_dot(e[u], vwin[win[u]]) for u in ids]
        for u, (c, _) in enumerate(units):
            o = jnp.where(placed, o2[u] / den[u], 0.0)
            if has_cache:
                o_ref[j * per_trip + c, :, lanes[u]] = o[:cq] + o[cq:]
            else:
                o_ref[0, pl.ds(r0s[c], cq), lanes[u]] = o[:cq] + o[cq:]
        return carry

    n_trips = n_chunks // per_trip
    if has_cache:
        lax.fori_loop(0, n_trips, functools.partial(chunk, mask_past=False), 0)
    else:
        @pl.when(i == 0)
        def _():
            lax.fori_loop(0, n_trips, functools.partial(chunk, mask_past=True), 0)

        @pl.when(i > 0)
        def _():
            lax.fori_loop(0, n_trips, functools.partial(chunk, mask_past=False), 0)

    if not has_cache:
        kwin[0:A_WIN, :] = kwin[tq:tq + A_WIN, :]
        vwin[0:A_WIN, :] = vwin[tq:tq + A_WIN, :]


def _attention(x, gain, w_a, q_gain, k_gain, bias, cache_k, cache_v, layer, *, tq, cq, nb):
    bsz, seq, _ = x.shape
    has_cache = cache_k is not None
    assert seq % tq == 0 and tq % cq == 0
    assert (seq == tq == cq and bsz % nb == 0) if has_cache else (tq == A_WIN and nb == 1)
    nkeys = A_WIN + cq
    tok = lambda w: pl.BlockSpec((nb, tq, w), lambda b, i: (b, i, 0))
    win_shape = (nb, nkeys, A_WIDTH) if has_cache else (A_WIN + tq, A_WIDTH)
    in_specs = [tok(D_MODEL), _const_spec((1, D_MODEL)), w_a[1],
                _const_spec((1, A_WIDTH)), _const_spec((1, A_WIDTH)),
                _const_spec((A_HEADS // 2, 2 * cq, nkeys)), _const_spec((MXU_TILE, MXU_TILE))]
    args = [x, gain, w_a[0], jnp.tile(q_gain, (1, A_HEADS)), jnp.tile(k_gain, (1, A_HEADS)),
            bias.reshape(A_HEADS // 2, 2 * cq, nkeys), _head_sum_matrix(A_HEAD_DIM)]
    if has_cache:
        cspec = pl.BlockSpec((None, nb, A_WIN, A_WIDTH), lambda b, i: (layer, b, 0, 0))
        in_specs += [cspec, cspec]
        args += [cache_k, cache_v]
    out = jax.ShapeDtypeStruct((bsz, seq, A_WIDTH), F32)
    return pl.pallas_call(
        functools.partial(_attn_kernel, tq=tq, cq=cq, has_cache=has_cache, nb=nb),
        grid=(bsz // nb, seq // tq),
        in_specs=in_specs,
        out_specs=[tok(A_WIDTH)] + [pl.BlockSpec((nb, tq, A_WIDTH), lambda b, i: (b, 0, 0))] * 2,
        out_shape=[out] + [jax.ShapeDtypeStruct((bsz, tq, A_WIDTH), F32)] * 2,
        scratch_shapes=[pltpu.VMEM(win_shape, BF16),
                        pltpu.VMEM(win_shape, BF16),
                        pltpu.VMEM((nb * tq, A_WIDTH), F32)],
        compiler_params=pltpu.CompilerParams(
            dimension_semantics=("arbitrary", "arbitrary"),
            vmem_limit_bytes=VMEM_LIMIT_BYTES),
        name="attention",
    )(*args)


def _ret_kernel(x_ref, g_ref, w_ref, cosa_ref, sina_ref, cosb_ref, sinb_ref, cosbs_ref, sinbs_ref,
                s0_ref, o_ref, sout_ref, s_scr, dmat_s, *, tb, nb, nsub):
    batched = nb > 1
    i = pl.program_id(1)

    if not batched:
        @pl.when(i == 0)
        def _():
            s_scr[...] = s0_ref[0]

    @pl.when(jnp.logical_and(pl.program_id(0) == 0, i == 0))
    def _():
        row = lax.broadcasted_iota(jnp.int32, (tb, tb), 0)
        col = lax.broadcasted_iota(jnp.int32, (tb, tb), 1)
        diff = row - col
        dist = jnp.maximum(diff, 0).astype(F32)
        for hh in range(B_HEADS):
            dmat_s[hh] = jnp.where(diff >= 0, jnp.exp(RET_LOG_GAMMA[hh] * dist), 0.0)

    rows = nb * nsub * tb
    h = _rms(x_ref[...].reshape(rows, D_MODEL)) * g_ref[...]
    z = _dot(h, w_ref[...])
    cos_a, sin_a = cosa_ref[0], sina_ref[0]
    cos_all = cos_a * cosb_ref[...] - sin_a * sinb_ref[...]
    sin_all = sin_a * cosbs_ref[...] + cos_a * sinbs_ref[...]
    n = lax.broadcasted_iota(jnp.int32, (tb, 1), 0).astype(F32)
    for hh in range(B_HEADS):
        lg = RET_LOG_GAMMA[hh]
        lo = hh * B_HEAD_DIM
        dmat = dmat_s[hh]
        dec_q = jnp.exp(lg * (n + 1.0))
        dec_k = jnp.exp(lg * ((tb - 1.0) - n))
        for bb in range(nb):
            state = s0_ref[bb, hh] if batched else s_scr[hh]
            for sub in range(nsub):
                rs = slice((bb * nsub + sub) * tb, (bb * nsub + sub + 1) * tb)
                pos = slice(sub * tb, (sub + 1) * tb)
                cosf = cos_all[pos, :]
                sinf = sin_all[pos, :]
                q = z[rs, lo:lo + B_HEAD_DIM]
                k = z[rs, B_WIDTH + lo:B_WIDTH + lo + B_HEAD_DIM]
                v = z[rs, 2 * B_WIDTH + lo:2 * B_WIDTH + lo + B_HEAD_DIM]
                gate = z[rs, 3 * B_WIDTH + lo:3 * B_WIDTH + lo + B_HEAD_DIM]
                q = q * cosf + pltpu.roll(q, B_HEAD_DIM // 2, 1) * sinf
                k = (k * cosf + pltpu.roll(k, B_HEAD_DIM // 2, 1) * sinf) * (B_HEAD_DIM ** -0.5)
                scores = _dot(q, k, NT) * dmat
                o = _dot(scores, v) + _dot(q, state) * dec_q
                state = math.exp(lg * tb) * state + _dot(k * dec_k, v, TN)
                o_ref[bb, pos, lo:lo + B_HEAD_DIM] = _rms(o) * (gate * _sigmoid(gate))
            if batched:
                sout_ref[bb, hh] = state
            else:
                s_scr[hh] = state
    if not batched:
        sout_ref[0] = s_scr[...]


def _retention(x, gain, w_b, pos0, s0, *, tb, nb, nsub):
    bsz, seq, _ = x.shape
    assert (seq == tb and bsz % nb == 0 and nsub == 1) if nb > 1 else seq % (nsub * tb) == 0
    sspec = pl.BlockSpec((nb, B_HEADS, B_HEAD_DIM, B_HEAD_DIM), lambda b, i: (b, 0, 0, 0))
    tok = lambda w: pl.BlockSpec((nb, nsub * tb, w), lambda b, i: (b, i, 0))
    step_rows = nsub * tb
    half = B_HEAD_DIM // 2
    inv = ROPE_BASE ** (-jnp.arange(half, dtype=F32) / half)
    base = (pos0 + step_rows * jnp.arange(seq // step_rows, dtype=jnp.int32)).astype(F32)[:, None] * inv[None, :]
    offs = jnp.arange(step_rows, dtype=jnp.int32).astype(F32)[:, None] * inv[None, :]
    both = lambda t: jnp.concatenate([t, t], axis=-1)
    signed = lambda t: jnp.concatenate([-t, t], axis=-1)
    step_row = lambda t: both(t).reshape(-1, 1, B_HEAD_DIM)
    a_spec = pl.BlockSpec((1, 1, B_HEAD_DIM), lambda b, i: (i, 0, 0))
    b_spec = _const_spec((step_rows, B_HEAD_DIM))
    return pl.pallas_call(
        functools.partial(_ret_kernel, tb=tb, nb=nb, nsub=nsub),
        grid=(bsz // nb, seq // (nsub * tb)),
        in_specs=[tok(D_MODEL),
                  _const_spec((1, D_MODEL)), w_b[1],
                  a_spec, a_spec, b_spec, b_spec, b_spec, b_spec,
                  sspec],
        out_specs=[tok(B_WIDTH), sspec],
        out_shape=[jax.ShapeDtypeStruct((bsz, seq, B_WIDTH), F32),
                   jax.ShapeDtypeStruct((bsz, B_HEADS, B_HEAD_DIM, B_HEAD_DIM), F32)],
        scratch_shapes=[pltpu.VMEM((B_HEADS, B_HEAD_DIM, B_HEAD_DIM), F32),
                        pltpu.VMEM((B_HEADS, tb, tb), F32)],
        compiler_params=pltpu.CompilerParams(
            dimension_semantics=("arbitrary", "arbitrary"),
            vmem_limit_bytes=VMEM_LIMIT_BYTES),
        name="retention",
    )(x, gain, w_b[0], step_row(jnp.cos(base)), step_row(jnp.sin(base)),
      both(jnp.cos(offs)), both(jnp.sin(offs)), signed(jnp.cos(offs)), signed(jnp.sin(offs)), s0)


PAIR_W = 2 * C_HEAD_DIM
N_PAIRS = C_HEADS // 2


def _dot2(a, b, dims=NN):
    a_hi, a_lo = _split_bf16(a)
    b = b.astype(BF16)
    d = functools.partial(lax.dot_general, dimension_numbers=dims, preferred_element_type=F32)
    return d(a_hi, b) + d(a_lo, b)


def _rwkv_kernel(*refs, tt, cc, lookahead, batched):
    if lookahead:
        x_ref, xn_ref = refs[:2]
        refs = refs[2:]
    else:
        x_ref, xn_ref = refs[0], None
        refs = refs[1:]
    (g_ref, w_ref, mu_ref, sh0_ref, s0_ref, w0_ref, w2_ref, a0_ref, a2_ref,
     g2_ref, kk_ref, ka_ref, rk_ref, lnw_ref, lnb_ref, hsum_ref,
     o_ref, sout_ref, shout_ref,
     s_scr, carry, rk_raw, uw_raw, ua_raw, rt_s, kkt_s, kh_s, bh_s, v_s, bon_s, g_s, cum_s) = refs
    i = pl.program_id(1)

    n_chunks = tt // cc

    def project(x, slot):
        h = _rms(x.reshape(tt, D_MODEL)) * g_ref[...]
        cz = _dot(h, w_ref[...])
        rows = lax.broadcasted_iota(jnp.int32, (tt, 1), 0)
        if batched:
            first = jnp.bitwise_and(rows, cc - 1) == 0
            before = jnp.broadcast_to(sh0_ref[...], (n_chunks, cc, C_SHIFT_WIDTH)).reshape(tt, C_SHIFT_WIDTH)
            shout_ref[...] = cz.reshape(n_chunks, cc, C_SHIFT_WIDTH)[:, cc - 1:cc, :]
        else:
            first = rows == 0
            before = carry[...]
            last = cz[tt - 1:tt, :]
            carry[...] = last
            shout_ref[0] = last
        prev = jnp.where(first, before, pltpu.roll(cz, 1, 0))
        cs = cz + (prev - cz) * mu_ref[...]
        off = 3 * C_WIDTH
        w_lo = cs[:, off:off + C_RANK_W]
        a_lo = cs[:, off + C_RANK_W:off + C_RANK_W + C_RANK_A]
        g_lo = cs[:, off + C_RANK_W + C_RANK_A:C_SHIFT_WIDTH]
        rk_raw[...] = cs[:, 0:2 * C_WIDTH]
        v_s[slot] = cs[:, 2 * C_WIDTH:3 * C_WIDTH]
        uw_raw[...] = w0_ref[...] + _dot(jnp.tanh(w_lo), w2_ref[...])
        ua_raw[...] = a0_ref[...] + _dot(a_lo, a2_ref[...])
        g_s[slot] = _dot(_sigmoid(g_lo), g2_ref[...])

    def token_block(b, slot):
        rs = slice(b * cc, (b + 1) * cc)
        r = rk_raw[rs, 0:C_WIDTH]
        k = rk_raw[rs, C_WIDTH:2 * C_WIDTH]
        v = v_s[slot, rs, :]
        lw = -math.exp(-0.5) * _sigmoid(uw_raw[rs, :])
        a = _sigmoid(ua_raw[rs, :])
        hsum = hsum_ref[...]
        kk_raw = k * kk_ref[...]
        kk = kk_raw / jnp.maximum(jnp.sqrt(_head_sum(kk_raw * kk_raw, hsum)), 1e-12)
        k2 = k * (1.0 + (a - 1.0) * ka_ref[...])
        pos = lax.broadcasted_iota(jnp.int32, (cc, 1), 0)
        cum = lw
        step = 1
        while step < cc:
            cum = cum + jnp.where(pos >= step, pltpu.roll(cum, step, 0), 0.0)
            step *= 2
        e_inv = jnp.exp(-cum)
        rt_s[slot, rs, :] = r * jnp.exp(cum)
        kkt_s[slot, rs, :] = kk * jnp.exp(cum - lw)
        kh_s[slot, rs, :] = k2 * e_inv
        bh_s[slot, rs, :] = kk * a * e_inv
        bon_s[slot, rs, :] = _head_sum(r * k2 * rk_ref[...], hsum) * v
        cum_s[slot, rs, :] = cum

    if not batched:
        @pl.when(i == 0)
        def _():
            s_scr[...] = s0_ref[0]
            carry[...] = sh0_ref[0]
            if lookahead:
                project(x_ref[...], 0)
                for b in range(n_chunks):
                    token_block(b, 0)

    if lookahead:
        cur = lax.rem(i, 2)
        nxt = 1 - cur
        project(xn_ref[...], nxt)
        pending = [functools.partial(token_block, b, nxt) for b in range(n_chunks)]
    else:
        cur = 0
        project(x_ref[...], 0)
        for b in range(n_chunks):
            token_block(b, 0)
        pending = []

    def interleave():
        if pending:
            pending.pop(0)()

    two = 2 * cc
    sh = cc.bit_length() - 1
    hd = C_HEAD_DIM.bit_length() - 1
    n_double = sh - 1
    rho = lax.broadcasted_iota(jnp.int32, (two, PAIR_W), 0)
    lane = lax.broadcasted_iota(jnp.int32, (two, PAIR_W), 1)
    placed = (rho >> sh) == (lane >> hd)
    r2 = lax.broadcasted_iota(jnp.int32, (two, two), 0)
    c2 = lax.broadcasted_iota(jnp.int32, (two, two), 1)
    same = (r2 >> sh) == (c2 >> sh)
    strict = jnp.logical_and(same, r2 > c2)
    incl = jnp.logical_and(same, r2 >= c2)
    eye = jnp.where(r2 == c2, 1.0, 0.0).astype(F32)
    gr = lax.broadcasted_iota(jnp.int32, (PAIR_W, PAIR_W), 0)
    gc = lax.broadcasted_iota(jnp.int32, (PAIR_W, PAIR_W), 1)
    hmean = jnp.where((gr >> hd) == (gc >> hd), 1.0 / C_HEAD_DIM, 0.0).astype(BF16)

    def place(xv):
        return jnp.where(placed, jnp.concatenate([xv, xv], axis=0), 0.0)

    per_trip = 4 if n_chunks % 4 == 0 else (2 if n_chunks % 2 == 0 else 1)
    units = [(c, p) for c in range(per_trip) for p in range(N_PAIRS)]
    pairs = range(len(units))

    def chunk(j):
        r0s = [(j * per_trip + c) * cc for c in range(per_trip)]
        w_end = [jnp.exp(cum_s[cur, r0 + cc - 1:r0 + cc, :]) for r0 in r0s]
        lanes = [slice(p * PAIR_W, (p + 1) * PAIR_W) for _, p in units]
        sl = [(cur, slice(r0s[c], r0s[c] + cc), lanes[u]) for u, (c, _) in enumerate(units)]
        we = [w_end[c][:, lanes[u]] for u, (c, _) in enumerate(units)]
        rt = [place(rt_s[sl[p]]) for p in pairs]
        kkt = [place(kkt_s[sl[p]]) for p in pairs]
        kh = [place(kh_s[sl[p]]) for p in pairs]
        bh = [place(bh_s[sl[p]]) for p in pairs]
        vv = [place(v_s[sl[p]]) for p in pairs]
        cat0 = lambda *xs: jnp.concatenate(xs, axis=0)
        cat1 = lambda *xs: jnp.concatenate(xs, axis=1)
        gg = [_dot(cat0(kkt[p], rt[p]), cat0(kh[p], bh[p]), NT) for p in pairs]
        interleave()
        a_kk = [jnp.where(strict, gg[p][:two, :two], 0.0) for p in pairs]
        b_kk = [jnp.where(incl, gg[p][two:, :two], 0.0) for p in pairs]
        b_bb = [jnp.where(incl, gg[p][two:, two:], 0.0) for p in pairs]
        pw = [jnp.where(strict, -gg[p][:two, two:], 0.0) for p in pairs]
        tinv = [eye + pw[p] for p in pairs]
        pw = [_dot(pw[p], pw[p]) for p in pairs]
        for _ in range(n_double - 1):
            pt = [_dot(pw[p], cat1(pw[p], tinv[p])) for p in pairs]
            pw = [pt[p][:, :two] for p in pairs]
            tinv = [tinv[p] + pt[p][:, two:] for p in pairs]
            interleave()
        tinv = [tinv[p] + _dot(pw[p], tinv[p]) for p in pairs]
        interleave()
        av = [_dot(a_kk[p], vv[p]) for p in pairs]
        kv = [_dot(tinv[p], cat1(kkt[p], av[p])) for p in pairs]
        zero = jnp.zeros((two, PAIR_W), F32)
        ry = [_dot(cat1(b_kk[p], -b_bb[p]), cat0(cat1(zero, vv[p]), kv[p])) for p in pairs]
        rp = [rt[p] + ry[p][:, :PAIR_W] for p in pairs]
        y0 = [ry[p][:, PAIR_W:] for p in pairs]
        kkp = [kv[p][:, :PAIR_W] for p in pairs]
        vp = [kv[p][:, PAIR_W:] for p in pairs]
        bd = [bh[p] * we[p] for p in pairs]
        kd = [kh[p] * we[p] for p in pairs]
        q = [_dot(kkp[p], bd[p], TN) for p in pairs]
        z = [_dot(cat0(vv[p], vp[p]), cat0(kd[p], -bd[p]), TN) for p in pairs]
        y2 = []
        if batched:
            for u, (c, p) in enumerate(units):
                s_old = s0_ref[j * per_trip + c, p]
                y2.append(_dot(rp[u], s_old, NT) + y0[u])
                sout_ref[j * per_trip + c, p] = s_old * we[u] - _dot(s_old, q[u]) + z[u]
        else:
            state = [s_scr[p] for p in range(N_PAIRS)]
            for u, (_, p) in enumerate(units):
                s_old = state[p]
                y2.append(_dot(rp[u], s_old, NT) + y0[u])
                state[p] = s_old * we[u] - _dot(s_old, q[u]) + z[u]
            for p in range(N_PAIRS):
                s_scr[p] = state[p]
        ys = [y2[u][:cc] + y2[u][cc:] for u in pairs]
        yc = [ys[u] - _dot(ys[u], hmean) for u in pairs]
        var = [_dot(yc[u] * yc[u], hmean) for u in pairs]
        for u, (c, _) in enumerate(units):
            yn = yc[u] * lax.rsqrt(var[u] + C_GN_EPS) * lnw_ref[:, lanes[u]] + lnb_ref[:, lanes[u]]
            out = (yn + bon_s[sl[u]]) * g_s[sl[u]]
            if batched:
                o_ref[j * per_trip + c, :, lanes[u]] = out
            else:
                o_ref[(0,) + sl[u][1:]] = out

    for j in range(n_chunks // per_trip):
        chunk(j)
    while pending:
        interleave()
    if not batched:
        sout_ref[0] = s_scr[...]


def _pair_states(s):
    bsz = s.shape[0]
    s = s.reshape(bsz, N_PAIRS, 2, C_HEAD_DIM, C_HEAD_DIM)
    zero = jnp.zeros_like(s[:, :, 0])
    top = jnp.concatenate([s[:, :, 0], zero], axis=-1)
    bot = jnp.concatenate([zero, s[:, :, 1]], axis=-1)
    return jnp.concatenate([top, bot], axis=-2)


def _unpair_states(s):
    bsz = s.shape[0]
    d = C_HEAD_DIM
    return jnp.stack([s[:, :, :d, :d], s[:, :, d:, d:]], axis=2).reshape(bsz, C_HEADS, d, d)


def _rwkv(x, gain, w_c, lw, shift0, s0, *, tt, cc, nb):
    bsz, seq, _ = x.shape
    batched = nb > 1
    assert cc & (cc - 1) == 0
    assert (seq == cc and tt == nb * cc and bsz % nb == 0) if batched else (seq % tt == 0 and tt % cc == 0)
    row = lambda a: a.reshape(1, -1)
    sspec = pl.BlockSpec((nb, N_PAIRS, PAIR_W, PAIR_W), lambda b, i: (b, 0, 0, 0))
    shspec = pl.BlockSpec((nb, 1, C_SHIFT_WIDTH), lambda b, i: (b, 0, 0))
    vec = _const_spec((1, C_WIDTH))
    n_tiles = 1 if batched else seq // tt
    lookahead = n_tiles > 1
    tile = pltpu.VMEM((2 if lookahead else 1, tt, C_WIDTH), F32)
    hsum = _head_sum_matrix(C_HEAD_DIM)
    tok = lambda w: pl.BlockSpec((nb, tt // nb, w), lambda b, i: (b, i, 0))
    x_specs = [tok(D_MODEL)]
    x_args = [x]
    if lookahead:
        x_specs.append(pl.BlockSpec((1, tt, D_MODEL), lambda b, i: (b, jnp.minimum(i + 1, n_tiles - 1), 0)))
        x_args.append(x)
    oc, s_new, shift_new = pl.pallas_call(
        functools.partial(_rwkv_kernel, tt=tt, cc=cc, lookahead=lookahead, batched=batched),
        grid=(bsz // nb, n_tiles),
        in_specs=x_specs + [
                  _const_spec((1, D_MODEL)), w_c[1],
                  _const_spec((1, C_SHIFT_WIDTH)), shspec, sspec,
                  vec, _const_spec((C_RANK_W, C_WIDTH)), vec, _const_spec((C_RANK_A, C_WIDTH)),
                  _const_spec((C_RANK_G, C_WIDTH)), vec, vec, vec, vec, vec,
                  _const_spec((MXU_TILE, MXU_TILE))],
        out_specs=[tok(C_WIDTH), sspec, shspec],
        out_shape=[jax.ShapeDtypeStruct((bsz, seq, C_WIDTH), F32),
                   jax.ShapeDtypeStruct((bsz, N_PAIRS, PAIR_W, PAIR_W), F32),
                   jax.ShapeDtypeStruct((bsz, 1, C_SHIFT_WIDTH), F32)],
        scratch_shapes=[pltpu.VMEM((N_PAIRS, PAIR_W, PAIR_W), F32),
                        pltpu.VMEM((1, C_SHIFT_WIDTH), F32),
                        pltpu.VMEM((tt, 2 * C_WIDTH), F32),
                        pltpu.VMEM((tt, C_WIDTH), F32),
                        pltpu.VMEM((tt, C_WIDTH), F32)] + [tile] * 8,
        compiler_params=pltpu.CompilerParams(
            dimension_semantics=("arbitrary", "arbitrary"),
            vmem_limit_bytes=VMEM_LIMIT_BYTES),
        name="rwkv",
    )(*x_args, gain, w_c[0], row(lw["c_shift_mu"]), shift0, _pair_states(s0),
      row(lw["c_w0"]), lw["c_w2"].astype(BF16), row(lw["c_a0"]), lw["c_a2"].astype(BF16),
      lw["c_g2"].astype(BF16), row(lw["c_k_k"]), row(lw["c_k_a"]), row(lw["c_r_k"]),
      row(lw["c_ln_w"]), row(lw["c_ln_b"]), hsum)
    return oc, _unpair_states(s_new), shift_new


def _merge_kernel(x_ref, oa_ref, ob_ref, oc_ref, g_ref, wg_ref, wb_ref, wo_ref, y_ref):
    x = x_ref[...]
    h = _rms(x) * g_ref[...]
    gl = _dot(h, wg_ref[...])
    m = None
    for b, o_ref in enumerate((oa_ref, ob_ref, oc_ref)):
        t = _sigmoid(gl[:, b * D_MODEL:(b + 1) * D_MODEL]) * _dot(o_ref[...], wb_ref[b])
        m = t if m is None else m + t
    y_ref[...] = x + _dot(m, wo_ref[...])


def _merge(x, oa, ob, oc, gain, w_g, w_b, w_o, *, tm):
    rows = x.shape[0]
    assert rows % tm == 0
    tok = lambda w: pl.BlockSpec((tm, w), lambda i: (i, 0))
    return pl.pallas_call(
        _merge_kernel,
        grid=(rows // tm,),
        in_specs=[tok(D_MODEL), tok(A_WIDTH), tok(B_WIDTH), tok(C_WIDTH),
                  _const_spec((1, D_MODEL)), w_g[1], w_b[1], w_o[1]],
        out_specs=tok(D_MODEL),
        out_shape=jax.ShapeDtypeStruct((rows, D_MODEL), F32),
        compiler_params=pltpu.CompilerParams(
            dimension_semantics=("arbitrary",), vmem_limit_bytes=VMEM_LIMIT_BYTES),
        name="merge",
    )(x, oa, ob, oc, gain, w_g[0], w_b[0], w_o[0])


def _ffn_kernel(x_ref, p_ref, g_ref, wgate_ref, wup_ref, wdown_ref, wpp_ref, pg_ref, wpg_ref, y_ref):
    x = x_ref[...]
    hf = (_rms(x) * g_ref[...]).astype(BF16)
    gate = _dot(hf, wgate_ref[...])
    up = _dot(hf, wup_ref[...])
    x = x + _dot(gate * _sigmoid(gate) * up, wdown_ref[...])
    e = _rms(_dot(p_ref[...], wpp_ref[...])) * pg_ref[...]
    y_ref[...] = x + _sigmoid(_dot(_rms(x), wpg_ref[...])) * e


def _ffn(x, p_all, layer, gain, w_gate, w_up, w_down, w_pp, p_gain, w_pg, *, tm):
    rows = x.shape[0]
    ple = p_all.shape[-1]
    p = p_all.reshape(p_all.shape[0], rows, ple)
    assert rows % tm == 0
    tok = lambda w: pl.BlockSpec((tm, w), lambda i: (i, 0))
    return pl.pallas_call(
        _ffn_kernel,
        grid=(rows // tm,),
        in_specs=[tok(D_MODEL), pl.BlockSpec((None, tm, ple), lambda i: (layer, i, 0)),
                  _const_spec((1, D_MODEL)),
                  w_gate[1], w_up[1], w_down[1], w_pp[1],
                  _const_spec((1, D_MODEL)), w_pg[1]],
        out_specs=tok(D_MODEL),
        out_shape=jax.ShapeDtypeStruct((rows, D_MODEL), F32),
        compiler_params=pltpu.CompilerParams(
            dimension_semantics=("arbitrary",), vmem_limit_bytes=VMEM_LIMIT_BYTES),
        name="ffn",
    )(x, p, gain, w_gate[0], w_up[0], w_down[0], w_pp[0], p_gain, w_pg[0])


def _rel_bias_table(rel_bias, cq):
    nkeys = A_WIN + cq
    t_max = A_WIN + CHUNK - 1 + cq - 1
    heads, n_rel = rel_bias.shape
    w = t_max + 1
    tail = jnp.broadcast_to(rel_bias[:, n_rel - 1:], (heads, w - n_rel))
    rev = jnp.concatenate([tail, rel_bias[:, ::-1].astype(F32), jnp.zeros((heads, 1), F32)], axis=1)
    skew = jnp.tile(rev, (1, cq))[:, :cq * w].reshape(heads, cq, w)
    return skew[:, :, cq - 1:cq - 1 + nkeys] * LOG2_E


def _layer(x, p_all, layer, pos0, a_ck, a_cv, ret_s0, rwkv_s0, shift_prev, lw, big, cfg):
    bsz, seq, _ = x.shape
    row = lambda a: a.reshape(1, -1)
    w_a = _layer_weight(big["w_in"], layer, (OFF_A, OFF_B - OFF_A))
    w_b = _layer_weight(big["w_in"], layer, (OFF_B, OFF_C - OFF_B))
    w_c = _layer_weight(big["w_in"], layer, (OFF_C, OFF_G - OFF_C))
    w_g = _layer_weight(big["w_in"], layer, (OFF_G, IN_WIDTH - OFF_G))
    gain = row(lw["norm_mix"])

    nb = math.gcd(bsz, cfg["nb"])
    bias = _rel_bias_table(lw["a_rel_bias"], cfg["cq"])
    if a_ck is not None:
        a_ck = a_ck.astype(BF16).reshape(a_ck.shape[0], bsz, A_WIN, A_WIDTH)
        a_cv = a_cv.astype(BF16).reshape(a_cv.shape[0], bsz, A_WIN, A_WIDTH)
    oa, kn, av = _attention(x, gain, w_a, row(lw["a_q_norm"]), row(lw["a_k_norm"]), bias,
                            a_ck, a_cv, layer, tq=cfg["tq"], cq=cfg["cq"], nb=nb)
    keep = min(A_WIN, seq)
    new_ak = kn.reshape(bsz, keep, A_HEADS, A_HEAD_DIM)
    new_av = av.reshape(bsz, keep, A_HEADS, A_HEAD_DIM)

    ob, new_ret = _retention(x, gain, w_b, pos0, ret_s0, tb=cfg["tb"], nb=nb, nsub=cfg["nsub"])

    oc, new_rwkv, new_shift = _rwkv(x, gain, w_c, lw, shift_prev, rwkv_s0,
                                    tt=nb * cfg["tt"], cc=cfg["cc"], nb=nb)

    rows = bsz * seq
    tm = min(cfg["tm"], rows)
    flat = lambda t: t.reshape(rows, t.shape[-1])
    pick = lambda name: _layer_weight(big[name], layer)
    x1 = _merge(flat(x), flat(oa), flat(ob), flat(oc), gain, w_g,
                pick("w_branch"), pick("w_out"), tm=tm)
    x2 = _ffn(x1, p_all, layer, row(lw["norm_ffn"]), pick("w_ffn_gate"), pick("w_ffn_up"),
              pick("w_ffn_down"), pick("w_ple_proj"), row(lw["ple_norm"]), pick("w_ple_gate"), tm=tm)
    return x2.reshape(bsz, seq, D_MODEL), (new_ak, new_av, new_ret, new_rwkv, new_shift)


def _group_config(seq):
    if seq >= A_WIN:
        return dict(tq=A_WIN, cq=CHUNK, tb=256, nsub=2, tt=512, cc=CHUNK, tm=512, nb=1)
    return dict(tq=seq, cq=seq, tb=seq, nsub=1, tt=seq, cc=seq, tm=256, nb=8)


def kernel(x_prompt, x_sample, p_prompt, p_sample, cache_a_k, cache_a_v, state_ret, state_rwkv, state_rwkv_shift, norm_mix, w_in, a_q_norm, a_k_norm, a_rel_bias, c_shift_mu, c_w0, c_w2, c_a0, c_a2, c_g2, c_k_k, c_k_a, c_r_k, c_ln_w, c_ln_b, w_branch, w_out, norm_ffn, w_ffn_gate, w_ffn_up, w_ffn_down, w_ple_proj, ple_norm, w_ple_gate):
    depth = w_in.shape[0]

    def layer_weights(i):
        return dict(norm_mix=norm_mix[i], a_q_norm=a_q_norm[i], a_k_norm=a_k_norm[i],
                    a_rel_bias=a_rel_bias[i], c_shift_mu=c_shift_mu[i], c_w0=c_w0[i], c_w2=c_w2[i],
                    c_a0=c_a0[i], c_a2=c_a2[i], c_g2=c_g2[i], c_k_k=c_k_k[i], c_k_a=c_k_a[i], c_r_k=c_r_k[i],
                    c_ln_w=c_ln_w[i], c_ln_b=c_ln_b[i], norm_ffn=norm_ffn[i], ple_norm=ple_norm[i])

    big = dict(w_in=w_in, w_branch=w_branch, w_out=w_out, w_ffn_gate=w_ffn_gate, w_ffn_up=w_ffn_up,
               w_ffn_down=w_ffn_down, w_ple_proj=w_ple_proj, w_ple_gate=w_ple_gate)
    big = {name: w.astype(BF16) for name, w in big.items()}

    bp, lp, _ = x_prompt.shape
    cfg_p = _group_config(lp)
    ret0 = jnp.zeros((bp, B_HEADS, B_HEAD_DIM, B_HEAD_DIM), F32)
    rwkv0 = jnp.zeros((bp, C_HEADS, C_HEAD_DIM, C_HEAD_DIM), F32)
    shift0 = jnp.zeros((bp, 1, C_SHIFT_WIDTH), F32)
    y_prompt = x_prompt
    st_p = []
    for i in range(depth):
        y_prompt, st = _layer(y_prompt, p_prompt, i, 0, None, None, ret0, rwkv0, shift0,
                              layer_weights(i), big, cfg_p)
        st_p.append(st)

    cfg_s = _group_config(x_sample.shape[1])
    y_sample = x_sample
    st_s = []
    for i in range(depth):
        y_sample, st = _layer(y_sample, p_sample, i, PAST_LEN, cache_a_k, cache_a_v, state_ret[i],
                              state_rwkv[i], state_rwkv_shift[i], layer_weights(i), big, cfg_s)
        st_s.append(st)

    stack = lambda sts, j: jnp.stack([s[j] for s in sts])
    return (y_prompt, y_sample,
            stack(st_p, 0), stack(st_p, 1), stack(st_p, 2), stack(st_p, 3), stack(st_p, 4),
            stack(st_s, 0), stack(st_s, 1), stack(st_s, 2), stack(st_s, 3), stack(st_s, 4))
```

```python
import functools
import math

import jax
import jax.numpy as jnp
from jax import lax
from jax.experimental import pallas as pl
from jax.experimental.pallas import tpu as pltpu

F32 = jnp.float32
BF16 = jnp.bfloat16

D_MODEL = 1024
PAST_LEN = 2048
CHUNK = 64
NORM_EPS = 1e-6

A_HEADS = 8
A_HEAD_DIM = 64
A_WIDTH = 512
A_WIN = 512
A_REL_MAX = 256

B_HEADS = 4
B_HEAD_DIM = 128
B_WIDTH = 512
ROPE_BASE = 10000.0

C_HEADS = 8
C_HEAD_DIM = 64
C_WIDTH = 512
C_RANK_W = 64
C_RANK_A = 64
C_RANK_G = 128
C_SHIFT_WIDTH = 3 * C_WIDTH + C_RANK_W + C_RANK_A + C_RANK_G
C_GN_EPS = 64e-5

N_BRANCHES = 3

OFF_A = 0
OFF_B = 3 * A_WIDTH
OFF_C = OFF_B + 4 * B_WIDTH
OFF_G = OFF_C + C_SHIFT_WIDTH
IN_WIDTH = OFF_G + N_BRANCHES * D_MODEL

RET_LOG_GAMMA = tuple(math.log1p(-(2.0 ** (-5.0 - h))) for h in range(B_HEADS))

VMEM_LIMIT_BYTES = 56 * 1024 * 1024
MXU_TILE = 256
LOG2_E = math.log2(math.e)

NN = (((1,), (0,)), ((), ()))
NT = (((1,), (1,)), ((), ()))
TN = (((0,), (0,)), ((), ()))


def _dot(a, b, dims=NN):
    return lax.dot_general(a.astype(BF16), b.astype(BF16), dims, preferred_element_type=F32)


def _rms(x):
    return x * lax.rsqrt(jnp.mean(x * x, axis=-1, keepdims=True) + NORM_EPS)


def _sigmoid(x):
    return 1.0 / (1.0 + jnp.exp(-x))


def _const_spec(shape):
    nd = len(shape)
    return pl.BlockSpec(shape, lambda *_: (0,) * nd, pipeline_mode=pl.Buffered(1))


def _layer_weight(stacked, layer, cols=None):
    if cols is not None:
        off, width = cols
        if off % width == 0:
            spec = pl.BlockSpec((None, stacked.shape[1], width), lambda *_: (layer, 0, off // width),
                                pipeline_mode=pl.Buffered(1))
            return stacked, spec
        stacked = stacked[:, :, off:off + width]
    shape = stacked.shape[1:]
    spec = pl.BlockSpec((None,) + shape, lambda *_: (layer,) + (0,) * len(shape),
                        pipeline_mode=pl.Buffered(1))
    return stacked, spec


def _head_sum_matrix(head_dim):
    head = jnp.arange(MXU_TILE, dtype=jnp.int32) // head_dim
    return (head[:, None] == head[None, :]).astype(BF16)


def _head_sum(x, hsum):
    w = hsum.shape[0]
    return jnp.concatenate([_dot(x[:, g * w:(g + 1) * w], hsum) for g in range(x.shape[1] // w)], axis=1)


def _attn_kernel(*refs, tq, cq, has_cache, nb):
    if has_cache:
        (x_ref, g_ref, w_ref, qg_ref, kg_ref, bias_ref, hsum_ref, kc_ref, vc_ref,
         o_ref, kn_ref, v_ref, kwin, vwin, qs) = refs
    else:
        (x_ref, g_ref, w_ref, qg_ref, kg_ref, bias_ref, hsum_ref,
         o_ref, kn_ref, v_ref, kwin, vwin, qs) = refs
    i = pl.program_id(1)
    nkeys = A_WIN + cq
    pair_w = 2 * A_HEAD_DIM

    if has_cache:
        kwin[:, 0:A_WIN, :] = kc_ref[...].astype(BF16)
        vwin[:, 0:A_WIN, :] = vc_ref[...].astype(BF16)
    else:
        @pl.when(i == 0)
        def _():
            kwin[0:A_WIN, :] = jnp.zeros((A_WIN, A_WIDTH), BF16)
            vwin[0:A_WIN, :] = jnp.zeros((A_WIN, A_WIDTH), BF16)

    h = _rms(x_ref[...].reshape(nb * tq, D_MODEL)) * g_ref[...]
    z = _dot(h, w_ref[...])
    q = z[:, 0:A_WIDTH]
    k = z[:, A_WIDTH:2 * A_WIDTH]
    v = z[:, 2 * A_WIDTH:3 * A_WIDTH]
    hsum = hsum_ref[...]
    inv_d = 1.0 / A_HEAD_DIM
    qn = q * lax.rsqrt(_head_sum(q * q, hsum) * inv_d + NORM_EPS) * qg_ref[...]
    kn = k * lax.rsqrt(_head_sum(k * k, hsum) * inv_d + NORM_EPS) * kg_ref[...]
    qs[...] = qn * (A_HEAD_DIM ** -0.5 * LOG2_E)
    if has_cache:
        kwin[:, A_WIN:A_WIN + tq, :] = kn.astype(BF16).reshape(nb, tq, A_WIDTH)
        vwin[:, A_WIN:A_WIN + tq, :] = v.astype(BF16).reshape(nb, tq, A_WIDTH)
    else:
        kwin[A_WIN:A_WIN + tq, :] = kn.astype(BF16)
        vwin[A_WIN:A_WIN + tq, :] = v.astype(BF16)
    kn_ref[...] = kn.reshape(nb, tq, A_WIDTH)
    v_ref[...] = v.reshape(nb, tq, A_WIDTH)

    rho = lax.broadcasted_iota(jnp.int32, (2 * cq, pair_w), 0)
    lane = lax.broadcasted_iota(jnp.int32, (2 * cq, pair_w), 1)
    placed = (rho >> (cq.bit_length() - 1)) == (lane >> (A_HEAD_DIM.bit_length() - 1))

    n_chunks = nb * tq // cq
    per_trip = 4 if n_chunks % 4 == 0 else (2 if n_chunks % 2 == 0 else 1)
    units = [(c, p) for c in range(per_trip) for p in range(A_HEADS // 2)]
    ids = range(len(units))

    def chunk(j, carry, mask_past):
        r0s = [pl.multiple_of((j * per_trip + c) * cq, cq) for c in range(per_trip)]
        lanes = [slice(p * pair_w, (p + 1) * pair_w) for _, p in units]
        qc = [qs[pl.ds(r0s[c], cq), lanes[u]] for u, (c, _) in enumerate(units)]
        qp = [jnp.where(placed, jnp.concatenate([qc[u], qc[u]], axis=0), 0.0) for u in ids]
        if has_cache:
            win = [(j * per_trip + c, slice(None), lanes[u]) for u, (c, _) in enumerate(units)]
        else:
            win = [(pl.ds(r0s[c], nkeys), lanes[u]) for u, (c, _) in enumerate(units)]
        s = [_dot(qp[u], kwin[win[u]], NT) + bias_ref[p] for u, (_, p) in enumerate(units)]
        if mask_past:
            col = lax.broadcasted_iota(jnp.int32, (2 * cq, nkeys), 1)
            s = [jnp.where(r0s[c] + col >= A_WIN, s[u], -1e30) for u, (c, _) in enumerate(units)]
        m = [jnp.max(s[u], axis=-1, keepdims=True) for u in ids]
        e = [jnp.exp2(s[u] - m[u]) for u in ids]
        den = [jnp.sum(e[u], axis=-1, keepdims=True) for u in ids]
        o2 = [_dot(e[u], vwin[win[u]]) for u in ids]
        for u, (c, _) in enumerate(units):
            o = jnp.where(placed, o2[u] / den[u], 0.0)
            if has_cache:
                o_ref[j * per_trip + c, :, lanes[u]] = o[:cq] + o[cq:]
            else:
                o_ref[0, pl.ds(r0s[c], cq), lanes[u]] = o[:cq] + o[cq:]
        return carry

    n_trips = n_chunks // per_trip
    if has_cache:
        lax.fori_loop(0, n_trips, functools.partial(chunk, mask_past=False), 0)
    else:
        @pl.when(i == 0)
        def _():
            lax.fori_loop(0, n_trips, functools.partial(chunk, mask_past=True), 0)

        @pl.when(i > 0)
        def _():
            lax.fori_loop(0, n_trips, functools.partial(chunk, mask_past=False), 0)

    if not has_cache:
        kwin[0:A_WIN, :] = kwin[tq:tq + A_WIN, :]
        vwin[0:A_WIN, :] = vwin[tq:tq + A_WIN, :]


def _attention(x, gain, w_a, q_gain, k_gain, bias, cache_k, cache_v, layer, *, tq, cq, nb):
    bsz, seq, _ = x.shape
    has_cache = cache_k is not None
    assert seq % tq == 0 and tq % cq == 0
    assert (seq == tq == cq <= A_WIN and bsz % nb == 0) if has_cache else (tq == A_WIN and nb == 1)
    nkeys = A_WIN + cq
    tok = lambda w: pl.BlockSpec((nb, tq, w), lambda b, i: (b, i, 0))
    win_shape = (nb, nkeys, A_WIDTH) if has_cache else (A_WIN + tq, A_WIDTH)
    in_specs = [tok(D_MODEL), _const_spec((1, D_MODEL)), w_a[1],
                _const_spec((1, A_WIDTH)), _const_spec((1, A_WIDTH)),
                _const_spec((A_HEADS // 2, 2 * cq, nkeys)), _const_spec((MXU_TILE, MXU_TILE))]
    args = [x, gain, w_a[0], jnp.tile(q_gain, (1, A_HEADS)), jnp.tile(k_gain, (1, A_HEADS)),
            bias.reshape(A_HEADS // 2, 2 * cq, nkeys), _head_sum_matrix(A_HEAD_DIM)]
    if has_cache:
        cspec = pl.BlockSpec((None, nb, A_WIN, A_WIDTH), lambda b, i: (layer, b, 0, 0))
        in_specs += [cspec, cspec]
        args += [cache_k, cache_v]
    out = jax.ShapeDtypeStruct((bsz, seq, A_WIDTH), F32)
    return pl.pallas_call(
        functools.partial(_attn_kernel, tq=tq, cq=cq, has_cache=has_cache, nb=nb),
        grid=(bsz // nb, seq // tq),
        in_specs=in_specs,
        out_specs=[tok(A_WIDTH)] + [pl.BlockSpec((nb, tq, A_WIDTH), lambda b, i: (b, 0, 0))] * 2,
        out_shape=[out] + [jax.ShapeDtypeStruct((bsz, tq, A_WIDTH), F32)] * 2,
        scratch_shapes=[pltpu.VMEM(win_shape, BF16),
                        pltpu.VMEM(win_shape, BF16),
                        pltpu.VMEM((nb * tq, A_WIDTH), F32)],
        compiler_params=pltpu.CompilerParams(
            dimension_semantics=("arbitrary", "arbitrary"),
            vmem_limit_bytes=VMEM_LIMIT_BYTES),
        name="attention",
    )(*args)


def _ret_kernel(x_ref, g_ref, w_ref, cosa_ref, sina_ref, cosb_ref, sinb_ref, cosbs_ref, sinbs_ref,
                s0_ref, o_ref, sout_ref, s_scr, dmat_s, *, tb, nb, nsub):
    batched = nb > 1
    i = pl.program_id(1)

    if not batched:
        @pl.when(i == 0)
        def _():
            s_scr[...] = s0_ref[0]

    @pl.when(jnp.logical_and(pl.program_id(0) == 0, i == 0))
    def _():
        row = lax.broadcasted_iota(jnp.int32, (tb, tb), 0)
        col = lax.broadcasted_iota(jnp.int32, (tb, tb), 1)
        diff = row - col
        dist = jnp.maximum(diff, 0).astype(F32)
        for hh in range(B_HEADS):
            dmat_s[hh] = jnp.where(diff >= 0, jnp.exp(RET_LOG_GAMMA[hh] * dist), 0.0)

    rows = nb * nsub * tb
    h = _rms(x_ref[...].reshape(rows, D_MODEL)) * g_ref[...]
    z = _dot(h, w_ref[...])
    cos_a, sin_a = cosa_ref[0], sina_ref[0]
    cos_all = cos_a * cosb_ref[...] - sin_a * sinb_ref[...]
    sin_all = sin_a * cosbs_ref[...] + cos_a * sinbs_ref[...]
    n = lax.broadcasted_iota(jnp.int32, (tb, 1), 0).astype(F32)
    for hh in range(B_HEADS):
        lg = RET_LOG_GAMMA[hh]
        lo = hh * B_HEAD_DIM
        dmat = dmat_s[hh]
        dec_q = jnp.exp(lg * (n + 1.0))
        dec_k = jnp.exp(lg * ((tb - 1.0) - n))
        for bb in range(nb):
            state = s0_ref[bb, hh] if batched else s_scr[hh]
            for sub in range(nsub):
                rs = slice((bb * nsub + sub) * tb, (bb * nsub + sub + 1) * tb)
                pos = slice(sub * tb, (sub + 1) * tb)
                cosf = cos_all[pos, :]
                sinf = sin_all[pos, :]
                q = z[rs, lo:lo + B_HEAD_DIM]
                k = z[rs, B_WIDTH + lo:B_WIDTH + lo + B_HEAD_DIM]
                v = z[rs, 2 * B_WIDTH + lo:2 * B_WIDTH + lo + B_HEAD_DIM]
                gate = z[rs, 3 * B_WIDTH + lo:3 * B_WIDTH + lo + B_HEAD_DIM]
                q = q * cosf + pltpu.roll(q, B_HEAD_DIM // 2, 1) * sinf
                k = (k * cosf + pltpu.roll(k, B_HEAD_DIM // 2, 1) * sinf) * (B_HEAD_DIM ** -0.5)
                scores = _dot(q, k, NT) * dmat
                o = _dot(scores, v) + _dot(q, state) * dec_q
                state = math.exp(lg * tb) * state + _dot(k * dec_k, v, TN)
                o_ref[bb, pos, lo:lo + B_HEAD_DIM] = _rms(o) * (gate * _sigmoid(gate))
            if batched:
                sout_ref[bb, hh] = state
            else:
                s_scr[hh] = state
    if not batched:
        sout_ref[0] = s_scr[...]


def _retention(x, gain, w_b, pos0, s0, *, tb, nb, nsub):
    bsz, seq, _ = x.shape
    assert (seq == tb and bsz % nb == 0 and nsub == 1) if nb > 1 else seq % (nsub * tb) == 0
    sspec = pl.BlockSpec((nb, B_HEADS, B_HEAD_DIM, B_HEAD_DIM), lambda b, i: (b, 0, 0, 0))
    tok = lambda w: pl.BlockSpec((nb, nsub * tb, w), lambda b, i: (b, i, 0))
    step_rows = nsub * tb
    half = B_HEAD_DIM // 2
    inv = ROPE_BASE ** (-jnp.arange(half, dtype=F32) / half)
    base = (pos0 + step_rows * jnp.arange(seq // step_rows, dtype=jnp.int32)).astype(F32)[:, None] * inv[None, :]
    offs = jnp.arange(step_rows, dtype=jnp.int32).astype(F32)[:, None] * inv[None, :]
    both = lambda t: jnp.concatenate([t, t], axis=-1)
    signed = lambda t: jnp.concatenate([-t, t], axis=-1)
    step_row = lambda t: both(t).reshape(-1, 1, B_HEAD_DIM)
    a_spec = pl.BlockSpec((1, 1, B_HEAD_DIM), lambda b, i: (i, 0, 0))
    b_spec = _const_spec((step_rows, B_HEAD_DIM))
    return pl.pallas_call(
        functools.partial(_ret_kernel, tb=tb, nb=nb, nsub=nsub),
        grid=(bsz // nb, seq // (nsub * tb)),
        in_specs=[tok(D_MODEL),
                  _const_spec((1, D_MODEL)), w_b[1],
                  a_spec, a_spec, b_spec, b_spec, b_spec, b_spec,
                  sspec],
        out_specs=[tok(B_WIDTH), sspec],
        out_shape=[jax.ShapeDtypeStruct((bsz, seq, B_WIDTH), F32),
                   jax.ShapeDtypeStruct((bsz, B_HEADS, B_HEAD_DIM, B_HEAD_DIM), F32)],
        scratch_shapes=[pltpu.VMEM((B_HEADS, B_HEAD_DIM, B_HEAD_DIM), F32),
                        pltpu.VMEM((B_HEADS, tb, tb), F32)],
        compiler_params=pltpu.CompilerParams(
            dimension_semantics=("arbitrary", "arbitrary"),
            vmem_limit_bytes=VMEM_LIMIT_BYTES),
        name="retention",
    )(x, gain, w_b[0], step_row(jnp.cos(base)), step_row(jnp.sin(base)),
      both(jnp.cos(offs)), both(jnp.sin(offs)), signed(jnp.cos(offs)), signed(jnp.sin(offs)), s0)


PAIR_W = 2 * C_HEAD_DIM
N_PAIRS = C_HEADS // 2


def _rwkv_kernel(*refs, tt, cc, lookahead, batched):
    if lookahead:
        x_ref, xn_ref = refs[:2]
        refs = refs[2:]
    else:
        x_ref, xn_ref = refs[0], None
        refs = refs[1:]
    (g_ref, w_ref, mu_ref, sh0_ref, s0_ref, w0_ref, w2_ref, a0_ref, a2_ref,
     g2_ref, kk_ref, ka_ref, rk_ref, lnw_ref, lnb_ref, hsum_ref,
     o_ref, sout_ref, shout_ref,
     s_scr, carry, rk_raw, uw_raw, ua_raw, rt_s, kkt_s, kh_s, bh_s, v_s, bon_s, g_s, cum_s) = refs
    i = pl.program_id(1)

    n_chunks = tt // cc

    def project(x, slot):
        h = _rms(x.reshape(tt, D_MODEL)) * g_ref[...]
        cz = _dot(h, w_ref[...])
        rows = lax.broadcasted_iota(jnp.int32, (tt, 1), 0)
        if batched:
            first = jnp.bitwise_and(rows, cc - 1) == 0
            before = jnp.broadcast_to(sh0_ref[...], (n_chunks, cc, C_SHIFT_WIDTH)).reshape(tt, C_SHIFT_WIDTH)
            shout_ref[...] = cz.reshape(n_chunks, cc, C_SHIFT_WIDTH)[:, cc - 1:cc, :]
        else:
            first = rows == 0
            before = carry[...]
            last = cz[tt - 1:tt, :]
            carry[...] = last
            shout_ref[0] = last
        prev = jnp.where(first, before, pltpu.roll(cz, 1, 0))
        cs = cz + (prev - cz) * mu_ref[...]
        off = 3 * C_WIDTH
        w_lo = cs[:, off:off + C_RANK_W]
        a_lo = cs[:, off + C_RANK_W:off + C_RANK_W + C_RANK_A]
        g_lo = cs[:, off + C_RANK_W + C_RANK_A:C_SHIFT_WIDTH]
        rk_raw[...] = cs[:, 0:2 * C_WIDTH]
        v_s[slot] = cs[:, 2 * C_WIDTH:3 * C_WIDTH]
        uw_raw[...] = w0_ref[...] + _dot(jnp.tanh(w_lo), w2_ref[...])
        ua_raw[...] = a0_ref[...] + _dot(a_lo, a2_ref[...])
        g_s[slot] = _dot(_sigmoid(g_lo), g2_ref[...])

    def token_block(b, slot):
        rs = slice(b * cc, (b + 1) * cc)
        r = rk_raw[rs, 0:C_WIDTH]
        k = rk_raw[rs, C_WIDTH:2 * C_WIDTH]
        v = v_s[slot, rs, :]
        lw = -math.exp(-0.5) * _sigmoid(uw_raw[rs, :])
        a = _sigmoid(ua_raw[rs, :])
        hsum = hsum_ref[...]
        kk_raw = k * kk_ref[...]
        kk = kk_raw / jnp.maximum(jnp.sqrt(_head_sum(kk_raw * kk_raw, hsum)), 1e-12)
        k2 = k * (1.0 + (a - 1.0) * ka_ref[...])
        pos = lax.broadcasted_iota(jnp.int32, (cc, 1), 0)
        cum = lw
        step = 1
        while step < cc:
            cum = cum + jnp.where(pos >= step, pltpu.roll(cum, step, 0), 0.0)
            step *= 2
        e_inv = jnp.exp(-cum)
        rt_s[slot, rs, :] = r * jnp.exp(cum)
        kkt_s[slot, rs, :] = kk * jnp.exp(cum - lw)
        kh_s[slot, rs, :] = k2 * e_inv
        bh_s[slot, rs, :] = kk * a * e_inv
        bon_s[slot, rs, :] = _head_sum(r * k2 * rk_ref[...], hsum) * v
        cum_s[slot, rs, :] = cum

    if not batched:
        @pl.when(i == 0)
        def _():
            s_scr[...] = s0_ref[0]
            carry[...] = sh0_ref[0]
            if lookahead:
                project(x_ref[...], 0)
                for b in range(n_chunks):
                    token_block(b, 0)

    if lookahead:
        cur = lax.rem(i, 2)
        nxt = 1 - cur
        project(xn_ref[...], nxt)
        pending = [functools.partial(token_block, b, nxt) for b in range(n_chunks)]
    else:
        cur = 0
        project(x_ref[...], 0)
        for b in range(n_chunks):
            token_block(b, 0)
        pending = []

    def interleave():
        if pending:
            pending.pop(0)()

    two = 2 * cc
    sh = cc.bit_length() - 1
    hd = C_HEAD_DIM.bit_length() - 1
    n_double = sh - 1
    rho = lax.broadcasted_iota(jnp.int32, (two, PAIR_W), 0)
    lane = lax.broadcasted_iota(jnp.int32, (two, PAIR_W), 1)
    placed = (rho >> sh) == (lane >> hd)
    r2 = lax.broadcasted_iota(jnp.int32, (two, two), 0)
    c2 = lax.broadcasted_iota(jnp.int32, (two, two), 1)
    same = (r2 >> sh) == (c2 >> sh)
    strict = jnp.logical_and(same, r2 > c2)
    incl = jnp.logical_and(same, r2 >= c2)
    eye = jnp.where(r2 == c2, 1.0, 0.0).astype(F32)
    gr = lax.broadcasted_iota(jnp.int32, (PAIR_W, PAIR_W), 0)
    gc = lax.broadcasted_iota(jnp.int32, (PAIR_W, PAIR_W), 1)
    hmean = jnp.where((gr >> hd) == (gc >> hd), 1.0 / C_HEAD_DIM, 0.0).astype(BF16)

    def place(xv):
        return jnp.where(placed, jnp.concatenate([xv, xv], axis=0), 0.0)

    per_trip = 4 if n_chunks % 4 == 0 else (2 if n_chunks % 2 == 0 else 1)
    units = [(c, p) for c in range(per_trip) for p in range(N_PAIRS)]
    pairs = range(len(units))

    def chunk(j):
        r0s = [(j * per_trip + c) * cc for c in range(per_trip)]
        w_end = [jnp.exp(cum_s[cur, r0 + cc - 1:r0 + cc, :]) for r0 in r0s]
        lanes = [slice(p * PAIR_W, (p + 1) * PAIR_W) for _, p in units]
        sl = [(cur, slice(r0s[c], r0s[c] + cc), lanes[u]) for u, (c, _) in enumerate(units)]
        we = [w_end[c][:, lanes[u]] for u, (c, _) in enumerate(units)]
        rt = [place(rt_s[sl[p]]) for p in pairs]
        kkt = [place(kkt_s[sl[p]]) for p in pairs]
        kh = [place(kh_s[sl[p]]) for p in pairs]
        bh = [place(bh_s[sl[p]]) for p in pairs]
        vv = [place(v_s[sl[p]]) for p in pairs]
        cat0 = lambda *xs: jnp.concatenate(xs, axis=0)
        cat1 = lambda *xs: jnp.concatenate(xs, axis=1)
        gg = [_dot(cat0(kkt[p], rt[p]), cat0(kh[p], bh[p]), NT) for p in pairs]
        interleave()
        a_kk = [jnp.where(strict, gg[p][:two, :two], 0.0) for p in pairs]
        b_kk = [jnp.where(incl, gg[p][two:, :two], 0.0) for p in pairs]
        b_bb = [jnp.where(incl, gg[p][two:, two:], 0.0) for p in pairs]
        pw = [jnp.where(strict, -gg[p][:two, two:], 0.0) for p in pairs]
        tinv = [eye + pw[p] for p in pairs]
        pw = [_dot(pw[p], pw[p]) for p in pairs]
        for _ in range(n_double - 1):
            pt = [_dot(pw[p], cat1(pw[p], tinv[p])) for p in pairs]
            pw = [pt[p][:, :two] for p in pairs]
            tinv = [tinv[p] + pt[p][:, two:] for p in pairs]
            interleave()
        tinv = [tinv[p] + _dot(pw[p], tinv[p]) for p in pairs]
        interleave()
        av = [_dot(a_kk[p], vv[p]) for p in pairs]
        kv = [_dot(tinv[p], cat1(kkt[p], av[p])) for p in pairs]
        zero = jnp.zeros((two, PAIR_W), F32)
        ry = [_dot(cat1(b_kk[p], -b_bb[p]), cat0(cat1(zero, vv[p]), kv[p])) for p in pairs]
        rp = [rt[p] + ry[p][:, :PAIR_W] for p in pairs]
        y0 = [ry[p][:, PAIR_W:] for p in pairs]
        kkp = [kv[p][:, :PAIR_W] for p in pairs]
        vp = [kv[p][:, PAIR_W:] for p in pairs]
        bd = [bh[p] * we[p] for p in pairs]
        kd = [kh[p] * we[p] for p in pairs]
        q = [_dot(kkp[p], bd[p], TN) for p in pairs]
        z = [_dot(cat0(vv[p], vp[p]), cat0(kd[p], -bd[p]), TN) for p in pairs]
        y2 = []

        def advance(u, s_old):
            y2.append(_dot(rp[u], s_old, NT) + y0[u])
            return s_old * we[u] - _dot(s_old, q[u]) + z[u]

        if batched:
            for u, (c, p) in enumerate(units):
                sout_ref[j * per_trip + c, p] = advance(u, s0_ref[j * per_trip + c, p])
        else:
            state = [s_scr[p] for p in range(N_PAIRS)]
            for u, (_, p) in enumerate(units):
                state[p] = advance(u, state[p])
            for p in range(N_PAIRS):
                s_scr[p] = state[p]
        ys = [y2[u][:cc] + y2[u][cc:] for u in pairs]
        yc = [ys[u] - _dot(ys[u], hmean) for u in pairs]
        var = [_dot(yc[u] * yc[u], hmean) for u in pairs]
        for u, (c, _) in enumerate(units):
            yn = yc[u] * lax.rsqrt(var[u] + C_GN_EPS) * lnw_ref[:, lanes[u]] + lnb_ref[:, lanes[u]]
            out = (yn + bon_s[sl[u]]) * g_s[sl[u]]
            if batched:
                o_ref[j * per_trip + c, :, lanes[u]] = out
            else:
                o_ref[(0,) + sl[u][1:]] = out

    for j in range(n_chunks // per_trip):
        chunk(j)
    while pending:
        interleave()
    if not batched:
        sout_ref[0] = s_scr[...]


def _pair_states(s):
    bsz = s.shape[0]
    s = s.reshape(bsz, N_PAIRS, 2, C_HEAD_DIM, C_HEAD_DIM)
    zero = jnp.zeros_like(s[:, :, 0])
    top = jnp.concatenate([s[:, :, 0], zero], axis=-1)
    bot = jnp.concatenate([zero, s[:, :, 1]], axis=-1)
    return jnp.concatenate([top, bot], axis=-2)


def _unpair_states(s):
    bsz = s.shape[0]
    d = C_HEAD_DIM
    return jnp.stack([s[:, :, :d, :d], s[:, :, d:, d:]], axis=2).reshape(bsz, C_HEADS, d, d)


def _rwkv(x, gain, w_c, lw, shift0, s0, *, tt, cc, nb):
    bsz, seq, _ = x.shape
    batched = nb > 1
    assert cc & (cc - 1) == 0
    assert (seq == cc and tt == nb * cc and bsz % nb == 0) if batched else (seq % tt == 0 and tt % cc == 0)
    row = lambda a: a.reshape(1, -1)
    sspec = pl.BlockSpec((nb, N_PAIRS, PAIR_W, PAIR_W), lambda b, i: (b, 0, 0, 0))
    shspec = pl.BlockSpec((nb, 1, C_SHIFT_WIDTH), lambda b, i: (b, 0, 0))
    vec = _const_spec((1, C_WIDTH))
    n_tiles = 1 if batched else seq // tt
    lookahead = n_tiles > 1
    tile = pltpu.VMEM((2 if lookahead else 1, tt, C_WIDTH), F32)
    hsum = _head_sum_matrix(C_HEAD_DIM)
    tok = lambda w: pl.BlockSpec((nb, tt // nb, w), lambda b, i: (b, i, 0))
    x_specs = [tok(D_MODEL)]
    x_args = [x]
    if lookahead:
        x_specs = [pl.BlockSpec((1, tt, D_MODEL), lambda b, i: (b, 0, 0)),
                   pl.BlockSpec((1, tt, D_MODEL), lambda b, i: (b, jnp.minimum(i + 1, n_tiles - 1), 0))]
        x_args = [x, x]
    oc, s_new, shift_new = pl.pallas_call(
        functools.partial(_rwkv_kernel, tt=tt, cc=cc, lookahead=lookahead, batched=batched),
        grid=(bsz // nb, n_tiles),
        in_specs=x_specs + [
                  _const_spec((1, D_MODEL)), w_c[1],
                  _const_spec((1, C_SHIFT_WIDTH)), shspec, sspec,
                  vec, _const_spec((C_RANK_W, C_WIDTH)), vec, _const_spec((C_RANK_A, C_WIDTH)),
                  _const_spec((C_RANK_G, C_WIDTH)), vec, vec, vec, vec, vec,
                  _const_spec((MXU_TILE, MXU_TILE))],
        out_specs=[tok(C_WIDTH), sspec, shspec],
        out_shape=[jax.ShapeDtypeStruct((bsz, seq, C_WIDTH), F32),
                   jax.ShapeDtypeStruct((bsz, N_PAIRS, PAIR_W, PAIR_W), F32),
                   jax.ShapeDtypeStruct((bsz, 1, C_SHIFT_WIDTH), F32)],
        scratch_shapes=[pltpu.VMEM((N_PAIRS, PAIR_W, PAIR_W), F32),
                        pltpu.VMEM((1, C_SHIFT_WIDTH), F32),
                        pltpu.VMEM((tt, 2 * C_WIDTH), F32),
                        pltpu.VMEM((tt, C_WIDTH), F32),
                        pltpu.VMEM((tt, C_WIDTH), F32)] + [tile] * 8,
        compiler_params=pltpu.CompilerParams(
            dimension_semantics=("arbitrary", "arbitrary"),
            vmem_limit_bytes=VMEM_LIMIT_BYTES),
        name="rwkv",
    )(*x_args, gain, w_c[0], row(lw["c_shift_mu"]), shift0, _pair_states(s0),
      row(lw["c_w0"]), lw["c_w2"].astype(BF16), row(lw["c_a0"]), lw["c_a2"].astype(BF16),
      lw["c_g2"].astype(BF16), row(lw["c_k_k"]), row(lw["c_k_a"]), row(lw["c_r_k"]),
      row(lw["c_ln_w"]), row(lw["c_ln_b"]), hsum)
    return oc, _unpair_states(s_new), shift_new


def _merge_kernel(x_ref, oa_ref, ob_ref, oc_ref, g_ref, wg_ref, wb_ref, wo_ref, y_ref):
    x = x_ref[...]
    h = _rms(x) * g_ref[...]
    gl = _dot(h, wg_ref[...])
    m = None
    for b, o_ref in enumerate((oa_ref, ob_ref, oc_ref)):
        t = _sigmoid(gl[:, b * D_MODEL:(b + 1) * D_MODEL]) * _dot(o_ref[...], wb_ref[b])
        m = t if m is None else m + t
    y_ref[...] = x + _dot(m, wo_ref[...])


def _merge(x, oa, ob, oc, gain, w_g, w_b, w_o, *, tm):
    rows = x.shape[0]
    assert rows % tm == 0
    tok = lambda w: pl.BlockSpec((tm, w), lambda i: (i, 0))
    return pl.pallas_call(
        _merge_kernel,
        grid=(rows // tm,),
        in_specs=[tok(D_MODEL), tok(A_WIDTH), tok(B_WIDTH), tok(C_WIDTH),
                  _const_spec((1, D_MODEL)), w_g[1], w_b[1], w_o[1]],
        out_specs=tok(D_MODEL),
        out_shape=jax.ShapeDtypeStruct((rows, D_MODEL), F32),
        compiler_params=pltpu.CompilerParams(
            dimension_semantics=("arbitrary",), vmem_limit_bytes=VMEM_LIMIT_BYTES),
        name="merge",
    )(x, oa, ob, oc, gain, w_g[0], w_b[0], w_o[0])


def _ffn_kernel(x_ref, p_ref, g_ref, wgate_ref, wup_ref, wdown_ref, wpp_ref, pg_ref, wpg_ref, y_ref):
    x = x_ref[...]
    hf = (_rms(x) * g_ref[...]).astype(BF16)
    gate = _dot(hf, wgate_ref[...])
    up = _dot(hf, wup_ref[...])
    x = x + _dot(gate * _sigmoid(gate) * up, wdown_ref[...])
    e = _rms(_dot(p_ref[...], wpp_ref[...])) * pg_ref[...]
    y_ref[...] = x + _sigmoid(_dot(_rms(x), wpg_ref[...])) * e


def _ffn(x, p_all, layer, gain, w_gate, w_up, w_down, w_pp, p_gain, w_pg, *, tm):
    rows = x.shape[0]
    ple = p_all.shape[-1]
    p = p_all.reshape(p_all.shape[0], rows, ple)
    assert rows % tm == 0
    tok = lambda w: pl.BlockSpec((tm, w), lambda i: (i, 0))
    return pl.pallas_call(
        _ffn_kernel,
        grid=(rows // tm,),
        in_specs=[tok(D_MODEL), pl.BlockSpec((None, tm, ple), lambda i: (layer, i, 0)),
                  _const_spec((1, D_MODEL)),
                  w_gate[1], w_up[1], w_down[1], w_pp[1],
                  _const_spec((1, D_MODEL)), w_pg[1]],
        out_specs=tok(D_MODEL),
        out_shape=jax.ShapeDtypeStruct((rows, D_MODEL), F32),
        compiler_params=pltpu.CompilerParams(
            dimension_semantics=("arbitrary",), vmem_limit_bytes=VMEM_LIMIT_BYTES),
        name="ffn",
    )(x, p, gain, w_gate[0], w_up[0], w_down[0], w_pp[0], p_gain, w_pg[0])


def _rel_bias_table(rel_bias, cq):
    nkeys = A_WIN + cq
    t_max = A_WIN + CHUNK - 1 + cq - 1
    heads, n_rel = rel_bias.shape
    w = t_max + 1
    tail = jnp.broadcast_to(rel_bias[:, n_rel - 1:], (heads, w - n_rel))
    rev = jnp.concatenate([tail, rel_bias[:, ::-1].astype(F32), jnp.zeros((heads, 1), F32)], axis=1)
    skew = jnp.tile(rev, (1, cq))[:, :cq * w].reshape(heads, cq, w)
    return skew[:, :, cq - 1:cq - 1 + nkeys] * LOG2_E


def _layer(x, p_all, layer, pos0, a_ck, a_cv, ret_s0, rwkv_s0, shift_prev, lw, big, cfg):
    bsz, seq, _ = x.shape
    row = lambda a: a.reshape(1, -1)
    w_a = _layer_weight(big["w_in"], layer, (OFF_A, OFF_B - OFF_A))
    w_b = _layer_weight(big["w_in"], layer, (OFF_B, OFF_C - OFF_B))
    w_c = _layer_weight(big["w_in"], layer, (OFF_C, OFF_G - OFF_C))
    w_g = _layer_weight(big["w_in"], layer, (OFF_G, IN_WIDTH - OFF_G))
    gain = row(lw["norm_mix"])

    nb = math.gcd(bsz, cfg["nb"])
    bias = _rel_bias_table(lw["a_rel_bias"], cfg["cq"])
    if a_ck is not None:
        a_ck = a_ck.astype(BF16).reshape(a_ck.shape[0], bsz, A_WIN, A_WIDTH)
        a_cv = a_cv.astype(BF16).reshape(a_cv.shape[0], bsz, A_WIN, A_WIDTH)
    oa, kn, av = _attention(x, gain, w_a, row(lw["a_q_norm"]), row(lw["a_k_norm"]), bias,
                            a_ck, a_cv, layer, tq=cfg["tq"], cq=cfg["cq"], nb=nb)
    keep = min(A_WIN, seq)
    new_ak = kn.reshape(bsz, keep, A_HEADS, A_HEAD_DIM)
    new_av = av.reshape(bsz, keep, A_HEADS, A_HEAD_DIM)

    ob, new_ret = _retention(x, gain, w_b, pos0, ret_s0, tb=cfg["tb"], nb=nb, nsub=cfg["nsub"])

    oc, new_rwkv, new_shift = _rwkv(x, gain, w_c, lw, shift_prev, rwkv_s0,
                                    tt=nb * cfg["tt"], cc=cfg["cc"], nb=nb)

    rows = bsz * seq
    tm = min(cfg["tm"], rows)
    flat = lambda t: t.reshape(rows, t.shape[-1])
    pick = lambda name: _layer_weight(big[name], layer)
    x1 = _merge(flat(x), flat(oa), flat(ob), flat(oc), gain, w_g,
                pick("w_branch"), pick("w_out"), tm=tm)
    x2 = _ffn(x1, p_all, layer, row(lw["norm_ffn"]), pick("w_ffn_gate"), pick("w_ffn_up"),
              pick("w_ffn_down"), pick("w_ple_proj"), row(lw["ple_norm"]), pick("w_ple_gate"), tm=tm)
    return x2.reshape(bsz, seq, D_MODEL), (new_ak, new_av, new_ret, new_rwkv, new_shift)


def _group_config(seq):
    if seq >= A_WIN:
        return dict(tq=A_WIN, cq=CHUNK, tb=256, nsub=2, tt=512, cc=CHUNK, tm=512, nb=1)
    return dict(tq=seq, cq=seq, tb=seq, nsub=1, tt=seq, cc=seq, tm=256, nb=8)


def kernel(x_prompt, x_sample, p_prompt, p_sample, cache_a_k, cache_a_v, state_ret, state_rwkv, state_rwkv_shift, norm_mix, w_in, a_q_norm, a_k_norm, a_rel_bias, c_shift_mu, c_w0, c_w2, c_a0, c_a2, c_g2, c_k_k, c_k_a, c_r_k, c_ln_w, c_ln_b, w_branch, w_out, norm_ffn, w_ffn_gate, w_ffn_up, w_ffn_down, w_ple_proj, ple_norm, w_ple_gate):
    depth = w_in.shape[0]

    def layer_weights(i):
        return dict(norm_mix=norm_mix[i], a_q_norm=a_q_norm[i], a_k_norm=a_k_norm[i],
                    a_rel_bias=a_rel_bias[i], c_shift_mu=c_shift_mu[i], c_w0=c_w0[i], c_w2=c_w2[i],
                    c_a0=c_a0[i], c_a2=c_a2[i], c_g2=c_g2[i], c_k_k=c_k_k[i], c_k_a=c_k_a[i], c_r_k=c_r_k[i],
                    c_ln_w=c_ln_w[i], c_ln_b=c_ln_b[i], norm_ffn=norm_ffn[i], ple_norm=ple_norm[i])

    big = dict(w_in=w_in, w_branch=w_branch, w_out=w_out, w_ffn_gate=w_ffn_gate, w_ffn_up=w_ffn_up,
               w_ffn_down=w_ffn_down, w_ple_proj=w_ple_proj, w_ple_gate=w_ple_gate)
    big = {name: w.astype(BF16) for name, w in big.items()}

    bp, lp, _ = x_prompt.shape
    cfg_p = _group_config(lp)
    ret0 = jnp.zeros((bp, B_HEADS, B_HEAD_DIM, B_HEAD_DIM), F32)
    rwkv0 = jnp.zeros((bp, C_HEADS, C_HEAD_DIM, C_HEAD_DIM), F32)
    shift0 = jnp.zeros((bp, 1, C_SHIFT_WIDTH), F32)
    y_prompt = x_prompt
    st_p = []
    for i in range(depth):
        y_prompt, st = _layer(y_prompt, p_prompt, i, 0, None, None, ret0, rwkv0, shift0,
                              layer_weights(i), big, cfg_p)
        st_p.append(st)

    cfg_s = _group_config(x_sample.shape[1])
    y_sample = x_sample
    st_s = []
    for i in range(depth):
        y_sample, st = _layer(y_sample, p_sample, i, PAST_LEN, cache_a_k, cache_a_v, state_ret[i],
                              state_rwkv[i], state_rwkv_shift[i], layer_weights(i), big, cfg_s)
        st_s.append(st)

    stack = lambda sts, j: jnp.stack([s[j] for s in sts])
    return (y_prompt, y_sample,
            stack(st_p, 0), stack(st_p, 1), stack(st_p, 2), stack(st_p, 3), stack(st_p, 4),
            stack(st_s, 0), stack(st_s, 1), stack(st_s, 2), stack(st_s, 3), stack(st_s, 4))
```

```python
import functools
import math

import jax
import jax.numpy as jnp
from jax import lax
from jax.experimental import pallas as pl
from jax.experimental.pallas import tpu as pltpu

F32 = jnp.float32
BF16 = jnp.bfloat16

D_MODEL = 1024
PAST_LEN = 2048
CHUNK = 64
NORM_EPS = 1e-6

A_HEADS = 8
A_HEAD_DIM = 64
A_WIDTH = 512
A_WIN = 512
A_REL_MAX = 256

B_HEADS = 4
B_HEAD_DIM = 128
B_WIDTH = 512
ROPE_BASE = 10000.0

C_HEADS = 8
C_HEAD_DIM = 64
C_WIDTH = 512
C_RANK_W = 64
C_RANK_A = 64
C_RANK_G = 128
C_SHIFT_WIDTH = 3 * C_WIDTH + C_RANK_W + C_RANK_A + C_RANK_G
C_GN_EPS = 64e-5

N_BRANCHES = 3

OFF_A = 0
OFF_B = 3 * A_WIDTH
OFF_C = OFF_B + 4 * B_WIDTH
OFF_G = OFF_C + C_SHIFT_WIDTH
IN_WIDTH = OFF_G + N_BRANCHES * D_MODEL

RET_LOG_GAMMA = tuple(math.log1p(-(2.0 ** (-5.0 - h))) for h in range(B_HEADS))

VMEM_LIMIT_BYTES = 56 * 1024 * 1024
MXU_TILE = 256
LOG2_E = math.log2(math.e)

NN = (((1,), (0,)), ((), ()))
NT = (((1,), (1,)), ((), ()))
TN = (((0,), (0,)), ((), ()))


def _dot(a, b, dims=NN):
    return lax.dot_general(a.astype(BF16), b.astype(BF16), dims, preferred_element_type=F32)


def _rms(x):
    return x * lax.rsqrt(jnp.mean(x * x, axis=-1, keepdims=True) + NORM_EPS)


def _sigmoid(x):
    return 1.0 / (1.0 + jnp.exp(-x))


def _const_spec(shape):
    nd = len(shape)
    return pl.BlockSpec(shape, lambda *_: (0,) * nd, pipeline_mode=pl.Buffered(1))


def _layer_weight(stacked, layer, cols=None):
    if cols is not None:
        off, width = cols
        if off % width == 0:
            spec = pl.BlockSpec((None, stacked.shape[1], width), lambda *_: (layer, 0, off // width),
                                pipeline_mode=pl.Buffered(1))
            return stacked, spec
        stacked = stacked[:, :, off:off + width]
    shape = stacked.shape[1:]
    spec = pl.BlockSpec((None,) + shape, lambda *_: (layer,) + (0,) * len(shape),
                        pipeline_mode=pl.Buffered(1))
    return stacked, spec


def _head_sum_matrix(head_dim):
    head = jnp.arange(MXU_TILE, dtype=jnp.int32) // head_dim
    return (head[:, None] == head[None, :]).astype(BF16)


def _head_sum(x, hsum):
    w = hsum.shape[0]
    return jnp.concatenate([_dot(x[:, g * w:(g + 1) * w], hsum) for g in range(x.shape[1] // w)], axis=1)


def _attn_kernel(*refs, tq, cq, has_cache, nb):
    if has_cache:
        (x_ref, g_ref, w_ref, qg_ref, kg_ref, bias_ref, hsum_ref, kc_ref, vc_ref,
         o_ref, kn_ref, v_ref, kwin, vwin, qs) = refs
    else:
        (x_ref, g_ref, w_ref, qg_ref, kg_ref, bias_ref, hsum_ref,
         o_ref, kn_ref, v_ref, kwin, vwin, qs) = refs
    i = pl.program_id(1)
    nkeys = A_WIN + cq
    pair_w = 2 * A_HEAD_DIM

    if has_cache:
        kwin[:, 0:A_WIN, :] = kc_ref[...].astype(BF16)
        vwin[:, 0:A_WIN, :] = vc_ref[...].astype(BF16)
    else:
        @pl.when(i == 0)
        def _():
            kwin[0:A_WIN, :] = jnp.zeros((A_WIN, A_WIDTH), BF16)
            vwin[0:A_WIN, :] = jnp.zeros((A_WIN, A_WIDTH), BF16)

    h = _rms(x_ref[...].reshape(nb * tq, D_MODEL)) * g_ref[...]
    z = _dot(h, w_ref[...])
    q = z[:, 0:A_WIDTH]
    k = z[:, A_WIDTH:2 * A_WIDTH]
    v = z[:, 2 * A_WIDTH:3 * A_WIDTH]
    hsum = hsum_ref[...]
    inv_d = 1.0 / A_HEAD_DIM
    qn = q * lax.rsqrt(_head_sum(q * q, hsum) * inv_d + NORM_EPS) * qg_ref[...]
    kn = k * lax.rsqrt(_head_sum(k * k, hsum) * inv_d + NORM_EPS) * kg_ref[...]
    qs[...] = qn * (A_HEAD_DIM ** -0.5 * LOG2_E)
    if has_cache:
        kwin[:, A_WIN:A_WIN + tq, :] = kn.astype(BF16).reshape(nb, tq, A_WIDTH)
        vwin[:, A_WIN:A_WIN + tq, :] = v.astype(BF16).reshape(nb, tq, A_WIDTH)
    else:
        kwin[A_WIN:A_WIN + tq, :] = kn.astype(BF16)
        vwin[A_WIN:A_WIN + tq, :] = v.astype(BF16)
    kn_ref[...] = kn.reshape(nb, tq, A_WIDTH)
    v_ref[...] = v.reshape(nb, tq, A_WIDTH)

    rho = lax.broadcasted_iota(jnp.int32, (2 * cq, pair_w), 0)
    lane = lax.broadcasted_iota(jnp.int32, (2 * cq, pair_w), 1)
    placed = (rho >> (cq.bit_length() - 1)) == (lane >> (A_HEAD_DIM.bit_length() - 1))

    n_chunks = nb * tq // cq
    per_trip = 4 if n_chunks % 4 == 0 else (2 if n_chunks % 2 == 0 else 1)
    units = [(c, p) for c in range(per_trip) for p in range(A_HEADS // 2)]
    ids = range(len(units))

    def chunk(j, carry, mask_past):
        r0s = [pl.multiple_of((j * per_trip + c) * cq, cq) for c in range(per_trip)]
        lanes = [slice(p * pair_w, (p + 1) * pair_w) for _, p in units]
        qc = [qs[pl.ds(r0s[c], cq), lanes[u]] for u, (c, _) in enumerate(units)]
        qp = [jnp.where(placed, jnp.concatenate([qc[u], qc[u]], axis=0), 0.0) for u in ids]
        if has_cache:
            win = [(j * per_trip + c, slice(None), lanes[u]) for u, (c, _) in enumerate(units)]
        else:
            win = [(pl.ds(r0s[c], nkeys), lanes[u]) for u, (c, _) in enumerate(units)]
        s = [_dot(qp[u], kwin[win[u]], NT) + bias_ref[p] for u, (_, p) in enumerate(units)]
        if mask_past:
            col = lax.broadcasted_iota(jnp.int32, (2 * cq, nkeys), 1)
            s = [jnp.where(r0s[c] + col >= A_WIN, s[u], -1e30) for u, (c, _) in enumerate(units)]
        m = [jnp.max(s[u], axis=-1, keepdims=True) for u in ids]
        e = [jnp.exp2(s[u] - m[u]) for u in ids]
        den = [jnp.sum(e[u], axis=-1, keepdims=True) for u in ids]
        o2 = [_dot(e[u], vwin[win[u]]) for u in ids]
        for u, (c, _) in enumerate(units):
            o = jnp.where(placed, o2[u] / den[u], 0.0)
            if has_cache:
                o_ref[j * per_trip + c, :, lanes[u]] = o[:cq] + o[cq:]
            else:
                o_ref[0, pl.ds(r0s[c], cq), lanes[u]] = o[:cq] + o[cq:]
        return carry

    n_trips = n_chunks // per_trip
    if has_cache:
        lax.fori_loop(0, n_trips, functools.partial(chunk, mask_past=False), 0)
    else:
        @pl.when(i == 0)
        def _():
            lax.fori_loop(0, n_trips, functools.partial(chunk, mask_past=True), 0)

        @pl.when(i > 0)
        def _():
            lax.fori_loop(0, n_trips, functools.partial(chunk, mask_past=False), 0)

    if not has_cache:
        kwin[0:A_WIN, :] = kwin[tq:tq + A_WIN, :]
        vwin[0:A_WIN, :] = vwin[tq:tq + A_WIN, :]


def _attention(x, gain, w_a, q_gain, k_gain, bias, cache_k, cache_v, layer, *, tq, cq, nb):
    bsz, seq, _ = x.shape
    has_cache = cache_k is not None
    assert seq % tq == 0 and tq % cq == 0
    assert (seq == tq == cq <= A_WIN and bsz % nb == 0) if has_cache else (tq == A_WIN and nb == 1)
    nkeys = A_WIN + cq
    tok = lambda w: pl.BlockSpec((nb, tq, w), lambda b, i: (b, i, 0))
    win_shape = (nb, nkeys, A_WIDTH) if has_cache else (A_WIN + tq, A_WIDTH)
    in_specs = [tok(D_MODEL), _const_spec((1, D_MODEL)), w_a[1],
                _const_spec((1, A_WIDTH)), _const_spec((1, A_WIDTH)),
                _const_spec((A_HEADS // 2, 2 * cq, nkeys)), _const_spec((MXU_TILE, MXU_TILE))]
    args = [x, gain, w_a[0], jnp.tile(q_gain, (1, A_HEADS)), jnp.tile(k_gain, (1, A_HEADS)),
            bias.reshape(A_HEADS // 2, 2 * cq, nkeys), _head_sum_matrix(A_HEAD_DIM)]
    if has_cache:
        cspec = pl.BlockSpec((None, nb, A_WIN, A_WIDTH), lambda b, i: (layer, b, 0, 0))
        in_specs += [cspec, cspec]
        args += [cache_k, cache_v]
    out = jax.ShapeDtypeStruct((bsz, seq, A_WIDTH), F32)
    return pl.pallas_call(
        functools.partial(_attn_kernel, tq=tq, cq=cq, has_cache=has_cache, nb=nb),
        grid=(bsz // nb, seq // tq),
        in_specs=in_specs,
        out_specs=[tok(A_WIDTH)] + [pl.BlockSpec((nb, tq, A_WIDTH), lambda b, i: (b, 0, 0))] * 2,
        out_shape=[out] + [jax.ShapeDtypeStruct((bsz, tq, A_WIDTH), F32)] * 2,
        scratch_shapes=[pltpu.VMEM(win_shape, BF16),
                        pltpu.VMEM(win_shape, BF16),
                        pltpu.VMEM((nb * tq, A_WIDTH), F32)],
        compiler_params=pltpu.CompilerParams(
            dimension_semantics=("arbitrary", "arbitrary"),
            vmem_limit_bytes=VMEM_LIMIT_BYTES),
        name="attention",
    )(*args)


def _ret_kernel(x_ref, g_ref, w_ref, cosa_ref, sina_ref, cosb_ref, sinb_ref, cosbs_ref, sinbs_ref,
                s0_ref, o_ref, sout_ref, s_scr, dmat_s, *, tb, nb, nsub):
    batched = nb > 1
    i = pl.program_id(1)

    if not batched:
        @pl.when(i == 0)
        def _():
            s_scr[...] = s0_ref[0]

    @pl.when(jnp.logical_and(pl.program_id(0) == 0, i == 0))
    def _():
        row = lax.broadcasted_iota(jnp.int32, (tb, tb), 0)
        col = lax.broadcasted_iota(jnp.int32, (tb, tb), 1)
        diff = row - col
        dist = jnp.maximum(diff, 0).astype(F32)
        for hh in range(B_HEADS):
            dmat_s[hh] = jnp.where(diff >= 0, jnp.exp(RET_LOG_GAMMA[hh] * dist), 0.0)

    rows = nb * nsub * tb
    h = _rms(x_ref[...].reshape(rows, D_MODEL)) * g_ref[...]
    z = _dot(h, w_ref[...])
    cos_a, sin_a = cosa_ref[0], sina_ref[0]
    cos_all = cos_a * cosb_ref[...] - sin_a * sinb_ref[...]
    sin_all = sin_a * cosbs_ref[...] + cos_a * sinbs_ref[...]
    n = lax.broadcasted_iota(jnp.int32, (tb, 1), 0).astype(F32)
    for hh in range(B_HEADS):
        lg = RET_LOG_GAMMA[hh]
        lo = hh * B_HEAD_DIM
        dmat = dmat_s[hh]
        dec_q = jnp.exp(lg * (n + 1.0))
        dec_k = jnp.exp(lg * ((tb - 1.0) - n))
        for bb in range(nb):
            state = s0_ref[bb, hh] if batched else s_scr[hh]
            for sub in range(nsub):
                rs = slice((bb * nsub + sub) * tb, (bb * nsub + sub + 1) * tb)
                pos = slice(sub * tb, (sub + 1) * tb)
                cosf = cos_all[pos, :]
                sinf = sin_all[pos, :]
                q = z[rs, lo:lo + B_HEAD_DIM]
                k = z[rs, B_WIDTH + lo:B_WIDTH + lo + B_HEAD_DIM]
                v = z[rs, 2 * B_WIDTH + lo:2 * B_WIDTH + lo + B_HEAD_DIM]
                gate = z[rs, 3 * B_WIDTH + lo:3 * B_WIDTH + lo + B_HEAD_DIM]
                q = q * cosf + pltpu.roll(q, B_HEAD_DIM // 2, 1) * sinf
                k = (k * cosf + pltpu.roll(k, B_HEAD_DIM // 2, 1) * sinf) * (B_HEAD_DIM ** -0.5)
                scores = _dot(q, k, NT) * dmat
                o = _dot(scores, v) + _dot(q, state) * dec_q
                state = math.exp(lg * tb) * state + _dot(k * dec_k, v, TN)
                o_ref[bb, pos, lo:lo + B_HEAD_DIM] = _rms(o) * (gate * _sigmoid(gate))
            if batched:
                sout_ref[bb, hh] = state
            else:
                s_scr[hh] = state
    if not batched:
        sout_ref[0] = s_scr[...]


def _retention(x, gain, w_b, pos0, s0, *, tb, nb, nsub):
    bsz, seq, _ = x.shape
    assert (seq == tb and bsz % nb == 0 and nsub == 1) if nb > 1 else seq % (nsub * tb) == 0
    sspec = pl.BlockSpec((nb, B_HEADS, B_HEAD_DIM, B_HEAD_DIM), lambda b, i: (b, 0, 0, 0))
    tok = lambda w: pl.BlockSpec((nb, nsub * tb, w), lambda b, i: (b, i, 0))
    step_rows = nsub * tb
    half = B_HEAD_DIM // 2
    inv = ROPE_BASE ** (-jnp.arange(half, dtype=F32) / half)
    base = (pos0 + step_rows * jnp.arange(seq // step_rows, dtype=jnp.int32)).astype(F32)[:, None] * inv[None, :]
    offs = jnp.arange(step_rows, dtype=jnp.int32).astype(F32)[:, None] * inv[None, :]
    both = lambda t: jnp.concatenate([t, t], axis=-1)
    signed = lambda t: jnp.concatenate([-t, t], axis=-1)
    step_row = lambda t: both(t).reshape(-1, 1, B_HEAD_DIM)
    a_spec = pl.BlockSpec((1, 1, B_HEAD_DIM), lambda b, i: (i, 0, 0))
    b_spec = _const_spec((step_rows, B_HEAD_DIM))
    return pl.pallas_call(
        functools.partial(_ret_kernel, tb=tb, nb=nb, nsub=nsub),
        grid=(bsz // nb, seq // (nsub * tb)),
        in_specs=[tok(D_MODEL),
                  _const_spec((1, D_MODEL)), w_b[1],
                  a_spec, a_spec, b_spec, b_spec, b_spec, b_spec,
                  sspec],
        out_specs=[tok(B_WIDTH), sspec],
        out_shape=[jax.ShapeDtypeStruct((bsz, seq, B_WIDTH), F32),
                   jax.ShapeDtypeStruct((bsz, B_HEADS, B_HEAD_DIM, B_HEAD_DIM), F32)],
        scratch_shapes=[pltpu.VMEM((B_HEADS, B_HEAD_DIM, B_HEAD_DIM), F32),
                        pltpu.VMEM((B_HEADS, tb, tb), F32)],
        compiler_params=pltpu.CompilerParams(
            dimension_semantics=("arbitrary", "arbitrary"),
            vmem_limit_bytes=VMEM_LIMIT_BYTES),
        name="retention",
    )(x, gain, w_b[0], step_row(jnp.cos(base)), step_row(jnp.sin(base)),
      both(jnp.cos(offs)), both(jnp.sin(offs)), signed(jnp.cos(offs)), signed(jnp.sin(offs)), s0)


PAIR_W = 2 * C_HEAD_DIM
N_PAIRS = C_HEADS // 2


def _rwkv_kernel(*refs, tt, cc, lookahead, batched):
    if lookahead:
        x_ref, xn_ref = refs[:2]
        refs = refs[2:]
    else:
        x_ref, xn_ref = refs[0], None
        refs = refs[1:]
    (g_ref, w_ref, mu_ref, sh0_ref, s0_ref, w0_ref, w2_ref, a0_ref, a2_ref,
     g2_ref, kk_ref, ka_ref, rk_ref, lnw_ref, lnb_ref, hsum_ref,
     o_ref, sout_ref, shout_ref,
     s_scr, carry, rk_raw, uw_raw, ua_raw, rt_s, kkt_s, kh_s, bh_s, v_s, bon_s, g_s, cum_s) = refs
    i = pl.program_id(1)

    n_chunks = tt // cc

    def project(x, slot):
        h = _rms(x.reshape(tt, D_MODEL)) * g_ref[...]
        cz = _dot(h, w_ref[...])
        rows = lax.broadcasted_iota(jnp.int32, (tt, 1), 0)
        if batched:
            first = jnp.bitwise_and(rows, cc - 1) == 0
            before = jnp.broadcast_to(sh0_ref[...], (n_chunks, cc, C_SHIFT_WIDTH)).reshape(tt, C_SHIFT_WIDTH)
            shout_ref[...] = cz.reshape(n_chunks, cc, C_SHIFT_WIDTH)[:, cc - 1:cc, :]
        else:
            first = rows == 0
            before = carry[...]
            last = cz[tt - 1:tt, :]
            carry[...] = last
            shout_ref[0] = last
        prev = jnp.where(first, before, pltpu.roll(cz, 1, 0))
        cs = cz + (prev - cz) * mu_ref[...]
        off = 3 * C_WIDTH
        w_lo = cs[:, off:off + C_RANK_W]
        a_lo = cs[:, off + C_RANK_W:off + C_RANK_W + C_RANK_A]
        g_lo = cs[:, off + C_RANK_W + C_RANK_A:C_SHIFT_WIDTH]
        rk_raw[...] = cs[:, 0:2 * C_WIDTH]
        v_s[slot] = cs[:, 2 * C_WIDTH:3 * C_WIDTH]
        uw_raw[...] = w0_ref[...] + _dot(jnp.tanh(w_lo), w2_ref[...])
        ua_raw[...] = a0_ref[...] + _dot(a_lo, a2_ref[...])
        g_s[slot] = _dot(_sigmoid(g_lo), g2_ref[...])

    def token_block(b, slot):
        rs = slice(b * cc, (b + 1) * cc)
        r = rk_raw[rs, 0:C_WIDTH]
        k = rk_raw[rs, C_WIDTH:2 * C_WIDTH]
        v = v_s[slot, rs, :]
        lw = -math.exp(-0.5) * _sigmoid(uw_raw[rs, :])
        a = _sigmoid(ua_raw[rs, :])
        hsum = hsum_ref[...]
        kk_raw = k * kk_ref[...]
        kk = kk_raw / jnp.maximum(jnp.sqrt(_head_sum(kk_raw * kk_raw, hsum)), 1e-12)
        k2 = k * (1.0 + (a - 1.0) * ka_ref[...])
        pos = lax.broadcasted_iota(jnp.int32, (cc, 1), 0)
        cum = lw
        step = 1
        while step < cc:
            cum = cum + jnp.where(pos >= step, pltpu.roll(cum, step, 0), 0.0)
            step *= 2
        e_inv = jnp.exp(-cum)
        rt_s[slot, rs, :] = r * jnp.exp(cum)
        kkt_s[slot, rs, :] = kk * jnp.exp(cum - lw)
        kh_s[slot, rs, :] = k2 * e_inv
        bh_s[slot, rs, :] = kk * a * e_inv
        bon_s[slot, rs, :] = _head_sum(r * k2 * rk_ref[...], hsum) * v
        cum_s[slot, rs, :] = cum

    if not batched:
        @pl.when(i == 0)
        def _():
            s_scr[...] = s0_ref[0]
            carry[...] = sh0_ref[0]
            if lookahead:
                project(x_ref[...], 0)
                for b in range(n_chunks):
                    token_block(b, 0)

    if lookahead:
        cur = lax.rem(i, 2)
        nxt = 1 - cur
        project(xn_ref[...], nxt)
        pending = [functools.partial(token_block, b, nxt) for b in range(n_chunks)]
    else:
        cur = 0
        project(x_ref[...], 0)
        for b in range(n_chunks):
            token_block(b, 0)
        pending = []

    def interleave():
        if pending:
            pending.pop(0)()

    two = 2 * cc
    sh = cc.bit_length() - 1
    hd = C_HEAD_DIM.bit_length() - 1
    n_double = sh - 1
    rho = lax.broadcasted_iota(jnp.int32, (two, PAIR_W), 0)
    lane = lax.broadcasted_iota(jnp.int32, (two, PAIR_W), 1)
    placed = (rho >> sh) == (lane >> hd)
    r2 = lax.broadcasted_iota(jnp.int32, (two, two), 0)
    c2 = lax.broadcasted_iota(jnp.int32, (two, two), 1)
    same = (r2 >> sh) == (c2 >> sh)
    strict = jnp.logical_and(same, r2 > c2)
    incl = jnp.logical_and(same, r2 >= c2)
    eye = jnp.where(r2 == c2, 1.0, 0.0).astype(F32)
    gr = lax.broadcasted_iota(jnp.int32, (PAIR_W, PAIR_W), 0)
    gc = lax.broadcasted_iota(jnp.int32, (PAIR_W, PAIR_W), 1)
    hmean = jnp.where((gr >> hd) == (gc >> hd), 1.0 / C_HEAD_DIM, 0.0).astype(BF16)

    def place(xv):
        return jnp.where(placed, jnp.concatenate([xv, xv], axis=0), 0.0)

    per_trip = 4 if n_chunks % 4 == 0 else (2 if n_chunks % 2 == 0 else 1)
    units = [(c, p) for c in range(per_trip) for p in range(N_PAIRS)]
    pairs = range(len(units))

    def chunk(j):
        r0s = [(j * per_trip + c) * cc for c in range(per_trip)]
        w_end = [jnp.exp(cum_s[cur, r0 + cc - 1:r0 + cc, :]) for r0 in r0s]
        lanes = [slice(p * PAIR_W, (p + 1) * PAIR_W) for _, p in units]
        sl = [(cur, slice(r0s[c], r0s[c] + cc), lanes[u]) for u, (c, _) in enumerate(units)]
        we = [w_end[c][:, lanes[u]] for u, (c, _) in enumerate(units)]
        rt = [place(rt_s[sl[p]]) for p in pairs]
        kkt = [place(kkt_s[sl[p]]) for p in pairs]
        kh = [place(kh_s[sl[p]]) for p in pairs]
        bh = [place(bh_s[sl[p]]) for p in pairs]
        vv = [place(v_s[sl[p]]) for p in pairs]
        cat0 = lambda *xs: jnp.concatenate(xs, axis=0)
        cat1 = lambda *xs: jnp.concatenate(xs, axis=1)
        gg = [_dot(cat0(kkt[p], rt[p]), cat0(kh[p], bh[p]), NT) for p in pairs]
        interleave()
        a_kk = [jnp.where(strict, gg[p][:two, :two], 0.0) for p in pairs]
        b_kk = [jnp.where(incl, gg[p][two:, :two], 0.0) for p in pairs]
        b_bb = [jnp.where(incl, gg[p][two:, two:], 0.0) for p in pairs]
        pw = [jnp.where(strict, -gg[p][:two, two:], 0.0) for p in pairs]
        tinv = [eye + pw[p] for p in pairs]
        pw = [_dot(pw[p], pw[p]) for p in pairs]
        for _ in range(n_double - 1):
            pt = [_dot(pw[p], cat1(pw[p], tinv[p])) for p in pairs]
            pw = [pt[p][:, :two] for p in pairs]
            tinv = [tinv[p] + pt[p][:, two:] for p in pairs]
            interleave()
        tinv = [tinv[p] + _dot(pw[p], tinv[p]) for p in pairs]
        interleave()
        av = [_dot(a_kk[p], vv[p]) for p in pairs]
        kv = [_dot(tinv[p], cat1(kkt[p], av[p])) for p in pairs]
        zero = jnp.zeros((two, PAIR_W), F32)
        ry = [_dot(cat1(b_kk[p], -b_bb[p]), cat0(cat1(zero, vv[p]), kv[p])) for p in pairs]
        rp = [rt[p] + ry[p][:, :PAIR_W] for p in pairs]
        y0 = [ry[p][:, PAIR_W:] for p in pairs]
        kkp = [kv[p][:, :PAIR_W] for p in pairs]
        vp = [kv[p][:, PAIR_W:] for p in pairs]
        bd = [bh[p] * we[p] for p in pairs]
        kd = [kh[p] * we[p] for p in pairs]
        q = [_dot(kkp[p], bd[p], TN) for p in pairs]
        z = [_dot(cat0(vv[p], vp[p]), cat0(kd[p], -bd[p]), TN) for p in pairs]
        y2 = []

        def advance(u, s_old):
            y2.append(_dot(rp[u], s_old, NT) + y0[u])
            return s_old * we[u] - _dot(s_old, q[u]) + z[u]

        if batched:
            for u, (c, p) in enumerate(units):
                sout_ref[j * per_trip + c, p] = advance(u, s0_ref[j * per_trip + c, p])
        else:
            state = [s_scr[p] for p in range(N_PAIRS)]
            for u, (_, p) in enumerate(units):
                state[p] = advance(u, state[p])
            for p in range(N_PAIRS):
                s_scr[p] = state[p]
        ys = [y2[u][:cc] + y2[u][cc:] for u in pairs]
        yc = [ys[u] - _dot(ys[u], hmean) for u in pairs]
        var = [_dot(yc[u] * yc[u], hmean) for u in pairs]
        for u, (c, _) in enumerate(units):
            yn = yc[u] * lax.rsqrt(var[u] + C_GN_EPS) * lnw_ref[:, lanes[u]] + lnb_ref[:, lanes[u]]
            out = (yn + bon_s[sl[u]]) * g_s[sl[u]]
            if batched:
                o_ref[j * per_trip + c, :, lanes[u]] = out
            else:
                o_ref[(0,) + sl[u][1:]] = out

    for j in range(n_chunks // per_trip):
        chunk(j)
    while pending:
        interleave()
    if not batched:
        sout_ref[0] = s_scr[...]


def _pair_states(s):
    bsz = s.shape[0]
    s = s.reshape(bsz, N_PAIRS, 2, C_HEAD_DIM, C_HEAD_DIM)
    zero = jnp.zeros_like(s[:, :, 0])
    top = jnp.concatenate([s[:, :, 0], zero], axis=-1)
    bot = jnp.concatenate([zero, s[:, :, 1]], axis=-1)
    return jnp.concatenate([top, bot], axis=-2)


def _unpair_states(s):
    bsz = s.shape[0]
    d = C_HEAD_DIM
    return jnp.stack([s[:, :, :d, :d], s[:, :, d:, d:]], axis=2).reshape(bsz, C_HEADS, d, d)


def _rwkv(x, gain, w_c, lw, shift0, s0, *, tt, cc, nb):
    bsz, seq, _ = x.shape
    batched = nb > 1
    assert cc & (cc - 1) == 0
    assert (seq == cc and tt == nb * cc and bsz % nb == 0) if batched else (seq % tt == 0 and tt % cc == 0)
    row = lambda a: a.reshape(1, -1)
    sspec = pl.BlockSpec((nb, N_PAIRS, PAIR_W, PAIR_W), lambda b, i: (b, 0, 0, 0))
    shspec = pl.BlockSpec((nb, 1, C_SHIFT_WIDTH), lambda b, i: (b, 0, 0))
    vec = _const_spec((1, C_WIDTH))
    n_tiles = 1 if batched else seq // tt
    lookahead = n_tiles > 1
    tile = pltpu.VMEM((2 if lookahead else 1, tt, C_WIDTH), F32)
    hsum = _head_sum_matrix(C_HEAD_DIM)
    tok = lambda w: pl.BlockSpec((nb, tt // nb, w), lambda b, i: (b, i, 0))
    x_specs = [tok(D_MODEL)]
    x_args = [x]
    if lookahead:
        x_specs = [pl.BlockSpec((1, tt, D_MODEL), lambda b, i: (b, 0, 0)),
                   pl.BlockSpec((1, tt, D_MODEL), lambda b, i: (b, jnp.minimum(i + 1, n_tiles - 1), 0))]
        x_args = [x, x]
    oc, s_new, shift_new = pl.pallas_call(
        functools.partial(_rwkv_kernel, tt=tt, cc=cc, lookahead=lookahead, batched=batched),
        grid=(bsz // nb, n_tiles),
        in_specs=x_specs + [
                  _const_spec((1, D_MODEL)), w_c[1],
                  _const_spec((1, C_SHIFT_WIDTH)), shspec, sspec,
                  vec, _const_spec((C_RANK_W, C_WIDTH)), vec, _const_spec((C_RANK_A, C_WIDTH)),
                  _const_spec((C_RANK_G, C_WIDTH)), vec, vec, vec, vec, vec,
                  _const_spec((MXU_TILE, MXU_TILE))],
        out_specs=[tok(C_WIDTH), sspec, shspec],
        out_shape=[jax.ShapeDtypeStruct((bsz, seq, C_WIDTH), F32),
                   jax.ShapeDtypeStruct((bsz, N_PAIRS, PAIR_W, PAIR_W), F32),
                   jax.ShapeDtypeStruct((bsz, 1, C_SHIFT_WIDTH), F32)],
        scratch_shapes=[pltpu.VMEM((N_PAIRS, PAIR_W, PAIR_W), F32),
                        pltpu.VMEM((1, C_SHIFT_WIDTH), F32),
                        pltpu.VMEM((tt, 2 * C_WIDTH), F32),
                        pltpu.VMEM((tt, C_WIDTH), F32),
                        pltpu.VMEM((tt, C_WIDTH), F32)] + [tile] * 8,
        compiler_params=pltpu.CompilerParams(
            dimension_semantics=("arbitrary", "arbitrary"),
            vmem_limit_bytes=VMEM_LIMIT_BYTES),
        name="rwkv",
    )(*x_args, gain, w_c[0], row(lw["c_shift_mu"]), shift0, _pair_states(s0),
      row(lw["c_w0"]), lw["c_w2"].astype(BF16), row(lw["c_a0"]), lw["c_a2"].astype(BF16),
      lw["c_g2"].astype(BF16), row(lw["c_k_k"]), row(lw["c_k_a"]), row(lw["c_r_k"]),
      row(lw["c_ln_w"]), row(lw["c_ln_b"]), hsum)
    return oc, _unpair_states(s_new), shift_new


def _merge_kernel(x_ref, oa_ref, ob_ref, oc_ref, g_ref, wg_ref, wb_ref, wo_ref, y_ref):
    x = x_ref[...]
    h = _rms(x) * g_ref[...]
    gl = _dot(h, wg_ref[...])
    m = None
    for b, o_ref in enumerate((oa_ref, ob_ref, oc_ref)):
        t = _sigmoid(gl[:, b * D_MODEL:(b + 1) * D_MODEL]) * _dot(o_ref[...], wb_ref[b])
        m = t if m is None else m + t
    y_ref[...] = x + _dot(m, wo_ref[...])


def _merge(x, oa, ob, oc, gain, w_g, w_b, w_o, *, tm):
    rows = x.shape[0]
    assert rows % tm == 0
    tok = lambda w: pl.BlockSpec((tm, w), lambda i: (i, 0))
    return pl.pallas_call(
        _merge_kernel,
        grid=(rows // tm,),
        in_specs=[tok(D_MODEL), tok(A_WIDTH), tok(B_WIDTH), tok(C_WIDTH),
                  _const_spec((1, D_MODEL)), w_g[1], w_b[1], w_o[1]],
        out_specs=tok(D_MODEL),
        out_shape=jax.ShapeDtypeStruct((rows, D_MODEL), F32),
        compiler_params=pltpu.CompilerParams(
            dimension_semantics=("arbitrary",), vmem_limit_bytes=VMEM_LIMIT_BYTES),
        name="merge",
    )(x, oa, ob, oc, gain, w_g[0], w_b[0], w_o[0])


def _ffn_kernel(x_ref, p_ref, g_ref, wgate_ref, wup_ref, wdown_ref, wpp_ref, pg_ref, wpg_ref, y_ref):
    x = x_ref[...]
    hf = (_rms(x) * g_ref[...]).astype(BF16)
    d_ff = wgate_ref.shape[1]
    for c in range(d_ff // MXU_TILE):
        cols = slice(c * MXU_TILE, (c + 1) * MXU_TILE)
        gate = _dot(hf, wgate_ref[:, cols])
        up = _dot(hf, wup_ref[:, cols])
        x = x + _dot(gate * _sigmoid(gate) * up, wdown_ref[cols, :])
    e = _rms(_dot(p_ref[...], wpp_ref[...])) * pg_ref[...]
    y_ref[...] = x + _sigmoid(_dot(_rms(x), wpg_ref[...])) * e


def _ffn(x, p_all, layer, gain, w_gate, w_up, w_down, w_pp, p_gain, w_pg, *, tm):
    rows = x.shape[0]
    ple = p_all.shape[-1]
    assert w_gate[0].shape[-1] % MXU_TILE == 0
    p = p_all.reshape(p_all.shape[0], rows, ple)
    assert rows % tm == 0
    tok = lambda w: pl.BlockSpec((tm, w), lambda i: (i, 0))
    return pl.pallas_call(
        _ffn_kernel,
        grid=(rows // tm,),
        in_specs=[tok(D_MODEL), pl.BlockSpec((None, tm, ple), lambda i: (layer, i, 0)),
                  _const_spec((1, D_MODEL)),
                  w_gate[1], w_up[1], w_down[1], w_pp[1],
                  _const_spec((1, D_MODEL)), w_pg[1]],
        out_specs=tok(D_MODEL),
        out_shape=jax.ShapeDtypeStruct((rows, D_MODEL), F32),
        compiler_params=pltpu.CompilerParams(
            dimension_semantics=("arbitrary",), vmem_limit_bytes=VMEM_LIMIT_BYTES),
        name="ffn",
    )(x, p, gain, w_gate[0], w_up[0], w_down[0], w_pp[0], p_gain, w_pg[0])


def _rel_bias_table(rel_bias, cq):
    nkeys = A_WIN + cq
    t_max = A_WIN + CHUNK - 1 + cq - 1
    heads, n_rel = rel_bias.shape
    w = t_max + 1
    tail = jnp.broadcast_to(rel_bias[:, n_rel - 1:], (heads, w - n_rel))
    rev = jnp.concatenate([tail, rel_bias[:, ::-1].astype(F32), jnp.zeros((heads, 1), F32)], axis=1)
    skew = jnp.tile(rev, (1, cq))[:, :cq * w].reshape(heads, cq, w)
    return skew[:, :, cq - 1:cq - 1 + nkeys] * LOG2_E


def _layer(x, p_all, layer, pos0, a_ck, a_cv, ret_s0, rwkv_s0, shift_prev, lw, big, cfg):
    bsz, seq, _ = x.shape
    row = lambda a: a.reshape(1, -1)
    w_a = _layer_weight(big["w_in"], layer, (OFF_A, OFF_B - OFF_A))
    w_b = _layer_weight(big["w_in"], layer, (OFF_B, OFF_C - OFF_B))
    w_c = _layer_weight(big["w_in"], layer, (OFF_C, OFF_G - OFF_C))
    w_g = _layer_weight(big["w_in"], layer, (OFF_G, IN_WIDTH - OFF_G))
    gain = row(lw["norm_mix"])

    nb = math.gcd(bsz, cfg["nb"])
    bias = _rel_bias_table(lw["a_rel_bias"], cfg["cq"])
    if a_ck is not None:
        a_ck = a_ck.astype(BF16).reshape(a_ck.shape[0], bsz, A_WIN, A_WIDTH)
        a_cv = a_cv.astype(BF16).reshape(a_cv.shape[0], bsz, A_WIN, A_WIDTH)
    oa, kn, av = _attention(x, gain, w_a, row(lw["a_q_norm"]), row(lw["a_k_norm"]), bias,
                            a_ck, a_cv, layer, tq=cfg["tq"], cq=cfg["cq"], nb=nb)
    keep = min(A_WIN, seq)
    new_ak = kn.reshape(bsz, keep, A_HEADS, A_HEAD_DIM)
    new_av = av.reshape(bsz, keep, A_HEADS, A_HEAD_DIM)

    ob, new_ret = _retention(x, gain, w_b, pos0, ret_s0, tb=cfg["tb"], nb=nb, nsub=cfg["nsub"])

    oc, new_rwkv, new_shift = _rwkv(x, gain, w_c, lw, shift_prev, rwkv_s0,
                                    tt=nb * cfg["tt"], cc=cfg["cc"], nb=nb)

    rows = bsz * seq
    tm = min(cfg["tm"], rows)
    flat = lambda t: t.reshape(rows, t.shape[-1])
    pick = lambda name: _layer_weight(big[name], layer)
    x1 = _merge(flat(x), flat(oa), flat(ob), flat(oc), gain, w_g,
                pick("w_branch"), pick("w_out"), tm=tm)
    x2 = _ffn(x1, p_all, layer, row(lw["norm_ffn"]), pick("w_ffn_gate"), pick("w_ffn_up"),
              pick("w_ffn_down"), pick("w_ple_proj"), row(lw["ple_norm"]), pick("w_ple_gate"),
              tm=min(cfg["tm_ffn"], rows))
    return x2.reshape(bsz, seq, D_MODEL), (new_ak, new_av, new_ret, new_rwkv, new_shift)


def _group_config(seq):
    if seq >= A_WIN:
        return dict(tq=A_WIN, cq=CHUNK, tb=256, nsub=2, tt=512, cc=CHUNK, tm=512, tm_ffn=1024, nb=1)
    return dict(tq=seq, cq=seq, tb=seq, nsub=1, tt=seq, cc=seq, tm=256, tm_ffn=512, nb=8)


def kernel(x_prompt, x_sample, p_prompt, p_sample, cache_a_k, cache_a_v, state_ret, state_rwkv, state_rwkv_shift, norm_mix, w_in, a_q_norm, a_k_norm, a_rel_bias, c_shift_mu, c_w0, c_w2, c_a0, c_a2, c_g2, c_k_k, c_k_a, c_r_k, c_ln_w, c_ln_b, w_branch, w_out, norm_ffn, w_ffn_gate, w_ffn_up, w_ffn_down, w_ple_proj, ple_norm, w_ple_gate):
    depth = w_in.shape[0]

    def layer_weights(i):
        return dict(norm_mix=norm_mix[i], a_q_norm=a_q_norm[i], a_k_norm=a_k_norm[i],
                    a_rel_bias=a_rel_bias[i], c_shift_mu=c_shift_mu[i], c_w0=c_w0[i], c_w2=c_w2[i],
                    c_a0=c_a0[i], c_a2=c_a2[i], c_g2=c_g2[i], c_k_k=c_k_k[i], c_k_a=c_k_a[i], c_r_k=c_r_k[i],
                    c_ln_w=c_ln_w[i], c_ln_b=c_ln_b[i], norm_ffn=norm_ffn[i], ple_norm=ple_norm[i])

    big = dict(w_in=w_in, w_branch=w_branch, w_out=w_out, w_ffn_gate=w_ffn_gate, w_ffn_up=w_ffn_up,
               w_ffn_down=w_ffn_down, w_ple_proj=w_ple_proj, w_ple_gate=w_ple_gate)
    big = {name: w.astype(BF16) for name, w in big.items()}

    bp, lp, _ = x_prompt.shape
    cfg_p = _group_config(lp)
    ret0 = jnp.zeros((bp, B_HEADS, B_HEAD_DIM, B_HEAD_DIM), F32)
    rwkv0 = jnp.zeros((bp, C_HEADS, C_HEAD_DIM, C_HEAD_DIM), F32)
    shift0 = jnp.zeros((bp, 1, C_SHIFT_WIDTH), F32)
    y_prompt = x_prompt
    st_p = []
    for i in range(depth):
        y_prompt, st = _layer(y_prompt, p_prompt, i, 0, None, None, ret0, rwkv0, shift0,
                              layer_weights(i), big, cfg_p)
        st_p.append(st)

    cfg_s = _group_config(x_sample.shape[1])
    y_sample = x_sample
    st_s = []
    for i in range(depth):
        y_sample, st = _layer(y_sample, p_sample, i, PAST_LEN, cache_a_k, cache_a_v, state_ret[i],
                              state_rwkv[i], state_rwkv_shift[i], layer_weights(i), big, cfg_s)
        st_s.append(st)

    stack = lambda sts, j: jnp.stack([s[j] for s in sts])
    return (y_prompt, y_sample,
            stack(st_p, 0), stack(st_p, 1), stack(st_p, 2), stack(st_p, 3), stack(st_p, 4),
            stack(st_s, 0), stack(st_s, 1), stack(st_s, 2), stack(st_s, 3), stack(st_s, 4))
```

```python
import functools
import math

import jax
import jax.numpy as jnp
from jax import lax
from jax.experimental import pallas as pl
from jax.experimental.pallas import tpu as pltpu

F32 = jnp.float32
BF16 = jnp.bfloat16

D_MODEL = 1024
PAST_LEN = 2048
CHUNK = 64
NORM_EPS = 1e-6

A_HEADS = 8
A_HEAD_DIM = 64
A_WIDTH = 512
A_WIN = 512
A_REL_MAX = 256

B_HEADS = 4
B_HEAD_DIM = 128
B_WIDTH = 512
ROPE_BASE = 10000.0

C_HEADS = 8
C_HEAD_DIM = 64
C_WIDTH = 512
C_RANK_W = 64
C_RANK_A = 64
C_RANK_G = 128
C_SHIFT_WIDTH = 3 * C_WIDTH + C_RANK_W + C_RANK_A + C_RANK_G
C_GN_EPS = 64e-5

N_BRANCHES = 3

OFF_A = 0
OFF_B = 3 * A_WIDTH
OFF_C = OFF_B + 4 * B_WIDTH
OFF_G = OFF_C + C_SHIFT_WIDTH
IN_WIDTH = OFF_G + N_BRANCHES * D_MODEL

RET_LOG_GAMMA = tuple(math.log1p(-(2.0 ** (-5.0 - h))) for h in range(B_HEADS))

VMEM_LIMIT_BYTES = 56 * 1024 * 1024
MXU_TILE = 256
LOG2_E = math.log2(math.e)

NN = (((1,), (0,)), ((), ()))
NT = (((1,), (1,)), ((), ()))
TN = (((0,), (0,)), ((), ()))


def _dot(a, b, dims=NN):
    return lax.dot_general(a.astype(BF16), b.astype(BF16), dims, preferred_element_type=F32)


def _rms(x):
    return x * lax.rsqrt(jnp.mean(x * x, axis=-1, keepdims=True) + NORM_EPS)


def _sigmoid(x):
    return 1.0 / (1.0 + jnp.exp(-x))


def _const_spec(shape):
    nd = len(shape)
    return pl.BlockSpec(shape, lambda *_: (0,) * nd, pipeline_mode=pl.Buffered(1))


def _layer_weight(stacked, layer, cols=None):
    if cols is not None:
        off, width = cols
        if off % width == 0:
            spec = pl.BlockSpec((None, stacked.shape[1], width), lambda *_: (layer, 0, off // width),
                                pipeline_mode=pl.Buffered(1))
            return stacked, spec
        stacked = stacked[:, :, off:off + width]
    shape = stacked.shape[1:]
    spec = pl.BlockSpec((None,) + shape, lambda *_: (layer,) + (0,) * len(shape),
                        pipeline_mode=pl.Buffered(1))
    return stacked, spec


def _head_sum_matrix(head_dim):
    head = jnp.arange(MXU_TILE, dtype=jnp.int32) // head_dim
    return (head[:, None] == head[None, :]).astype(BF16)


def _head_sum(x, hsum):
    w = hsum.shape[0]
    return jnp.concatenate([_dot(x[:, g * w:(g + 1) * w], hsum) for g in range(x.shape[1] // w)], axis=1)


def _attn_kernel(*refs, tq, cq, has_cache, nb):
    if has_cache:
        (x_ref, g_ref, w_ref, qg_ref, kg_ref, bias_ref, hsum_ref, kc_ref, vc_ref,
         o_ref, kn_ref, v_ref, kwin, vwin, qs) = refs
    else:
        (x_ref, g_ref, w_ref, qg_ref, kg_ref, bias_ref, hsum_ref,
         o_ref, kn_ref, v_ref, kwin, vwin, qs) = refs
    i = pl.program_id(1)
    nkeys = A_WIN + cq
    pair_w = 2 * A_HEAD_DIM

    if has_cache:
        kwin[:, 0:A_WIN, :] = kc_ref[...].astype(BF16)
        vwin[:, 0:A_WIN, :] = vc_ref[...].astype(BF16)
    else:
        @pl.when(i == 0)
        def _():
            kwin[0:A_WIN, :] = jnp.zeros((A_WIN, A_WIDTH), BF16)
            vwin[0:A_WIN, :] = jnp.zeros((A_WIN, A_WIDTH), BF16)

    h = _rms(x_ref[...].reshape(nb * tq, D_MODEL)) * g_ref[...]
    z = _dot(h, w_ref[...])
    q = z[:, 0:A_WIDTH]
    k = z[:, A_WIDTH:2 * A_WIDTH]
    v = z[:, 2 * A_WIDTH:3 * A_WIDTH]
    hsum = hsum_ref[...]
    inv_d = 1.0 / A_HEAD_DIM
    qn = q * lax.rsqrt(_head_sum(q * q, hsum) * inv_d + NORM_EPS) * qg_ref[...]
    kn = k * lax.rsqrt(_head_sum(k * k, hsum) * inv_d + NORM_EPS) * kg_ref[...]
    qs[...] = qn * (A_HEAD_DIM ** -0.5 * LOG2_E)
    if has_cache:
        kwin[:, A_WIN:A_WIN + tq, :] = kn.astype(BF16).reshape(nb, tq, A_WIDTH)
        vwin[:, A_WIN:A_WIN + tq, :] = v.astype(BF16).reshape(nb, tq, A_WIDTH)
    else:
        kwin[A_WIN:A_WIN + tq, :] = kn.astype(BF16)
        vwin[A_WIN:A_WIN + tq, :] = v.astype(BF16)
    kn_ref[...] = kn.reshape(nb, tq, A_WIDTH)
    v_ref[...] = v.reshape(nb, tq, A_WIDTH)

    rho = lax.broadcasted_iota(jnp.int32, (2 * cq, pair_w), 0)
    lane = lax.broadcasted_iota(jnp.int32, (2 * cq, pair_w), 1)
    placed = (rho >> (cq.bit_length() - 1)) == (lane >> (A_HEAD_DIM.bit_length() - 1))

    n_chunks = nb * tq // cq
    per_trip = 4 if n_chunks % 4 == 0 else (2 if n_chunks % 2 == 0 else 1)
    units = [(c, p) for c in range(per_trip) for p in range(A_HEADS // 2)]
    ids = range(len(units))

    def chunk(j, carry, mask_past):
        r0s = [pl.multiple_of((j * per_trip + c) * cq, cq) for c in range(per_trip)]
        lanes = [slice(p * pair_w, (p + 1) * pair_w) for _, p in units]
        qc = [qs[pl.ds(r0s[c], cq), lanes[u]] for u, (c, _) in enumerate(units)]
        qp = [jnp.where(placed, jnp.concatenate([qc[u], qc[u]], axis=0), 0.0) for u in ids]
        if has_cache:
            win = [(j * per_trip + c, slice(None), lanes[u]) for u, (c, _) in enumerate(units)]
        else:
            win = [(pl.ds(r0s[c], nkeys), lanes[u]) for u, (c, _) in enumerate(units)]
        s = [_dot(qp[u], kwin[win[u]], NT) + bias_ref[p] for u, (_, p) in enumerate(units)]
        if mask_past:
            col = lax.broadcasted_iota(jnp.int32, (2 * cq, nkeys), 1)
            s = [jnp.where(r0s[c] + col >= A_WIN, s[u], -1e30) for u, (c, _) in enumerate(units)]
        m = [jnp.max(s[u], axis=-1, keepdims=True) for u in ids]
        e = [jnp.exp2(s[u] - m[u]) for u in ids]
        den = [jnp.sum(e[u], axis=-1, keepdims=True) for u in ids]
        o2 = [_dot(e[u], vwin[win[u]]) for u in ids]
        for u, (c, _) in enumerate(units):
            o = jnp.where(placed, o2[u] / den[u], 0.0)
            if has_cache:
                o_ref[j * per_trip + c, :, lanes[u]] = o[:cq] + o[cq:]
            else:
                o_ref[0, pl.ds(r0s[c], cq), lanes[u]] = o[:cq] + o[cq:]
        return carry

    n_trips = n_chunks // per_trip
    if has_cache:
        lax.fori_loop(0, n_trips, functools.partial(chunk, mask_past=False), 0)
    else:
        @pl.when(i == 0)
        def _():
            lax.fori_loop(0, n_trips, functools.partial(chunk, mask_past=True), 0)

        @pl.when(i > 0)
        def _():
            lax.fori_loop(0, n_trips, functools.partial(chunk, mask_past=False), 0)

    if not has_cache:
        kwin[0:A_WIN, :] = kwin[tq:tq + A_WIN, :]
        vwin[0:A_WIN, :] = vwin[tq:tq + A_WIN, :]


def _attention(x, gain, w_a, q_gain, k_gain, bias, cache_k, cache_v, layer, *, tq, cq, nb):
    bsz, seq, _ = x.shape
    has_cache = cache_k is not None
    assert seq % tq == 0 and tq % cq == 0
    assert (seq == tq == cq <= A_WIN and bsz % nb == 0) if has_cache else (tq == A_WIN and nb == 1)
    nkeys = A_WIN + cq
    tok = lambda w: pl.BlockSpec((nb, tq, w), lambda b, i: (b, i, 0))
    win_shape = (nb, nkeys, A_WIDTH) if has_cache else (A_WIN + tq, A_WIDTH)
    in_specs = [tok(D_MODEL), _const_spec((1, D_MODEL)), w_a[1],
                _const_spec((1, A_WIDTH)), _const_spec((1, A_WIDTH)),
                _const_spec((A_HEADS // 2, 2 * cq, nkeys)), _const_spec((MXU_TILE, MXU_TILE))]
    args = [x, gain, w_a[0], jnp.tile(q_gain, (1, A_HEADS)), jnp.tile(k_gain, (1, A_HEADS)),
            bias.reshape(A_HEADS // 2, 2 * cq, nkeys), _head_sum_matrix(A_HEAD_DIM)]
    if has_cache:
        cspec = pl.BlockSpec((None, nb, A_WIN, A_WIDTH), lambda b, i: (layer, b, 0, 0))
        in_specs += [cspec, cspec]
        args += [cache_k, cache_v]
    out = jax.ShapeDtypeStruct((bsz, seq, A_WIDTH), F32)
    return pl.pallas_call(
        functools.partial(_attn_kernel, tq=tq, cq=cq, has_cache=has_cache, nb=nb),
        grid=(bsz // nb, seq // tq),
        in_specs=in_specs,
        out_specs=[tok(A_WIDTH)] + [pl.BlockSpec((nb, tq, A_WIDTH), lambda b, i: (b, 0, 0))] * 2,
        out_shape=[out] + [jax.ShapeDtypeStruct((bsz, tq, A_WIDTH), F32)] * 2,
        scratch_shapes=[pltpu.VMEM(win_shape, BF16),
                        pltpu.VMEM(win_shape, BF16),
                        pltpu.VMEM((nb * tq, A_WIDTH), F32)],
        compiler_params=pltpu.CompilerParams(
            dimension_semantics=("arbitrary", "arbitrary"),
            vmem_limit_bytes=VMEM_LIMIT_BYTES),
        name="attention",
    )(*args)


def _ret_kernel(x_ref, g_ref, w_ref, cosa_ref, sina_ref, cosb_ref, sinb_ref, cosbs_ref, sinbs_ref,
                s0_ref, o_ref, sout_ref, s_scr, dmat_s, *, tb, nb, nsub):
    batched = nb > 1
    i = pl.program_id(1)

    if not batched:
        @pl.when(i == 0)
        def _():
            s_scr[...] = s0_ref[0]

    @pl.when(jnp.logical_and(pl.program_id(0) == 0, i == 0))
    def _():
        row = lax.broadcasted_iota(jnp.int32, (tb, tb), 0)
        col = lax.broadcasted_iota(jnp.int32, (tb, tb), 1)
        diff = row - col
        dist = jnp.maximum(diff, 0).astype(F32)
        for hh in range(B_HEADS):
            dmat_s[hh] = jnp.where(diff >= 0, jnp.exp(RET_LOG_GAMMA[hh] * dist), 0.0)

    rows = nb * nsub * tb
    h = _rms(x_ref[...].reshape(rows, D_MODEL)) * g_ref[...]
    z = _dot(h, w_ref[...])
    cos_a, sin_a = cosa_ref[0], sina_ref[0]
    cos_all = cos_a * cosb_ref[...] - sin_a * sinb_ref[...]
    sin_all = sin_a * cosbs_ref[...] + cos_a * sinbs_ref[...]
    n = lax.broadcasted_iota(jnp.int32, (tb, 1), 0).astype(F32)
    for hh in range(B_HEADS):
        lg = RET_LOG_GAMMA[hh]
        lo = hh * B_HEAD_DIM
        dmat = dmat_s[hh]
        dec_q = jnp.exp(lg * (n + 1.0))
        dec_k = jnp.exp(lg * ((tb - 1.0) - n))
        for bb in range(nb):
            state = s0_ref[bb, hh] if batched else s_scr[hh]
            for sub in range(nsub):
                rs = slice((bb * nsub + sub) * tb, (bb * nsub + sub + 1) * tb)
                pos = slice(sub * tb, (sub + 1) * tb)
                cosf = cos_all[pos, :]
                sinf = sin_all[pos, :]
                q = z[rs, lo:lo + B_HEAD_DIM]
                k = z[rs, B_WIDTH + lo:B_WIDTH + lo + B_HEAD_DIM]
                v = z[rs, 2 * B_WIDTH + lo:2 * B_WIDTH + lo + B_HEAD_DIM]
                gate = z[rs, 3 * B_WIDTH + lo:3 * B_WIDTH + lo + B_HEAD_DIM]
                q = q * cosf + pltpu.roll(q, B_HEAD_DIM // 2, 1) * sinf
                k = (k * cosf + pltpu.roll(k, B_HEAD_DIM // 2, 1) * sinf) * (B_HEAD_DIM ** -0.5)
                scores = _dot(q, k, NT) * dmat
                o = _dot(scores, v) + _dot(q, state) * dec_q
                state = math.exp(lg * tb) * state + _dot(k * dec_k, v, TN)
                o_ref[bb, pos, lo:lo + B_HEAD_DIM] = _rms(o) * (gate * _sigmoid(gate))
            if batched:
                sout_ref[bb, hh] = state
            else:
                s_scr[hh] = state
    if not batched:
        sout_ref[0] = s_scr[...]


def _retention(x, gain, w_b, pos0, s0, *, tb, nb, nsub):
    bsz, seq, _ = x.shape
    assert (seq == tb and bsz % nb == 0 and nsub == 1) if nb > 1 else seq % (nsub * tb) == 0
    sspec = pl.BlockSpec((nb, B_HEADS, B_HEAD_DIM, B_HEAD_DIM), lambda b, i: (b, 0, 0, 0))
    tok = lambda w: pl.BlockSpec((nb, nsub * tb, w), lambda b, i: (b, i, 0))
    step_rows = nsub * tb
    half = B_HEAD_DIM // 2
    inv = ROPE_BASE ** (-jnp.arange(half, dtype=F32) / half)
    base = (pos0 + step_rows * jnp.arange(seq // step_rows, dtype=jnp.int32)).astype(F32)[:, None] * inv[None, :]
    offs = jnp.arange(step_rows, dtype=jnp.int32).astype(F32)[:, None] * inv[None, :]
    both = lambda t: jnp.concatenate([t, t], axis=-1)
    signed = lambda t: jnp.concatenate([-t, t], axis=-1)
    step_row = lambda t: both(t).reshape(-1, 1, B_HEAD_DIM)
    a_spec = pl.BlockSpec((1, 1, B_HEAD_DIM), lambda b, i: (i, 0, 0))
    b_spec = _const_spec((step_rows, B_HEAD_DIM))
    return pl.pallas_call(
        functools.partial(_ret_kernel, tb=tb, nb=nb, nsub=nsub),
        grid=(bsz // nb, seq // (nsub * tb)),
        in_specs=[tok(D_MODEL),
                  _const_spec((1, D_MODEL)), w_b[1],
                  a_spec, a_spec, b_spec, b_spec, b_spec, b_spec,
                  sspec],
        out_specs=[tok(B_WIDTH), sspec],
        out_shape=[jax.ShapeDtypeStruct((bsz, seq, B_WIDTH), F32),
                   jax.ShapeDtypeStruct((bsz, B_HEADS, B_HEAD_DIM, B_HEAD_DIM), F32)],
        scratch_shapes=[pltpu.VMEM((B_HEADS, B_HEAD_DIM, B_HEAD_DIM), F32),
                        pltpu.VMEM((B_HEADS, tb, tb), F32)],
        compiler_params=pltpu.CompilerParams(
            dimension_semantics=("arbitrary", "arbitrary"),
            vmem_limit_bytes=VMEM_LIMIT_BYTES),
        name="retention",
    )(x, gain, w_b[0], step_row(jnp.cos(base)), step_row(jnp.sin(base)),
      both(jnp.cos(offs)), both(jnp.sin(offs)), signed(jnp.cos(offs)), signed(jnp.sin(offs)), s0)


PAIR_W = 2 * C_HEAD_DIM
N_PAIRS = C_HEADS // 2


def _rwkv_kernel(*refs, tt, cc, lookahead, batched):
    if lookahead:
        x_ref, xn_ref = refs[:2]
        refs = refs[2:]
    else:
        x_ref, xn_ref = refs[0], None
        refs = refs[1:]
    (g_ref, w_ref, mu_ref, sh0_ref, s0_ref, w0_ref, w2_ref, a0_ref, a2_ref,
     g2_ref, kk_ref, ka_ref, rk_ref, lnw_ref, lnb_ref, hsum_ref,
     o_ref, sout_ref, shout_ref,
     s_scr, carry, rk_raw, uw_raw, ua_raw, rt_s, kkt_s, kh_s, bh_s, v_s, bon_s, g_s, cum_s) = refs
    i = pl.program_id(1)

    n_chunks = tt // cc

    def project(x, slot):
        h = _rms(x.reshape(tt, D_MODEL)) * g_ref[...]
        cz = _dot(h, w_ref[...])
        rows = lax.broadcasted_iota(jnp.int32, (tt, 1), 0)
        if batched:
            first = jnp.bitwise_and(rows, cc - 1) == 0
            before = jnp.broadcast_to(sh0_ref[...], (n_chunks, cc, C_SHIFT_WIDTH)).reshape(tt, C_SHIFT_WIDTH)
            shout_ref[...] = cz.reshape(n_chunks, cc, C_SHIFT_WIDTH)[:, cc - 1:cc, :]
        else:
            first = rows == 0
            before = carry[...]
            last = cz[tt - 1:tt, :]
            carry[...] = last
            shout_ref[0] = last
        prev = jnp.where(first, before, pltpu.roll(cz, 1, 0))
        cs = cz + (prev - cz) * mu_ref[...]
        off = 3 * C_WIDTH
        w_lo = cs[:, off:off + C_RANK_W]
        a_lo = cs[:, off + C_RANK_W:off + C_RANK_W + C_RANK_A]
        g_lo = cs[:, off + C_RANK_W + C_RANK_A:C_SHIFT_WIDTH]
        rk_raw[...] = cs[:, 0:2 * C_WIDTH]
        v_s[slot] = cs[:, 2 * C_WIDTH:3 * C_WIDTH]
        uw_raw[...] = w0_ref[...] + _dot(jnp.tanh(w_lo), w2_ref[...])
        ua_raw[...] = a0_ref[...] + _dot(a_lo, a2_ref[...])
        g_s[slot] = _dot(_sigmoid(g_lo), g2_ref[...])

    def token_block(b, slot):
        rs = slice(b * cc, (b + 1) * cc)
        r = rk_raw[rs, 0:C_WIDTH]
        k = rk_raw[rs, C_WIDTH:2 * C_WIDTH]
        v = v_s[slot, rs, :]
        lw = -math.exp(-0.5) * _sigmoid(uw_raw[rs, :])
        a = _sigmoid(ua_raw[rs, :])
        hsum = hsum_ref[...]
        kk_raw = k * kk_ref[...]
        kk = kk_raw / jnp.maximum(jnp.sqrt(_head_sum(kk_raw * kk_raw, hsum)), 1e-12)
        k2 = k * (1.0 + (a - 1.0) * ka_ref[...])
        pos = lax.broadcasted_iota(jnp.int32, (cc, 1), 0)
        cum = lw
        step = 1
        while step < cc:
            cum = cum + jnp.where(pos >= step, pltpu.roll(cum, step, 0), 0.0)
            step *= 2
        e_inv = jnp.exp(-cum)
        rt_s[slot, rs, :] = r * jnp.exp(cum)
        kkt_s[slot, rs, :] = kk * jnp.exp(cum - lw)
        kh_s[slot, rs, :] = k2 * e_inv
        bh_s[slot, rs, :] = kk * a * e_inv
        bon_s[slot, rs, :] = _head_sum(r * k2 * rk_ref[...], hsum) * v
        cum_s[slot, rs, :] = cum

    if not batched:
        @pl.when(i == 0)
        def _():
            s_scr[...] = s0_ref[0]
            carry[...] = sh0_ref[0]
            if lookahead:
                project(x_ref[...], 0)
                for b in range(n_chunks):
                    token_block(b, 0)

    if lookahead:
        cur = lax.rem(i, 2)
        nxt = 1 - cur
        project(xn_ref[...], nxt)
        pending = [functools.partial(token_block, b, nxt) for b in range(n_chunks)]
    else:
        cur = 0
        project(x_ref[...], 0)
        for b in range(n_chunks):
            token_block(b, 0)
        pending = []

    def interleave():
        if pending:
            pending.pop(0)()

    two = 2 * cc
    sh = cc.bit_length() - 1
    hd = C_HEAD_DIM.bit_length() - 1
    n_double = sh - 1
    rho = lax.broadcasted_iota(jnp.int32, (two, PAIR_W), 0)
    lane = lax.broadcasted_iota(jnp.int32, (two, PAIR_W), 1)
    placed = (rho >> sh) == (lane >> hd)
    r2 = lax.broadcasted_iota(jnp.int32, (two, two), 0)
    c2 = lax.broadcasted_iota(jnp.int32, (two, two), 1)
    same = (r2 >> sh) == (c2 >> sh)
    strict = jnp.logical_and(same, r2 > c2)
    incl = jnp.logical_and(same, r2 >= c2)
    eye = jnp.where(r2 == c2, 1.0, 0.0).astype(F32)
    gr = lax.broadcasted_iota(jnp.int32, (PAIR_W, PAIR_W), 0)
    gc = lax.broadcasted_iota(jnp.int32, (PAIR_W, PAIR_W), 1)
    hmean = jnp.where((gr >> hd) == (gc >> hd), 1.0 / C_HEAD_DIM, 0.0).astype(BF16)

    def place(xv):
        return jnp.where(placed, jnp.concatenate([xv, xv], axis=0), 0.0)

    per_trip = 4 if n_chunks % 4 == 0 else (2 if n_chunks % 2 == 0 else 1)
    units = [(c, p) for c in range(per_trip) for p in range(N_PAIRS)]
    pairs = range(len(units))

    def chunk(j):
        r0s = [(j * per_trip + c) * cc for c in range(per_trip)]
        w_end = [jnp.exp(cum_s[cur, r0 + cc - 1:r0 + cc, :]) for r0 in r0s]
        lanes = [slice(p * PAIR_W, (p + 1) * PAIR_W) for _, p in units]
        sl = [(cur, slice(r0s[c], r0s[c] + cc), lanes[u]) for u, (c, _) in enumerate(units)]
        we = [w_end[c][:, lanes[u]] for u, (c, _) in enumerate(units)]
        rt = [place(rt_s[sl[p]]) for p in pairs]
        kkt = [place(kkt_s[sl[p]]) for p in pairs]
        kh = [place(kh_s[sl[p]]) for p in pairs]
        bh = [place(bh_s[sl[p]]) for p in pairs]
        vv = [place(v_s[sl[p]]) for p in pairs]
        cat0 = lambda *xs: jnp.concatenate(xs, axis=0)
        cat1 = lambda *xs: jnp.concatenate(xs, axis=1)
        gg = [_dot(cat0(kkt[p], rt[p]), cat0(kh[p], bh[p]), NT) for p in pairs]
        interleave()
        a_kk = [jnp.where(strict, gg[p][:two, :two], 0.0) for p in pairs]
        b_kk = [jnp.where(incl, gg[p][two:, :two], 0.0) for p in pairs]
        b_bb = [jnp.where(incl, gg[p][two:, two:], 0.0) for p in pairs]
        pw = [jnp.where(strict, -gg[p][:two, two:], 0.0) for p in pairs]
        tinv = [eye + pw[p] for p in pairs]
        pw = [_dot(pw[p], pw[p]) for p in pairs]
        for _ in range(n_double - 1):
            pt = [_dot(pw[p], cat1(pw[p], tinv[p])) for p in pairs]
            pw = [pt[p][:, :two] for p in pairs]
            tinv = [tinv[p] + pt[p][:, two:] for p in pairs]
            interleave()
        tinv = [tinv[p] + _dot(pw[p], tinv[p]) for p in pairs]
        interleave()
        av = [_dot(a_kk[p], vv[p]) for p in pairs]
        kv = [_dot(tinv[p], cat1(kkt[p], av[p])) for p in pairs]
        zero = jnp.zeros((two, PAIR_W), F32)
        ry = [_dot(cat1(b_kk[p], -b_bb[p]), cat0(cat1(zero, vv[p]), kv[p])) for p in pairs]
        rp = [rt[p] + ry[p][:, :PAIR_W] for p in pairs]
        y0 = [ry[p][:, PAIR_W:] for p in pairs]
        kkp = [kv[p][:, :PAIR_W] for p in pairs]
        vp = [kv[p][:, PAIR_W:] for p in pairs]
        bd = [bh[p] * we[p] for p in pairs]
        kd = [kh[p] * we[p] for p in pairs]
        q = [_dot(kkp[p], bd[p], TN) for p in pairs]
        z = [_dot(cat0(vv[p], vp[p]), cat0(kd[p], -bd[p]), TN) for p in pairs]
        y2 = []

        def advance(u, s_old):
            y2.append(_dot(rp[u], s_old, NT) + y0[u])
            return s_old * we[u] - _dot(s_old, q[u]) + z[u]

        if batched:
            for u, (c, p) in enumerate(units):
                sout_ref[j * per_trip + c, p] = advance(u, s0_ref[j * per_trip + c, p])
        else:
            state = [s_scr[p] for p in range(N_PAIRS)]
            for u, (_, p) in enumerate(units):
                state[p] = advance(u, state[p])
            for p in range(N_PAIRS):
                s_scr[p] = state[p]
        ys = [y2[u][:cc] + y2[u][cc:] for u in pairs]
        yc = [ys[u] - _dot(ys[u], hmean) for u in pairs]
        var = [_dot(yc[u] * yc[u], hmean) for u in pairs]
        for u, (c, _) in enumerate(units):
            yn = yc[u] * lax.rsqrt(var[u] + C_GN_EPS) * lnw_ref[:, lanes[u]] + lnb_ref[:, lanes[u]]
            out = (yn + bon_s[sl[u]]) * g_s[sl[u]]
            if batched:
                o_ref[j * per_trip + c, :, lanes[u]] = out
            else:
                o_ref[(0,) + sl[u][1:]] = out

    for j in range(n_chunks // per_trip):
        chunk(j)
    while pending:
        interleave()
    if not batched:
        sout_ref[0] = s_scr[...]


def _pair_states(s):
    bsz = s.shape[0]
    s = s.reshape(bsz, N_PAIRS, 2, C_HEAD_DIM, C_HEAD_DIM)
    zero = jnp.zeros_like(s[:, :, 0])
    top = jnp.concatenate([s[:, :, 0], zero], axis=-1)
    bot = jnp.concatenate([zero, s[:, :, 1]], axis=-1)
    return jnp.concatenate([top, bot], axis=-2)


def _unpair_states(s):
    bsz = s.shape[0]
    d = C_HEAD_DIM
    return jnp.stack([s[:, :, :d, :d], s[:, :, d:, d:]], axis=2).reshape(bsz, C_HEADS, d, d)


def _rwkv(x, gain, w_c, lw, shift0, s0, *, tt, cc, nb):
    bsz, seq, _ = x.shape
    batched = nb > 1
    assert cc & (cc - 1) == 0
    assert (seq == cc and tt == nb * cc and bsz % nb == 0) if batched else (seq % tt == 0 and tt % cc == 0)
    row = lambda a: a.reshape(1, -1)
    sspec = pl.BlockSpec((nb, N_PAIRS, PAIR_W, PAIR_W), lambda b, i: (b, 0, 0, 0))
    shspec = pl.BlockSpec((nb, 1, C_SHIFT_WIDTH), lambda b, i: (b, 0, 0))
    vec = _const_spec((1, C_WIDTH))
    n_tiles = 1 if batched else seq // tt
    lookahead = n_tiles > 1
    tile = pltpu.VMEM((2 if lookahead else 1, tt, C_WIDTH), F32)
    hsum = _head_sum_matrix(C_HEAD_DIM)
    tok = lambda w: pl.BlockSpec((nb, tt // nb, w), lambda b, i: (b, i, 0))
    x_specs = [tok(D_MODEL)]
    x_args = [x]
    if lookahead:
        x_specs = [pl.BlockSpec((1, tt, D_MODEL), lambda b, i: (b, 0, 0)),
                   pl.BlockSpec((1, tt, D_MODEL), lambda b, i: (b, jnp.minimum(i + 1, n_tiles - 1), 0))]
        x_args = [x, x]
    oc, s_new, shift_new = pl.pallas_call(
        functools.partial(_rwkv_kernel, tt=tt, cc=cc, lookahead=lookahead, batched=batched),
        grid=(bsz // nb, n_tiles),
        in_specs=x_specs + [
                  _const_spec((1, D_MODEL)), w_c[1],
                  _const_spec((1, C_SHIFT_WIDTH)), shspec, sspec,
                  vec, _const_spec((C_RANK_W, C_WIDTH)), vec, _const_spec((C_RANK_A, C_WIDTH)),
                  _const_spec((C_RANK_G, C_WIDTH)), vec, vec, vec, vec, vec,
                  _const_spec((MXU_TILE, MXU_TILE))],
        out_specs=[tok(C_WIDTH), sspec, shspec],
        out_shape=[jax.ShapeDtypeStruct((bsz, seq, C_WIDTH), F32),
                   jax.ShapeDtypeStruct((bsz, N_PAIRS, PAIR_W, PAIR_W), F32),
                   jax.ShapeDtypeStruct((bsz, 1, C_SHIFT_WIDTH), F32)],
        scratch_shapes=[pltpu.VMEM((N_PAIRS, PAIR_W, PAIR_W), F32),
                        pltpu.VMEM((1, C_SHIFT_WIDTH), F32),
                        pltpu.VMEM((tt, 2 * C_WIDTH), F32),
                        pltpu.VMEM((tt, C_WIDTH), F32),
                        pltpu.VMEM((tt, C_WIDTH), F32)] + [tile] * 8,
        compiler_params=pltpu.CompilerParams(
            dimension_semantics=("arbitrary", "arbitrary"),
            vmem_limit_bytes=VMEM_LIMIT_BYTES),
        name="rwkv",
    )(*x_args, gain, w_c[0], row(lw["c_shift_mu"]), shift0, _pair_states(s0),
      row(lw["c_w0"]), lw["c_w2"].astype(BF16), row(lw["c_a0"]), lw["c_a2"].astype(BF16),
      lw["c_g2"].astype(BF16), row(lw["c_k_k"]), row(lw["c_k_a"]), row(lw["c_r_k"]),
      row(lw["c_ln_w"]), row(lw["c_ln_b"]), hsum)
    return oc, _unpair_states(s_new), shift_new


def _merge_kernel(x_ref, oa_ref, ob_ref, oc_ref, g_ref, wg_ref, wb_ref, wo_ref, y_ref):
    x = x_ref[...]
    h = (_rms(x) * g_ref[...]).astype(BF16)
    branches = [o_ref[...].astype(BF16) for o_ref in (oa_ref, ob_ref, oc_ref)]
    y = x
    for c in range(D_MODEL // MXU_TILE):
        cols = slice(c * MXU_TILE, (c + 1) * MXU_TILE)
        m = None
        for b, o in enumerate(branches):
            gate = _dot(h, wg_ref[:, b * D_MODEL + c * MXU_TILE:b * D_MODEL + (c + 1) * MXU_TILE])
            t = _sigmoid(gate) * _dot(o, wb_ref[b, :, cols])
            m = t if m is None else m + t
        y = y + _dot(m, wo_ref[cols, :])
    y_ref[...] = y


def _merge(x, oa, ob, oc, gain, w_g, w_b, w_o, *, tm):
    rows = x.shape[0]
    assert rows % tm == 0
    tok = lambda w: pl.BlockSpec((tm, w), lambda i: (i, 0))
    return pl.pallas_call(
        _merge_kernel,
        grid=(rows // tm,),
        in_specs=[tok(D_MODEL), tok(A_WIDTH), tok(B_WIDTH), tok(C_WIDTH),
                  _const_spec((1, D_MODEL)), w_g[1], w_b[1], w_o[1]],
        out_specs=tok(D_MODEL),
        out_shape=jax.ShapeDtypeStruct((rows, D_MODEL), F32),
        compiler_params=pltpu.CompilerParams(
            dimension_semantics=("arbitrary",), vmem_limit_bytes=VMEM_LIMIT_BYTES),
        name="merge",
    )(x, oa, ob, oc, gain, w_g[0], w_b[0], w_o[0])


def _ffn_kernel(x_ref, p_ref, g_ref, wgate_ref, wup_ref, wdown_ref, wpp_ref, pg_ref, wpg_ref, y_ref):
    x = x_ref[...]
    hf = (_rms(x) * g_ref[...]).astype(BF16)
    d_ff = wgate_ref.shape[1]
    for c in range(d_ff // MXU_TILE):
        cols = slice(c * MXU_TILE, (c + 1) * MXU_TILE)
        gate = _dot(hf, wgate_ref[:, cols])
        up = _dot(hf, wup_ref[:, cols])
        x = x + _dot(gate * _sigmoid(gate) * up, wdown_ref[cols, :])
    e = _rms(_dot(p_ref[...], wpp_ref[...])) * pg_ref[...]
    y_ref[...] = x + _sigmoid(_dot(_rms(x), wpg_ref[...])) * e


def _ffn(x, p_all, layer, gain, w_gate, w_up, w_down, w_pp, p_gain, w_pg, *, tm):
    rows = x.shape[0]
    ple = p_all.shape[-1]
    assert w_gate[0].shape[-1] % MXU_TILE == 0
    p = p_all.reshape(p_all.shape[0], rows, ple)
    assert rows % tm == 0
    tok = lambda w: pl.BlockSpec((tm, w), lambda i: (i, 0))
    return pl.pallas_call(
        _ffn_kernel,
        grid=(rows // tm,),
        in_specs=[tok(D_MODEL), pl.BlockSpec((None, tm, ple), lambda i: (layer, i, 0)),
                  _const_spec((1, D_MODEL)),
                  w_gate[1], w_up[1], w_down[1], w_pp[1],
                  _const_spec((1, D_MODEL)), w_pg[1]],
        out_specs=tok(D_MODEL),
        out_shape=jax.ShapeDtypeStruct((rows, D_MODEL), F32),
        compiler_params=pltpu.CompilerParams(
            dimension_semantics=("arbitrary",), vmem_limit_bytes=VMEM_LIMIT_BYTES),
        name="ffn",
    )(x, p, gain, w_gate[0], w_up[0], w_down[0], w_pp[0], p_gain, w_pg[0])


def _rel_bias_table(rel_bias, cq):
    nkeys = A_WIN + cq
    t_max = A_WIN + CHUNK - 1 + cq - 1
    heads, n_rel = rel_bias.shape
    w = t_max + 1
    tail = jnp.broadcast_to(rel_bias[:, n_rel - 1:], (heads, w - n_rel))
    rev = jnp.concatenate([tail, rel_bias[:, ::-1].astype(F32), jnp.zeros((heads, 1), F32)], axis=1)
    skew = jnp.tile(rev, (1, cq))[:, :cq * w].reshape(heads, cq, w)
    return skew[:, :, cq - 1:cq - 1 + nkeys] * LOG2_E


def _layer(x, p_all, layer, pos0, a_ck, a_cv, ret_s0, rwkv_s0, shift_prev, lw, big, cfg):
    bsz, seq, _ = x.shape
    row = lambda a: a.reshape(1, -1)
    w_a = _layer_weight(big["w_in"], layer, (OFF_A, OFF_B - OFF_A))
    w_b = _layer_weight(big["w_in"], layer, (OFF_B, OFF_C - OFF_B))
    w_c = _layer_weight(big["w_in"], layer, (OFF_C, OFF_G - OFF_C))
    w_g = _layer_weight(big["w_in"], layer, (OFF_G, IN_WIDTH - OFF_G))
    gain = row(lw["norm_mix"])

    nb = math.gcd(bsz, cfg["nb"])
    bias = _rel_bias_table(lw["a_rel_bias"], cfg["cq"])
    if a_ck is not None:
        a_ck = a_ck.astype(BF16).reshape(a_ck.shape[0], bsz, A_WIN, A_WIDTH)
        a_cv = a_cv.astype(BF16).reshape(a_cv.shape[0], bsz, A_WIN, A_WIDTH)
    oa, kn, av = _attention(x, gain, w_a, row(lw["a_q_norm"]), row(lw["a_k_norm"]), bias,
                            a_ck, a_cv, layer, tq=cfg["tq"], cq=cfg["cq"], nb=nb)
    keep = min(A_WIN, seq)
    new_ak = kn.reshape(bsz, keep, A_HEADS, A_HEAD_DIM)
    new_av = av.reshape(bsz, keep, A_HEADS, A_HEAD_DIM)

    ob, new_ret = _retention(x, gain, w_b, pos0, ret_s0, tb=cfg["tb"], nb=nb, nsub=cfg["nsub"])

    oc, new_rwkv, new_shift = _rwkv(x, gain, w_c, lw, shift_prev, rwkv_s0,
                                    tt=nb * cfg["tt"], cc=cfg["cc"], nb=nb)

    rows = bsz * seq
    tm = min(cfg["tm"], rows)
    flat = lambda t: t.reshape(rows, t.shape[-1])
    pick = lambda name: _layer_weight(big[name], layer)
    x1 = _merge(flat(x), flat(oa), flat(ob), flat(oc), gain, w_g,
                pick("w_branch"), pick("w_out"), tm=tm)
    x2 = _ffn(x1, p_all, layer, row(lw["norm_ffn"]), pick("w_ffn_gate"), pick("w_ffn_up"),
              pick("w_ffn_down"), pick("w_ple_proj"), row(lw["ple_norm"]), pick("w_ple_gate"),
              tm=min(cfg["tm_ffn"], rows))
    return x2.reshape(bsz, seq, D_MODEL), (new_ak, new_av, new_ret, new_rwkv, new_shift)


def _group_config(seq):
    if seq >= A_WIN:
        return dict(tq=A_WIN, cq=CHUNK, tb=256, nsub=2, tt=512, cc=CHUNK, tm=1024, tm_ffn=1024, nb=1)
    return dict(tq=seq, cq=seq, tb=seq, nsub=1, tt=seq, cc=seq, tm=512, tm_ffn=512, nb=8)


def kernel(x_prompt, x_sample, p_prompt, p_sample, cache_a_k, cache_a_v, state_ret, state_rwkv, state_rwkv_shift, norm_mix, w_in, a_q_norm, a_k_norm, a_rel_bias, c_shift_mu, c_w0, c_w2, c_a0, c_a2, c_g2, c_k_k, c_k_a, c_r_k, c_ln_w, c_ln_b, w_branch, w_out, norm_ffn, w_ffn_gate, w_ffn_up, w_ffn_down, w_ple_proj, ple_norm, w_ple_gate):
    depth = w_in.shape[0]

    def layer_weights(i):
        return dict(norm_mix=norm_mix[i], a_q_norm=a_q_norm[i], a_k_norm=a_k_norm[i],
                    a_rel_bias=a_rel_bias[i], c_shift_mu=c_shift_mu[i], c_w0=c_w0[i], c_w2=c_w2[i],
                    c_a0=c_a0[i], c_a2=c_a2[i], c_g2=c_g2[i], c_k_k=c_k_k[i], c_k_a=c_k_a[i], c_r_k=c_r_k[i],
                    c_ln_w=c_ln_w[i], c_ln_b=c_ln_b[i], norm_ffn=norm_ffn[i], ple_norm=ple_norm[i])

    big = dict(w_in=w_in, w_branch=w_branch, w_out=w_out, w_ffn_gate=w_ffn_gate, w_ffn_up=w_ffn_up,
               w_ffn_down=w_ffn_down, w_ple_proj=w_ple_proj, w_ple_gate=w_ple_gate)
    big = {name: w.astype(BF16) for name, w in big.items()}

    bp, lp, _ = x_prompt.shape
    cfg_p = _group_config(lp)
    ret0 = jnp.zeros((bp, B_HEADS, B_HEAD_DIM, B_HEAD_DIM), F32)
    rwkv0 = jnp.zeros((bp, C_HEADS, C_HEAD_DIM, C_HEAD_DIM), F32)
    shift0 = jnp.zeros((bp, 1, C_SHIFT_WIDTH), F32)
    y_prompt = x_prompt
    st_p = []
    for i in range(depth):
        y_prompt, st = _layer(y_prompt, p_prompt, i, 0, None, None, ret0, rwkv0, shift0,
                              layer_weights(i), big, cfg_p)
        st_p.append(st)

    cfg_s = _group_config(x_sample.shape[1])
    y_sample = x_sample
    st_s = []
    for i in range(depth):
        y_sample, st = _layer(y_sample, p_sample, i, PAST_LEN, cache_a_k, cache_a_v, state_ret[i],
                              state_rwkv[i], state_rwkv_shift[i], layer_weights(i), big, cfg_s)
        st_s.append(st)

    stack = lambda sts, j: jnp.stack([s[j] for s in sts])
    return (y_prompt, y_sample,
            stack(st_p, 0), stack(st_p, 1), stack(st_p, 2), stack(st_p, 3), stack(st_p, 4),
            stack(st_s, 0), stack(st_s, 1), stack(st_s, 2), stack(st_s, 3), stack(st_s, 4))
```

```python
import functools
import math

import jax
import jax.numpy as jnp
from jax import lax
from jax.experimental import pallas as pl
from jax.experimental.pallas import tpu as pltpu

F32 = jnp.float32
BF16 = jnp.bfloat16

D_MODEL = 1024
PAST_LEN = 2048
CHUNK = 64
NORM_EPS = 1e-6

A_HEADS = 8
A_HEAD_DIM = 64
A_WIDTH = 512
A_WIN = 512
A_REL_MAX = 256

B_HEADS = 4
B_HEAD_DIM = 128
B_WIDTH = 512
ROPE_BASE = 10000.0

C_HEADS = 8
C_HEAD_DIM = 64
C_WIDTH = 512
C_RANK_W = 64
C_RANK_A = 64
C_RANK_G = 128
C_SHIFT_WIDTH = 3 * C_WIDTH + C_RANK_W + C_RANK_A + C_RANK_G
C_GN_EPS = 64e-5

N_BRANCHES = 3

OFF_A = 0
OFF_B = 3 * A_WIDTH
OFF_C = OFF_B + 4 * B_WIDTH
OFF_G = OFF_C + C_SHIFT_WIDTH
IN_WIDTH = OFF_G + N_BRANCHES * D_MODEL

RET_LOG_GAMMA = tuple(math.log1p(-(2.0 ** (-5.0 - h))) for h in range(B_HEADS))

VMEM_LIMIT_BYTES = 56 * 1024 * 1024
MXU_TILE = 256
LOG2_E = math.log2(math.e)

NN = (((1,), (0,)), ((), ()))
NT = (((1,), (1,)), ((), ()))
TN = (((0,), (0,)), ((), ()))


def _dot(a, b, dims=NN):
    return lax.dot_general(a.astype(BF16), b.astype(BF16), dims, preferred_element_type=F32)


def _rms(x):
    return x * lax.rsqrt(jnp.mean(x * x, axis=-1, keepdims=True) + NORM_EPS)


def _sigmoid(x):
    return 1.0 / (1.0 + jnp.exp(-x))


def _const_spec(shape):
    nd = len(shape)
    return pl.BlockSpec(shape, lambda *_: (0,) * nd, pipeline_mode=pl.Buffered(1))


def _layer_weight(stacked, layer, cols=None):
    if cols is not None:
        off, width = cols
        if off % width == 0:
            spec = pl.BlockSpec((None, stacked.shape[1], width), lambda *_: (layer, 0, off // width),
                                pipeline_mode=pl.Buffered(1))
            return stacked, spec
        stacked = stacked[:, :, off:off + width]
    shape = stacked.shape[1:]
    spec = pl.BlockSpec((None,) + shape, lambda *_: (layer,) + (0,) * len(shape),
                        pipeline_mode=pl.Buffered(1))
    return stacked, spec


def _head_sum_matrix(head_dim):
    head = jnp.arange(MXU_TILE, dtype=jnp.int32) // head_dim
    return (head[:, None] == head[None, :]).astype(BF16)


def _head_sum(x, hsum):
    w = hsum.shape[0]
    return jnp.concatenate([_dot(x[:, g * w:(g + 1) * w], hsum) for g in range(x.shape[1] // w)], axis=1)


def _attn_kernel(*refs, tq, cq, has_cache, nb):
    if has_cache:
        (x_ref, g_ref, w_ref, qg_ref, kg_ref, bias_ref, hsum_ref, kc_ref, vc_ref,
         o_ref, kn_ref, v_ref, kwin, vwin, qs) = refs
    else:
        (x_ref, g_ref, w_ref, qg_ref, kg_ref, bias_ref, hsum_ref,
         o_ref, kn_ref, v_ref, kwin, vwin, qs) = refs
    i = pl.program_id(1)
    nkeys = A_WIN + cq
    pair_w = 2 * A_HEAD_DIM

    if has_cache:
        kwin[:, 0:A_WIN, :] = kc_ref[...].astype(BF16)
        vwin[:, 0:A_WIN, :] = vc_ref[...].astype(BF16)
    else:
        @pl.when(i == 0)
        def _():
            kwin[0:A_WIN, :] = jnp.zeros((A_WIN, A_WIDTH), BF16)
            vwin[0:A_WIN, :] = jnp.zeros((A_WIN, A_WIDTH), BF16)

    h = _rms(x_ref[...].reshape(nb * tq, D_MODEL)) * g_ref[...]
    z = _dot(h, w_ref[...])
    q = z[:, 0:A_WIDTH]
    k = z[:, A_WIDTH:2 * A_WIDTH]
    v = z[:, 2 * A_WIDTH:3 * A_WIDTH]
    hsum = hsum_ref[...]
    inv_d = 1.0 / A_HEAD_DIM
    qn = q * lax.rsqrt(_head_sum(q * q, hsum) * inv_d + NORM_EPS) * qg_ref[...]
    kn = k * lax.rsqrt(_head_sum(k * k, hsum) * inv_d + NORM_EPS) * kg_ref[...]
    qs[...] = qn * (A_HEAD_DIM ** -0.5 * LOG2_E)
    if has_cache:
        kwin[:, A_WIN:A_WIN + tq, :] = kn.astype(BF16).reshape(nb, tq, A_WIDTH)
        vwin[:, A_WIN:A_WIN + tq, :] = v.astype(BF16).reshape(nb, tq, A_WIDTH)
    else:
        kwin[A_WIN:A_WIN + tq, :] = kn.astype(BF16)
        vwin[A_WIN:A_WIN + tq, :] = v.astype(BF16)
    kn_ref[...] = kn.reshape(nb, tq, A_WIDTH)
    v_ref[...] = v.reshape(nb, tq, A_WIDTH)

    rho = lax.broadcasted_iota(jnp.int32, (2 * cq, pair_w), 0)
    lane = lax.broadcasted_iota(jnp.int32, (2 * cq, pair_w), 1)
    placed = (rho >> (cq.bit_length() - 1)) == (lane >> (A_HEAD_DIM.bit_length() - 1))

    n_chunks = nb * tq // cq
    per_trip = 4 if n_chunks % 4 == 0 else (2 if n_chunks % 2 == 0 else 1)
    units = [(c, p) for c in range(per_trip) for p in range(A_HEADS // 2)]
    ids = range(len(units))

    def chunk(j, carry, mask_past):
        r0s = [pl.multiple_of((j * per_trip + c) * cq, cq) for c in range(per_trip)]
        lanes = [slice(p * pair_w, (p + 1) * pair_w) for _, p in units]
        qc = [qs[pl.ds(r0s[c], cq), lanes[u]] for u, (c, _) in enumerate(units)]
        qp = [jnp.where(placed, jnp.concatenate([qc[u], qc[u]], axis=0), 0.0) for u in ids]
        if has_cache:
            win = [(j * per_trip + c, slice(None), lanes[u]) for u, (c, _) in enumerate(units)]
        else:
            win = [(pl.ds(r0s[c], nkeys), lanes[u]) for u, (c, _) in enumerate(units)]
        s = [_dot(qp[u], kwin[win[u]], NT) + bias_ref[p] for u, (_, p) in enumerate(units)]
        if mask_past:
            col = lax.broadcasted_iota(jnp.int32, (2 * cq, nkeys), 1)
            s = [jnp.where(r0s[c] + col >= A_WIN, s[u], -1e30) for u, (c, _) in enumerate(units)]
        m = [jnp.max(s[u], axis=-1, keepdims=True) for u in ids]
        e = [jnp.exp2(s[u] - m[u]) for u in ids]
        den = [jnp.sum(e[u], axis=-1, keepdims=True) for u in ids]
        o2 = [_dot(e[u], vwin[win[u]]) for u in ids]
        for u, (c, _) in enumerate(units):
            o = jnp.where(placed, o2[u] / den[u], 0.0)
            if has_cache:
                o_ref[j * per_trip + c, :, lanes[u]] = o[:cq] + o[cq:]
            else:
                o_ref[0, pl.ds(r0s[c], cq), lanes[u]] = o[:cq] + o[cq:]
        return carry

    n_trips = n_chunks // per_trip
    if has_cache:
        lax.fori_loop(0, n_trips, functools.partial(chunk, mask_past=False), 0)
    else:
        @pl.when(i == 0)
        def _():
            lax.fori_loop(0, n_trips, functools.partial(chunk, mask_past=True), 0)

        @pl.when(i > 0)
        def _():
            lax.fori_loop(0, n_trips, functools.partial(chunk, mask_past=False), 0)

    if not has_cache:
        kwin[0:A_WIN, :] = kwin[tq:tq + A_WIN, :]
        vwin[0:A_WIN, :] = vwin[tq:tq + A_WIN, :]


def _attention(x, gain, w_a, q_gain, k_gain, bias, cache_k, cache_v, layer, *, tq, cq, nb):
    bsz, seq, _ = x.shape
    has_cache = cache_k is not None
    assert seq % tq == 0 and tq % cq == 0
    assert (seq == tq == cq <= A_WIN and bsz % nb == 0) if has_cache else (tq == A_WIN and nb == 1)
    nkeys = A_WIN + cq
    tok = lambda w: pl.BlockSpec((nb, tq, w), lambda b, i: (b, i, 0))
    win_shape = (nb, nkeys, A_WIDTH) if has_cache else (A_WIN + tq, A_WIDTH)
    in_specs = [tok(D_MODEL), _const_spec((1, D_MODEL)), w_a[1],
                _const_spec((1, A_WIDTH)), _const_spec((1, A_WIDTH)),
                _const_spec((A_HEADS // 2, 2 * cq, nkeys)), _const_spec((MXU_TILE, MXU_TILE))]
    args = [x, gain, w_a[0], jnp.tile(q_gain, (1, A_HEADS)), jnp.tile(k_gain, (1, A_HEADS)),
            bias.reshape(A_HEADS // 2, 2 * cq, nkeys), _head_sum_matrix(A_HEAD_DIM)]
    if has_cache:
        cspec = pl.BlockSpec((None, nb, A_WIN, A_WIDTH), lambda b, i: (layer, b, 0, 0))
        in_specs += [cspec, cspec]
        args += [cache_k, cache_v]
    out = jax.ShapeDtypeStruct((bsz, seq, A_WIDTH), F32)
    return pl.pallas_call(
        functools.partial(_attn_kernel, tq=tq, cq=cq, has_cache=has_cache, nb=nb),
        grid=(bsz // nb, seq // tq),
        in_specs=in_specs,
        out_specs=[tok(A_WIDTH)] + [pl.BlockSpec((nb, tq, A_WIDTH), lambda b, i: (b, 0, 0))] * 2,
        out_shape=[out] + [jax.ShapeDtypeStruct((bsz, tq, A_WIDTH), F32)] * 2,
        scratch_shapes=[pltpu.VMEM(win_shape, BF16),
                        pltpu.VMEM(win_shape, BF16),
                        pltpu.VMEM((nb * tq, A_WIDTH), F32)],
        compiler_params=pltpu.CompilerParams(
            dimension_semantics=("arbitrary", "arbitrary"),
            vmem_limit_bytes=VMEM_LIMIT_BYTES),
        name="attention",
    )(*args)


def _ret_kernel(x_ref, g_ref, w_ref, cosa_ref, sina_ref, cosb_ref, sinb_ref, cosbs_ref, sinbs_ref,
                s0_ref, o_ref, sout_ref, s_scr, dmat_s, *, tb, nb, nsub):
    batched = nb > 1
    i = pl.program_id(1)

    if not batched:
        @pl.when(i == 0)
        def _():
            s_scr[...] = s0_ref[0]

    @pl.when(jnp.logical_and(pl.program_id(0) == 0, i == 0))
    def _():
        row = lax.broadcasted_iota(jnp.int32, (tb, tb), 0)
        col = lax.broadcasted_iota(jnp.int32, (tb, tb), 1)
        diff = row - col
        dist = jnp.maximum(diff, 0).astype(F32)
        for hh in range(B_HEADS):
            dmat_s[hh] = jnp.where(diff >= 0, jnp.exp(RET_LOG_GAMMA[hh] * dist), 0.0)

    rows = nb * nsub * tb
    h = _rms(x_ref[...].reshape(rows, D_MODEL)) * g_ref[...]
    z = _dot(h, w_ref[...])
    cos_a, sin_a = cosa_ref[0], sina_ref[0]
    cos_all = cos_a * cosb_ref[...] - sin_a * sinb_ref[...]
    sin_all = sin_a * cosbs_ref[...] + cos_a * sinbs_ref[...]
    n = lax.broadcasted_iota(jnp.int32, (tb, 1), 0).astype(F32)
    for hh in range(B_HEADS):
        lg = RET_LOG_GAMMA[hh]
        lo = hh * B_HEAD_DIM
        dmat = dmat_s[hh]
        dec_q = jnp.exp(lg * (n + 1.0))
        dec_k = jnp.exp(lg * ((tb - 1.0) - n))
        for bb in range(nb):
            state = s0_ref[bb, hh] if batched else s_scr[hh]
            for sub in range(nsub):
                rs = slice((bb * nsub + sub) * tb, (bb * nsub + sub + 1) * tb)
                pos = slice(sub * tb, (sub + 1) * tb)
                cosf = cos_all[pos, :]
                sinf = sin_all[pos, :]
                q = z[rs, lo:lo + B_HEAD_DIM]
                k = z[rs, B_WIDTH + lo:B_WIDTH + lo + B_HEAD_DIM]
                v = z[rs, 2 * B_WIDTH + lo:2 * B_WIDTH + lo + B_HEAD_DIM]
                gate = z[rs, 3 * B_WIDTH + lo:3 * B_WIDTH + lo + B_HEAD_DIM]
                q = q * cosf + pltpu.roll(q, B_HEAD_DIM // 2, 1) * sinf
                k = (k * cosf + pltpu.roll(k, B_HEAD_DIM // 2, 1) * sinf) * (B_HEAD_DIM ** -0.5)
                scores = _dot(q, k, NT) * dmat
                o = _dot(scores, v) + _dot(q, state) * dec_q
                state = math.exp(lg * tb) * state + _dot(k * dec_k, v, TN)
                o_ref[bb, pos, lo:lo + B_HEAD_DIM] = _rms(o) * (gate * _sigmoid(gate))
            if batched:
                sout_ref[bb, hh] = state
            else:
                s_scr[hh] = state
    if not batched:
        sout_ref[0] = s_scr[...]


def _retention(x, gain, w_b, pos0, s0, *, tb, nb, nsub):
    bsz, seq, _ = x.shape
    assert (seq == tb and bsz % nb == 0 and nsub == 1) if nb > 1 else seq % (nsub * tb) == 0
    sspec = pl.BlockSpec((nb, B_HEADS, B_HEAD_DIM, B_HEAD_DIM), lambda b, i: (b, 0, 0, 0))
    tok = lambda w: pl.BlockSpec((nb, nsub * tb, w), lambda b, i: (b, i, 0))
    step_rows = nsub * tb
    half = B_HEAD_DIM // 2
    inv = ROPE_BASE ** (-jnp.arange(half, dtype=F32) / half)
    base = (pos0 + step_rows * jnp.arange(seq // step_rows, dtype=jnp.int32)).astype(F32)[:, None] * inv[None, :]
    offs = jnp.arange(step_rows, dtype=jnp.int32).astype(F32)[:, None] * inv[None, :]
    both = lambda t: jnp.concatenate([t, t], axis=-1)
    signed = lambda t: jnp.concatenate([-t, t], axis=-1)
    step_row = lambda t: both(t).reshape(-1, 1, B_HEAD_DIM)
    a_spec = pl.BlockSpec((1, 1, B_HEAD_DIM), lambda b, i: (i, 0, 0))
    b_spec = _const_spec((step_rows, B_HEAD_DIM))
    return pl.pallas_call(
        functools.partial(_ret_kernel, tb=tb, nb=nb, nsub=nsub),
        grid=(bsz // nb, seq // (nsub * tb)),
        in_specs=[tok(D_MODEL),
                  _const_spec((1, D_MODEL)), w_b[1],
                  a_spec, a_spec, b_spec, b_spec, b_spec, b_spec,
                  sspec],
        out_specs=[tok(B_WIDTH), sspec],
        out_shape=[jax.ShapeDtypeStruct((bsz, seq, B_WIDTH), F32),
                   jax.ShapeDtypeStruct((bsz, B_HEADS, B_HEAD_DIM, B_HEAD_DIM), F32)],
        scratch_shapes=[pltpu.VMEM((B_HEADS, B_HEAD_DIM, B_HEAD_DIM), F32),
                        pltpu.VMEM((B_HEADS, tb, tb), F32)],
        compiler_params=pltpu.CompilerParams(
            dimension_semantics=("arbitrary", "arbitrary"),
            vmem_limit_bytes=VMEM_LIMIT_BYTES),
        name="retention",
    )(x, gain, w_b[0], step_row(jnp.cos(base)), step_row(jnp.sin(base)),
      both(jnp.cos(offs)), both(jnp.sin(offs)), signed(jnp.cos(offs)), signed(jnp.sin(offs)), s0)


PAIR_W = 2 * C_HEAD_DIM
N_PAIRS = C_HEADS // 2


def _rwkv_kernel(*refs, tt, cc, lookahead, batched):
    if lookahead:
        x_ref, xn_ref = refs[:2]
        refs = refs[2:]
    else:
        x_ref, xn_ref = refs[0], None
        refs = refs[1:]
    (g_ref, w_ref, mu_ref, sh0_ref, s0_ref, w0_ref, w2_ref, a0_ref, a2_ref,
     g2_ref, kk_ref, ka_ref, rk_ref, lnw_ref, lnb_ref, hsum_ref,
     o_ref, sout_ref, shout_ref,
     s_scr, carry, rk_raw, uw_raw, ua_raw, rt_s, kkt_s, kh_s, bh_s, v_s, bon_s, g_s, cum_s) = refs
    i = pl.program_id(1)

    n_chunks = tt // cc

    def project(x, slot):
        h = _rms(x.reshape(tt, D_MODEL)) * g_ref[...]
        cz = _dot(h, w_ref[...])
        rows = lax.broadcasted_iota(jnp.int32, (tt, 1), 0)
        if batched:
            first = jnp.bitwise_and(rows, cc - 1) == 0
            before = jnp.broadcast_to(sh0_ref[...], (n_chunks, cc, C_SHIFT_WIDTH)).reshape(tt, C_SHIFT_WIDTH)
            shout_ref[...] = cz.reshape(n_chunks, cc, C_SHIFT_WIDTH)[:, cc - 1:cc, :]
        else:
            first = rows == 0
            before = carry[...]
            last = cz[tt - 1:tt, :]
            carry[...] = last
            shout_ref[0] = last
        prev = jnp.where(first, before, pltpu.roll(cz, 1, 0))
        cs = cz + (prev - cz) * mu_ref[...]
        off = 3 * C_WIDTH
        w_lo = cs[:, off:off + C_RANK_W]
        a_lo = cs[:, off + C_RANK_W:off + C_RANK_W + C_RANK_A]
        g_lo = cs[:, off + C_RANK_W + C_RANK_A:C_SHIFT_WIDTH]
        rk_raw[...] = cs[:, 0:2 * C_WIDTH]
        v_s[slot] = cs[:, 2 * C_WIDTH:3 * C_WIDTH]
        uw_raw[...] = w0_ref[...] + _dot(jnp.tanh(w_lo), w2_ref[...])
        ua_raw[...] = a0_ref[...] + _dot(a_lo, a2_ref[...])
        g_s[slot] = _dot(_sigmoid(g_lo), g2_ref[...])

    def token_block(b, slot):
        rs = slice(b * cc, (b + 1) * cc)
        r = rk_raw[rs, 0:C_WIDTH]
        k = rk_raw[rs, C_WIDTH:2 * C_WIDTH]
        v = v_s[slot, rs, :]
        lw = -math.exp(-0.5) * _sigmoid(uw_raw[rs, :])
        a = _sigmoid(ua_raw[rs, :])
        hsum = hsum_ref[...]
        kk_raw = k * kk_ref[...]
        kk = kk_raw / jnp.maximum(jnp.sqrt(_head_sum(kk_raw * kk_raw, hsum)), 1e-12)
        k2 = k * (1.0 + (a - 1.0) * ka_ref[...])
        pos = lax.broadcasted_iota(jnp.int32, (cc, 1), 0)
        cum = lw
        step = 1
        while step < cc:
            cum = cum + jnp.where(pos >= step, pltpu.roll(cum, step, 0), 0.0)
            step *= 2
        e_inv = jnp.exp(-cum)
        rt_s[slot, rs, :] = r * jnp.exp(cum)
        kkt_s[slot, rs, :] = kk * jnp.exp(cum - lw)
        kh_s[slot, rs, :] = k2 * e_inv
        bh_s[slot, rs, :] = kk * a * e_inv
        bon_s[slot, rs, :] = _head_sum(r * k2 * rk_ref[...], hsum) * v
        cum_s[slot, rs, :] = cum

    if not batched:
        @pl.when(i == 0)
        def _():
            s_scr[...] = s0_ref[0]
            carry[...] = sh0_ref[0]
            if lookahead:
                project(x_ref[...], 0)
                for b in range(n_chunks):
                    token_block(b, 0)

    if lookahead:
        cur = lax.rem(i, 2)
        nxt = 1 - cur
        project(xn_ref[...], nxt)
        pending = [functools.partial(token_block, b, nxt) for b in range(n_chunks)]
    else:
        cur = 0
        project(x_ref[...], 0)
        for b in range(n_chunks):
            token_block(b, 0)
        pending = []

    def interleave():
        if pending:
            pending.pop(0)()

    two = 2 * cc
    sh = cc.bit_length() - 1
    hd = C_HEAD_DIM.bit_length() - 1
    n_double = sh - 1
    rho = lax.broadcasted_iota(jnp.int32, (two, PAIR_W), 0)
    lane = lax.broadcasted_iota(jnp.int32, (two, PAIR_W), 1)
    placed = (rho >> sh) == (lane >> hd)
    r2 = lax.broadcasted_iota(jnp.int32, (two, two), 0)
    c2 = lax.broadcasted_iota(jnp.int32, (two, two), 1)
    same = (r2 >> sh) == (c2 >> sh)
    strict = jnp.logical_and(same, r2 > c2)
    incl = jnp.logical_and(same, r2 >= c2)
    eye = jnp.where(r2 == c2, 1.0, 0.0).astype(F32)
    gr = lax.broadcasted_iota(jnp.int32, (PAIR_W, PAIR_W), 0)
    gc = lax.broadcasted_iota(jnp.int32, (PAIR_W, PAIR_W), 1)
    hmean = jnp.where((gr >> hd) == (gc >> hd), 1.0 / C_HEAD_DIM, 0.0).astype(BF16)

    def place(xv):
        return jnp.where(placed, jnp.concatenate([xv, xv], axis=0), 0.0)

    per_trip = 4 if n_chunks % 4 == 0 else (2 if n_chunks % 2 == 0 else 1)
    units = [(c, p) for c in range(per_trip) for p in range(N_PAIRS)]
    pairs = range(len(units))

    def chunk(j):
        r0s = [(j * per_trip + c) * cc for c in range(per_trip)]
        w_end = [jnp.exp(cum_s[cur, r0 + cc - 1:r0 + cc, :]) for r0 in r0s]
        lanes = [slice(p * PAIR_W, (p + 1) * PAIR_W) for _, p in units]
        sl = [(cur, slice(r0s[c], r0s[c] + cc), lanes[u]) for u, (c, _) in enumerate(units)]
        we = [w_end[c][:, lanes[u]] for u, (c, _) in enumerate(units)]
        rt = [place(rt_s[sl[p]]) for p in pairs]
        kkt = [place(kkt_s[sl[p]]) for p in pairs]
        kh = [place(kh_s[sl[p]]) for p in pairs]
        bh = [place(bh_s[sl[p]]) for p in pairs]
        vv = [place(v_s[sl[p]]) for p in pairs]
        cat0 = lambda *xs: jnp.concatenate(xs, axis=0)
        cat1 = lambda *xs: jnp.concatenate(xs, axis=1)
        gg = [_dot(cat0(kkt[p], rt[p]), cat0(kh[p], bh[p]), NT) for p in pairs]
        interleave()
        a_kk = [jnp.where(strict, gg[p][:two, :two], 0.0) for p in pairs]
        b_kk = [jnp.where(incl, gg[p][two:, :two], 0.0) for p in pairs]
        b_bb = [jnp.where(incl, gg[p][two:, two:], 0.0) for p in pairs]
        pw = [jnp.where(strict, -gg[p][:two, two:], 0.0) for p in pairs]
        tinv = [eye + pw[p] for p in pairs]
        pw = [_dot(pw[p], pw[p]) for p in pairs]
        for _ in range(n_double - 1):
            pt = [_dot(pw[p], cat1(pw[p], tinv[p])) for p in pairs]
            pw = [pt[p][:, :two] for p in pairs]
            tinv = [tinv[p] + pt[p][:, two:] for p in pairs]
            interleave()
        tinv = [tinv[p] + _dot(pw[p], tinv[p]) for p in pairs]
        interleave()
        av = [_dot(a_kk[p], vv[p]) for p in pairs]
        kv = [_dot(tinv[p], cat1(kkt[p], av[p])) for p in pairs]
        zero = jnp.zeros((two, PAIR_W), F32)
        ry = [_dot(cat1(b_kk[p], -b_bb[p]), cat0(cat1(zero, vv[p]), kv[p])) for p in pairs]
        rp = [rt[p] + ry[p][:, :PAIR_W] for p in pairs]
        y0 = [ry[p][:, PAIR_W:] for p in pairs]
        kkp = [kv[p][:, :PAIR_W] for p in pairs]
        vp = [kv[p][:, PAIR_W:] for p in pairs]
        bd = [bh[p] * we[p] for p in pairs]
        kd = [kh[p] * we[p] for p in pairs]
        q = [_dot(kkp[p], bd[p], TN) for p in pairs]
        z = [_dot(cat0(vv[p], vp[p]), cat0(kd[p], -bd[p]), TN) for p in pairs]
        y2 = []

        def advance(u, s_old):
            y2.append(_dot(rp[u], s_old, NT) + y0[u])
            return s_old * we[u] - _dot(s_old, q[u]) + z[u]

        if batched:
            for u, (c, p) in enumerate(units):
                sout_ref[j * per_trip + c, p] = advance(u, s0_ref[j * per_trip + c, p])
        else:
            state = [s_scr[p] for p in range(N_PAIRS)]
            for u, (_, p) in enumerate(units):
                state[p] = advance(u, state[p])
            for p in range(N_PAIRS):
                s_scr[p] = state[p]
        ys = [y2[u][:cc] + y2[u][cc:] for u in pairs]
        yc = [ys[u] - _dot(ys[u], hmean) for u in pairs]
        var = [_dot(yc[u] * yc[u], hmean) for u in pairs]
        for u, (c, _) in enumerate(units):
            yn = yc[u] * lax.rsqrt(var[u] + C_GN_EPS) * lnw_ref[:, lanes[u]] + lnb_ref[:, lanes[u]]
            out = (yn + bon_s[sl[u]]) * g_s[sl[u]]
            if batched:
                o_ref[j * per_trip + c, :, lanes[u]] = out
            else:
                o_ref[(0,) + sl[u][1:]] = out

    for j in range(n_chunks // per_trip):
        chunk(j)
    while pending:
        interleave()
    if not batched:
        sout_ref[0] = s_scr[...]


def _pair_states(s):
    bsz = s.shape[0]
    s = s.reshape(bsz, N_PAIRS, 2, C_HEAD_DIM, C_HEAD_DIM)
    zero = jnp.zeros_like(s[:, :, 0])
    top = jnp.concatenate([s[:, :, 0], zero], axis=-1)
    bot = jnp.concatenate([zero, s[:, :, 1]], axis=-1)
    return jnp.concatenate([top, bot], axis=-2)


def _unpair_states(s):
    bsz = s.shape[0]
    d = C_HEAD_DIM
    return jnp.stack([s[:, :, :d, :d], s[:, :, d:, d:]], axis=2).reshape(bsz, C_HEADS, d, d)


def _rwkv(x, gain, w_c, lw, shift0, s0, *, tt, cc, nb):
    bsz, seq, _ = x.shape
    batched = nb > 1
    assert cc & (cc - 1) == 0
    assert (seq == cc and tt == nb * cc and bsz % nb == 0) if batched else (seq % tt == 0 and tt % cc == 0)
    row = lambda a: a.reshape(1, -1)
    sspec = pl.BlockSpec((nb, N_PAIRS, PAIR_W, PAIR_W), lambda b, i: (b, 0, 0, 0))
    shspec = pl.BlockSpec((nb, 1, C_SHIFT_WIDTH), lambda b, i: (b, 0, 0))
    vec = _const_spec((1, C_WIDTH))
    n_tiles = 1 if batched else seq // tt
    lookahead = n_tiles > 1
    tile = pltpu.VMEM((2 if lookahead else 1, tt, C_WIDTH), F32)
    hsum = _head_sum_matrix(C_HEAD_DIM)
    tok = lambda w: pl.BlockSpec((nb, tt // nb, w), lambda b, i: (b, i, 0))
    x_specs = [tok(D_MODEL)]
    x_args = [x]
    if lookahead:
        x_specs = [pl.BlockSpec((1, tt, D_MODEL), lambda b, i: (b, 0, 0)),
                   pl.BlockSpec((1, tt, D_MODEL), lambda b, i: (b, jnp.minimum(i + 1, n_tiles - 1), 0))]
        x_args = [x, x]
    oc, s_new, shift_new = pl.pallas_call(
        functools.partial(_rwkv_kernel, tt=tt, cc=cc, lookahead=lookahead, batched=batched),
        grid=(bsz // nb, n_tiles),
        in_specs=x_specs + [
                  _const_spec((1, D_MODEL)), w_c[1],
                  _const_spec((1, C_SHIFT_WIDTH)), shspec, sspec,
                  vec, _const_spec((C_RANK_W, C_WIDTH)), vec, _const_spec((C_RANK_A, C_WIDTH)),
                  _const_spec((C_RANK_G, C_WIDTH)), vec, vec, vec, vec, vec,
                  _const_spec((MXU_TILE, MXU_TILE))],
        out_specs=[tok(C_WIDTH), sspec, shspec],
        out_shape=[jax.ShapeDtypeStruct((bsz, seq, C_WIDTH), F32),
                   jax.ShapeDtypeStruct((bsz, N_PAIRS, PAIR_W, PAIR_W), F32),
                   jax.ShapeDtypeStruct((bsz, 1, C_SHIFT_WIDTH), F32)],
        scratch_shapes=[pltpu.VMEM((N_PAIRS, PAIR_W, PAIR_W), F32),
                        pltpu.VMEM((1, C_SHIFT_WIDTH), F32),
                        pltpu.VMEM((tt, 2 * C_WIDTH), F32),
                        pltpu.VMEM((tt, C_WIDTH), F32),
                        pltpu.VMEM((tt, C_WIDTH), F32)] + [tile] * 8,
        compiler_params=pltpu.CompilerParams(
            dimension_semantics=("arbitrary", "arbitrary"),
            vmem_limit_bytes=VMEM_LIMIT_BYTES),
        name="rwkv",
    )(*x_args, gain, w_c[0], row(lw["c_shift_mu"]), shift0, _pair_states(s0),
      row(lw["c_w0"]), lw["c_w2"].astype(BF16), row(lw["c_a0"]), lw["c_a2"].astype(BF16),
      lw["c_g2"].astype(BF16), row(lw["c_k_k"]), row(lw["c_k_a"]), row(lw["c_r_k"]),
      row(lw["c_ln_w"]), row(lw["c_ln_b"]), hsum)
    return oc, _unpair_states(s_new), shift_new


def _merge_kernel(x_ref, oa_ref, ob_ref, oc_ref, g_ref, wg_ref, wb_ref, wo_ref, y_ref):
    x = x_ref[...]
    h = (_rms(x) * g_ref[...]).astype(BF16)
    branches = [o_ref[...].astype(BF16) for o_ref in (oa_ref, ob_ref, oc_ref)]
    y = x
    for c in range(D_MODEL // MXU_TILE):
        cols = slice(c * MXU_TILE, (c + 1) * MXU_TILE)
        m = None
        for b, o in enumerate(branches):
            gate = _dot(h, wg_ref[:, b * D_MODEL + c * MXU_TILE:b * D_MODEL + (c + 1) * MXU_TILE])
            t = _sigmoid(gate) * _dot(o, wb_ref[b, :, cols])
            m = t if m is None else m + t
        y = y + _dot(m, wo_ref[cols, :])
    y_ref[...] = y


def _merge(x, oa, ob, oc, gain, w_g, w_b, w_o, *, tm):
    rows = x.shape[0]
    assert rows % tm == 0
    tok = lambda w: pl.BlockSpec((tm, w), lambda i: (i, 0))
    return pl.pallas_call(
        _merge_kernel,
        grid=(rows // tm,),
        in_specs=[tok(D_MODEL), tok(A_WIDTH), tok(B_WIDTH), tok(C_WIDTH),
                  _const_spec((1, D_MODEL)), w_g[1], w_b[1], w_o[1]],
        out_specs=tok(D_MODEL),
        out_shape=jax.ShapeDtypeStruct((rows, D_MODEL), F32),
        compiler_params=pltpu.CompilerParams(
            dimension_semantics=("arbitrary",), vmem_limit_bytes=VMEM_LIMIT_BYTES),
        name="merge",
    )(x, oa, ob, oc, gain, w_g[0], w_b[0], w_o[0])


def _ffn_kernel(x_ref, p_ref, g_ref, wgate_ref, wup_ref, wdown_ref, wpp_ref, pg_ref, wpg_ref, y_ref):
    x = x_ref[...]
    hf = (_rms(x) * g_ref[...]).astype(BF16)
    d_ff = wgate_ref.shape[1]
    for c in range(d_ff // MXU_TILE):
        cols = slice(c * MXU_TILE, (c + 1) * MXU_TILE)
        gate = _dot(hf, wgate_ref[:, cols])
        up = _dot(hf, wup_ref[:, cols])
        x = x + _dot(gate * _sigmoid(gate) * up, wdown_ref[cols, :])
    e = _rms(_dot(p_ref[...], wpp_ref[...])) * pg_ref[...]
    y_ref[...] = x + _sigmoid(_dot(_rms(x), wpg_ref[...])) * e


def _ffn(x, p_all, layer, gain, w_gate, w_up, w_down, w_pp, p_gain, w_pg, *, tm):
    rows = x.shape[0]
    ple = p_all.shape[-1]
    assert w_gate[0].shape[-1] % MXU_TILE == 0
    p = p_all.reshape(p_all.shape[0], rows, ple)
    assert rows % tm == 0
    tok = lambda w: pl.BlockSpec((tm, w), lambda i: (i, 0))
    return pl.pallas_call(
        _ffn_kernel,
        grid=(rows // tm,),
        in_specs=[tok(D_MODEL), pl.BlockSpec((None, tm, ple), lambda i: (layer, i, 0)),
                  _const_spec((1, D_MODEL)),
                  w_gate[1], w_up[1], w_down[1], w_pp[1],
                  _const_spec((1, D_MODEL)), w_pg[1]],
        out_specs=tok(D_MODEL),
        out_shape=jax.ShapeDtypeStruct((rows, D_MODEL), F32),
        compiler_params=pltpu.CompilerParams(
            dimension_semantics=("arbitrary",), vmem_limit_bytes=VMEM_LIMIT_BYTES),
        name="ffn",
    )(x, p, gain, w_gate[0], w_up[0], w_down[0], w_pp[0], p_gain, w_pg[0])


def _rel_bias_table(rel_bias, cq):
    nkeys = A_WIN + cq
    t_max = A_WIN + CHUNK - 1 + cq - 1
    heads, n_rel = rel_bias.shape
    w = t_max + 1
    tail = jnp.broadcast_to(rel_bias[:, n_rel - 1:], (heads, w - n_rel))
    rev = jnp.concatenate([tail, rel_bias[:, ::-1].astype(F32), jnp.zeros((heads, 1), F32)], axis=1)
    skew = jnp.tile(rev, (1, cq))[:, :cq * w].reshape(heads, cq, w)
    return skew[:, :, cq - 1:cq - 1 + nkeys] * LOG2_E


def _layer(x, p_all, layer, pos0, a_ck, a_cv, ret_s0, rwkv_s0, shift_prev, lw, big, cfg):
    bsz, seq, _ = x.shape
    row = lambda a: a.reshape(1, -1)
    w_a = _layer_weight(big["w_in"], layer, (OFF_A, OFF_B - OFF_A))
    w_b = _layer_weight(big["w_in"], layer, (OFF_B, OFF_C - OFF_B))
    w_c = _layer_weight(big["w_in"], layer, (OFF_C, OFF_G - OFF_C))
    w_g = _layer_weight(big["w_in"], layer, (OFF_G, IN_WIDTH - OFF_G))
    gain = row(lw["norm_mix"])

    nb = math.gcd(bsz, cfg["nb"])
    bias = _rel_bias_table(lw["a_rel_bias"], cfg["cq"])
    if a_ck is not None:
        a_ck = a_ck.reshape(a_ck.shape[0], bsz, A_WIN, A_WIDTH)
        a_cv = a_cv.reshape(a_cv.shape[0], bsz, A_WIN, A_WIDTH)
    oa, kn, av = _attention(x, gain, w_a, row(lw["a_q_norm"]), row(lw["a_k_norm"]), bias,
                            a_ck, a_cv, layer, tq=cfg["tq"], cq=cfg["cq"], nb=nb)
    keep = min(A_WIN, seq)
    new_ak = kn.reshape(bsz, keep, A_HEADS, A_HEAD_DIM)
    new_av = av.reshape(bsz, keep, A_HEADS, A_HEAD_DIM)

    ob, new_ret = _retention(x, gain, w_b, pos0, ret_s0, tb=cfg["tb"], nb=nb, nsub=cfg["nsub"])

    oc, new_rwkv, new_shift = _rwkv(x, gain, w_c, lw, shift_prev, rwkv_s0,
                                    tt=nb * cfg["tt"], cc=cfg["cc"], nb=nb)

    rows = bsz * seq
    tm = min(cfg["tm"], rows)
    flat = lambda t: t.reshape(rows, t.shape[-1])
    pick = lambda name: _layer_weight(big[name], layer)
    x1 = _merge(flat(x), flat(oa), flat(ob), flat(oc), gain, w_g,
                pick("w_branch"), pick("w_out"), tm=tm)
    x2 = _ffn(x1, p_all, layer, row(lw["norm_ffn"]), pick("w_ffn_gate"), pick("w_ffn_up"),
              pick("w_ffn_down"), pick("w_ple_proj"), row(lw["ple_norm"]), pick("w_ple_gate"),
              tm=min(cfg["tm_ffn"], rows))
    return x2.reshape(bsz, seq, D_MODEL), (new_ak, new_av, new_ret, new_rwkv, new_shift)


def _group_config(seq):
    if seq >= A_WIN:
        return dict(tq=A_WIN, cq=CHUNK, tb=256, nsub=2, tt=512, cc=CHUNK, tm=1024, tm_ffn=1024, nb=1)
    return dict(tq=seq, cq=seq, tb=seq, nsub=1, tt=seq, cc=seq, tm=512, tm_ffn=512, nb=8)


def kernel(x_prompt, x_sample, p_prompt, p_sample, cache_a_k, cache_a_v, state_ret, state_rwkv, state_rwkv_shift, norm_mix, w_in, a_q_norm, a_k_norm, a_rel_bias, c_shift_mu, c_w0, c_w2, c_a0, c_a2, c_g2, c_k_k, c_k_a, c_r_k, c_ln_w, c_ln_b, w_branch, w_out, norm_ffn, w_ffn_gate, w_ffn_up, w_ffn_down, w_ple_proj, ple_norm, w_ple_gate):
    depth = w_in.shape[0]

    def layer_weights(i):
        return dict(norm_mix=norm_mix[i], a_q_norm=a_q_norm[i], a_k_norm=a_k_norm[i],
                    a_rel_bias=a_rel_bias[i], c_shift_mu=c_shift_mu[i], c_w0=c_w0[i], c_w2=c_w2[i],
                    c_a0=c_a0[i], c_a2=c_a2[i], c_g2=c_g2[i], c_k_k=c_k_k[i], c_k_a=c_k_a[i], c_r_k=c_r_k[i],
                    c_ln_w=c_ln_w[i], c_ln_b=c_ln_b[i], norm_ffn=norm_ffn[i], ple_norm=ple_norm[i])

    big = dict(w_in=w_in, w_branch=w_branch, w_out=w_out, w_ffn_gate=w_ffn_gate, w_ffn_up=w_ffn_up,
               w_ffn_down=w_ffn_down, w_ple_proj=w_ple_proj, w_ple_gate=w_ple_gate)
    big = {name: w.astype(BF16) for name, w in big.items()}

    bp, lp, _ = x_prompt.shape
    cfg_p = _group_config(lp)
    ret0 = jnp.zeros((bp, B_HEADS, B_HEAD_DIM, B_HEAD_DIM), F32)
    rwkv0 = jnp.zeros((bp, C_HEADS, C_HEAD_DIM, C_HEAD_DIM), F32)
    shift0 = jnp.zeros((bp, 1, C_SHIFT_WIDTH), F32)
    y_prompt = x_prompt
    st_p = []
    for i in range(depth):
        y_prompt, st = _layer(y_prompt, p_prompt, i, 0, None, None, ret0, rwkv0, shift0,
                              layer_weights(i), big, cfg_p)
        st_p.append(st)

    cfg_s = _group_config(x_sample.shape[1])
    y_sample = x_sample
    st_s = []
    for i in range(depth):
        y_sample, st = _layer(y_sample, p_sample, i, PAST_LEN, cache_a_k, cache_a_v, state_ret[i],
                              state_rwkv[i], state_rwkv_shift[i], layer_weights(i), big, cfg_s)
        st_s.append(st)

    stack = lambda sts, j: jnp.stack([s[j] for s in sts])
    return (y_prompt, y_sample,
            stack(st_p, 0), stack(st_p, 1), stack(st_p, 2), stack(st_p, 3), stack(st_p, 4),
            stack(st_s, 0), stack(st_s, 1), stack(st_s, 2), stack(st_s, 3), stack(st_s, 4))
```

```python
import functools
import math

import jax
import jax.numpy as jnp
from jax import lax
from jax.experimental import pallas as pl
from jax.experimental.pallas import tpu as pltpu

F32 = jnp.float32
BF16 = jnp.bfloat16

D_MODEL = 1024
PAST_LEN = 2048
CHUNK = 64
NORM_EPS = 1e-6

A_HEADS = 8
A_HEAD_DIM = 64
A_WIDTH = 512
A_WIN = 512
A_REL_MAX = 256

B_HEADS = 4
B_HEAD_DIM = 128
B_WIDTH = 512
ROPE_BASE = 10000.0

C_HEADS = 8
C_HEAD_DIM = 64
C_WIDTH = 512
C_RANK_W = 64
C_RANK_A = 64
C_RANK_G = 128
C_SHIFT_WIDTH = 3 * C_WIDTH + C_RANK_W + C_RANK_A + C_RANK_G
C_GN_EPS = 64e-5

N_BRANCHES = 3

OFF_A = 0
OFF_B = 3 * A_WIDTH
OFF_C = OFF_B + 4 * B_WIDTH
OFF_G = OFF_C + C_SHIFT_WIDTH
IN_WIDTH = OFF_G + N_BRANCHES * D_MODEL

RET_LOG_GAMMA = tuple(math.log1p(-(2.0 ** (-5.0 - h))) for h in range(B_HEADS))

VMEM_LIMIT_BYTES = 56 * 1024 * 1024
MXU_TILE = 256
LOG2_E = math.log2(math.e)

NN = (((1,), (0,)), ((), ()))
NT = (((1,), (1,)), ((), ()))
TN = (((0,), (0,)), ((), ()))


def _dot(a, b, dims=NN):
    return lax.dot_general(a.astype(BF16), b.astype(BF16), dims, preferred_element_type=F32)


def _rms(x):
    return x * lax.rsqrt(jnp.mean(x * x, axis=-1, keepdims=True) + NORM_EPS)


def _sigmoid(x):
    return 1.0 / (1.0 + jnp.exp(-x))


def _const_spec(shape):
    nd = len(shape)
    return pl.BlockSpec(shape, lambda *_: (0,) * nd, pipeline_mode=pl.Buffered(1))


def _layer_weight(stacked, layer, cols=None):
    if cols is not None:
        off, width = cols
        if off % width == 0:
            spec = pl.BlockSpec((None, stacked.shape[1], width), lambda *_: (layer, 0, off // width),
                                pipeline_mode=pl.Buffered(1))
            return stacked, spec
        stacked = stacked[:, :, off:off + width]
    shape = stacked.shape[1:]
    spec = pl.BlockSpec((None,) + shape, lambda *_: (layer,) + (0,) * len(shape),
                        pipeline_mode=pl.Buffered(1))
    return stacked, spec


def _head_sum_matrix(head_dim):
    head = jnp.arange(MXU_TILE, dtype=jnp.int32) // head_dim
    return (head[:, None] == head[None, :]).astype(BF16)


def _head_sum(x, hsum):
    w = hsum.shape[0]
    return jnp.concatenate([_dot(x[:, g * w:(g + 1) * w], hsum) for g in range(x.shape[1] // w)], axis=1)


def _attn_kernel(*refs, tq, cq, has_cache, nb):
    if has_cache:
        (x_ref, g_ref, w_ref, qg_ref, kg_ref, bias_ref, hsum_ref, kc_ref, vc_ref,
         o_ref, kn_ref, v_ref, kwin, vwin, qs) = refs
    else:
        (x_ref, g_ref, w_ref, qg_ref, kg_ref, bias_ref, hsum_ref,
         o_ref, kn_ref, v_ref, kwin, vwin, qs) = refs
    i = pl.program_id(1)
    nkeys = A_WIN + cq
    pair_w = 2 * A_HEAD_DIM

    if has_cache:
        kwin[:, 0:A_WIN, :] = kc_ref[...].astype(BF16)
        vwin[:, 0:A_WIN, :] = vc_ref[...].astype(BF16)
    else:
        @pl.when(i == 0)
        def _():
            kwin[0:A_WIN, :] = jnp.zeros((A_WIN, A_WIDTH), BF16)
            vwin[0:A_WIN, :] = jnp.zeros((A_WIN, A_WIDTH), BF16)

    h = _rms(x_ref[...].reshape(nb * tq, D_MODEL)) * g_ref[...]
    z = _dot(h, w_ref[...])
    q = z[:, 0:A_WIDTH]
    k = z[:, A_WIDTH:2 * A_WIDTH]
    v = z[:, 2 * A_WIDTH:3 * A_WIDTH]
    hsum = hsum_ref[...]
    inv_d = 1.0 / A_HEAD_DIM
    qn = q * lax.rsqrt(_head_sum(q * q, hsum) * inv_d + NORM_EPS) * qg_ref[...]
    kn = k * lax.rsqrt(_head_sum(k * k, hsum) * inv_d + NORM_EPS) * kg_ref[...]
    qs[...] = qn * (A_HEAD_DIM ** -0.5 * LOG2_E)
    if has_cache:
        kwin[:, A_WIN:A_WIN + tq, :] = kn.astype(BF16).reshape(nb, tq, A_WIDTH)
        vwin[:, A_WIN:A_WIN + tq, :] = v.astype(BF16).reshape(nb, tq, A_WIDTH)
    else:
        kwin[A_WIN:A_WIN + tq, :] = kn.astype(BF16)
        vwin[A_WIN:A_WIN + tq, :] = v.astype(BF16)
    kn_ref[...] = kn.reshape(nb, tq, A_WIDTH)
    v_ref[...] = v.reshape(nb, tq, A_WIDTH)

    rho = lax.broadcasted_iota(jnp.int32, (2 * cq, pair_w), 0)
    lane = lax.broadcasted_iota(jnp.int32, (2 * cq, pair_w), 1)
    placed = (rho >> (cq.bit_length() - 1)) == (lane >> (A_HEAD_DIM.bit_length() - 1))

    n_chunks = nb * tq // cq
    per_trip = 8 if n_chunks % 8 == 0 else (4 if n_chunks % 4 == 0 else (2 if n_chunks % 2 == 0 else 1))
    units = [(c, p) for c in range(per_trip) for p in range(A_HEADS // 2)]
    ids = range(len(units))

    def chunk(j, carry, mask_past):
        r0s = [pl.multiple_of((j * per_trip + c) * cq, cq) for c in range(per_trip)]
        lanes = [slice(p * pair_w, (p + 1) * pair_w) for _, p in units]
        qc = [qs[pl.ds(r0s[c], cq), lanes[u]] for u, (c, _) in enumerate(units)]
        qp = [jnp.where(placed, jnp.concatenate([qc[u], qc[u]], axis=0), 0.0) for u in ids]
        if has_cache:
            win = [(j * per_trip + c, slice(None), lanes[u]) for u, (c, _) in enumerate(units)]
        else:
            win = [(pl.ds(r0s[c], nkeys), lanes[u]) for u, (c, _) in enumerate(units)]
        s = [_dot(qp[u], kwin[win[u]], NT) + bias_ref[p] for u, (_, p) in enumerate(units)]
        if mask_past:
            col = lax.broadcasted_iota(jnp.int32, (2 * cq, nkeys), 1)
            s = [jnp.where(r0s[c] + col >= A_WIN, s[u], -1e30) for u, (c, _) in enumerate(units)]
        m = [jnp.max(s[u], axis=-1, keepdims=True) for u in ids]
        e = [jnp.exp2(s[u] - m[u]) for u in ids]
        den = [jnp.sum(e[u], axis=-1, keepdims=True) for u in ids]
        o2 = [_dot(e[u], vwin[win[u]]) for u in ids]
        for u, (c, _) in enumerate(units):
            o = jnp.where(placed, o2[u] / den[u], 0.0)
            if has_cache:
                o_ref[j * per_trip + c, :, lanes[u]] = o[:cq] + o[cq:]
            else:
                o_ref[0, pl.ds(r0s[c], cq), lanes[u]] = o[:cq] + o[cq:]
        return carry

    n_trips = n_chunks // per_trip
    if has_cache:
        lax.fori_loop(0, n_trips, functools.partial(chunk, mask_past=False), 0)
    else:
        @pl.when(i == 0)
        def _():
            lax.fori_loop(0, n_trips, functools.partial(chunk, mask_past=True), 0)

        @pl.when(i > 0)
        def _():
            lax.fori_loop(0, n_trips, functools.partial(chunk, mask_past=False), 0)

    if not has_cache:
        kwin[0:A_WIN, :] = kwin[tq:tq + A_WIN, :]
        vwin[0:A_WIN, :] = vwin[tq:tq + A_WIN, :]


def _attention(x, gain, w_a, q_gain, k_gain, bias, cache_k, cache_v, layer, *, tq, cq, nb):
    bsz, seq, _ = x.shape
    has_cache = cache_k is not None
    assert seq % tq == 0 and tq % cq == 0
    assert (seq == tq == cq <= A_WIN and bsz % nb == 0) if has_cache else (tq == A_WIN and nb == 1)
    nkeys = A_WIN + cq
    tok = lambda w: pl.BlockSpec((nb, tq, w), lambda b, i: (b, i, 0))
    win_shape = (nb, nkeys, A_WIDTH) if has_cache else (A_WIN + tq, A_WIDTH)
    in_specs = [tok(D_MODEL), _const_spec((1, D_MODEL)), w_a[1],
                _const_spec((1, A_WIDTH)), _const_spec((1, A_WIDTH)),
                _const_spec((A_HEADS // 2, 2 * cq, nkeys)), _const_spec((MXU_TILE, MXU_TILE))]
    args = [x, gain, w_a[0], jnp.tile(q_gain, (1, A_HEADS)), jnp.tile(k_gain, (1, A_HEADS)),
            bias.reshape(A_HEADS // 2, 2 * cq, nkeys), _head_sum_matrix(A_HEAD_DIM)]
    if has_cache:
        cspec = pl.BlockSpec((None, nb, A_WIN, A_WIDTH), lambda b, i: (layer, b, 0, 0))
        in_specs += [cspec, cspec]
        args += [cache_k, cache_v]
    out = jax.ShapeDtypeStruct((bsz, seq, A_WIDTH), F32)
    return pl.pallas_call(
        functools.partial(_attn_kernel, tq=tq, cq=cq, has_cache=has_cache, nb=nb),
        grid=(bsz // nb, seq // tq),
        in_specs=in_specs,
        out_specs=[tok(A_WIDTH)] + [pl.BlockSpec((nb, tq, A_WIDTH), lambda b, i: (b, 0, 0))] * 2,
        out_shape=[out] + [jax.ShapeDtypeStruct((bsz, tq, A_WIDTH), F32)] * 2,
        scratch_shapes=[pltpu.VMEM(win_shape, BF16),
                        pltpu.VMEM(win_shape, BF16),
                        pltpu.VMEM((nb * tq, A_WIDTH), F32)],
        compiler_params=pltpu.CompilerParams(
            dimension_semantics=("arbitrary", "arbitrary"),
            vmem_limit_bytes=VMEM_LIMIT_BYTES),
        name="attention",
    )(*args)


def _ret_kernel(x_ref, g_ref, w_ref, cosa_ref, sina_ref, cosb_ref, sinb_ref, cosbs_ref, sinbs_ref,
                s0_ref, o_ref, sout_ref, s_scr, dmat_s, *, tb, nb, nsub):
    batched = nb > 1
    i = pl.program_id(1)

    if not batched:
        @pl.when(i == 0)
        def _():
            s_scr[...] = s0_ref[0]

    @pl.when(jnp.logical_and(pl.program_id(0) == 0, i == 0))
    def _():
        row = lax.broadcasted_iota(jnp.int32, (tb, tb), 0)
        col = lax.broadcasted_iota(jnp.int32, (tb, tb), 1)
        diff = row - col
        dist = jnp.maximum(diff, 0).astype(F32)
        for hh in range(B_HEADS):
            dmat_s[hh] = jnp.where(diff >= 0, jnp.exp(RET_LOG_GAMMA[hh] * dist), 0.0)

    rows = nb * nsub * tb
    h = _rms(x_ref[...].reshape(rows, D_MODEL)) * g_ref[...]
    z = _dot(h, w_ref[...])
    cos_a, sin_a = cosa_ref[0], sina_ref[0]
    cos_all = cos_a * cosb_ref[...] - sin_a * sinb_ref[...]
    sin_all = sin_a * cosbs_ref[...] + cos_a * sinbs_ref[...]
    n = lax.broadcasted_iota(jnp.int32, (tb, 1), 0).astype(F32)
    for hh in range(B_HEADS):
        lg = RET_LOG_GAMMA[hh]
        lo = hh * B_HEAD_DIM
        dmat = dmat_s[hh]
        dec_q = jnp.exp(lg * (n + 1.0))
        dec_k = jnp.exp(lg * ((tb - 1.0) - n))
        for bb in range(nb):
            state = s0_ref[bb, hh] if batched else s_scr[hh]
            for sub in range(nsub):
                rs = slice((bb * nsub + sub) * tb, (bb * nsub + sub + 1) * tb)
                pos = slice(sub * tb, (sub + 1) * tb)
                cosf = cos_all[pos, :]
                sinf = sin_all[pos, :]
                q = z[rs, lo:lo + B_HEAD_DIM]
                k = z[rs, B_WIDTH + lo:B_WIDTH + lo + B_HEAD_DIM]
                v = z[rs, 2 * B_WIDTH + lo:2 * B_WIDTH + lo + B_HEAD_DIM]
                gate = z[rs, 3 * B_WIDTH + lo:3 * B_WIDTH + lo + B_HEAD_DIM]
                q = q * cosf + pltpu.roll(q, B_HEAD_DIM // 2, 1) * sinf
                k = (k * cosf + pltpu.roll(k, B_HEAD_DIM // 2, 1) * sinf) * (B_HEAD_DIM ** -0.5)
                scores = _dot(q, k, NT) * dmat
                o = _dot(scores, v) + _dot(q, state) * dec_q
                state = math.exp(lg * tb) * state + _dot(k * dec_k, v, TN)
                o_ref[bb, pos, lo:lo + B_HEAD_DIM] = _rms(o) * (gate * _sigmoid(gate))
            if batched:
                sout_ref[bb, hh] = state
            else:
                s_scr[hh] = state
    if not batched:
        sout_ref[0] = s_scr[...]


def _retention(x, gain, w_b, pos0, s0, *, tb, nb, nsub):
    bsz, seq, _ = x.shape
    assert (seq == tb and bsz % nb == 0 and nsub == 1) if nb > 1 else seq % (nsub * tb) == 0
    sspec = pl.BlockSpec((nb, B_HEADS, B_HEAD_DIM, B_HEAD_DIM), lambda b, i: (b, 0, 0, 0))
    tok = lambda w: pl.BlockSpec((nb, nsub * tb, w), lambda b, i: (b, i, 0))
    step_rows = nsub * tb
    half = B_HEAD_DIM // 2
    inv = ROPE_BASE ** (-jnp.arange(half, dtype=F32) / half)
    base = (pos0 + step_rows * jnp.arange(seq // step_rows, dtype=jnp.int32)).astype(F32)[:, None] * inv[None, :]
    offs = jnp.arange(step_rows, dtype=jnp.int32).astype(F32)[:, None] * inv[None, :]
    both = lambda t: jnp.concatenate([t, t], axis=-1)
    signed = lambda t: jnp.concatenate([-t, t], axis=-1)
    step_row = lambda t: both(t).reshape(-1, 1, B_HEAD_DIM)
    a_spec = pl.BlockSpec((1, 1, B_HEAD_DIM), lambda b, i: (i, 0, 0))
    b_spec = _const_spec((step_rows, B_HEAD_DIM))
    return pl.pallas_call(
        functools.partial(_ret_kernel, tb=tb, nb=nb, nsub=nsub),
        grid=(bsz // nb, seq // (nsub * tb)),
        in_specs=[tok(D_MODEL),
                  _const_spec((1, D_MODEL)), w_b[1],
                  a_spec, a_spec, b_spec, b_spec, b_spec, b_spec,
                  sspec],
        out_specs=[tok(B_WIDTH), sspec],
        out_shape=[jax.ShapeDtypeStruct((bsz, seq, B_WIDTH), F32),
                   jax.ShapeDtypeStruct((bsz, B_HEADS, B_HEAD_DIM, B_HEAD_DIM), F32)],
        scratch_shapes=[pltpu.VMEM((B_HEADS, B_HEAD_DIM, B_HEAD_DIM), F32),
                        pltpu.VMEM((B_HEADS, tb, tb), F32)],
        compiler_params=pltpu.CompilerParams(
            dimension_semantics=("arbitrary", "arbitrary"),
            vmem_limit_bytes=VMEM_LIMIT_BYTES),
        name="retention",
    )(x, gain, w_b[0], step_row(jnp.cos(base)), step_row(jnp.sin(base)),
      both(jnp.cos(offs)), both(jnp.sin(offs)), signed(jnp.cos(offs)), signed(jnp.sin(offs)), s0)


PAIR_W = 2 * C_HEAD_DIM
N_PAIRS = C_HEADS // 2


def _rwkv_kernel(*refs, tt, cc, lookahead, batched):
    if lookahead:
        x_ref, xn_ref = refs[:2]
        refs = refs[2:]
    else:
        x_ref, xn_ref = refs[0], None
        refs = refs[1:]
    (g_ref, w_ref, mu_ref, sh0_ref, s0_ref, w0_ref, w2_ref, a0_ref, a2_ref,
     g2_ref, kk_ref, ka_ref, rk_ref, lnw_ref, lnb_ref, hsum_ref,
     o_ref, sout_ref, shout_ref,
     s_scr, carry, rk_raw, uw_raw, ua_raw, rt_s, kkt_s, kh_s, bh_s, v_s, bon_s, g_s, cum_s) = refs
    i = pl.program_id(1)

    n_chunks = tt // cc

    def project(x, slot):
        h = _rms(x.reshape(tt, D_MODEL)) * g_ref[...]
        cz = _dot(h, w_ref[...])
        rows = lax.broadcasted_iota(jnp.int32, (tt, 1), 0)
        if batched:
            first = jnp.bitwise_and(rows, cc - 1) == 0
            before = jnp.broadcast_to(sh0_ref[...], (n_chunks, cc, C_SHIFT_WIDTH)).reshape(tt, C_SHIFT_WIDTH)
            shout_ref[...] = cz.reshape(n_chunks, cc, C_SHIFT_WIDTH)[:, cc - 1:cc, :]
        else:
            first = rows == 0
            before = carry[...]
            last = cz[tt - 1:tt, :]
            carry[...] = last
            shout_ref[0] = last
        prev = jnp.where(first, before, pltpu.roll(cz, 1, 0))
        cs = cz + (prev - cz) * mu_ref[...]
        off = 3 * C_WIDTH
        w_lo = cs[:, off:off + C_RANK_W]
        a_lo = cs[:, off + C_RANK_W:off + C_RANK_W + C_RANK_A]
        g_lo = cs[:, off + C_RANK_W + C_RANK_A:C_SHIFT_WIDTH]
        rk_raw[...] = cs[:, 0:2 * C_WIDTH]
        v_s[slot] = cs[:, 2 * C_WIDTH:3 * C_WIDTH]
        uw_raw[...] = w0_ref[...] + _dot(jnp.tanh(w_lo), w2_ref[...])
        ua_raw[...] = a0_ref[...] + _dot(a_lo, a2_ref[...])
        g_s[slot] = _dot(_sigmoid(g_lo), g2_ref[...])

    def token_block(b, slot):
        rs = slice(b * cc, (b + 1) * cc)
        r = rk_raw[rs, 0:C_WIDTH]
        k = rk_raw[rs, C_WIDTH:2 * C_WIDTH]
        v = v_s[slot, rs, :]
        lw = -math.exp(-0.5) * _sigmoid(uw_raw[rs, :])
        a = _sigmoid(ua_raw[rs, :])
        hsum = hsum_ref[...]
        kk_raw = k * kk_ref[...]
        kk = kk_raw / jnp.maximum(jnp.sqrt(_head_sum(kk_raw * kk_raw, hsum)), 1e-12)
        k2 = k * (1.0 + (a - 1.0) * ka_ref[...])
        pos = lax.broadcasted_iota(jnp.int32, (cc, 1), 0)
        cum = lw
        step = 1
        while step < cc:
            cum = cum + jnp.where(pos >= step, pltpu.roll(cum, step, 0), 0.0)
            step *= 2
        e_inv = jnp.exp(-cum)
        rt_s[slot, rs, :] = r * jnp.exp(cum)
        kkt_s[slot, rs, :] = kk * jnp.exp(cum - lw)
        kh_s[slot, rs, :] = k2 * e_inv
        bh_s[slot, rs, :] = kk * a * e_inv
        bon_s[slot, rs, :] = _head_sum(r * k2 * rk_ref[...], hsum) * v
        cum_s[slot, rs, :] = cum

    if not batched:
        @pl.when(i == 0)
        def _():
            s_scr[...] = s0_ref[0]
            carry[...] = sh0_ref[0]
            if lookahead:
                project(x_ref[...], 0)
                for b in range(n_chunks):
                    token_block(b, 0)

    if lookahead:
        cur = lax.rem(i, 2)
        nxt = 1 - cur
        project(xn_ref[...], nxt)
        pending = [functools.partial(token_block, b, nxt) for b in range(n_chunks)]
    else:
        cur = 0
        project(x_ref[...], 0)
        for b in range(n_chunks):
            token_block(b, 0)
        pending = []

    def interleave():
        if pending:
            pending.pop(0)()

    two = 2 * cc
    sh = cc.bit_length() - 1
    hd = C_HEAD_DIM.bit_length() - 1
    n_double = sh - 1
    rho = lax.broadcasted_iota(jnp.int32, (two, PAIR_W), 0)
    lane = lax.broadcasted_iota(jnp.int32, (two, PAIR_W), 1)
    placed = (rho >> sh) == (lane >> hd)
    r2 = lax.broadcasted_iota(jnp.int32, (two, two), 0)
    c2 = lax.broadcasted_iota(jnp.int32, (two, two), 1)
    same = (r2 >> sh) == (c2 >> sh)
    strict = jnp.logical_and(same, r2 > c2)
    incl = jnp.logical_and(same, r2 >= c2)
    eye = jnp.where(r2 == c2, 1.0, 0.0).astype(F32)
    gr = lax.broadcasted_iota(jnp.int32, (PAIR_W, PAIR_W), 0)
    gc = lax.broadcasted_iota(jnp.int32, (PAIR_W, PAIR_W), 1)
    hmean = jnp.where((gr >> hd) == (gc >> hd), 1.0 / C_HEAD_DIM, 0.0).astype(BF16)

    def place(xv):
        return jnp.where(placed, jnp.concatenate([xv, xv], axis=0), 0.0)

    per_trip = 4 if n_chunks % 4 == 0 else (2 if n_chunks % 2 == 0 else 1)
    units = [(c, p) for c in range(per_trip) for p in range(N_PAIRS)]
    pairs = range(len(units))

    def chunk(j):
        r0s = [(j * per_trip + c) * cc for c in range(per_trip)]
        w_end = [jnp.exp(cum_s[cur, r0 + cc - 1:r0 + cc, :]) for r0 in r0s]
        lanes = [slice(p * PAIR_W, (p + 1) * PAIR_W) for _, p in units]
        sl = [(cur, slice(r0s[c], r0s[c] + cc), lanes[u]) for u, (c, _) in enumerate(units)]
        we = [w_end[c][:, lanes[u]] for u, (c, _) in enumerate(units)]
        rt = [place(rt_s[sl[p]]) for p in pairs]
        kkt = [place(kkt_s[sl[p]]) for p in pairs]
        kh = [place(kh_s[sl[p]]) for p in pairs]
        bh = [place(bh_s[sl[p]]) for p in pairs]
        vv = [place(v_s[sl[p]]) for p in pairs]
        cat0 = lambda *xs: jnp.concatenate(xs, axis=0)
        cat1 = lambda *xs: jnp.concatenate(xs, axis=1)
        gg = [_dot(cat0(kkt[p], rt[p]), cat0(kh[p], bh[p]), NT) for p in pairs]
        interleave()
        a_kk = [jnp.where(strict, gg[p][:two, :two], 0.0) for p in pairs]
        b_kk = [jnp.where(incl, gg[p][two:, :two], 0.0) for p in pairs]
        b_bb = [jnp.where(incl, gg[p][two:, two:], 0.0) for p in pairs]
        pw = [jnp.where(strict, -gg[p][:two, two:], 0.0) for p in pairs]
        tinv = [eye + pw[p] for p in pairs]
        pw = [_dot(pw[p], pw[p]) for p in pairs]
        for _ in range(n_double - 1):
            pt = [_dot(pw[p], cat1(pw[p], tinv[p])) for p in pairs]
            pw = [pt[p][:, :two] for p in pairs]
            tinv = [tinv[p] + pt[p][:, two:] for p in pairs]
            interleave()
        tinv = [tinv[p] + _dot(pw[p], tinv[p]) for p in pairs]
        interleave()
        av = [_dot(a_kk[p], vv[p]) for p in pairs]
        kv = [_dot(tinv[p], cat1(kkt[p], av[p])) for p in pairs]
        zero = jnp.zeros((two, PAIR_W), F32)
        ry = [_dot(cat1(b_kk[p], -b_bb[p]), cat0(cat1(zero, vv[p]), kv[p])) for p in pairs]
        rp = [rt[p] + ry[p][:, :PAIR_W] for p in pairs]
        y0 = [ry[p][:, PAIR_W:] for p in pairs]
        kkp = [kv[p][:, :PAIR_W] for p in pairs]
        vp = [kv[p][:, PAIR_W:] for p in pairs]
        bd = [bh[p] * we[p] for p in pairs]
        kd = [kh[p] * we[p] for p in pairs]
        q = [_dot(kkp[p], bd[p], TN) for p in pairs]
        z = [_dot(cat0(vv[p], vp[p]), cat0(kd[p], -bd[p]), TN) for p in pairs]
        y2 = []

        def advance(u, s_old):
            y2.append(_dot(rp[u], s_old, NT) + y0[u])
            return s_old * we[u] - _dot(s_old, q[u]) + z[u]

        if batched:
            for u, (c, p) in enumerate(units):
                sout_ref[j * per_trip + c, p] = advance(u, s0_ref[j * per_trip + c, p])
        else:
            state = [s_scr[p] for p in range(N_PAIRS)]
            for u, (_, p) in enumerate(units):
                state[p] = advance(u, state[p])
            for p in range(N_PAIRS):
                s_scr[p] = state[p]
        ys = [y2[u][:cc] + y2[u][cc:] for u in pairs]
        yc = [ys[u] - _dot(ys[u], hmean) for u in pairs]
        var = [_dot(yc[u] * yc[u], hmean) for u in pairs]
        for u, (c, _) in enumerate(units):
            yn = yc[u] * lax.rsqrt(var[u] + C_GN_EPS) * lnw_ref[:, lanes[u]] + lnb_ref[:, lanes[u]]
            out = (yn + bon_s[sl[u]]) * g_s[sl[u]]
            if batched:
                o_ref[j * per_trip + c, :, lanes[u]] = out
            else:
                o_ref[(0,) + sl[u][1:]] = out

    for j in range(n_chunks // per_trip):
        chunk(j)
    while pending:
        interleave()
    if not batched:
        sout_ref[0] = s_scr[...]


def _pair_states(s):
    bsz = s.shape[0]
    s = s.reshape(bsz, N_PAIRS, 2, C_HEAD_DIM, C_HEAD_DIM)
    zero = jnp.zeros_like(s[:, :, 0])
    top = jnp.concatenate([s[:, :, 0], zero], axis=-1)
    bot = jnp.concatenate([zero, s[:, :, 1]], axis=-1)
    return jnp.concatenate([top, bot], axis=-2)


def _unpair_states(s):
    bsz = s.shape[0]
    d = C_HEAD_DIM
    return jnp.stack([s[:, :, :d, :d], s[:, :, d:, d:]], axis=2).reshape(bsz, C_HEADS, d, d)


def _rwkv(x, gain, w_c, lw, shift0, s0, *, tt, cc, nb):
    bsz, seq, _ = x.shape
    batched = nb > 1
    assert cc & (cc - 1) == 0
    assert (seq == cc and tt == nb * cc and bsz % nb == 0) if batched else (seq % tt == 0 and tt % cc == 0)
    row = lambda a: a.reshape(1, -1)
    sspec = pl.BlockSpec((nb, N_PAIRS, PAIR_W, PAIR_W), lambda b, i: (b, 0, 0, 0))
    shspec = pl.BlockSpec((nb, 1, C_SHIFT_WIDTH), lambda b, i: (b, 0, 0))
    vec = _const_spec((1, C_WIDTH))
    n_tiles = 1 if batched else seq // tt
    lookahead = n_tiles > 1
    tile = pltpu.VMEM((2 if lookahead else 1, tt, C_WIDTH), F32)
    hsum = _head_sum_matrix(C_HEAD_DIM)
    tok = lambda w: pl.BlockSpec((nb, tt // nb, w), lambda b, i: (b, i, 0))
    x_specs = [tok(D_MODEL)]
    x_args = [x]
    if lookahead:
        x_specs = [pl.BlockSpec((1, tt, D_MODEL), lambda b, i: (b, 0, 0)),
                   pl.BlockSpec((1, tt, D_MODEL), lambda b, i: (b, jnp.minimum(i + 1, n_tiles - 1), 0))]
        x_args = [x, x]
    oc, s_new, shift_new = pl.pallas_call(
        functools.partial(_rwkv_kernel, tt=tt, cc=cc, lookahead=lookahead, batched=batched),
        grid=(bsz // nb, n_tiles),
        in_specs=x_specs + [
                  _const_spec((1, D_MODEL)), w_c[1],
                  _const_spec((1, C_SHIFT_WIDTH)), shspec, sspec,
                  vec, _const_spec((C_RANK_W, C_WIDTH)), vec, _const_spec((C_RANK_A, C_WIDTH)),
                  _const_spec((C_RANK_G, C_WIDTH)), vec, vec, vec, vec, vec,
                  _const_spec((MXU_TILE, MXU_TILE))],
        out_specs=[tok(C_WIDTH), sspec, shspec],
        out_shape=[jax.ShapeDtypeStruct((bsz, seq, C_WIDTH), F32),
                   jax.ShapeDtypeStruct((bsz, N_PAIRS, PAIR_W, PAIR_W), F32),
                   jax.ShapeDtypeStruct((bsz, 1, C_SHIFT_WIDTH), F32)],
        scratch_shapes=[pltpu.VMEM((N_PAIRS, PAIR_W, PAIR_W), F32),
                        pltpu.VMEM((1, C_SHIFT_WIDTH), F32),
                        pltpu.VMEM((tt, 2 * C_WIDTH), F32),
                        pltpu.VMEM((tt, C_WIDTH), F32),
                        pltpu.VMEM((tt, C_WIDTH), F32)] + [tile] * 8,
        compiler_params=pltpu.CompilerParams(
            dimension_semantics=("arbitrary", "arbitrary"),
            vmem_limit_bytes=VMEM_LIMIT_BYTES),
        name="rwkv",
    )(*x_args, gain, w_c[0], row(lw["c_shift_mu"]), shift0, _pair_states(s0),
      row(lw["c_w0"]), lw["c_w2"].astype(BF16), row(lw["c_a0"]), lw["c_a2"].astype(BF16),
      lw["c_g2"].astype(BF16), row(lw["c_k_k"]), row(lw["c_k_a"]), row(lw["c_r_k"]),
      row(lw["c_ln_w"]), row(lw["c_ln_b"]), hsum)
    return oc, _unpair_states(s_new), shift_new


def _merge_kernel(x_ref, oa_ref, ob_ref, oc_ref, g_ref, wg_ref, wb_ref, wo_ref, y_ref):
    x = x_ref[...]
    h = (_rms(x) * g_ref[...]).astype(BF16)
    branches = [o_ref[...].astype(BF16) for o_ref in (oa_ref, ob_ref, oc_ref)]
    y = x
    for c in range(D_MODEL // MXU_TILE):
        cols = slice(c * MXU_TILE, (c + 1) * MXU_TILE)
        m = None
        for b, o in enumerate(branches):
            gate = _dot(h, wg_ref[:, b * D_MODEL + c * MXU_TILE:b * D_MODEL + (c + 1) * MXU_TILE])
            t = _sigmoid(gate) * _dot(o, wb_ref[b, :, cols])
            m = t if m is None else m + t
        y = y + _dot(m, wo_ref[cols, :])
    y_ref[...] = y


def _merge(x, oa, ob, oc, gain, w_g, w_b, w_o, *, tm):
    rows = x.shape[0]
    assert rows % tm == 0
    tok = lambda w: pl.BlockSpec((tm, w), lambda i: (i, 0))
    return pl.pallas_call(
        _merge_kernel,
        grid=(rows // tm,),
        in_specs=[tok(D_MODEL), tok(A_WIDTH), tok(B_WIDTH), tok(C_WIDTH),
                  _const_spec((1, D_MODEL)), w_g[1], w_b[1], w_o[1]],
        out_specs=tok(D_MODEL),
        out_shape=jax.ShapeDtypeStruct((rows, D_MODEL), F32),
        compiler_params=pltpu.CompilerParams(
            dimension_semantics=("arbitrary",), vmem_limit_bytes=VMEM_LIMIT_BYTES),
        name="merge",
    )(x, oa, ob, oc, gain, w_g[0], w_b[0], w_o[0])


def _ffn_kernel(x_ref, p_ref, g_ref, wgate_ref, wup_ref, wdown_ref, wpp_ref, pg_ref, wpg_ref, y_ref):
    x = x_ref[...]
    hf = (_rms(x) * g_ref[...]).astype(BF16)
    d_ff = wgate_ref.shape[1]
    for c in range(d_ff // MXU_TILE):
        cols = slice(c * MXU_TILE, (c + 1) * MXU_TILE)
        gate = _dot(hf, wgate_ref[:, cols])
        up = _dot(hf, wup_ref[:, cols])
        x = x + _dot(gate * _sigmoid(gate) * up, wdown_ref[cols, :])
    e = _rms(_dot(p_ref[...], wpp_ref[...])) * pg_ref[...]
    y_ref[...] = x + _sigmoid(_dot(_rms(x), wpg_ref[...])) * e


def _ffn(x, p_all, layer, gain, w_gate, w_up, w_down, w_pp, p_gain, w_pg, *, tm):
    rows = x.shape[0]
    ple = p_all.shape[-1]
    assert w_gate[0].shape[-1] % MXU_TILE == 0
    p = p_all.reshape(p_all.shape[0], rows, ple)
    assert rows % tm == 0
    tok = lambda w: pl.BlockSpec((tm, w), lambda i: (i, 0))
    return pl.pallas_call(
        _ffn_kernel,
        grid=(rows // tm,),
        in_specs=[tok(D_MODEL), pl.BlockSpec((None, tm, ple), lambda i: (layer, i, 0)),
                  _const_spec((1, D_MODEL)),
                  w_gate[1], w_up[1], w_down[1], w_pp[1],
                  _const_spec((1, D_MODEL)), w_pg[1]],
        out_specs=tok(D_MODEL),
        out_shape=jax.ShapeDtypeStruct((rows, D_MODEL), F32),
        compiler_params=pltpu.CompilerParams(
            dimension_semantics=("arbitrary",), vmem_limit_bytes=VMEM_LIMIT_BYTES),
        name="ffn",
    )(x, p, gain, w_gate[0], w_up[0], w_down[0], w_pp[0], p_gain, w_pg[0])


def _rel_bias_table(rel_bias, cq):
    nkeys = A_WIN + cq
    t_max = A_WIN + CHUNK - 1 + cq - 1
    heads, n_rel = rel_bias.shape
    w = t_max + 1
    tail = jnp.broadcast_to(rel_bias[:, n_rel - 1:], (heads, w - n_rel))
    rev = jnp.concatenate([tail, rel_bias[:, ::-1].astype(F32), jnp.zeros((heads, 1), F32)], axis=1)
    skew = jnp.tile(rev, (1, cq))[:, :cq * w].reshape(heads, cq, w)
    return skew[:, :, cq - 1:cq - 1 + nkeys] * LOG2_E


def _layer(x, p_all, layer, pos0, a_ck, a_cv, ret_s0, rwkv_s0, shift_prev, lw, big, cfg):
    bsz, seq, _ = x.shape
    row = lambda a: a.reshape(1, -1)
    w_a = _layer_weight(big["w_in"], layer, (OFF_A, OFF_B - OFF_A))
    w_b = _layer_weight(big["w_in"], layer, (OFF_B, OFF_C - OFF_B))
    w_c = _layer_weight(big["w_in"], layer, (OFF_C, OFF_G - OFF_C))
    w_g = _layer_weight(big["w_in"], layer, (OFF_G, IN_WIDTH - OFF_G))
    gain = row(lw["norm_mix"])

    nb = math.gcd(bsz, cfg["nb"])
    bias = _rel_bias_table(lw["a_rel_bias"], cfg["cq"])
    if a_ck is not None:
        a_ck = a_ck.reshape(a_ck.shape[0], bsz, A_WIN, A_WIDTH)
        a_cv = a_cv.reshape(a_cv.shape[0], bsz, A_WIN, A_WIDTH)
    oa, kn, av = _attention(x, gain, w_a, row(lw["a_q_norm"]), row(lw["a_k_norm"]), bias,
                            a_ck, a_cv, layer, tq=cfg["tq"], cq=cfg["cq"], nb=nb)
    keep = min(A_WIN, seq)
    new_ak = kn.reshape(bsz, keep, A_HEADS, A_HEAD_DIM)
    new_av = av.reshape(bsz, keep, A_HEADS, A_HEAD_DIM)

    ob, new_ret = _retention(x, gain, w_b, pos0, ret_s0, tb=cfg["tb"], nb=nb, nsub=cfg["nsub"])

    oc, new_rwkv, new_shift = _rwkv(x, gain, w_c, lw, shift_prev, rwkv_s0,
                                    tt=nb * cfg["tt"], cc=cfg["cc"], nb=nb)

    rows = bsz * seq
    tm = min(cfg["tm"], rows)
    flat = lambda t: t.reshape(rows, t.shape[-1])
    pick = lambda name: _layer_weight(big[name], layer)
    x1 = _merge(flat(x), flat(oa), flat(ob), flat(oc), gain, w_g,
                pick("w_branch"), pick("w_out"), tm=tm)
    x2 = _ffn(x1, p_all, layer, row(lw["norm_ffn"]), pick("w_ffn_gate"), pick("w_ffn_up"),
              pick("w_ffn_down"), pick("w_ple_proj"), row(lw["ple_norm"]), pick("w_ple_gate"),
              tm=min(cfg["tm_ffn"], rows))
    return x2.reshape(bsz, seq, D_MODEL), (new_ak, new_av, new_ret, new_rwkv, new_shift)


def _group_config(seq):
    if seq >= A_WIN:
        return dict(tq=A_WIN, cq=CHUNK, tb=256, nsub=2, tt=512, cc=CHUNK, tm=1024, tm_ffn=1024, nb=1)
    return dict(tq=seq, cq=seq, tb=seq, nsub=1, tt=seq, cc=seq, tm=512, tm_ffn=512, nb=8)


def kernel(x_prompt, x_sample, p_prompt, p_sample, cache_a_k, cache_a_v, state_ret, state_rwkv, state_rwkv_shift, norm_mix, w_in, a_q_norm, a_k_norm, a_rel_bias, c_shift_mu, c_w0, c_w2, c_a0, c_a2, c_g2, c_k_k, c_k_a, c_r_k, c_ln_w, c_ln_b, w_branch, w_out, norm_ffn, w_ffn_gate, w_ffn_up, w_ffn_down, w_ple_proj, ple_norm, w_ple_gate):
    depth = w_in.shape[0]

    def layer_weights(i):
        return dict(norm_mix=norm_mix[i], a_q_norm=a_q_norm[i], a_k_norm=a_k_norm[i],
                    a_rel_bias=a_rel_bias[i], c_shift_mu=c_shift_mu[i], c_w0=c_w0[i], c_w2=c_w2[i],
                    c_a0=c_a0[i], c_a2=c_a2[i], c_g2=c_g2[i], c_k_k=c_k_k[i], c_k_a=c_k_a[i], c_r_k=c_r_k[i],
                    c_ln_w=c_ln_w[i], c_ln_b=c_ln_b[i], norm_ffn=norm_ffn[i], ple_norm=ple_norm[i])

    big = dict(w_in=w_in, w_branch=w_branch, w_out=w_out, w_ffn_gate=w_ffn_gate, w_ffn_up=w_ffn_up,
               w_ffn_down=w_ffn_down, w_ple_proj=w_ple_proj, w_ple_gate=w_ple_gate)
    big = {name: w.astype(BF16) for name, w in big.items()}

    bp, lp, _ = x_prompt.shape
    cfg_p = _group_config(lp)
    ret0 = jnp.zeros((bp, B_HEADS, B_HEAD_DIM, B_HEAD_DIM), F32)
    rwkv0 = jnp.zeros((bp, C_HEADS, C_HEAD_DIM, C_HEAD_DIM), F32)
    shift0 = jnp.zeros((bp, 1, C_SHIFT_WIDTH), F32)
    y_prompt = x_prompt
    st_p = []
    for i in range(depth):
        y_prompt, st = _layer(y_prompt, p_prompt, i, 0, None, None, ret0, rwkv0, shift0,
                              layer_weights(i), big, cfg_p)
        st_p.append(st)

    cfg_s = _group_config(x_sample.shape[1])
    y_sample = x_sample
    st_s = []
    for i in range(depth):
        y_sample, st = _layer(y_sample, p_sample, i, PAST_LEN, cache_a_k, cache_a_v, state_ret[i],
                              state_rwkv[i], state_rwkv_shift[i], layer_weights(i), big, cfg_s)
        st_s.append(st)

    stack = lambda sts, j: jnp.stack([s[j] for s in sts])
    return (y_prompt, y_sample,
            stack(st_p, 0), stack(st_p, 1), stack(st_p, 2), stack(st_p, 3), stack(st_p, 4),
            stack(st_s, 0), stack(st_s, 1), stack(st_s, 2), stack(st_s, 3), stack(st_s, 4))
```

```python
import functools
import math

import jax
import jax.numpy as jnp
from jax import lax
from jax.experimental import pallas as pl
from jax.experimental.pallas import tpu as pltpu

F32 = jnp.float32
BF16 = jnp.bfloat16

D_MODEL = 1024
PAST_LEN = 2048
CHUNK = 64
NORM_EPS = 1e-6

A_HEADS = 8
A_HEAD_DIM = 64
A_WIDTH = 512
A_WIN = 512
A_REL_MAX = 256

B_HEADS = 4
B_HEAD_DIM = 128
B_WIDTH = 512
ROPE_BASE = 10000.0

C_HEADS = 8
C_HEAD_DIM = 64
C_WIDTH = 512
C_RANK_W = 64
C_RANK_A = 64
C_RANK_G = 128
C_SHIFT_WIDTH = 3 * C_WIDTH + C_RANK_W + C_RANK_A + C_RANK_G
C_GN_EPS = 64e-5

N_BRANCHES = 3

OFF_A = 0
OFF_B = 3 * A_WIDTH
OFF_C = OFF_B + 4 * B_WIDTH
OFF_G = OFF_C + C_SHIFT_WIDTH
IN_WIDTH = OFF_G + N_BRANCHES * D_MODEL

RET_LOG_GAMMA = tuple(math.log1p(-(2.0 ** (-5.0 - h))) for h in range(B_HEADS))

VMEM_LIMIT_BYTES = 56 * 1024 * 1024
MXU_TILE = 256
LOG2_E = math.log2(math.e)

NN = (((1,), (0,)), ((), ()))
NT = (((1,), (1,)), ((), ()))
TN = (((0,), (0,)), ((), ()))


def _dot(a, b, dims=NN):
    return lax.dot_general(a.astype(BF16), b.astype(BF16), dims, preferred_element_type=F32)


def _rms(x):
    return x * lax.rsqrt(jnp.mean(x * x, axis=-1, keepdims=True) + NORM_EPS)


def _sigmoid(x):
    return 1.0 / (1.0 + jnp.exp(-x))


def _const_spec(shape):
    nd = len(shape)
    return pl.BlockSpec(shape, lambda *_: (0,) * nd, pipeline_mode=pl.Buffered(1))


def _layer_weight(stacked, layer, cols=None):
    if cols is not None:
        off, width = cols
        if off % width == 0:
            spec = pl.BlockSpec((None, stacked.shape[1], width), lambda *_: (layer, 0, off // width),
                                pipeline_mode=pl.Buffered(1))
            return stacked, spec
        stacked = stacked[:, :, off:off + width]
    shape = stacked.shape[1:]
    spec = pl.BlockSpec((None,) + shape, lambda *_: (layer,) + (0,) * len(shape),
                        pipeline_mode=pl.Buffered(1))
    return stacked, spec


def _head_sum_matrix(head_dim):
    head = jnp.arange(MXU_TILE, dtype=jnp.int32) // head_dim
    return (head[:, None] == head[None, :]).astype(BF16)


def _head_sum(x, hsum):
    w = hsum.shape[0]
    return jnp.concatenate([_dot(x[:, g * w:(g + 1) * w], hsum) for g in range(x.shape[1] // w)], axis=1)


def _attn_kernel(*refs, tq, cq, has_cache, nb):
    if has_cache:
        (x_ref, g_ref, w_ref, qg_ref, kg_ref, bias_ref, hsum_ref, kc_ref, vc_ref,
         o_ref, kn_ref, v_ref, kwin, vwin, qs) = refs
    else:
        (x_ref, g_ref, w_ref, qg_ref, kg_ref, bias_ref, hsum_ref,
         o_ref, kn_ref, v_ref, kwin, vwin, qs) = refs
    i = pl.program_id(1)
    nkeys = A_WIN + cq
    pair_w = 2 * A_HEAD_DIM

    if has_cache:
        kwin[:, 0:A_WIN, :] = kc_ref[...].astype(BF16)
        vwin[:, 0:A_WIN, :] = vc_ref[...].astype(BF16)
    else:
        @pl.when(i == 0)
        def _():
            kwin[0:A_WIN, :] = jnp.zeros((A_WIN, A_WIDTH), BF16)
            vwin[0:A_WIN, :] = jnp.zeros((A_WIN, A_WIDTH), BF16)

    h = _rms(x_ref[...].reshape(nb * tq, D_MODEL)) * g_ref[...]
    z = _dot(h, w_ref[...])
    q = z[:, 0:A_WIDTH]
    k = z[:, A_WIDTH:2 * A_WIDTH]
    v = z[:, 2 * A_WIDTH:3 * A_WIDTH]
    hsum = hsum_ref[...]
    inv_d = 1.0 / A_HEAD_DIM
    qn = q * lax.rsqrt(_head_sum(q * q, hsum) * inv_d + NORM_EPS) * qg_ref[...]
    kn = k * lax.rsqrt(_head_sum(k * k, hsum) * inv_d + NORM_EPS) * kg_ref[...]
    qs[...] = qn * (A_HEAD_DIM ** -0.5 * LOG2_E)
    if has_cache:
        kwin[:, A_WIN:A_WIN + tq, :] = kn.astype(BF16).reshape(nb, tq, A_WIDTH)
        vwin[:, A_WIN:A_WIN + tq, :] = v.astype(BF16).reshape(nb, tq, A_WIDTH)
    else:
        kwin[A_WIN:A_WIN + tq, :] = kn.astype(BF16)
        vwin[A_WIN:A_WIN + tq, :] = v.astype(BF16)
    keep = min(A_WIN, tq)
    kn_ref[...] = kn[nb * tq - nb * keep:].reshape(nb, keep, A_WIDTH)
    v_ref[...] = v[nb * tq - nb * keep:].reshape(nb, keep, A_WIDTH)

    rho = lax.broadcasted_iota(jnp.int32, (2 * cq, pair_w), 0)
    lane = lax.broadcasted_iota(jnp.int32, (2 * cq, pair_w), 1)
    placed = (rho >> (cq.bit_length() - 1)) == (lane >> (A_HEAD_DIM.bit_length() - 1))

    n_chunks = nb * tq // cq
    per_trip = 8 if n_chunks % 8 == 0 else (4 if n_chunks % 4 == 0 else (2 if n_chunks % 2 == 0 else 1))
    units = [(c, p) for c in range(per_trip) for p in range(A_HEADS // 2)]
    ids = range(len(units))

    def chunk(j, carry, mask_past):
        r0s = [pl.multiple_of((j * per_trip + c) * cq, cq) for c in range(per_trip)]
        lanes = [slice(p * pair_w, (p + 1) * pair_w) for _, p in units]
        qc = [qs[pl.ds(r0s[c], cq), lanes[u]] for u, (c, _) in enumerate(units)]
        qp = [jnp.where(placed, jnp.concatenate([qc[u], qc[u]], axis=0), 0.0) for u in ids]
        if has_cache:
            win = [(j * per_trip + c, slice(None), lanes[u]) for u, (c, _) in enumerate(units)]
        else:
            win = [(pl.ds(r0s[c], nkeys), lanes[u]) for u, (c, _) in enumerate(units)]
        s = [_dot(qp[u], kwin[win[u]], NT) + bias_ref[p] for u, (_, p) in enumerate(units)]
        if mask_past:
            col = lax.broadcasted_iota(jnp.int32, (2 * cq, nkeys), 1)
            s = [jnp.where(r0s[c] + col >= A_WIN, s[u], -1e30) for u, (c, _) in enumerate(units)]
        m = [jnp.max(s[u], axis=-1, keepdims=True) for u in ids]
        e = [jnp.exp2(s[u] - m[u]) for u in ids]
        den = [jnp.sum(e[u], axis=-1, keepdims=True) for u in ids]
        o2 = [_dot(e[u], vwin[win[u]]) for u in ids]
        for u, (c, _) in enumerate(units):
            o = jnp.where(placed, o2[u] / den[u], 0.0)
            if has_cache:
                o_ref[j * per_trip + c, :, lanes[u]] = o[:cq] + o[cq:]
            else:
                o_ref[0, pl.ds(r0s[c], cq), lanes[u]] = o[:cq] + o[cq:]
        return carry

    n_trips = n_chunks // per_trip
    if has_cache:
        lax.fori_loop(0, n_trips, functools.partial(chunk, mask_past=False), 0)
    else:
        @pl.when(i == 0)
        def _():
            lax.fori_loop(0, n_trips, functools.partial(chunk, mask_past=True), 0)

        @pl.when(i > 0)
        def _():
            lax.fori_loop(0, n_trips, functools.partial(chunk, mask_past=False), 0)

    if not has_cache:
        kwin[0:A_WIN, :] = kwin[tq:tq + A_WIN, :]
        vwin[0:A_WIN, :] = vwin[tq:tq + A_WIN, :]


def _attention(x, gain, w_a, q_gain, k_gain, bias, cache_k, cache_v, layer, *, tq, cq, nb):
    bsz, seq, _ = x.shape
    has_cache = cache_k is not None
    assert seq % tq == 0 and tq % cq == 0
    assert (seq == tq == cq <= A_WIN and bsz % nb == 0) if has_cache else (tq % A_WIN == 0 and nb == 1)
    keep = min(A_WIN, tq)
    nkeys = A_WIN + cq
    tok = lambda w: pl.BlockSpec((nb, tq, w), lambda b, i: (b, i, 0))
    win_shape = (nb, nkeys, A_WIDTH) if has_cache else (A_WIN + tq, A_WIDTH)
    in_specs = [tok(D_MODEL), _const_spec((1, D_MODEL)), w_a[1],
                _const_spec((1, A_WIDTH)), _const_spec((1, A_WIDTH)),
                _const_spec((A_HEADS // 2, 2 * cq, nkeys)), _const_spec((MXU_TILE, MXU_TILE))]
    args = [x, gain, w_a[0], jnp.tile(q_gain, (1, A_HEADS)), jnp.tile(k_gain, (1, A_HEADS)),
            bias.reshape(A_HEADS // 2, 2 * cq, nkeys), _head_sum_matrix(A_HEAD_DIM)]
    if has_cache:
        cspec = pl.BlockSpec((None, nb, A_WIN, A_WIDTH), lambda b, i: (layer, b, 0, 0))
        in_specs += [cspec, cspec]
        args += [cache_k, cache_v]
    out = jax.ShapeDtypeStruct((bsz, seq, A_WIDTH), F32)
    return pl.pallas_call(
        functools.partial(_attn_kernel, tq=tq, cq=cq, has_cache=has_cache, nb=nb),
        grid=(bsz // nb, seq // tq),
        in_specs=in_specs,
        out_specs=[tok(A_WIDTH)] + [pl.BlockSpec((nb, keep, A_WIDTH), lambda b, i: (b, 0, 0))] * 2,
        out_shape=[out] + [jax.ShapeDtypeStruct((bsz, keep, A_WIDTH), F32)] * 2,
        scratch_shapes=[pltpu.VMEM(win_shape, BF16),
                        pltpu.VMEM(win_shape, BF16),
                        pltpu.VMEM((nb * tq, A_WIDTH), F32)],
        compiler_params=pltpu.CompilerParams(
            dimension_semantics=("arbitrary", "arbitrary"),
            vmem_limit_bytes=VMEM_LIMIT_BYTES),
        name="attention",
    )(*args)


def _ret_kernel(x_ref, g_ref, w_ref, cosa_ref, sina_ref, cosb_ref, sinb_ref, cosbs_ref, sinbs_ref,
                s0_ref, o_ref, sout_ref, s_scr, dmat_s, *, tb, nb, nsub):
    batched = nb > 1
    i = pl.program_id(1)

    if not batched:
        @pl.when(i == 0)
        def _():
            s_scr[...] = s0_ref[0]

    @pl.when(jnp.logical_and(pl.program_id(0) == 0, i == 0))
    def _():
        row = lax.broadcasted_iota(jnp.int32, (tb, tb), 0)
        col = lax.broadcasted_iota(jnp.int32, (tb, tb), 1)
        diff = row - col
        dist = jnp.maximum(diff, 0).astype(F32)
        for hh in range(B_HEADS):
            dmat_s[hh] = jnp.where(diff >= 0, jnp.exp(RET_LOG_GAMMA[hh] * dist), 0.0)

    rows = nb * nsub * tb
    h = _rms(x_ref[...].reshape(rows, D_MODEL)) * g_ref[...]
    z = _dot(h, w_ref[...])
    cos_a, sin_a = cosa_ref[0], sina_ref[0]
    cos_all = cos_a * cosb_ref[...] - sin_a * sinb_ref[...]
    sin_all = sin_a * cosbs_ref[...] + cos_a * sinbs_ref[...]
    n = lax.broadcasted_iota(jnp.int32, (tb, 1), 0).astype(F32)
    for hh in range(B_HEADS):
        lg = RET_LOG_GAMMA[hh]
        lo = hh * B_HEAD_DIM
        dmat = dmat_s[hh]
        dec_q = jnp.exp(lg * (n + 1.0))
        dec_k = jnp.exp(lg * ((tb - 1.0) - n))
        for bb in range(nb):
            state = s0_ref[bb, hh] if batched else s_scr[hh]
            for sub in range(nsub):
                rs = slice((bb * nsub + sub) * tb, (bb * nsub + sub + 1) * tb)
                pos = slice(sub * tb, (sub + 1) * tb)
                cosf = cos_all[pos, :]
                sinf = sin_all[pos, :]
                q = z[rs, lo:lo + B_HEAD_DIM]
                k = z[rs, B_WIDTH + lo:B_WIDTH + lo + B_HEAD_DIM]
                v = z[rs, 2 * B_WIDTH + lo:2 * B_WIDTH + lo + B_HEAD_DIM]
                gate = z[rs, 3 * B_WIDTH + lo:3 * B_WIDTH + lo + B_HEAD_DIM]
                q = q * cosf + pltpu.roll(q, B_HEAD_DIM // 2, 1) * sinf
                k = (k * cosf + pltpu.roll(k, B_HEAD_DIM // 2, 1) * sinf) * (B_HEAD_DIM ** -0.5)
                scores = _dot(q, k, NT) * dmat
                o = _dot(scores, v) + _dot(q, state) * dec_q
                state = math.exp(lg * tb) * state + _dot(k * dec_k, v, TN)
                o_ref[bb, pos, lo:lo + B_HEAD_DIM] = _rms(o) * (gate * _sigmoid(gate))
            if batched:
                sout_ref[bb, hh] = state
            else:
                s_scr[hh] = state
    if not batched:
        sout_ref[0] = s_scr[...]


def _retention(x, gain, w_b, pos0, s0, *, tb, nb, nsub):
    bsz, seq, _ = x.shape
    assert (seq == tb and bsz % nb == 0 and nsub == 1) if nb > 1 else seq % (nsub * tb) == 0
    sspec = pl.BlockSpec((nb, B_HEADS, B_HEAD_DIM, B_HEAD_DIM), lambda b, i: (b, 0, 0, 0))
    tok = lambda w: pl.BlockSpec((nb, nsub * tb, w), lambda b, i: (b, i, 0))
    step_rows = nsub * tb
    half = B_HEAD_DIM // 2
    inv = ROPE_BASE ** (-jnp.arange(half, dtype=F32) / half)
    base = (pos0 + step_rows * jnp.arange(seq // step_rows, dtype=jnp.int32)).astype(F32)[:, None] * inv[None, :]
    offs = jnp.arange(step_rows, dtype=jnp.int32).astype(F32)[:, None] * inv[None, :]
    both = lambda t: jnp.concatenate([t, t], axis=-1)
    signed = lambda t: jnp.concatenate([-t, t], axis=-1)
    step_row = lambda t: both(t).reshape(-1, 1, B_HEAD_DIM)
    a_spec = pl.BlockSpec((1, 1, B_HEAD_DIM), lambda b, i: (i, 0, 0))
    b_spec = _const_spec((step_rows, B_HEAD_DIM))
    return pl.pallas_call(
        functools.partial(_ret_kernel, tb=tb, nb=nb, nsub=nsub),
        grid=(bsz // nb, seq // (nsub * tb)),
        in_specs=[tok(D_MODEL),
                  _const_spec((1, D_MODEL)), w_b[1],
                  a_spec, a_spec, b_spec, b_spec, b_spec, b_spec,
                  sspec],
        out_specs=[tok(B_WIDTH), sspec],
        out_shape=[jax.ShapeDtypeStruct((bsz, seq, B_WIDTH), F32),
                   jax.ShapeDtypeStruct((bsz, B_HEADS, B_HEAD_DIM, B_HEAD_DIM), F32)],
        scratch_shapes=[pltpu.VMEM((B_HEADS, B_HEAD_DIM, B_HEAD_DIM), F32),
                        pltpu.VMEM((B_HEADS, tb, tb), F32)],
        compiler_params=pltpu.CompilerParams(
            dimension_semantics=("arbitrary", "arbitrary"),
            vmem_limit_bytes=VMEM_LIMIT_BYTES),
        name="retention",
    )(x, gain, w_b[0], step_row(jnp.cos(base)), step_row(jnp.sin(base)),
      both(jnp.cos(offs)), both(jnp.sin(offs)), signed(jnp.cos(offs)), signed(jnp.sin(offs)), s0)


PAIR_W = 2 * C_HEAD_DIM
N_PAIRS = C_HEADS // 2


def _rwkv_kernel(*refs, tt, cc, lookahead, batched):
    if lookahead:
        x_ref, xn_ref = refs[:2]
        refs = refs[2:]
    else:
        x_ref, xn_ref = refs[0], None
        refs = refs[1:]
    (g_ref, w_ref, mu_ref, sh0_ref, s0_ref, w0_ref, w2_ref, a0_ref, a2_ref,
     g2_ref, kk_ref, ka_ref, rk_ref, lnw_ref, lnb_ref, hsum_ref,
     o_ref, sout_ref, shout_ref,
     s_scr, carry, rk_raw, uw_raw, ua_raw, rt_s, kkt_s, kh_s, bh_s, v_s, bon_s, g_s, cum_s) = refs
    i = pl.program_id(1)

    n_chunks = tt // cc

    def project(x, slot):
        h = _rms(x.reshape(tt, D_MODEL)) * g_ref[...]
        cz = _dot(h, w_ref[...])
        rows = lax.broadcasted_iota(jnp.int32, (tt, 1), 0)
        if batched:
            first = jnp.bitwise_and(rows, cc - 1) == 0
            before = jnp.broadcast_to(sh0_ref[...], (n_chunks, cc, C_SHIFT_WIDTH)).reshape(tt, C_SHIFT_WIDTH)
            shout_ref[...] = cz.reshape(n_chunks, cc, C_SHIFT_WIDTH)[:, cc - 1:cc, :]
        else:
            first = rows == 0
            before = carry[...]
            last = cz[tt - 1:tt, :]
            carry[...] = last
            shout_ref[0] = last
        prev = jnp.where(first, before, pltpu.roll(cz, 1, 0))
        cs = cz + (prev - cz) * mu_ref[...]
        off = 3 * C_WIDTH
        w_lo = cs[:, off:off + C_RANK_W]
        a_lo = cs[:, off + C_RANK_W:off + C_RANK_W + C_RANK_A]
        g_lo = cs[:, off + C_RANK_W + C_RANK_A:C_SHIFT_WIDTH]
        rk_raw[...] = cs[:, 0:2 * C_WIDTH]
        v_s[slot] = cs[:, 2 * C_WIDTH:3 * C_WIDTH]
        uw_raw[...] = w0_ref[...] + _dot(jnp.tanh(w_lo), w2_ref[...])
        ua_raw[...] = a0_ref[...] + _dot(a_lo, a2_ref[...])
        g_s[slot] = _dot(_sigmoid(g_lo), g2_ref[...])

    def token_block(b, slot):
        rs = slice(b * cc, (b + 1) * cc)
        r = rk_raw[rs, 0:C_WIDTH]
        k = rk_raw[rs, C_WIDTH:2 * C_WIDTH]
        v = v_s[slot, rs, :]
        lw = -math.exp(-0.5) * _sigmoid(uw_raw[rs, :])
        a = _sigmoid(ua_raw[rs, :])
        hsum = hsum_ref[...]
        kk_raw = k * kk_ref[...]
        kk = kk_raw / jnp.maximum(jnp.sqrt(_head_sum(kk_raw * kk_raw, hsum)), 1e-12)
        k2 = k * (1.0 + (a - 1.0) * ka_ref[...])
        pos = lax.broadcasted_iota(jnp.int32, (cc, 1), 0)
        cum = lw
        step = 1
        while step < cc:
            cum = cum + jnp.where(pos >= step, pltpu.roll(cum, step, 0), 0.0)
            step *= 2
        e_inv = jnp.exp(-cum)
        rt_s[slot, rs, :] = r * jnp.exp(cum)
        kkt_s[slot, rs, :] = kk * jnp.exp(cum - lw)
        kh_s[slot, rs, :] = k2 * e_inv
        bh_s[slot, rs, :] = kk * a * e_inv
        bon_s[slot, rs, :] = _head_sum(r * k2 * rk_ref[...], hsum) * v
        cum_s[slot, rs, :] = cum

    if not batched:
        @pl.when(i == 0)
        def _():
            s_scr[...] = s0_ref[0]
            carry[...] = sh0_ref[0]
            if lookahead:
                project(x_ref[...], 0)
                for b in range(n_chunks):
                    token_block(b, 0)

    if lookahead:
        cur = lax.rem(i, 2)
        nxt = 1 - cur
        project(xn_ref[...], nxt)
        pending = [functools.partial(token_block, b, nxt) for b in range(n_chunks)]
    else:
        cur = 0
        project(x_ref[...], 0)
        for b in range(n_chunks):
            token_block(b, 0)
        pending = []

    def interleave():
        if pending:
            pending.pop(0)()

    two = 2 * cc
    sh = cc.bit_length() - 1
    hd = C_HEAD_DIM.bit_length() - 1
    n_double = sh - 1
    rho = lax.broadcasted_iota(jnp.int32, (two, PAIR_W), 0)
    lane = lax.broadcasted_iota(jnp.int32, (two, PAIR_W), 1)
    placed = (rho >> sh) == (lane >> hd)
    r2 = lax.broadcasted_iota(jnp.int32, (two, two), 0)
    c2 = lax.broadcasted_iota(jnp.int32, (two, two), 1)
    same = (r2 >> sh) == (c2 >> sh)
    strict = jnp.logical_and(same, r2 > c2)
    incl = jnp.logical_and(same, r2 >= c2)
    eye = jnp.where(r2 == c2, 1.0, 0.0).astype(F32)
    gr = lax.broadcasted_iota(jnp.int32, (PAIR_W, PAIR_W), 0)
    gc = lax.broadcasted_iota(jnp.int32, (PAIR_W, PAIR_W), 1)
    hmean = jnp.where((gr >> hd) == (gc >> hd), 1.0 / C_HEAD_DIM, 0.0).astype(BF16)

    def place(xv):
        return jnp.where(placed, jnp.concatenate([xv, xv], axis=0), 0.0)

    per_trip = 4 if n_chunks % 4 == 0 else (2 if n_chunks % 2 == 0 else 1)
    units = [(c, p) for c in range(per_trip) for p in range(N_PAIRS)]
    pairs = range(len(units))

    def chunk(j):
        r0s = [(j * per_trip + c) * cc for c in range(per_trip)]
        w_end = [jnp.exp(cum_s[cur, r0 + cc - 1:r0 + cc, :]) for r0 in r0s]
        lanes = [slice(p * PAIR_W, (p + 1) * PAIR_W) for _, p in units]
        sl = [(cur, slice(r0s[c], r0s[c] + cc), lanes[u]) for u, (c, _) in enumerate(units)]
        we = [w_end[c][:, lanes[u]] for u, (c, _) in enumerate(units)]
        rt = [place(rt_s[sl[p]]) for p in pairs]
        kkt = [place(kkt_s[sl[p]]) for p in pairs]
        kh = [place(kh_s[sl[p]]) for p in pairs]
        bh = [place(bh_s[sl[p]]) for p in pairs]
        vv = [place(v_s[sl[p]]) for p in pairs]
        cat0 = lambda *xs: jnp.concatenate(xs, axis=0)
        cat1 = lambda *xs: jnp.concatenate(xs, axis=1)
        gg = [_dot(cat0(kkt[p], rt[p]), cat0(kh[p], bh[p]), NT) for p in pairs]
        interleave()
        a_kk = [jnp.where(strict, gg[p][:two, :two], 0.0) for p in pairs]
        b_kk = [jnp.where(incl, gg[p][two:, :two], 0.0) for p in pairs]
        b_bb = [jnp.where(incl, gg[p][two:, two:], 0.0) for p in pairs]
        pw = [jnp.where(strict, -gg[p][:two, two:], 0.0) for p in pairs]
        tinv = [eye + pw[p] for p in pairs]
        pw = [_dot(pw[p], pw[p]) for p in pairs]
        for _ in range(n_double - 1):
            pt = [_dot(pw[p], cat1(pw[p], tinv[p])) for p in pairs]
            pw = [pt[p][:, :two] for p in pairs]
            tinv = [tinv[p] + pt[p][:, two:] for p in pairs]
            interleave()
        tinv = [tinv[p] + _dot(pw[p], tinv[p]) for p in pairs]
        interleave()
        av = [_dot(a_kk[p], vv[p]) for p in pairs]
        kv = [_dot(tinv[p], cat1(kkt[p], av[p])) for p in pairs]
        zero = jnp.zeros((two, PAIR_W), F32)
        ry = [_dot(cat1(b_kk[p], -b_bb[p]), cat0(cat1(zero, vv[p]), kv[p])) for p in pairs]
        rp = [rt[p] + ry[p][:, :PAIR_W] for p in pairs]
        y0 = [ry[p][:, PAIR_W:] for p in pairs]
        kkp = [kv[p][:, :PAIR_W] for p in pairs]
        vp = [kv[p][:, PAIR_W:] for p in pairs]
        bd = [bh[p] * we[p] for p in pairs]
        kd = [kh[p] * we[p] for p in pairs]
        q = [_dot(kkp[p], bd[p], TN) for p in pairs]
        z = [_dot(cat0(vv[p], vp[p]), cat0(kd[p], -bd[p]), TN) for p in pairs]
        y2 = []

        def advance(u, s_old):
            y2.append(_dot(rp[u], s_old, NT) + y0[u])
            return s_old * we[u] - _dot(s_old, q[u]) + z[u]

        if batched:
            for u, (c, p) in enumerate(units):
                sout_ref[j * per_trip + c, p] = advance(u, s0_ref[j * per_trip + c, p])
        else:
            state = [s_scr[p] for p in range(N_PAIRS)]
            for u, (_, p) in enumerate(units):
                state[p] = advance(u, state[p])
            for p in range(N_PAIRS):
                s_scr[p] = state[p]
        ys = [y2[u][:cc] + y2[u][cc:] for u in pairs]
        yc = [ys[u] - _dot(ys[u], hmean) for u in pairs]
        var = [_dot(yc[u] * yc[u], hmean) for u in pairs]
        for u, (c, _) in enumerate(units):
            yn = yc[u] * lax.rsqrt(var[u] + C_GN_EPS) * lnw_ref[:, lanes[u]] + lnb_ref[:, lanes[u]]
            out = (yn + bon_s[sl[u]]) * g_s[sl[u]]
            if batched:
                o_ref[j * per_trip + c, :, lanes[u]] = out
            else:
                o_ref[(0,) + sl[u][1:]] = out

    for j in range(n_chunks // per_trip):
        chunk(j)
    while pending:
        interleave()
    if not batched:
        sout_ref[0] = s_scr[...]


def _pair_states(s):
    bsz = s.shape[0]
    s = s.reshape(bsz, N_PAIRS, 2, C_HEAD_DIM, C_HEAD_DIM)
    zero = jnp.zeros_like(s[:, :, 0])
    top = jnp.concatenate([s[:, :, 0], zero], axis=-1)
    bot = jnp.concatenate([zero, s[:, :, 1]], axis=-1)
    return jnp.concatenate([top, bot], axis=-2)


def _unpair_states(s):
    bsz = s.shape[0]
    d = C_HEAD_DIM
    return jnp.stack([s[:, :, :d, :d], s[:, :, d:, d:]], axis=2).reshape(bsz, C_HEADS, d, d)


def _rwkv(x, gain, w_c, lw, shift0, s0, *, tt, cc, nb):
    bsz, seq, _ = x.shape
    batched = nb > 1
    assert cc & (cc - 1) == 0
    assert (seq == cc and tt == nb * cc and bsz % nb == 0) if batched else (seq % tt == 0 and tt % cc == 0)
    row = lambda a: a.reshape(1, -1)
    sspec = pl.BlockSpec((nb, N_PAIRS, PAIR_W, PAIR_W), lambda b, i: (b, 0, 0, 0))
    shspec = pl.BlockSpec((nb, 1, C_SHIFT_WIDTH), lambda b, i: (b, 0, 0))
    vec = _const_spec((1, C_WIDTH))
    n_tiles = 1 if batched else seq // tt
    lookahead = n_tiles > 1
    tile = pltpu.VMEM((2 if lookahead else 1, tt, C_WIDTH), F32)
    hsum = _head_sum_matrix(C_HEAD_DIM)
    tok = lambda w: pl.BlockSpec((nb, tt // nb, w), lambda b, i: (b, i, 0))
    x_specs = [tok(D_MODEL)]
    x_args = [x]
    if lookahead:
        x_specs = [pl.BlockSpec((1, tt, D_MODEL), lambda b, i: (b, 0, 0)),
                   pl.BlockSpec((1, tt, D_MODEL), lambda b, i: (b, jnp.minimum(i + 1, n_tiles - 1), 0))]
        x_args = [x, x]
    oc, s_new, shift_new = pl.pallas_call(
        functools.partial(_rwkv_kernel, tt=tt, cc=cc, lookahead=lookahead, batched=batched),
        grid=(bsz // nb, n_tiles),
        in_specs=x_specs + [
                  _const_spec((1, D_MODEL)), w_c[1],
                  _const_spec((1, C_SHIFT_WIDTH)), shspec, sspec,
                  vec, _const_spec((C_RANK_W, C_WIDTH)), vec, _const_spec((C_RANK_A, C_WIDTH)),
                  _const_spec((C_RANK_G, C_WIDTH)), vec, vec, vec, vec, vec,
                  _const_spec((MXU_TILE, MXU_TILE))],
        out_specs=[tok(C_WIDTH), sspec, shspec],
        out_shape=[jax.ShapeDtypeStruct((bsz, seq, C_WIDTH), F32),
                   jax.ShapeDtypeStruct((bsz, N_PAIRS, PAIR_W, PAIR_W), F32),
                   jax.ShapeDtypeStruct((bsz, 1, C_SHIFT_WIDTH), F32)],
        scratch_shapes=[pltpu.VMEM((N_PAIRS, PAIR_W, PAIR_W), F32),
                        pltpu.VMEM((1, C_SHIFT_WIDTH), F32),
                        pltpu.VMEM((tt, 2 * C_WIDTH), F32),
                        pltpu.VMEM((tt, C_WIDTH), F32),
                        pltpu.VMEM((tt, C_WIDTH), F32)] + [tile] * 8,
        compiler_params=pltpu.CompilerParams(
            dimension_semantics=("arbitrary", "arbitrary"),
            vmem_limit_bytes=VMEM_LIMIT_BYTES),
        name="rwkv",
    )(*x_args, gain, w_c[0], row(lw["c_shift_mu"]), shift0, _pair_states(s0),
      row(lw["c_w0"]), lw["c_w2"].astype(BF16), row(lw["c_a0"]), lw["c_a2"].astype(BF16),
      lw["c_g2"].astype(BF16), row(lw["c_k_k"]), row(lw["c_k_a"]), row(lw["c_r_k"]),
      row(lw["c_ln_w"]), row(lw["c_ln_b"]), hsum)
    return oc, _unpair_states(s_new), shift_new


def _merge_kernel(x_ref, oa_ref, ob_ref, oc_ref, g_ref, wg_ref, wb_ref, wo_ref, y_ref):
    x = x_ref[...]
    h = (_rms(x) * g_ref[...]).astype(BF16)
    branches = [o_ref[...].astype(BF16) for o_ref in (oa_ref, ob_ref, oc_ref)]
    y = x
    for c in range(D_MODEL // MXU_TILE):
        cols = slice(c * MXU_TILE, (c + 1) * MXU_TILE)
        m = None
        for b, o in enumerate(branches):
            gate = _dot(h, wg_ref[:, b * D_MODEL + c * MXU_TILE:b * D_MODEL + (c + 1) * MXU_TILE])
            t = _sigmoid(gate) * _dot(o, wb_ref[b, :, cols])
            m = t if m is None else m + t
        y = y + _dot(m, wo_ref[cols, :])
    y_ref[...] = y


def _merge(x, oa, ob, oc, gain, w_g, w_b, w_o, *, tm):
    rows = x.shape[0]
    assert rows % tm == 0
    tok = lambda w: pl.BlockSpec((tm, w), lambda i: (i, 0))
    return pl.pallas_call(
        _merge_kernel,
        grid=(rows // tm,),
        in_specs=[tok(D_MODEL), tok(A_WIDTH), tok(B_WIDTH), tok(C_WIDTH),
                  _const_spec((1, D_MODEL)), w_g[1], w_b[1], w_o[1]],
        out_specs=tok(D_MODEL),
        out_shape=jax.ShapeDtypeStruct((rows, D_MODEL), F32),
        compiler_params=pltpu.CompilerParams(
            dimension_semantics=("arbitrary",), vmem_limit_bytes=VMEM_LIMIT_BYTES),
        name="merge",
    )(x, oa, ob, oc, gain, w_g[0], w_b[0], w_o[0])


def _ffn_kernel(x_ref, p_ref, g_ref, wgate_ref, wup_ref, wdown_ref, wpp_ref, pg_ref, wpg_ref, y_ref):
    x = x_ref[...]
    hf = (_rms(x) * g_ref[...]).astype(BF16)
    d_ff = wgate_ref.shape[1]
    for c in range(d_ff // MXU_TILE):
        cols = slice(c * MXU_TILE, (c + 1) * MXU_TILE)
        gate = _dot(hf, wgate_ref[:, cols])
        up = _dot(hf, wup_ref[:, cols])
        x = x + _dot(gate * _sigmoid(gate) * up, wdown_ref[cols, :])
    e = _rms(_dot(p_ref[...], wpp_ref[...])) * pg_ref[...]
    y_ref[...] = x + _sigmoid(_dot(_rms(x), wpg_ref[...])) * e


def _ffn(x, p_all, layer, gain, w_gate, w_up, w_down, w_pp, p_gain, w_pg, *, tm):
    rows = x.shape[0]
    ple = p_all.shape[-1]
    assert w_gate[0].shape[-1] % MXU_TILE == 0
    p = p_all.reshape(p_all.shape[0], rows, ple)
    assert rows % tm == 0
    tok = lambda w: pl.BlockSpec((tm, w), lambda i: (i, 0))
    return pl.pallas_call(
        _ffn_kernel,
        grid=(rows // tm,),
        in_specs=[tok(D_MODEL), pl.BlockSpec((None, tm, ple), lambda i: (layer, i, 0)),
                  _const_spec((1, D_MODEL)),
                  w_gate[1], w_up[1], w_down[1], w_pp[1],
                  _const_spec((1, D_MODEL)), w_pg[1]],
        out_specs=tok(D_MODEL),
        out_shape=jax.ShapeDtypeStruct((rows, D_MODEL), F32),
        compiler_params=pltpu.CompilerParams(
            dimension_semantics=("arbitrary",), vmem_limit_bytes=VMEM_LIMIT_BYTES),
        name="ffn",
    )(x, p, gain, w_gate[0], w_up[0], w_down[0], w_pp[0], p_gain, w_pg[0])


def _rel_bias_table(rel_bias, cq):
    nkeys = A_WIN + cq
    t_max = A_WIN + CHUNK - 1 + cq - 1
    heads, n_rel = rel_bias.shape
    w = t_max + 1
    tail = jnp.broadcast_to(rel_bias[:, n_rel - 1:], (heads, w - n_rel))
    rev = jnp.concatenate([tail, rel_bias[:, ::-1].astype(F32), jnp.zeros((heads, 1), F32)], axis=1)
    skew = jnp.tile(rev, (1, cq))[:, :cq * w].reshape(heads, cq, w)
    return skew[:, :, cq - 1:cq - 1 + nkeys] * LOG2_E


def _layer(x, p_all, layer, pos0, a_ck, a_cv, ret_s0, rwkv_s0, shift_prev, lw, big, cfg):
    bsz, seq, _ = x.shape
    row = lambda a: a.reshape(1, -1)
    w_a = _layer_weight(big["w_in"], layer, (OFF_A, OFF_B - OFF_A))
    w_b = _layer_weight(big["w_in"], layer, (OFF_B, OFF_C - OFF_B))
    w_c = _layer_weight(big["w_in"], layer, (OFF_C, OFF_G - OFF_C))
    w_g = _layer_weight(big["w_in"], layer, (OFF_G, IN_WIDTH - OFF_G))
    gain = row(lw["norm_mix"])

    nb = math.gcd(bsz, cfg["nb"])
    bias = _rel_bias_table(lw["a_rel_bias"], cfg["cq"])
    if a_ck is not None:
        a_ck = a_ck.reshape(a_ck.shape[0], bsz, A_WIN, A_WIDTH)
        a_cv = a_cv.reshape(a_cv.shape[0], bsz, A_WIN, A_WIDTH)
    oa, kn, av = _attention(x, gain, w_a, row(lw["a_q_norm"]), row(lw["a_k_norm"]), bias,
                            a_ck, a_cv, layer, tq=cfg["tq"], cq=cfg["cq"], nb=nb)
    keep = min(A_WIN, seq)
    new_ak = kn.reshape(bsz, keep, A_HEADS, A_HEAD_DIM)
    new_av = av.reshape(bsz, keep, A_HEADS, A_HEAD_DIM)

    ob, new_ret = _retention(x, gain, w_b, pos0, ret_s0, tb=cfg["tb"], nb=nb, nsub=cfg["nsub"])

    oc, new_rwkv, new_shift = _rwkv(x, gain, w_c, lw, shift_prev, rwkv_s0,
                                    tt=nb * cfg["tt"], cc=cfg["cc"], nb=nb)

    rows = bsz * seq
    tm = min(cfg["tm"], rows)
    flat = lambda t: t.reshape(rows, t.shape[-1])
    pick = lambda name: _layer_weight(big[name], layer)
    x1 = _merge(flat(x), flat(oa), flat(ob), flat(oc), gain, w_g,
                pick("w_branch"), pick("w_out"), tm=tm)
    x2 = _ffn(x1, p_all, layer, row(lw["norm_ffn"]), pick("w_ffn_gate"), pick("w_ffn_up"),
              pick("w_ffn_down"), pick("w_ple_proj"), row(lw["ple_norm"]), pick("w_ple_gate"),
              tm=min(cfg["tm_ffn"], rows))
    return x2.reshape(bsz, seq, D_MODEL), (new_ak, new_av, new_ret, new_rwkv, new_shift)


def _group_config(seq):
    if seq >= A_WIN:
        tq = 2 * A_WIN if seq % (2 * A_WIN) == 0 else A_WIN
        return dict(tq=tq, cq=CHUNK, tb=256, nsub=4, tt=512, cc=CHUNK, tm=1024, tm_ffn=1024, nb=1)
    return dict(tq=seq, cq=seq, tb=seq, nsub=1, tt=seq, cc=seq, tm=512, tm_ffn=512, nb=8)


def kernel(x_prompt, x_sample, p_prompt, p_sample, cache_a_k, cache_a_v, state_ret, state_rwkv, state_rwkv_shift, norm_mix, w_in, a_q_norm, a_k_norm, a_rel_bias, c_shift_mu, c_w0, c_w2, c_a0, c_a2, c_g2, c_k_k, c_k_a, c_r_k, c_ln_w, c_ln_b, w_branch, w_out, norm_ffn, w_ffn_gate, w_ffn_up, w_ffn_down, w_ple_proj, ple_norm, w_ple_gate):
    depth = w_in.shape[0]

    def layer_weights(i):
        return dict(norm_mix=norm_mix[i], a_q_norm=a_q_norm[i], a_k_norm=a_k_norm[i],
                    a_rel_bias=a_rel_bias[i], c_shift_mu=c_shift_mu[i], c_w0=c_w0[i], c_w2=c_w2[i],
                    c_a0=c_a0[i], c_a2=c_a2[i], c_g2=c_g2[i], c_k_k=c_k_k[i], c_k_a=c_k_a[i], c_r_k=c_r_k[i],
                    c_ln_w=c_ln_w[i], c_ln_b=c_ln_b[i], norm_ffn=norm_ffn[i], ple_norm=ple_norm[i])

    big = dict(w_in=w_in, w_branch=w_branch, w_out=w_out, w_ffn_gate=w_ffn_gate, w_ffn_up=w_ffn_up,
               w_ffn_down=w_ffn_down, w_ple_proj=w_ple_proj, w_ple_gate=w_ple_gate)
    big = {name: w.astype(BF16) for name, w in big.items()}

    bp, lp, _ = x_prompt.shape
    cfg_p = _group_config(lp)
    ret0 = jnp.zeros((bp, B_HEADS, B_HEAD_DIM, B_HEAD_DIM), F32)
    rwkv0 = jnp.zeros((bp, C_HEADS, C_HEAD_DIM, C_HEAD_DIM), F32)
    shift0 = jnp.zeros((bp, 1, C_SHIFT_WIDTH), F32)
    y_prompt = x_prompt
    st_p = []
    for i in range(depth):
        y_prompt, st = _layer(y_prompt, p_prompt, i, 0, None, None, ret0, rwkv0, shift0,
                              layer_weights(i), big, cfg_p)
        st_p.append(st)

    cfg_s = _group_config(x_sample.shape[1])
    y_sample = x_sample
    st_s = []
    for i in range(depth):
        y_sample, st = _layer(y_sample, p_sample, i, PAST_LEN, cache_a_k, cache_a_v, state_ret[i],
                              state_rwkv[i], state_rwkv_shift[i], layer_weights(i), big, cfg_s)
        st_s.append(st)

    stack = lambda sts, j: jnp.stack([s[j] for s in sts])
    return (y_prompt, y_sample,
            stack(st_p, 0), stack(st_p, 1), stack(st_p, 2), stack(st_p, 3), stack(st_p, 4),
            stack(st_s, 0), stack(st_s, 1), stack(st_s, 2), stack(st_s, 3), stack(st_s, 4))
```

```python
import functools
import math

import jax
import jax.numpy as jnp
from jax import lax
from jax.experimental import pallas as pl
from jax.experimental.pallas import tpu as pltpu

F32 = jnp.float32
BF16 = jnp.bfloat16

D_MODEL = 1024
PAST_LEN = 2048
CHUNK = 64
NORM_EPS = 1e-6

A_HEADS = 8
A_HEAD_DIM = 64
A_WIDTH = 512
A_WIN = 512
A_REL_MAX = 256

B_HEADS = 4
B_HEAD_DIM = 128
B_WIDTH = 512
ROPE_BASE = 10000.0

C_HEADS = 8
C_HEAD_DIM = 64
C_WIDTH = 512
C_RANK_W = 64
C_RANK_A = 64
C_RANK_G = 128
C_SHIFT_WIDTH = 3 * C_WIDTH + C_RANK_W + C_RANK_A + C_RANK_G
C_GN_EPS = 64e-5

N_BRANCHES = 3

OFF_A = 0
OFF_B = 3 * A_WIDTH
OFF_C = OFF_B + 4 * B_WIDTH
OFF_G = OFF_C + C_SHIFT_WIDTH
IN_WIDTH = OFF_G + N_BRANCHES * D_MODEL

RET_LOG_GAMMA = tuple(math.log1p(-(2.0 ** (-5.0 - h))) for h in range(B_HEADS))

VMEM_LIMIT_BYTES = 56 * 1024 * 1024
MXU_TILE = 256
LOG2_E = math.log2(math.e)

NN = (((1,), (0,)), ((), ()))
NT = (((1,), (1,)), ((), ()))
TN = (((0,), (0,)), ((), ()))


def _dot(a, b, dims=NN):
    return lax.dot_general(a.astype(BF16), b.astype(BF16), dims, preferred_element_type=F32)


def _rms(x):
    return x * lax.rsqrt(jnp.mean(x * x, axis=-1, keepdims=True) + NORM_EPS)


def _sigmoid(x):
    return 1.0 / (1.0 + jnp.exp(-x))


def _const_spec(shape):
    nd = len(shape)
    return pl.BlockSpec(shape, lambda *_: (0,) * nd, pipeline_mode=pl.Buffered(1))


def _layer_weight(stacked, layer, cols=None):
    if cols is not None:
        off, width = cols
        if off % width == 0:
            spec = pl.BlockSpec((None, stacked.shape[1], width), lambda *_: (layer, 0, off // width),
                                pipeline_mode=pl.Buffered(1))
            return stacked, spec
        stacked = stacked[:, :, off:off + width]
    shape = stacked.shape[1:]
    spec = pl.BlockSpec((None,) + shape, lambda *_: (layer,) + (0,) * len(shape),
                        pipeline_mode=pl.Buffered(1))
    return stacked, spec


def _head_sum_matrix(head_dim):
    head = jnp.arange(MXU_TILE, dtype=jnp.int32) // head_dim
    return (head[:, None] == head[None, :]).astype(BF16)


def _head_sum(x, hsum):
    w = hsum.shape[0]
    return jnp.concatenate([_dot(x[:, g * w:(g + 1) * w], hsum) for g in range(x.shape[1] // w)], axis=1)


def _attn_kernel(*refs, tq, cq, has_cache, nb):
    if has_cache:
        (x_ref, g_ref, w_ref, qg_ref, kg_ref, bias_ref, hsum_ref, kc_ref, vc_ref,
         o_ref, kn_ref, v_ref, kwin, vwin, qs) = refs
    else:
        (x_ref, g_ref, w_ref, qg_ref, kg_ref, bias_ref, hsum_ref,
         o_ref, kn_ref, v_ref, kwin, vwin, qs) = refs
    i = pl.program_id(1)
    nkeys = A_WIN + cq
    pair_w = 2 * A_HEAD_DIM

    if has_cache:
        kwin[:, 0:A_WIN, :] = kc_ref[...].astype(BF16)
        vwin[:, 0:A_WIN, :] = vc_ref[...].astype(BF16)
    else:
        @pl.when(i == 0)
        def _():
            kwin[0:A_WIN, :] = jnp.zeros((A_WIN, A_WIDTH), BF16)
            vwin[0:A_WIN, :] = jnp.zeros((A_WIN, A_WIDTH), BF16)

    h = _rms(x_ref[...].reshape(nb * tq, D_MODEL)) * g_ref[...]
    z = _dot(h, w_ref[...])
    q = z[:, 0:A_WIDTH]
    k = z[:, A_WIDTH:2 * A_WIDTH]
    v = z[:, 2 * A_WIDTH:3 * A_WIDTH]
    hsum = hsum_ref[...]
    inv_d = 1.0 / A_HEAD_DIM
    qn = q * lax.rsqrt(_head_sum(q * q, hsum) * inv_d + NORM_EPS) * qg_ref[...]
    kn = k * lax.rsqrt(_head_sum(k * k, hsum) * inv_d + NORM_EPS) * kg_ref[...]
    qs[...] = qn * (A_HEAD_DIM ** -0.5 * LOG2_E)
    if has_cache:
        kwin[:, A_WIN:A_WIN + tq, :] = kn.astype(BF16).reshape(nb, tq, A_WIDTH)
        vwin[:, A_WIN:A_WIN + tq, :] = v.astype(BF16).reshape(nb, tq, A_WIDTH)
    else:
        kwin[A_WIN:A_WIN + tq, :] = kn.astype(BF16)
        vwin[A_WIN:A_WIN + tq, :] = v.astype(BF16)
    keep = min(A_WIN, tq)
    kn_ref[...] = kn[nb * tq - nb * keep:].reshape(nb, keep, A_WIDTH)
    v_ref[...] = v[nb * tq - nb * keep:].reshape(nb, keep, A_WIDTH)

    rho = lax.broadcasted_iota(jnp.int32, (2 * cq, pair_w), 0)
    lane = lax.broadcasted_iota(jnp.int32, (2 * cq, pair_w), 1)
    placed = (rho >> (cq.bit_length() - 1)) == (lane >> (A_HEAD_DIM.bit_length() - 1))

    n_chunks = nb * tq // cq
    per_trip = 8 if n_chunks % 8 == 0 else (4 if n_chunks % 4 == 0 else (2 if n_chunks % 2 == 0 else 1))
    units = [(c, p) for c in range(per_trip) for p in range(A_HEADS // 2)]
    ids = range(len(units))

    def chunk(j, carry, mask_past):
        r0s = [(j * per_trip + c) * cq for c in range(per_trip)]
        if not isinstance(j, int):
            r0s = [pl.multiple_of(r0, cq) for r0 in r0s]
        lanes = [slice(p * pair_w, (p + 1) * pair_w) for _, p in units]
        qc = [qs[pl.ds(r0s[c], cq), lanes[u]] for u, (c, _) in enumerate(units)]
        qp = [jnp.where(placed, jnp.concatenate([qc[u], qc[u]], axis=0), 0.0) for u in ids]
        if has_cache:
            win = [(j * per_trip + c, slice(None), lanes[u]) for u, (c, _) in enumerate(units)]
        else:
            win = [(pl.ds(r0s[c], nkeys), lanes[u]) for u, (c, _) in enumerate(units)]
        s = [_dot(qp[u], kwin[win[u]], NT) + bias_ref[p] for u, (_, p) in enumerate(units)]
        if mask_past:
            col = lax.broadcasted_iota(jnp.int32, (2 * cq, nkeys), 1)
            s = [jnp.where(r0s[c] + col >= A_WIN, s[u], -1e30) for u, (c, _) in enumerate(units)]
        m = [jnp.max(s[u], axis=-1, keepdims=True) for u in ids]
        e = [jnp.exp2(s[u] - m[u]) for u in ids]
        den = [jnp.sum(e[u], axis=-1, keepdims=True) for u in ids]
        o2 = [_dot(e[u], vwin[win[u]]) for u in ids]
        for u, (c, _) in enumerate(units):
            o = jnp.where(placed, o2[u] / den[u], 0.0)
            if has_cache:
                o_ref[j * per_trip + c, :, lanes[u]] = o[:cq] + o[cq:]
            else:
                o_ref[0, pl.ds(r0s[c], cq), lanes[u]] = o[:cq] + o[cq:]
        return carry

    n_trips = n_chunks // per_trip
    if has_cache:
        lax.fori_loop(0, n_trips, functools.partial(chunk, mask_past=False), 0)
    else:
        @pl.when(i == 0)
        def _():
            lax.fori_loop(0, n_trips, functools.partial(chunk, mask_past=True), 0)

        @pl.when(i > 0)
        def _():
            for j in range(n_trips):
                chunk(j, 0, mask_past=False)

    if not has_cache:
        kwin[0:A_WIN, :] = kwin[tq:tq + A_WIN, :]
        vwin[0:A_WIN, :] = vwin[tq:tq + A_WIN, :]


def _attention(x, gain, w_a, q_gain, k_gain, bias, cache_k, cache_v, layer, *, tq, cq, nb):
    bsz, seq, _ = x.shape
    has_cache = cache_k is not None
    assert seq % tq == 0 and tq % cq == 0
    assert (seq == tq == cq <= A_WIN and bsz % nb == 0) if has_cache else (tq % A_WIN == 0 and nb == 1)
    keep = min(A_WIN, tq)
    nkeys = A_WIN + cq
    tok = lambda w: pl.BlockSpec((nb, tq, w), lambda b, i: (b, i, 0))
    win_shape = (nb, nkeys, A_WIDTH) if has_cache else (A_WIN + tq, A_WIDTH)
    in_specs = [tok(D_MODEL), _const_spec((1, D_MODEL)), w_a[1],
                _const_spec((1, A_WIDTH)), _const_spec((1, A_WIDTH)),
                _const_spec((A_HEADS // 2, 2 * cq, nkeys)), _const_spec((MXU_TILE, MXU_TILE))]
    args = [x, gain, w_a[0], jnp.tile(q_gain, (1, A_HEADS)), jnp.tile(k_gain, (1, A_HEADS)),
            bias.reshape(A_HEADS // 2, 2 * cq, nkeys), _head_sum_matrix(A_HEAD_DIM)]
    if has_cache:
        cspec = pl.BlockSpec((None, nb, A_WIN, A_WIDTH), lambda b, i: (layer, b, 0, 0))
        in_specs += [cspec, cspec]
        args += [cache_k, cache_v]
    out = jax.ShapeDtypeStruct((bsz, seq, A_WIDTH), F32)
    return pl.pallas_call(
        functools.partial(_attn_kernel, tq=tq, cq=cq, has_cache=has_cache, nb=nb),
        grid=(bsz // nb, seq // tq),
        in_specs=in_specs,
        out_specs=[tok(A_WIDTH)] + [pl.BlockSpec((nb, keep, A_WIDTH), lambda b, i: (b, 0, 0))] * 2,
        out_shape=[out] + [jax.ShapeDtypeStruct((bsz, keep, A_WIDTH), F32)] * 2,
        scratch_shapes=[pltpu.VMEM(win_shape, BF16),
                        pltpu.VMEM(win_shape, BF16),
                        pltpu.VMEM((nb * tq, A_WIDTH), F32)],
        compiler_params=pltpu.CompilerParams(
            dimension_semantics=("arbitrary", "arbitrary"),
            vmem_limit_bytes=VMEM_LIMIT_BYTES),
        name="attention",
    )(*args)


def _ret_kernel(x_ref, g_ref, w_ref, cosa_ref, sina_ref, cosb_ref, sinb_ref, cosbs_ref, sinbs_ref,
                s0_ref, o_ref, sout_ref, s_scr, dmat_s, *, tb, nb, nsub):
    batched = nb > 1
    i = pl.program_id(1)

    if not batched:
        @pl.when(i == 0)
        def _():
            s_scr[...] = s0_ref[0]

    @pl.when(jnp.logical_and(pl.program_id(0) == 0, i == 0))
    def _():
        row = lax.broadcasted_iota(jnp.int32, (tb, tb), 0)
        col = lax.broadcasted_iota(jnp.int32, (tb, tb), 1)
        diff = row - col
        dist = jnp.maximum(diff, 0).astype(F32)
        for hh in range(B_HEADS):
            dmat_s[hh] = jnp.where(diff >= 0, jnp.exp(RET_LOG_GAMMA[hh] * dist), 0.0)

    rows = nb * nsub * tb
    h = _rms(x_ref[...].reshape(rows, D_MODEL)) * g_ref[...]
    z = _dot(h, w_ref[...])
    cos_a, sin_a = cosa_ref[0], sina_ref[0]
    cos_all = cos_a * cosb_ref[...] - sin_a * sinb_ref[...]
    sin_all = sin_a * cosbs_ref[...] + cos_a * sinbs_ref[...]
    n = lax.broadcasted_iota(jnp.int32, (tb, 1), 0).astype(F32)
    for hh in range(B_HEADS):
        lg = RET_LOG_GAMMA[hh]
        lo = hh * B_HEAD_DIM
        dmat = dmat_s[hh]
        dec_q = jnp.exp(lg * (n + 1.0))
        dec_k = jnp.exp(lg * ((tb - 1.0) - n))
        for bb in range(nb):
            state = s0_ref[bb, hh] if batched else s_scr[hh]
            for sub in range(nsub):
                rs = slice((bb * nsub + sub) * tb, (bb * nsub + sub + 1) * tb)
                pos = slice(sub * tb, (sub + 1) * tb)
                cosf = cos_all[pos, :]
                sinf = sin_all[pos, :]
                q = z[rs, lo:lo + B_HEAD_DIM]
                k = z[rs, B_WIDTH + lo:B_WIDTH + lo + B_HEAD_DIM]
                v = z[rs, 2 * B_WIDTH + lo:2 * B_WIDTH + lo + B_HEAD_DIM]
                gate = z[rs, 3 * B_WIDTH + lo:3 * B_WIDTH + lo + B_HEAD_DIM]
                q = q * cosf + pltpu.roll(q, B_HEAD_DIM // 2, 1) * sinf
                k = (k * cosf + pltpu.roll(k, B_HEAD_DIM // 2, 1) * sinf) * (B_HEAD_DIM ** -0.5)
                scores = _dot(q, k, NT) * dmat
                o = _dot(scores, v) + _dot(q, state) * dec_q
                state = math.exp(lg * tb) * state + _dot(k * dec_k, v, TN)
                o_ref[bb, pos, lo:lo + B_HEAD_DIM] = _rms(o) * (gate * _sigmoid(gate))
            if batched:
                sout_ref[bb, hh] = state
            else:
                s_scr[hh] = state
    if not batched:
        sout_ref[0] = s_scr[...]


def _retention(x, gain, w_b, pos0, s0, *, tb, nb, nsub):
    bsz, seq, _ = x.shape
    assert (seq == tb and bsz % nb == 0 and nsub == 1) if nb > 1 else seq % (nsub * tb) == 0
    sspec = pl.BlockSpec((nb, B_HEADS, B_HEAD_DIM, B_HEAD_DIM), lambda b, i: (b, 0, 0, 0))
    tok = lambda w: pl.BlockSpec((nb, nsub * tb, w), lambda b, i: (b, i, 0))
    step_rows = nsub * tb
    half = B_HEAD_DIM // 2
    inv = ROPE_BASE ** (-jnp.arange(half, dtype=F32) / half)
    base = (pos0 + step_rows * jnp.arange(seq // step_rows, dtype=jnp.int32)).astype(F32)[:, None] * inv[None, :]
    offs = jnp.arange(step_rows, dtype=jnp.int32).astype(F32)[:, None] * inv[None, :]
    both = lambda t: jnp.concatenate([t, t], axis=-1)
    signed = lambda t: jnp.concatenate([-t, t], axis=-1)
    step_row = lambda t: both(t).reshape(-1, 1, B_HEAD_DIM)
    a_spec = pl.BlockSpec((1, 1, B_HEAD_DIM), lambda b, i: (i, 0, 0))
    b_spec = _const_spec((step_rows, B_HEAD_DIM))
    return pl.pallas_call(
        functools.partial(_ret_kernel, tb=tb, nb=nb, nsub=nsub),
        grid=(bsz // nb, seq // (nsub * tb)),
        in_specs=[tok(D_MODEL),
                  _const_spec((1, D_MODEL)), w_b[1],
                  a_spec, a_spec, b_spec, b_spec, b_spec, b_spec,
                  sspec],
        out_specs=[tok(B_WIDTH), sspec],
        out_shape=[jax.ShapeDtypeStruct((bsz, seq, B_WIDTH), F32),
                   jax.ShapeDtypeStruct((bsz, B_HEADS, B_HEAD_DIM, B_HEAD_DIM), F32)],
        scratch_shapes=[pltpu.VMEM((B_HEADS, B_HEAD_DIM, B_HEAD_DIM), F32),
                        pltpu.VMEM((B_HEADS, tb, tb), F32)],
        compiler_params=pltpu.CompilerParams(
            dimension_semantics=("arbitrary", "arbitrary"),
            vmem_limit_bytes=VMEM_LIMIT_BYTES),
        name="retention",
    )(x, gain, w_b[0], step_row(jnp.cos(base)), step_row(jnp.sin(base)),
      both(jnp.cos(offs)), both(jnp.sin(offs)), signed(jnp.cos(offs)), signed(jnp.sin(offs)), s0)


PAIR_W = 2 * C_HEAD_DIM
N_PAIRS = C_HEADS // 2


def _rwkv_kernel(*refs, tt, cc, lookahead, batched):
    if lookahead:
        x_ref, xn_ref = refs[:2]
        refs = refs[2:]
    else:
        x_ref, xn_ref = refs[0], None
        refs = refs[1:]
    (g_ref, w_ref, mu_ref, sh0_ref, s0_ref, w0_ref, w2_ref, a0_ref, a2_ref,
     g2_ref, kk_ref, ka_ref, rk_ref, lnw_ref, lnb_ref, hsum_ref,
     o_ref, sout_ref, shout_ref,
     s_scr, carry, rk_raw, uw_raw, ua_raw, rt_s, kkt_s, kh_s, bh_s, v_s, bon_s, g_s, cum_s) = refs
    i = pl.program_id(1)

    n_chunks = tt // cc

    def project(x, slot):
        h = _rms(x.reshape(tt, D_MODEL)) * g_ref[...]
        cz = _dot(h, w_ref[...])
        rows = lax.broadcasted_iota(jnp.int32, (tt, 1), 0)
        if batched:
            first = jnp.bitwise_and(rows, cc - 1) == 0
            before = jnp.broadcast_to(sh0_ref[...], (n_chunks, cc, C_SHIFT_WIDTH)).reshape(tt, C_SHIFT_WIDTH)
            shout_ref[...] = cz.reshape(n_chunks, cc, C_SHIFT_WIDTH)[:, cc - 1:cc, :]
        else:
            first = rows == 0
            before = carry[...]
            last = cz[tt - 1:tt, :]
            carry[...] = last
            shout_ref[0] = last
        prev = jnp.where(first, before, pltpu.roll(cz, 1, 0))
        cs = cz + (prev - cz) * mu_ref[...]
        off = 3 * C_WIDTH
        w_lo = cs[:, off:off + C_RANK_W]
        a_lo = cs[:, off + C_RANK_W:off + C_RANK_W + C_RANK_A]
        g_lo = cs[:, off + C_RANK_W + C_RANK_A:C_SHIFT_WIDTH]
        rk_raw[...] = cs[:, 0:2 * C_WIDTH]
        v_s[slot] = cs[:, 2 * C_WIDTH:3 * C_WIDTH]
        uw_raw[...] = w0_ref[...] + _dot(jnp.tanh(w_lo), w2_ref[...])
        ua_raw[...] = a0_ref[...] + _dot(a_lo, a2_ref[...])
        g_s[slot] = _dot(_sigmoid(g_lo), g2_ref[...])

    def token_block(b, slot):
        rs = slice(b * cc, (b + 1) * cc)
        r = rk_raw[rs, 0:C_WIDTH]
        k = rk_raw[rs, C_WIDTH:2 * C_WIDTH]
        v = v_s[slot, rs, :]
        lw = -math.exp(-0.5) * _sigmoid(uw_raw[rs, :])
        a = _sigmoid(ua_raw[rs, :])
        hsum = hsum_ref[...]
        kk_raw = k * kk_ref[...]
        kk = kk_raw / jnp.maximum(jnp.sqrt(_head_sum(kk_raw * kk_raw, hsum)), 1e-12)
        k2 = k * (1.0 + (a - 1.0) * ka_ref[...])
        pos = lax.broadcasted_iota(jnp.int32, (cc, 1), 0)
        cum = lw
        step = 1
        while step < cc:
            cum = cum + jnp.where(pos >= step, pltpu.roll(cum, step, 0), 0.0)
            step *= 2
        e_inv = jnp.exp(-cum)
        rt_s[slot, rs, :] = r * jnp.exp(cum)
        kkt_s[slot, rs, :] = kk * jnp.exp(cum - lw)
        kh_s[slot, rs, :] = k2 * e_inv
        bh_s[slot, rs, :] = kk * a * e_inv
        bon_s[slot, rs, :] = _head_sum(r * k2 * rk_ref[...], hsum) * v
        cum_s[slot, rs, :] = cum

    if not batched:
        @pl.when(i == 0)
        def _():
            s_scr[...] = s0_ref[0]
            carry[...] = sh0_ref[0]
            if lookahead:
                project(x_ref[...], 0)
                for b in range(n_chunks):
                    token_block(b, 0)

    if lookahead:
        cur = lax.rem(i, 2)
        nxt = 1 - cur
        project(xn_ref[...], nxt)
        pending = [functools.partial(token_block, b, nxt) for b in range(n_chunks)]
    else:
        cur = 0
        project(x_ref[...], 0)
        for b in range(n_chunks):
            token_block(b, 0)
        pending = []

    def interleave():
        if pending:
            pending.pop(0)()

    two = 2 * cc
    sh = cc.bit_length() - 1
    hd = C_HEAD_DIM.bit_length() - 1
    n_double = sh - 1
    rho = lax.broadcasted_iota(jnp.int32, (two, PAIR_W), 0)
    lane = lax.broadcasted_iota(jnp.int32, (two, PAIR_W), 1)
    placed = (rho >> sh) == (lane >> hd)
    r2 = lax.broadcasted_iota(jnp.int32, (two, two), 0)
    c2 = lax.broadcasted_iota(jnp.int32, (two, two), 1)
    same = (r2 >> sh) == (c2 >> sh)
    strict = jnp.logical_and(same, r2 > c2)
    incl = jnp.logical_and(same, r2 >= c2)
    eye = jnp.where(r2 == c2, 1.0, 0.0).astype(F32)
    gr = lax.broadcasted_iota(jnp.int32, (PAIR_W, PAIR_W), 0)
    gc = lax.broadcasted_iota(jnp.int32, (PAIR_W, PAIR_W), 1)
    hmean = jnp.where((gr >> hd) == (gc >> hd), 1.0 / C_HEAD_DIM, 0.0).astype(BF16)

    def place(xv):
        return jnp.where(placed, jnp.concatenate([xv, xv], axis=0), 0.0)

    per_trip = 4 if n_chunks % 4 == 0 else (2 if n_chunks % 2 == 0 else 1)
    units = [(c, p) for c in range(per_trip) for p in range(N_PAIRS)]
    pairs = range(len(units))

    def chunk(j):
        r0s = [(j * per_trip + c) * cc for c in range(per_trip)]
        w_end = [jnp.exp(cum_s[cur, r0 + cc - 1:r0 + cc, :]) for r0 in r0s]
        lanes = [slice(p * PAIR_W, (p + 1) * PAIR_W) for _, p in units]
        sl = [(cur, slice(r0s[c], r0s[c] + cc), lanes[u]) for u, (c, _) in enumerate(units)]
        we = [w_end[c][:, lanes[u]] for u, (c, _) in enumerate(units)]
        rt = [place(rt_s[sl[p]]) for p in pairs]
        kkt = [place(kkt_s[sl[p]]) for p in pairs]
        kh = [place(kh_s[sl[p]]) for p in pairs]
        bh = [place(bh_s[sl[p]]) for p in pairs]
        vv = [place(v_s[sl[p]]) for p in pairs]
        cat0 = lambda *xs: jnp.concatenate(xs, axis=0)
        cat1 = lambda *xs: jnp.concatenate(xs, axis=1)
        gg = [_dot(cat0(kkt[p], rt[p]), cat0(kh[p], bh[p]), NT) for p in pairs]
        interleave()
        a_kk = [jnp.where(strict, gg[p][:two, :two], 0.0) for p in pairs]
        b_kk = [jnp.where(incl, gg[p][two:, :two], 0.0) for p in pairs]
        b_bb = [jnp.where(incl, gg[p][two:, two:], 0.0) for p in pairs]
        pw = [jnp.where(strict, -gg[p][:two, two:], 0.0) for p in pairs]
        tinv = [eye + pw[p] for p in pairs]
        pw = [_dot(pw[p], pw[p]) for p in pairs]
        for _ in range(n_double - 1):
            pt = [_dot(pw[p], cat1(pw[p], tinv[p])) for p in pairs]
            pw = [pt[p][:, :two] for p in pairs]
            tinv = [tinv[p] + pt[p][:, two:] for p in pairs]
            interleave()
        tinv = [tinv[p] + _dot(pw[p], tinv[p]) for p in pairs]
        interleave()
        av = [_dot(a_kk[p], vv[p]) for p in pairs]
        kv = [_dot(tinv[p], cat1(kkt[p], av[p])) for p in pairs]
        zero = jnp.zeros((two, PAIR_W), F32)
        ry = [_dot(cat1(b_kk[p], -b_bb[p]), cat0(cat1(zero, vv[p]), kv[p])) for p in pairs]
        rp = [rt[p] + ry[p][:, :PAIR_W] for p in pairs]
        y0 = [ry[p][:, PAIR_W:] for p in pairs]
        kkp = [kv[p][:, :PAIR_W] for p in pairs]
        vp = [kv[p][:, PAIR_W:] for p in pairs]
        bd = [bh[p] * we[p] for p in pairs]
        kd = [kh[p] * we[p] for p in pairs]
        q = [_dot(kkp[p], bd[p], TN) for p in pairs]
        z = [_dot(cat0(vv[p], vp[p]), cat0(kd[p], -bd[p]), TN) for p in pairs]
        y2 = []

        def advance(u, s_old):
            y2.append(_dot(rp[u], s_old, NT) + y0[u])
            return s_old * we[u] - _dot(s_old, q[u]) + z[u]

        if batched:
            for u, (c, p) in enumerate(units):
                sout_ref[j * per_trip + c, p] = advance(u, s0_ref[j * per_trip + c, p])
        else:
            state = [s_scr[p] for p in range(N_PAIRS)]
            for u, (_, p) in enumerate(units):
                state[p] = advance(u, state[p])
            for p in range(N_PAIRS):
                s_scr[p] = state[p]
        ys = [y2[u][:cc] + y2[u][cc:] for u in pairs]
        yc = [ys[u] - _dot(ys[u], hmean) for u in pairs]
        var = [_dot(yc[u] * yc[u], hmean) for u in pairs]
        for u, (c, _) in enumerate(units):
            yn = yc[u] * lax.rsqrt(var[u] + C_GN_EPS) * lnw_ref[:, lanes[u]] + lnb_ref[:, lanes[u]]
            out = (yn + bon_s[sl[u]]) * g_s[sl[u]]
            if batched:
                o_ref[j * per_trip + c, :, lanes[u]] = out
            else:
                o_ref[(0,) + sl[u][1:]] = out

    for j in range(n_chunks // per_trip):
        chunk(j)
    while pending:
        interleave()
    if not batched:
        sout_ref[0] = s_scr[...]


def _pair_states(s):
    bsz = s.shape[0]
    s = s.reshape(bsz, N_PAIRS, 2, C_HEAD_DIM, C_HEAD_DIM)
    zero = jnp.zeros_like(s[:, :, 0])
    top = jnp.concatenate([s[:, :, 0], zero], axis=-1)
    bot = jnp.concatenate([zero, s[:, :, 1]], axis=-1)
    return jnp.concatenate([top, bot], axis=-2)


def _unpair_states(s):
    bsz = s.shape[0]
    d = C_HEAD_DIM
    return jnp.stack([s[:, :, :d, :d], s[:, :, d:, d:]], axis=2).reshape(bsz, C_HEADS, d, d)


def _rwkv(x, gain, w_c, lw, shift0, s0, *, tt, cc, nb):
    bsz, seq, _ = x.shape
    batched = nb > 1
    assert cc & (cc - 1) == 0
    assert (seq == cc and tt == nb * cc and bsz % nb == 0) if batched else (seq % tt == 0 and tt % cc == 0)
    row = lambda a: a.reshape(1, -1)
    sspec = pl.BlockSpec((nb, N_PAIRS, PAIR_W, PAIR_W), lambda b, i: (b, 0, 0, 0))
    shspec = pl.BlockSpec((nb, 1, C_SHIFT_WIDTH), lambda b, i: (b, 0, 0))
    vec = _const_spec((1, C_WIDTH))
    n_tiles = 1 if batched else seq // tt
    lookahead = n_tiles > 1
    tile = pltpu.VMEM((2 if lookahead else 1, tt, C_WIDTH), F32)
    hsum = _head_sum_matrix(C_HEAD_DIM)
    tok = lambda w: pl.BlockSpec((nb, tt // nb, w), lambda b, i: (b, i, 0))
    x_specs = [tok(D_MODEL)]
    x_args = [x]
    if lookahead:
        x_specs = [pl.BlockSpec((1, tt, D_MODEL), lambda b, i: (b, 0, 0)),
                   pl.BlockSpec((1, tt, D_MODEL), lambda b, i: (b, jnp.minimum(i + 1, n_tiles - 1), 0))]
        x_args = [x, x]
    oc, s_new, shift_new = pl.pallas_call(
        functools.partial(_rwkv_kernel, tt=tt, cc=cc, lookahead=lookahead, batched=batched),
        grid=(bsz // nb, n_tiles),
        in_specs=x_specs + [
                  _const_spec((1, D_MODEL)), w_c[1],
                  _const_spec((1, C_SHIFT_WIDTH)), shspec, sspec,
                  vec, _const_spec((C_RANK_W, C_WIDTH)), vec, _const_spec((C_RANK_A, C_WIDTH)),
                  _const_spec((C_RANK_G, C_WIDTH)), vec, vec, vec, vec, vec,
                  _const_spec((MXU_TILE, MXU_TILE))],
        out_specs=[tok(C_WIDTH), sspec, shspec],
        out_shape=[jax.ShapeDtypeStruct((bsz, seq, C_WIDTH), F32),
                   jax.ShapeDtypeStruct((bsz, N_PAIRS, PAIR_W, PAIR_W), F32),
                   jax.ShapeDtypeStruct((bsz, 1, C_SHIFT_WIDTH), F32)],
        scratch_shapes=[pltpu.VMEM((N_PAIRS, PAIR_W, PAIR_W), F32),
                        pltpu.VMEM((1, C_SHIFT_WIDTH), F32),
                        pltpu.VMEM((tt, 2 * C_WIDTH), F32),
                        pltpu.VMEM((tt, C_WIDTH), F32),
                        pltpu.VMEM((tt, C_WIDTH), F32)] + [tile] * 8,
        compiler_params=pltpu.CompilerParams(
            dimension_semantics=("arbitrary", "arbitrary"),
            vmem_limit_bytes=VMEM_LIMIT_BYTES),
        name="rwkv",
    )(*x_args, gain, w_c[0], row(lw["c_shift_mu"]), shift0, _pair_states(s0),
      row(lw["c_w0"]), lw["c_w2"].astype(BF16), row(lw["c_a0"]), lw["c_a2"].astype(BF16),
      lw["c_g2"].astype(BF16), row(lw["c_k_k"]), row(lw["c_k_a"]), row(lw["c_r_k"]),
      row(lw["c_ln_w"]), row(lw["c_ln_b"]), hsum)
    return oc, _unpair_states(s_new), shift_new


def _merge_kernel(x_ref, oa_ref, ob_ref, oc_ref, g_ref, wg_ref, wb_ref, wo_ref, y_ref):
    x = x_ref[...]
    h = (_rms(x) * g_ref[...]).astype(BF16)
    branches = [o_ref[...].astype(BF16) for o_ref in (oa_ref, ob_ref, oc_ref)]
    y = x
    for c in range(D_MODEL // MXU_TILE):
        cols = slice(c * MXU_TILE, (c + 1) * MXU_TILE)
        m = None
        for b, o in enumerate(branches):
            gate = _dot(h, wg_ref[:, b * D_MODEL + c * MXU_TILE:b * D_MODEL + (c + 1) * MXU_TILE])
            t = _sigmoid(gate) * _dot(o, wb_ref[b, :, cols])
            m = t if m is None else m + t
        y = y + _dot(m, wo_ref[cols, :])
    y_ref[...] = y


def _merge(x, oa, ob, oc, gain, w_g, w_b, w_o, *, tm):
    rows = x.shape[0]
    assert rows % tm == 0
    tok = lambda w: pl.BlockSpec((tm, w), lambda i: (i, 0))
    return pl.pallas_call(
        _merge_kernel,
        grid=(rows // tm,),
        in_specs=[tok(D_MODEL), tok(A_WIDTH), tok(B_WIDTH), tok(C_WIDTH),
                  _const_spec((1, D_MODEL)), w_g[1], w_b[1], w_o[1]],
        out_specs=tok(D_MODEL),
        out_shape=jax.ShapeDtypeStruct((rows, D_MODEL), F32),
        compiler_params=pltpu.CompilerParams(
            dimension_semantics=("arbitrary",), vmem_limit_bytes=VMEM_LIMIT_BYTES),
        name="merge",
    )(x, oa, ob, oc, gain, w_g[0], w_b[0], w_o[0])


def _ffn_kernel(x_ref, p_ref, g_ref, wgate_ref, wup_ref, wdown_ref, wpp_ref, pg_ref, wpg_ref, y_ref):
    x = x_ref[...]
    hf = (_rms(x) * g_ref[...]).astype(BF16)
    d_ff = wgate_ref.shape[1]
    for c in range(d_ff // MXU_TILE):
        cols = slice(c * MXU_TILE, (c + 1) * MXU_TILE)
        gate = _dot(hf, wgate_ref[:, cols])
        up = _dot(hf, wup_ref[:, cols])
        x = x + _dot(gate * _sigmoid(gate) * up, wdown_ref[cols, :])
    e = _rms(_dot(p_ref[...], wpp_ref[...])) * pg_ref[...]
    y_ref[...] = x + _sigmoid(_dot(_rms(x), wpg_ref[...])) * e


def _ffn(x, p_all, layer, gain, w_gate, w_up, w_down, w_pp, p_gain, w_pg, *, tm):
    rows = x.shape[0]
    ple = p_all.shape[-1]
    assert w_gate[0].shape[-1] % MXU_TILE == 0
    p = p_all.reshape(p_all.shape[0], rows, ple)
    assert rows % tm == 0
    tok = lambda w: pl.BlockSpec((tm, w), lambda i: (i, 0))
    return pl.pallas_call(
        _ffn_kernel,
        grid=(rows // tm,),
        in_specs=[tok(D_MODEL), pl.BlockSpec((None, tm, ple), lambda i: (layer, i, 0)),
                  _const_spec((1, D_MODEL)),
                  w_gate[1], w_up[1], w_down[1], w_pp[1],
                  _const_spec((1, D_MODEL)), w_pg[1]],
        out_specs=tok(D_MODEL),
        out_shape=jax.ShapeDtypeStruct((rows, D_MODEL), F32),
        compiler_params=pltpu.CompilerParams(
            dimension_semantics=("arbitrary",), vmem_limit_bytes=VMEM_LIMIT_BYTES),
        name="ffn",
    )(x, p, gain, w_gate[0], w_up[0], w_down[0], w_pp[0], p_gain, w_pg[0])


def _rel_bias_table(rel_bias, cq):
    nkeys = A_WIN + cq
    t_max = A_WIN + CHUNK - 1 + cq - 1
    heads, n_rel = rel_bias.shape
    w = t_max + 1
    tail = jnp.broadcast_to(rel_bias[:, n_rel - 1:], (heads, w - n_rel))
    rev = jnp.concatenate([tail, rel_bias[:, ::-1].astype(F32), jnp.zeros((heads, 1), F32)], axis=1)
    skew = jnp.tile(rev, (1, cq))[:, :cq * w].reshape(heads, cq, w)
    return skew[:, :, cq - 1:cq - 1 + nkeys] * LOG2_E


def _layer(x, p_all, layer, pos0, a_ck, a_cv, ret_s0, rwkv_s0, shift_prev, lw, big, cfg):
    bsz, seq, _ = x.shape
    row = lambda a: a.reshape(1, -1)
    w_a = _layer_weight(big["w_in"], layer, (OFF_A, OFF_B - OFF_A))
    w_b = _layer_weight(big["w_in"], layer, (OFF_B, OFF_C - OFF_B))
    w_c = _layer_weight(big["w_in"], layer, (OFF_C, OFF_G - OFF_C))
    w_g = _layer_weight(big["w_in"], layer, (OFF_G, IN_WIDTH - OFF_G))
    gain = row(lw["norm_mix"])

    nb = math.gcd(bsz, cfg["nb"])
    bias = _rel_bias_table(lw["a_rel_bias"], cfg["cq"])
    if a_ck is not None:
        a_ck = a_ck.reshape(a_ck.shape[0], bsz, A_WIN, A_WIDTH)
        a_cv = a_cv.reshape(a_cv.shape[0], bsz, A_WIN, A_WIDTH)
    oa, kn, av = _attention(x, gain, w_a, row(lw["a_q_norm"]), row(lw["a_k_norm"]), bias,
                            a_ck, a_cv, layer, tq=cfg["tq"], cq=cfg["cq"], nb=nb)
    keep = min(A_WIN, seq)
    new_ak = kn.reshape(bsz, keep, A_HEADS, A_HEAD_DIM)
    new_av = av.reshape(bsz, keep, A_HEADS, A_HEAD_DIM)

    ob, new_ret = _retention(x, gain, w_b, pos0, ret_s0, tb=cfg["tb"], nb=nb, nsub=cfg["nsub"])

    oc, new_rwkv, new_shift = _rwkv(x, gain, w_c, lw, shift_prev, rwkv_s0,
                                    tt=nb * cfg["tt"], cc=cfg["cc"], nb=nb)

    rows = bsz * seq
    tm = min(cfg["tm"], rows)
    flat = lambda t: t.reshape(rows, t.shape[-1])
    pick = lambda name: _layer_weight(big[name], layer)
    x1 = _merge(flat(x), flat(oa), flat(ob), flat(oc), gain, w_g,
                pick("w_branch"), pick("w_out"), tm=tm)
    x2 = _ffn(x1, p_all, layer, row(lw["norm_ffn"]), pick("w_ffn_gate"), pick("w_ffn_up"),
              pick("w_ffn_down"), pick("w_ple_proj"), row(lw["ple_norm"]), pick("w_ple_gate"),
              tm=min(cfg["tm_ffn"], rows))
    return x2.reshape(bsz, seq, D_MODEL), (new_ak, new_av, new_ret, new_rwkv, new_shift)


def _group_config(seq):
    if seq >= A_WIN:
        tq = 2 * A_WIN if seq % (2 * A_WIN) == 0 else A_WIN
        return dict(tq=tq, cq=CHUNK, tb=256, nsub=4, tt=512, cc=CHUNK, tm=1024, tm_ffn=1024, nb=1)
    return dict(tq=seq, cq=seq, tb=seq, nsub=1, tt=seq, cc=seq, tm=512, tm_ffn=512, nb=8)


def kernel(x_prompt, x_sample, p_prompt, p_sample, cache_a_k, cache_a_v, state_ret, state_rwkv, state_rwkv_shift, norm_mix, w_in, a_q_norm, a_k_norm, a_rel_bias, c_shift_mu, c_w0, c_w2, c_a0, c_a2, c_g2, c_k_k, c_k_a, c_r_k, c_ln_w, c_ln_b, w_branch, w_out, norm_ffn, w_ffn_gate, w_ffn_up, w_ffn_down, w_ple_proj, ple_norm, w_ple_gate):
    depth = w_in.shape[0]

    def layer_weights(i):
        return dict(norm_mix=norm_mix[i], a_q_norm=a_q_norm[i], a_k_norm=a_k_norm[i],
                    a_rel_bias=a_rel_bias[i], c_shift_mu=c_shift_mu[i], c_w0=c_w0[i], c_w2=c_w2[i],
                    c_a0=c_a0[i], c_a2=c_a2[i], c_g2=c_g2[i], c_k_k=c_k_k[i], c_k_a=c_k_a[i], c_r_k=c_r_k[i],
                    c_ln_w=c_ln_w[i], c_ln_b=c_ln_b[i], norm_ffn=norm_ffn[i], ple_norm=ple_norm[i])

    big = dict(w_in=w_in, w_branch=w_branch, w_out=w_out, w_ffn_gate=w_ffn_gate, w_ffn_up=w_ffn_up,
               w_ffn_down=w_ffn_down, w_ple_proj=w_ple_proj, w_ple_gate=w_ple_gate)
    big = {name: w.astype(BF16) for name, w in big.items()}

    bp, lp, _ = x_prompt.shape
    cfg_p = _group_config(lp)
    ret0 = jnp.zeros((bp, B_HEADS, B_HEAD_DIM, B_HEAD_DIM), F32)
    rwkv0 = jnp.zeros((bp, C_HEADS, C_HEAD_DIM, C_HEAD_DIM), F32)
    shift0 = jnp.zeros((bp, 1, C_SHIFT_WIDTH), F32)
    y_prompt = x_prompt
    st_p = []
    for i in range(depth):
        y_prompt, st = _layer(y_prompt, p_prompt, i, 0, None, None, ret0, rwkv0, shift0,
                              layer_weights(i), big, cfg_p)
        st_p.append(st)

    cfg_s = _group_config(x_sample.shape[1])
    y_sample = x_sample
    st_s = []
    for i in range(depth):
        y_sample, st = _layer(y_sample, p_sample, i, PAST_LEN, cache_a_k, cache_a_v, state_ret[i],
                              state_rwkv[i], state_rwkv_shift[i], layer_weights(i), big, cfg_s)
        st_s.append(st)

    stack = lambda sts, j: jnp.stack([s[j] for s in sts])
    return (y_prompt, y_sample,
            stack(st_p, 0), stack(st_p, 1), stack(st_p, 2), stack(st_p, 3), stack(st_p, 4),
            stack(st_s, 0), stack(st_s, 1), stack(st_s, 2), stack(st_s, 3), stack(st_s, 4))
```
